```python
import math
import jax, jax.numpy as jnp
from jax import lax
import numpy as np

D_MODEL = 1024
BATCH = 4
SEQ = 4096
DEPTH = 2

N_MIXERS = 2
SSM_EXPAND = 2
SSM_D_INNER = SSM_EXPAND * D_MODEL
SSM_HEADDIM = 64
SSM_N_HEADS = SSM_D_INNER // SSM_HEADDIM
SSM_N_GROUPS = 8
SSM_HEADS_PER_GROUP = SSM_N_HEADS // SSM_N_GROUPS
SSM_D_STATE = 128
SSM_CONV_K = 4
SSM_CHUNK = 128
SSM_CONV_DIM = SSM_D_INNER + 2 * SSM_N_GROUPS * SSM_D_STATE
SSM_IN_DIM = SSM_D_INNER + SSM_CONV_DIM + SSM_N_HEADS
ATTN_HEAD_DIM = 64
ATTN_N_HEADS = D_MODEL // (2 * ATTN_HEAD_DIM)
ATTN_V_DIM = 2 * ATTN_HEAD_DIM
ATTN_QKV_DIM = 3 * D_MODEL
ROT_DIM = ATTN_HEAD_DIM // 4
ROPE_THETA = 500000.0
Q_BLOCK = 128
MOE_GROUPS = 4
MOE_EXPERTS_PER_GROUP = 8
MOE_N_EXPERTS = MOE_GROUPS * MOE_EXPERTS_PER_GROUP
MOE_TOP_K = 2
MOE_D_FF = 512
MOE_BLOCK = 128
DEEPNORM_ALPHA = (2 * DEPTH) ** 0.25
DEEPNORM_BETA = (8 * DEPTH) ** -0.25
NORM_EPS = 1e-5
N_SSM_LAYERS = (DEPTH + N_MIXERS - 1) // N_MIXERS
N_ATTN_LAYERS = DEPTH // N_MIXERS

kernel_name = "hybrid_ssd_diffattn_hiermoe_deepnorm"


def layer_norm(x, g, b):
    xf = x.astype(jnp.float32)
    mu = jnp.mean(xf, axis=-1, keepdims=True)
    var = jnp.mean(jnp.square(xf - mu), axis=-1, keepdims=True)
    y = (xf - mu) * lax.rsqrt(var + NORM_EPS) * g.astype(jnp.float32) + b.astype(jnp.float32)
    return y.astype(x.dtype)


def causal_depthwise_conv(u, w, bias):
    k = w.shape[0]
    out = lax.conv_general_dilated(u, w[:, None, :], window_strides=(1,), padding=[(k - 1, 0)],
                                   dimension_numbers=("NWC", "WIO", "NWC"),
                                   feature_group_count=u.shape[-1])
    return out + bias


def ssd_chunked(xh, dt, a, bm, cm):
    b, s = xh.shape[0], xh.shape[1]
    c = s // SSM_CHUNK
    G, R, P, N, L = SSM_N_GROUPS, SSM_HEADS_PER_GROUP, SSM_HEADDIM, SSM_D_STATE, SSM_CHUNK
    dtc = dt.reshape(b, c, L, G, R)
    xdt = xh.reshape(b, c, L, G, R, P) * dtc[..., None]
    bc = bm.reshape(b, c, L, G, N)
    cc = cm.reshape(b, c, L, G, N)
    a_cum = jnp.cumsum(dtc * a, axis=2)
    causal = jnp.tril(jnp.ones((L, L), dtype=bool))
    seg = a_cum[:, :, :, None] - a_cum[:, :, None, :]
    decay_ls = jnp.exp(jnp.where(causal[:, :, None, None], seg, -jnp.inf))
    cb = jnp.einsum("bclgn,bcsgn->bclsg", cc, bc)
    y_diag = jnp.einsum("bclsgr,bcsgrp->bclgrp", cb[..., None] * decay_ls, xdt)
    decay_to_end = jnp.exp(a_cum[:, :, -1:] - a_cum)
    chunk_states = jnp.einsum("bclgn,bclgrp->bcgrpn", bc, xdt * decay_to_end[..., None])
    chunk_decay = jnp.exp(a_cum[:, :, -1])

    def step(h, inp):
        st, dec = inp
        return h * dec[..., None, None] + st, h

    h0 = jnp.zeros((b, G, R, P, N), jnp.float32)
    _, prev = lax.scan(step, h0, (jnp.moveaxis(chunk_states, 1, 0), jnp.moveaxis(chunk_decay, 1, 0)))
    prev = jnp.moveaxis(prev, 0, 1)
    y_off = jnp.einsum("bclgn,bcgrpn->bclgrp", cc, prev) * jnp.exp(a_cum)[..., None]
    return (y_diag + y_off).reshape(b, s, G, R, P)


def mamba2_mixer(x, w_in, conv_w, conv_b, dt_bias, a_log, d_skip, norm_w, w_out):
    b, s, _ = x.shape
    G, R, P, N = SSM_N_GROUPS, SSM_HEADS_PER_GROUP, SSM_HEADDIM, SSM_D_STATE
    zxbcdt = x @ w_in
    z = zxbcdt[..., :SSM_D_INNER]
    xbc = zxbcdt[..., SSM_D_INNER:SSM_D_INNER + SSM_CONV_DIM]
    dt_raw = zxbcdt[..., SSM_D_INNER + SSM_CONV_DIM:]
    xbc = jax.nn.silu(causal_depthwise_conv(xbc, conv_w, conv_b))
    xs = xbc[..., :SSM_D_INNER]
    bm = xbc[..., SSM_D_INNER:SSM_D_INNER + G * N].reshape(b, s, G, N).astype(jnp.float32)
    cm = xbc[..., SSM_D_INNER + G * N:].reshape(b, s, G, N).astype(jnp.float32)
    dt = jax.nn.softplus(dt_raw.astype(jnp.float32) + dt_bias.astype(jnp.float32)).reshape(b, s, G, R)
    a = -jnp.exp(a_log.astype(jnp.float32)).reshape(G, R)
    xh = xs.reshape(b, s, G, R, P).astype(jnp.float32)
    y = ssd_chunked(xh, dt, a, bm, cm) + d_skip.astype(jnp.float32).reshape(G, R)[..., None] * xh
    yg = (y.reshape(b, s, SSM_D_INNER) * jax.nn.silu(z.astype(jnp.float32))).reshape(b, s, G, SSM_D_INNER // G)
    yg = yg * lax.rsqrt(jnp.mean(jnp.square(yg), axis=-1, keepdims=True) + NORM_EPS)
    yg = yg.reshape(b, s, SSM_D_INNER) * norm_w.astype(jnp.float32)
    return yg.astype(x.dtype) @ w_out


def rope_tables(positions):
    inv = ROPE_THETA ** (-jnp.arange(0, ROT_DIM, 2, dtype=jnp.float32) / ROT_DIM)
    ang = positions.astype(jnp.float32)[..., None] * inv
    return jnp.cos(ang), jnp.sin(ang)


def apply_partial_rope(t, cos, sin):
    half = ROT_DIM // 2
    tf = t[..., :ROT_DIM].astype(jnp.float32)
    r1, r2 = tf[..., :half], tf[..., half:]
    c = cos[:, :, None, None, :]
    s = sin[:, :, None, None, :]
    rot = jnp.concatenate([r1 * c - r2 * s, r2 * c + r1 * s], axis=-1).astype(t.dtype)
    return jnp.concatenate([rot, t[..., ROT_DIM:]], axis=-1)


def diff_attention(x, cos, sin, w_qkv, lam_q1, lam_k1, lam_q2, lam_k2, subln_w, w_o, lambda_init):
    b, s, _ = x.shape
    H, Dh = ATTN_N_HEADS, ATTN_HEAD_DIM
    qkv = x @ w_qkv
    q = qkv[..., :D_MODEL].reshape(b, s, H, 2, Dh)
    k = qkv[..., D_MODEL:2 * D_MODEL].reshape(b, s, H, 2, Dh)
    v = qkv[..., 2 * D_MODEL:].reshape(b, s, H, ATTN_V_DIM)
    q = apply_partial_rope(q, cos, sin)
    k = apply_partial_rope(k, cos, sin)
    scale = Dh ** -0.5
    lam = (jnp.exp(jnp.sum(lam_q1.astype(jnp.float32) * lam_k1.astype(jnp.float32)))
           - jnp.exp(jnp.sum(lam_q2.astype(jnp.float32) * lam_k2.astype(jnp.float32))) + lambda_init)
    outs = []
    for i in range(s // Q_BLOCK):
        kv_len = (i + 1) * Q_BLOCK
        qb = q[:, i * Q_BLOCK:kv_len]
        kb = k[:, :kv_len]
        vb = v[:, :kv_len]
        sc = jnp.einsum("bqhcd,bkhcd->bhcqk", qb, kb).astype(jnp.float32) * scale
        q_pos = i * Q_BLOCK + jnp.arange(Q_BLOCK)
        mask = jnp.arange(kv_len)[None, :] <= q_pos[:, None]
        p = jax.nn.softmax(jnp.where(mask, sc, -jnp.inf), axis=-1)
        attn = p[:, :, 0] - lam * p[:, :, 1]
        outs.append(jnp.einsum("bhqk,bkhe->bqhe", attn.astype(v.dtype), vb))
    o = jnp.concatenate(outs, axis=1).astype(jnp.float32)
    o = o * lax.rsqrt(jnp.mean(jnp.square(o), axis=-1, keepdims=True) + NORM_EPS)
    o = o * subln_w.astype(jnp.float32) * (1.0 - lambda_init)
    return o.reshape(b, s, H * ATTN_V_DIM).astype(x.dtype) @ w_o


def hier_moe(x, w_group, w_expert, w_gate, w_up, w_down):
    b, s, d = x.shape
    T = b * s
    A = T * MOE_TOP_K
    xf = x.reshape(T, d)
    g_prob = jax.nn.softmax((xf @ w_group).astype(jnp.float32), axis=-1)
    g_sel = jnp.argmax(g_prob, axis=-1)
    g_gate = jnp.take_along_axis(g_prob, g_sel[:, None], axis=-1)[:, 0]
    e_logits = (xf @ w_expert).astype(jnp.float32).reshape(T, MOE_GROUPS, MOE_EXPERTS_PER_GROUP)
    e_logits = jnp.take_along_axis(e_logits, g_sel[:, None, None], axis=1)[:, 0]
    top_p, top_i = lax.top_k(jax.nn.softmax(e_logits, axis=-1), MOE_TOP_K)
    top_p = top_p / jnp.sum(top_p, axis=-1, keepdims=True)
    weight = (g_gate[:, None] * top_p).reshape(A)
    expert_id = (g_sel[:, None] * MOE_EXPERTS_PER_GROUP + top_i).reshape(A).astype(jnp.int32)
    token_id = jnp.repeat(jnp.arange(T, dtype=jnp.int32), MOE_TOP_K)
    order = jnp.argsort(expert_id)
    sorted_e = expert_id[order]
    counts = jax.ops.segment_sum(jnp.ones((A,), jnp.int32), expert_id, num_segments=MOE_N_EXPERTS)
    padded = ((counts + MOE_BLOCK - 1) // MOE_BLOCK) * MOE_BLOCK
    starts = jnp.cumsum(counts) - counts
    pends = jnp.cumsum(padded)
    pstarts = pends - padded
    dest = pstarts[sorted_e] + (jnp.arange(A, dtype=jnp.int32) - starts[sorted_e])
    n_blocks = -(-A // MOE_BLOCK) + MOE_N_EXPERTS
    P = n_blocks * MOE_BLOCK
    row_tok = jnp.full((P,), T, jnp.int32).at[dest].set(token_id[order])
    row_w = jnp.zeros((P,), x.dtype).at[dest].set(weight[order].astype(x.dtype))
    block_expert = jnp.minimum(jnp.searchsorted(pends, jnp.arange(n_blocks, dtype=jnp.int32) * MOE_BLOCK,
                                                side="right"), MOE_N_EXPERTS - 1)
    xpad = jnp.concatenate([xf, jnp.zeros((1, d), x.dtype)], axis=0)
    rows = xpad[row_tok].reshape(n_blocks, MOE_BLOCK, d)

    def expert_block(args):
        xb, e = args
        h = jax.nn.silu(xb @ w_gate[e]) * (xb @ w_up[e])
        return h @ w_down[e]

    yb = lax.map(expert_block, (rows, block_expert)).reshape(P, d)
    out = jax.ops.segment_sum(yb * row_w[:, None], row_tok, num_segments=T + 1)[:T]
    return out.reshape(b, s, d)


def setup_inputs(seed: int = 0) -> dict:
    key = jax.random.key(seed)
    ks = jax.random.split(key, 32)
    f32 = jnp.float32
    nrm = lambda k, shape, sc: jax.random.normal(k, shape, f32) * sc
    x = jax.random.normal(ks[0], (BATCH, SEQ, D_MODEL), f32)
    start = jax.random.randint(ks[1], (BATCH, 1), 0, 1024, dtype=jnp.int32)
    positions = (start + jnp.arange(SEQ, dtype=jnp.int32)[None, :]).astype(jnp.int32)
    ln_mix_g = 1.0 + nrm(ks[2], (DEPTH, D_MODEL), 0.02)
    ln_mix_b = nrm(ks[3], (DEPTH, D_MODEL), 0.02)
    ln_ffn_g = 1.0 + nrm(ks[4], (DEPTH, D_MODEL), 0.02)
    ln_ffn_b = nrm(ks[5], (DEPTH, D_MODEL), 0.02)
    ns = N_SSM_LAYERS
    ssm_w_in = nrm(ks[6], (ns, D_MODEL, SSM_IN_DIM), D_MODEL ** -0.5)
    ssm_conv_w = nrm(ks[7], (ns, SSM_CONV_K, SSM_CONV_DIM), SSM_CONV_K ** -0.5)
    ssm_conv_b = nrm(ks[8], (ns, SSM_CONV_DIM), 0.02)
    u = jax.random.uniform(ks[9], (ns, SSM_N_HEADS), f32)
    dt0 = jnp.exp(u * (math.log(0.1) - math.log(0.001)) + math.log(0.001))
    ssm_dt_bias = dt0 + jnp.log(-jnp.expm1(-dt0))
    ssm_a_log = jnp.log(jax.random.uniform(ks[10], (ns, SSM_N_HEADS), f32, 1.0, 16.0))
    ssm_d = 1.0 + nrm(ks[11], (ns, SSM_N_HEADS), 0.02)
    ssm_norm_w = 1.0 + nrm(ks[12], (ns, SSM_D_INNER), 0.02)
    ssm_w_out = nrm(ks[13], (ns, SSM_D_INNER, D_MODEL), SSM_D_INNER ** -0.5 * DEEPNORM_BETA)
    na = N_ATTN_LAYERS
    attn_w_qkv = nrm(ks[14], (na, D_MODEL, ATTN_QKV_DIM), D_MODEL ** -0.5)
    attn_lam_q1 = nrm(ks[15], (na, ATTN_HEAD_DIM), 0.1)
    attn_lam_k1 = nrm(ks[16], (na, ATTN_HEAD_DIM), 0.1)
    attn_lam_q2 = nrm(ks[17], (na, ATTN_HEAD_DIM), 0.1)
    attn_lam_k2 = nrm(ks[18], (na, ATTN_HEAD_DIM), 0.1)
    attn_subln_w = 1.0 + nrm(ks[19], (na, ATTN_V_DIM), 0.02)
    attn_w_o = nrm(ks[20], (na, ATTN_N_HEADS * ATTN_V_DIM, D_MODEL), D_MODEL ** -0.5 * DEEPNORM_BETA)
    moe_w_group = nrm(ks[21], (DEPTH, D_MODEL, MOE_GROUPS), D_MODEL ** -0.5)
    moe_w_expert = nrm(ks[22], (DEPTH, D_MODEL, MOE_N_EXPERTS), D_MODEL ** -0.5)
    moe_w_gate = nrm(ks[23], (DEPTH, MOE_N_EXPERTS, D_MODEL, MOE_D_FF), D_MODEL ** -0.5)
    moe_w_up = nrm(ks[24], (DEPTH, MOE_N_EXPERTS, D_MODEL, MOE_D_FF), D_MODEL ** -0.5)
    moe_w_down = nrm(ks[25], (DEPTH, MOE_N_EXPERTS, MOE_D_FF, D_MODEL), MOE_D_FF ** -0.5 * DEEPNORM_BETA)
    return {"x": x, "positions": positions,
            "ln_mix_g": ln_mix_g, "ln_mix_b": ln_mix_b, "ln_ffn_g": ln_ffn_g, "ln_ffn_b": ln_ffn_b,
            "ssm_w_in": ssm_w_in, "ssm_conv_w": ssm_conv_w, "ssm_conv_b": ssm_conv_b,
            "ssm_dt_bias": ssm_dt_bias, "ssm_a_log": ssm_a_log, "ssm_d": ssm_d,
            "ssm_norm_w": ssm_norm_w, "ssm_w_out": ssm_w_out,
            "attn_w_qkv": attn_w_qkv, "attn_lam_q1": attn_lam_q1, "attn_lam_k1": attn_lam_k1,
            "attn_lam_q2": attn_lam_q2, "attn_lam_k2": attn_lam_k2, "attn_subln_w": attn_subln_w,
            "attn_w_o": attn_w_o,
            "moe_w_group": moe_w_group, "moe_w_expert": moe_w_expert,
            "moe_w_gate": moe_w_gate, "moe_w_up": moe_w_up, "moe_w_down": moe_w_down}


def reference(x, positions, ln_mix_g, ln_mix_b, ln_ffn_g, ln_ffn_b,
              ssm_w_in, ssm_conv_w, ssm_conv_b, ssm_dt_bias, ssm_a_log, ssm_d, ssm_norm_w, ssm_w_out,
              attn_w_qkv, attn_lam_q1, attn_lam_k1, attn_lam_q2, attn_lam_k2, attn_subln_w, attn_w_o,
              moe_w_group, moe_w_expert, moe_w_gate, moe_w_up, moe_w_down):
    cos, sin = rope_tables(positions)
    h = x
    for layer in range(DEPTH):
        j = layer // N_MIXERS
        if layer % N_MIXERS == 0:
            mix = mamba2_mixer(h, ssm_w_in[j], ssm_conv_w[j], ssm_conv_b[j], ssm_dt_bias[j],
                               ssm_a_log[j], ssm_d[j], ssm_norm_w[j], ssm_w_out[j])
        else:
            lambda_init = 0.8 - 0.6 * math.exp(-0.3 * layer)
            mix = diff_attention(h, cos, sin, attn_w_qkv[j], attn_lam_q1[j], attn_lam_k1[j],
                                 attn_lam_q2[j], attn_lam_k2[j], attn_subln_w[j], attn_w_o[j], lambda_init)
        h = layer_norm(DEEPNORM_ALPHA * h + mix, ln_mix_g[layer], ln_mix_b[layer])
        ffn = hier_moe(h, moe_w_group[layer], moe_w_expert[layer], moe_w_gate[layer],
                       moe_w_up[layer], moe_w_down[layer])
        h = layer_norm(DEEPNORM_ALPHA * h + ffn, ln_ffn_g[layer], ln_ffn_b[layer])
    return h
```

```python
import functools
import math

import jax
import jax.numpy as jnp
from jax import lax
from jax.experimental import pallas as pl
from jax.experimental.pallas import tpu as pltpu

F32 = jnp.float32
BF16 = jnp.bfloat16
I32 = jnp.int32

D_MODEL = 1024
DEPTH = 2
SSM_D_INNER = 2048
SSM_HEADDIM = 64
SSM_N_HEADS = 32
SSM_N_GROUPS = 8
SSM_HEADS_PER_GROUP = 4
SSM_D_STATE = 128
SSM_CONV_K = 4
SSM_CHUNK = 128
SSM_GROUP_WIDTH = SSM_HEADS_PER_GROUP * SSM_HEADDIM
SSM_ZX_DIM = 2 * SSM_D_INNER + 2 * SSM_N_GROUPS * SSM_D_STATE
ATTN_HEAD_DIM = 64
ATTN_N_HEADS = 8
ATTN_V_DIM = 128
ROT_DIM = 16
ROPE_THETA = 500000.0
MOE_GROUPS = 4
MOE_EXPERTS_PER_GROUP = 8
MOE_N_EXPERTS = 32
MOE_TOP_K = 2
MOE_D_FF = 512
DEEPNORM_ALPHA = (2 * DEPTH) ** 0.25
NORM_EPS = 1e-5

LANES = 128
CONV_TAIL = 8

MM_TM = 512
MM_TN = 1024
FFN_BLK = 256
CMB_TM = 256
ATT_TQ = 256


def _arb(n):
    return pltpu.CompilerParams(dimension_semantics=("arbitrary",) * n,
                                vmem_limit_bytes=56 * 1024 * 1024)


def _sigmoid(x):
    return 1.0 / (1.0 + jnp.exp(-x))


def _silu(x):
    return x * _sigmoid(x)


def _softplus(x):
    return jnp.maximum(x, 0.0) + jnp.log(1.0 + jnp.exp(-jnp.abs(x)))


def _layer_norm(y, g, b):
    mu = jnp.mean(y, axis=-1, keepdims=True)
    d = y - mu
    var = jnp.mean(d * d, axis=-1, keepdims=True)
    return d * lax.rsqrt(var + NORM_EPS) * g + b


def _split3(a):
    a1 = a.astype(BF16)
    r1 = a - a1.astype(F32)
    a2 = r1.astype(BF16)
    a3 = (r1 - a2.astype(F32)).astype(BF16)
    return a1, a2, a3


def _dot(a, b):
    return jnp.dot(a, b, preferred_element_type=F32)


def _dot_sel(a, sel):
    a1, a2, a3 = _split3(a)
    return _dot(a1, sel) + _dot(a2, sel) + _dot(a3, sel)


def _dot_f32(a, b):
    a1, a2, a3 = _split3(a)
    b1, b2, b3 = _split3(b)
    return (_dot(a1, b1) + _dot(a1, b2) + _dot(a2, b1)
            + _dot(a2, b2) + _dot(a1, b3) + _dot(a3, b1))


def _mm_kernel(x_ref, w_ref, o_ref, wb_ref):
    @pl.when(pl.program_id(1) == 0)
    def _():
        wb_ref[...] = w_ref[...].astype(BF16)

    o_ref[...] = _dot(x_ref[...].astype(BF16), wb_ref[...]).astype(o_ref.dtype)


def _matmul(x, w, n_cols, out_dtype):
    m, k = x.shape
    tm = min(MM_TM, m)
    return pl.pallas_call(
        _mm_kernel,
        grid=(n_cols // MM_TN, m // tm),
        in_specs=[pl.BlockSpec((tm, k), lambda j, i: (i, 0)),
                  pl.BlockSpec((k, MM_TN), lambda j, i: (0, j))],
        out_specs=pl.BlockSpec((tm, MM_TN), lambda j, i: (i, j)),
        out_shape=jax.ShapeDtypeStruct((m, n_cols), out_dtype),
        scratch_shapes=[pltpu.VMEM((k, MM_TN), BF16)],
        compiler_params=_arb(2),
        name="mm_inproj",
    )(x, w)


def _ssd_kernel(z_ref, xs_ref, bc_ref, x_ref, wdt_ref, cw_ref, cb_ref, dtb_ref, alog_ref, dskip_ref,
                nw_ref, e64_ref, e128_ref, o_ref,
                convbuf, xc, bcs, ccs, state, ybuf, dt_s, ac_s, act_s):
    G, W, N = SSM_N_GROUPS, SSM_GROUP_WIDTH, SSM_D_STATE
    L = z_ref.shape[0]
    c = pl.program_id(1)

    @pl.when(c == 0)
    def _():
        convbuf[0:CONV_TAIL, :] = jnp.zeros((CONV_TAIL, convbuf.shape[1]), F32)
        state[...] = jnp.zeros(state.shape, F32)

    convbuf[CONV_TAIL:CONV_TAIL + L, 0:SSM_D_INNER] = xs_ref[...]
    convbuf[CONV_TAIL:CONV_TAIL + L, SSM_D_INNER:] = bc_ref[...]

    def conv(c0, w):
        acc = cb_ref[:, c0:c0 + w]
        for k in range(SSM_CONV_K):
            r0 = CONV_TAIL - (SSM_CONV_K - 1) + k
            acc = acc + cw_ref[k:k + 1, c0:c0 + w] * convbuf[r0:r0 + L, c0:c0 + w]
        return _silu(acc)

    for g in range(G):
        xc[g] = conv(g * W, W)
        bcs[g] = conv(SSM_D_INNER + g * N, N).astype(BF16)
        ccs[g] = conv(SSM_D_INNER + G * N + g * N, N).astype(BF16)
    convbuf[0:CONV_TAIL, :] = convbuf[L:L + CONV_TAIL, :]

    dt = _softplus(_dot(x_ref[...].astype(BF16), wdt_ref[...].astype(BF16)) + dtb_ref[...])
    a = -jnp.exp(alog_ref[...])
    row_i = lax.broadcasted_iota(I32, (L, L), 0)
    col_i = lax.broadcasted_iota(I32, (L, L), 1)
    causal = row_i >= col_i
    a1, a2, a3 = _split3(dt * a)
    tri = causal.astype(BF16)
    ac = _dot(tri, a1) + _dot(tri, a2) + _dot(tri, a3)
    dt_s[...] = dt
    ac_s[...] = ac
    act_s[...] = ac.T
    lane = lax.broadcasted_iota(I32, (L, LANES), 1)
    dsk8 = jnp.broadcast_to(dskip_ref[...], (8, LANES))

    def group_body(g, carry):
        xg = xc[g]
        bg = bcs[g]
        cg = ccs[g]
        e64 = e64_ref[g]
        a_e = _dot_sel(ac_s[...], e64)
        dt_e = _dot_sel(dt_s[...], e64)
        dsk_e = _dot_sel(dsk8, e64)[0:1, :]
        al_e = a_e[L - 1:L, :]
        xdt = xg * dt_e
        cb = lax.dot_general(cg, bg, (((1,), (1,)), ((), ())), preferred_element_type=F32)
        col4 = _dot_sel(ac_s[...], e128_ref[g])
        xdt_b = xdt.astype(BF16)
        halves = []
        for p in range(2):
            xp = xdt_b[:, LANES * p:LANES * (p + 1)]
            ys = []
            for q in range(2):
                r = 2 * p + q
                row = act_s[pl.ds(SSM_HEADS_PER_GROUP * g + r, 1), :]
                seg = col4[:, LANES * r:LANES * (r + 1)] - row
                dec = jnp.where(causal, jnp.exp(seg), 0.0)
                ys.append(_dot((cb * dec).astype(BF16), xp))
            halves.append(jnp.where(lane < SSM_HEADDIM, ys[0], ys[1]))
        y_diag = jnp.concatenate(halves, axis=1)
        s_prev = state[g]
        y_off = _dot(cg, s_prev.astype(BF16)) * jnp.exp(a_e)
        new = lax.dot_general(bg, (xdt * jnp.exp(al_e - a_e)).astype(BF16),
                              (((0,), (0,)), ((), ())), preferred_element_type=F32)
        state[g] = s_prev * jnp.exp(al_e) + new
        ybuf[g] = y_diag + y_off + xg * dsk_e
        return carry

    lax.fori_loop(0, G, group_body, 0)

    for g in range(G):
        yg = ybuf[g] * _silu(z_ref[:, g * W:(g + 1) * W])
        ms = jnp.mean(yg * yg, axis=-1, keepdims=True)
        o_ref[:, g * W:(g + 1) * W] = (yg * lax.rsqrt(ms + NORM_EPS)
                                       * nw_ref[:, g * W:(g + 1) * W]).astype(o_ref.dtype)


def _head_selectors():
    G, R = SSM_N_GROUPS, SSM_HEADS_PER_GROUP
    h = jnp.arange(LANES)[None, :, None]
    g = jnp.arange(G)[:, None, None]
    j64 = jnp.arange(SSM_GROUP_WIDTH)[None, None, :]
    j128 = jnp.arange(R * LANES)[None, None, :]
    e64 = (h == g * R + j64 // SSM_HEADDIM).astype(BF16)
    e128 = (h == g * R + j128 // LANES).astype(BF16)
    return e64, e128


def _ssd(zx, x2, w_dt, conv_w, conv_b, dt_bias, a_log, d_skip, norm_w, batch, seq):
    L, G, W, N = SSM_CHUNK, SSM_N_GROUPS, SSM_GROUP_WIDTH, SSM_D_STATE
    nc = seq // L
    t = batch * seq
    pad = LANES - SSM_N_HEADS
    wdt = jnp.pad(w_dt, ((0, 0), (0, pad)))
    dtb = jnp.pad(dt_bias, (0, pad))[None, :]
    alog = jnp.pad(a_log, (0, pad))[None, :]
    dsk = jnp.pad(d_skip, (0, pad))[None, :]
    e64, e128 = _head_selectors()
    row = lambda b, c: (b * nc + c, 0)
    full2 = lambda b, c: (0, 0)
    full3 = lambda b, c: (0, 0, 0)
    conv_dim = conv_w.shape[1]
    return pl.pallas_call(
        _ssd_kernel,
        grid=(batch, nc),
        in_specs=[pl.BlockSpec((L, SSM_D_INNER), row),
                  pl.BlockSpec((L, SSM_D_INNER), lambda b, c: (b * nc + c, 1)),
                  pl.BlockSpec((L, SSM_D_INNER), lambda b, c: (b * nc + c, 2)),
                  pl.BlockSpec((L, D_MODEL), row),
                  pl.BlockSpec((D_MODEL, LANES), full2),
                  pl.BlockSpec((SSM_CONV_K, conv_dim), full2),
                  pl.BlockSpec((1, conv_dim), full2),
                  pl.BlockSpec((1, LANES), full2),
                  pl.BlockSpec((1, LANES), full2),
                  pl.BlockSpec((1, LANES), full2),
                  pl.BlockSpec((1, SSM_D_INNER), full2),
                  pl.BlockSpec((G, LANES, W), full3),
                  pl.BlockSpec((G, LANES, SSM_HEADS_PER_GROUP * LANES), full3)],
        out_specs=pl.BlockSpec((L, SSM_D_INNER), row),
        out_shape=jax.ShapeDtypeStruct((t, SSM_D_INNER), BF16),
        scratch_shapes=[pltpu.VMEM((CONV_TAIL + L, conv_dim), F32),
                        pltpu.VMEM((G, L, W), F32),
                        pltpu.VMEM((G, L, N), BF16),
                        pltpu.VMEM((G, L, N), BF16),
                        pltpu.VMEM((G, N, W), F32),
                        pltpu.VMEM((G, L, W), F32),
                        pltpu.VMEM((L, LANES), F32),
                        pltpu.VMEM((L, LANES), F32),
                        pltpu.VMEM((LANES, L), F32)],
        compiler_params=_arb(2),
        name="ssd_scan",
    )(zx, zx, zx, x2, wdt, conv_w, conv_b[None, :], dtb, alog, dsk, norm_w[None, :], e64, e128)


def _route(h, wr):
    tm = h.shape[0]
    logits = _dot_f32(h, wr)
    lane_i = lax.broadcasted_iota(I32, (tm, LANES), 1)
    lane = lane_i.astype(F32)
    neg = -jnp.inf
    big = float(LANES)

    def first_argmax(v, vmax):
        return jnp.min(jnp.where(v == vmax, lane, big), axis=-1, keepdims=True)

    gl = jnp.where((lane_i >= MOE_N_EXPERTS) & (lane_i < MOE_N_EXPERTS + MOE_GROUPS), logits, neg)
    gm = jnp.max(gl, axis=-1, keepdims=True)
    g_sel = first_argmax(gl, gm) - float(MOE_N_EXPERTS)
    g_gate = 1.0 / jnp.sum(jnp.exp(gl - gm), axis=-1, keepdims=True)
    lo = g_sel * float(MOE_EXPERTS_PER_GROUP)
    el = jnp.where((lane >= lo) & (lane < lo + float(MOE_EXPERTS_PER_GROUP)), logits, neg)
    m1 = jnp.max(el, axis=-1, keepdims=True)
    i1 = first_argmax(el, m1)
    el2 = jnp.where(lane == i1, neg, el)
    m2 = jnp.max(el2, axis=-1, keepdims=True)
    i2 = first_argmax(el2, m2)
    p2 = jnp.exp(m2 - m1)
    t1 = 1.0 / (1.0 + p2)
    t2 = p2 / (1.0 + p2)
    eid = jnp.where(lane_i == 0, i1, jnp.where(lane_i == 1, i2, 0.0)).astype(I32)
    wt = jnp.where(lane_i == 0, g_gate * t1, jnp.where(lane_i == 1, g_gate * t2, 0.0))
    return eid, wt


def _router_weights(w_group, w_expert):
    pad = LANES - MOE_N_EXPERTS - MOE_GROUPS
    return jnp.pad(jnp.concatenate([w_expert, w_group], axis=1), ((0, 0), (0, pad)))


def _mm_ln_router_kernel(x_ref, w_ref, r_ref, g_ref, b_ref, wr_ref, h_ref, eid_ref, wt_ref, wb_ref):
    @pl.when(pl.program_id(0) == 0)
    def _():
        wb_ref[...] = w_ref[...].astype(BF16)

    y = DEEPNORM_ALPHA * r_ref[...] + _dot(x_ref[...].astype(BF16), wb_ref[...])
    h = _layer_norm(y, g_ref[...], b_ref[...])
    h_ref[...] = h
    eid, wt = _route(h, wr_ref[...])
    eid_ref[...] = eid
    wt_ref[...] = wt


def _mm_ln_router(x, w, resid, g, b, wr, name):
    m, k = x.shape
    d = w.shape[1]
    tm = min(MM_TM, m)
    row = lambda i: (i, 0)
    full = lambda i: (0, 0)
    return pl.pallas_call(
        _mm_ln_router_kernel,
        grid=(m // tm,),
        in_specs=[pl.BlockSpec((tm, k), row), pl.BlockSpec((k, d), full), pl.BlockSpec((tm, d), row),
                  pl.BlockSpec((1, d), full), pl.BlockSpec((1, d), full), pl.BlockSpec((d, LANES), full)],
        out_specs=[pl.BlockSpec((tm, d), row), pl.BlockSpec((tm, LANES), row),
                   pl.BlockSpec((tm, LANES), row)],
        out_shape=[jax.ShapeDtypeStruct((m, d), F32), jax.ShapeDtypeStruct((m, LANES), I32),
                   jax.ShapeDtypeStruct((m, LANES), F32)],
        scratch_shapes=[pltpu.VMEM((k, d), BF16)],
        compiler_params=_arb(1),
        name=name,
    )(x, w, resid, g[None, :], b[None, :], wr)


def _dispatch_plan(eid, blk):
    t = eid.shape[0]
    a = t * MOE_TOP_K
    e = eid.reshape(a)
    onehot = (e[:, None] == jnp.arange(MOE_N_EXPERTS, dtype=I32)[None, :]).astype(I32)
    csum = jnp.cumsum(onehot, axis=0)
    rank = jnp.take_along_axis(csum, e[:, None], axis=1)[:, 0] - 1
    counts = csum[-1]
    padded = ((counts + blk - 1) // blk) * blk
    pends = jnp.cumsum(padded)
    pstarts = pends - padded
    dest = (pstarts[e] + rank).astype(I32)
    n_blocks = -(-a // blk) + MOE_N_EXPERTS
    row_tok = jnp.zeros((n_blocks * blk,), I32).at[dest].set(jnp.arange(a, dtype=I32) // MOE_TOP_K)
    block_expert = jnp.minimum(
        jnp.searchsorted(pends, jnp.arange(n_blocks, dtype=I32) * blk, side="right"),
        MOE_N_EXPERTS - 1).astype(I32)
    n_used = (pends[-1] // blk).astype(I32).reshape(1)
    return dest, row_tok, block_expert, n_used, n_blocks


def _ffn_kernel(be_ref, nb_ref, tok_ref, tokn_ref, x_hbm, wg_ref, wu_ref, wd_ref, o_ref,
                xbuf, wgb, wub, wdb, sem):
    i = pl.program_id(0)
    nb = nb_ref[0]
    blk = xbuf.shape[1]

    def row_copy(tref, r, slot):
        return pltpu.make_async_copy(x_hbm.at[pl.ds(tref[0, 0, r], 1)],
                                     xbuf.at[slot, pl.ds(r, 1)], sem.at[slot])

    def start_gather(tref, slot):
        def body(r, carry):
            row_copy(tref, r, slot).start()
            return carry
        lax.fori_loop(0, blk, body, 0, unroll=8)

    @pl.when(i == 0)
    def _():
        start_gather(tok_ref, 0)

    @pl.when(i + 1 < nb)
    def _():
        start_gather(tokn_ref, (i + 1) % 2)

    @pl.when(i < nb)
    def _():
        prev = be_ref[jnp.maximum(i - 1, 0)]

        @pl.when((i == 0) | (be_ref[i] != prev))
        def _():
            wgb[...] = wg_ref[0].astype(BF16)
            wub[...] = wu_ref[0].astype(BF16)
            wdb[...] = wd_ref[0].astype(BF16)

        slot = i % 2
        pltpu.make_async_copy(x_hbm.at[pl.ds(0, blk)], xbuf.at[slot], sem.at[slot]).wait()
        xb = xbuf[slot].astype(BF16)
        hid = _silu(_dot(xb, wgb[...])) * _dot(xb, wub[...])
        o_ref[...] = _dot(hid.astype(BF16), wdb[...])

    @pl.when(i >= nb)
    def _():
        o_ref[...] = jnp.zeros(o_ref.shape, o_ref.dtype)


def _moe_ffn(h, row_tok, block_expert, n_used, n_blocks, w_gate, w_up, w_down, name):
    t, d = h.shape
    f = w_gate.shape[2]
    blk = FFN_BLK
    tok3 = row_tok.reshape(n_blocks, 1, blk)
    grid_spec = pltpu.PrefetchScalarGridSpec(
        num_scalar_prefetch=2,
        grid=(n_blocks,),
        in_specs=[pl.BlockSpec((1, 1, blk), lambda i, be, nb: (i, 0, 0), memory_space=pltpu.SMEM),
                  pl.BlockSpec((1, 1, blk), lambda i, be, nb: (jnp.minimum(i + 1, n_blocks - 1), 0, 0),
                               memory_space=pltpu.SMEM),
                  pl.BlockSpec(memory_space=pl.ANY),
                  pl.BlockSpec((1, d, f), lambda i, be, nb: (be[i], 0, 0)),
                  pl.BlockSpec((1, d, f), lambda i, be, nb: (be[i], 0, 0)),
                  pl.BlockSpec((1, f, d), lambda i, be, nb: (be[i], 0, 0))],
        out_specs=pl.BlockSpec((blk, d), lambda i, be, nb: (i, 0)),
        scratch_shapes=[pltpu.VMEM((2, blk, d), F32),
                        pltpu.VMEM((d, f), BF16), pltpu.VMEM((d, f), BF16), pltpu.VMEM((f, d), BF16),
                        pltpu.SemaphoreType.DMA((2,))])
    return pl.pallas_call(
        _ffn_kernel,
        grid_spec=grid_spec,
        out_shape=jax.ShapeDtypeStruct((n_blocks * blk, d), F32),
        compiler_params=_arb(1),
        name=name,
    )(block_expert, n_used, tok3, tok3, h, w_gate, w_up, w_down)


def _combine_ln_kernel(dst_ref, dstn_ref, h_ref, wt_ref, g_ref, b_ref, yb_hbm, o_ref, ybuf, sem):
    i = pl.program_id(0)
    n = pl.num_programs(0)
    tm = h_ref.shape[0]

    def start_gather(dref, slot):
        def body(r, carry):
            for k in range(MOE_TOP_K):
                pltpu.make_async_copy(yb_hbm.at[pl.ds(dref[0, k, r], 1)],
                                      ybuf.at[slot, k, pl.ds(r, 1)], sem.at[slot]).start()
            return carry
        lax.fori_loop(0, tm, body, 0, unroll=4)

    @pl.when(i == 0)
    def _():
        start_gather(dst_ref, 0)

    @pl.when(i + 1 < n)
    def _():
        start_gather(dstn_ref, (i + 1) % 2)

    slot = i % 2
    for k in range(MOE_TOP_K):
        pltpu.make_async_copy(yb_hbm.at[pl.ds(0, tm)], ybuf.at[slot, k], sem.at[slot]).wait()
    wt = wt_ref[...]
    ffn = wt[:, 0:1] * ybuf[slot, 0] + wt[:, 1:2] * ybuf[slot, 1]
    o_ref[...] = _layer_norm(DEEPNORM_ALPHA * h_ref[...] + ffn, g_ref[...], b_ref[...])


def _combine_ln(h, wt, dest, yb, g, b, name):
    t, d = h.shape
    tm = min(CMB_TM, t)
    nblk = t // tm
    dst3 = dest.reshape(nblk, tm, MOE_TOP_K).transpose(0, 2, 1)
    row = lambda i: (i, 0)
    full = lambda i: (0, 0)
    return pl.pallas_call(
        _combine_ln_kernel,
        grid=(nblk,),
        in_specs=[pl.BlockSpec((1, MOE_TOP_K, tm), lambda i: (i, 0, 0), memory_space=pltpu.SMEM),
                  pl.BlockSpec((1, MOE_TOP_K, tm), lambda i: (jnp.minimum(i + 1, nblk - 1), 0, 0),
                               memory_space=pltpu.SMEM),
                  pl.BlockSpec((tm, d), row), pl.BlockSpec((tm, LANES), row),
                  pl.BlockSpec((1, d), full), pl.BlockSpec((1, d), full),
                  pl.BlockSpec(memory_space=pl.ANY)],
        out_specs=pl.BlockSpec((tm, d), row),
        out_shape=jax.ShapeDtypeStruct((t, d), F32),
        scratch_shapes=[pltpu.VMEM((2, MOE_TOP_K, tm, d), F32), pltpu.SemaphoreType.DMA((2,))],
        compiler_params=_arb(1),
        name=name,
    )(dst3, dst3, h, wt, g[None, :], b[None, :], yb)


def _hier_moe_ln(h, eid, wt, w_gate, w_up, w_down, g, b, layer):
    dest, row_tok, block_expert, n_used, n_blocks = _dispatch_plan(eid[:, :MOE_TOP_K], FFN_BLK)
    yb = _moe_ffn(h, row_tok, block_expert, n_used, n_blocks, w_gate, w_up, w_down, f"moe_ffn{layer}")
    return _combine_ln(h, wt, dest, yb, g, b, f"moe_combine_ln{layer}")


def _qkv_rope_kernel(x_ref, w_ref, pos_ref, inv_ref, o_ref, wb_ref):
    j = pl.program_id(0)

    @pl.when(pl.program_id(1) == 0)
    def _():
        wb_ref[...] = w_ref[...].astype(BF16)

    acc = _dot(x_ref[...].astype(BF16), wb_ref[...])
    tm, n = acc.shape

    @pl.when(j == 2)
    def _():
        o_ref[...] = acc.astype(o_ref.dtype)

    @pl.when(j < 2)
    def _():
        sc = jnp.where(j == 0, ATTN_HEAD_DIM ** -0.5, 1.0).astype(F32)
        ang = pos_ref[...].astype(F32) * inv_ref[...]
        lane = lax.broadcasted_iota(I32, (tm, LANES), 1)
        dd = lane & (ATTN_HEAD_DIM - 1)
        half = ROT_DIM // 2
        cosv = jnp.cos(ang) * sc
        sinv = jnp.sin(ang) * sc
        c_t = jnp.where(dd < ROT_DIM, cosv, sc)
        s_up = jnp.where(dd < half, -sinv, 0.0)
        s_dn = jnp.where((dd >= half) & (dd < ROT_DIM), sinv, 0.0)
        for blk in range(n // LANES):
            tt = acc[:, blk * LANES:(blk + 1) * LANES]
            out = (tt * c_t + pltpu.roll(tt, LANES - half, 1) * s_up + pltpu.roll(tt, half, 1) * s_dn)
            o_ref[:, blk * LANES:(blk + 1) * LANES] = out.astype(o_ref.dtype)


def _rope_inv_table():
    inv = ROPE_THETA ** (-jnp.arange(0, ROT_DIM, 2, dtype=F32) / ROT_DIM)
    head = jnp.concatenate([inv, inv, jnp.zeros((ATTN_HEAD_DIM - ROT_DIM,), F32)])
    return jnp.tile(head, LANES // ATTN_HEAD_DIM)[None, :]


def _qkv_rope(h, w_qkv, positions):
    m, k = h.shape
    n = w_qkv.shape[1]
    tm = min(MM_TM, m)
    pos = positions.reshape(m, 1)
    return pl.pallas_call(
        _qkv_rope_kernel,
        grid=(n // MM_TN, m // tm),
        in_specs=[pl.BlockSpec((tm, k), lambda j, i: (i, 0)),
                  pl.BlockSpec((k, MM_TN), lambda j, i: (0, j)),
                  pl.BlockSpec((tm, 1), lambda j, i: (i, 0)),
                  pl.BlockSpec((1, LANES), lambda j, i: (0, 0))],
        out_specs=pl.BlockSpec((tm, MM_TN), lambda j, i: (i, j)),
        out_shape=jax.ShapeDtypeStruct((m, n), BF16),
        scratch_shapes=[pltpu.VMEM((k, MM_TN), BF16)],
        compiler_params=_arb(2),
        name="mm_qkv_rope",
    )(h, w_qkv, pos, _rope_inv_table())


def _attn_kernel(q_ref, k_ref, v_ref, lq1_ref, lk1_ref, lq2_ref, lk2_ref, sw_ref, o_ref, *, lambda_init):
    i = pl.program_id(2)
    tq = q_ref.shape[0]
    q = q_ref[...]
    lane = lax.broadcasted_iota(I32, (tq, LANES), 1)
    zero = jnp.zeros((), q.dtype)
    qs = (jnp.where(lane < ATTN_HEAD_DIM, q, zero), jnp.where(lane >= ATTN_HEAD_DIM, q, zero))
    row_i = lax.broadcasted_iota(I32, (tq, tq), 0)
    col_i = lax.broadcasted_iota(I32, (tq, tq), 1)
    causal = row_i >= col_i

    def step(kb, vb, carry, masked):
        out = []
        for c in range(2):
            m, l, acc = carry[c]
            s = lax.dot_general(qs[c], kb, (((1,), (1,)), ((), ())), preferred_element_type=F32)
            if masked:
                s = jnp.where(causal, s, -jnp.inf)
            mn = jnp.maximum(m, jnp.max(s, axis=-1, keepdims=True))
            alpha = jnp.exp(m - mn)
            p = jnp.exp(s - mn)
            l = alpha * l + jnp.sum(p, axis=-1, keepdims=True)
            acc = alpha * acc + _dot(p.astype(BF16), vb)
            out.append((mn, l, acc))
        return tuple(out)

    def body(j, carry):
        off = pl.multiple_of(j * tq, tq)
        return step(k_ref[pl.ds(off, tq), :], v_ref[pl.ds(off, tq), :], carry, False)

    init = tuple((jnp.full((tq, 1), -jnp.inf, F32), jnp.zeros((tq, 1), F32), jnp.zeros((tq, LANES), F32))
                 for _ in range(2))
    carry = lax.fori_loop(0, i, body, init)
    off = pl.multiple_of(i * tq, tq)
    carry = step(k_ref[pl.ds(off, tq), :], v_ref[pl.ds(off, tq), :], carry, True)

    lam = (jnp.exp(jnp.sum(lq1_ref[...] * lk1_ref[...], axis=-1, keepdims=True))
           - jnp.exp(jnp.sum(lq2_ref[...] * lk2_ref[...], axis=-1, keepdims=True)) + lambda_init)
    (_, l1, a1), (_, l2, a2) = carry
    o = a1 / l1 - lam * (a2 / l2)
    o = o * lax.rsqrt(jnp.mean(o * o, axis=-1, keepdims=True) + NORM_EPS)
    o_ref[...] = (o * sw_ref[...] * (1.0 - lambda_init)).astype(o_ref.dtype)


def _diff_attention(qkv, lq1, lk1, lq2, lk2, subln_w, lambda_init, batch, seq):
    t = batch * seq
    tq = min(ATT_TQ, seq)
    nq = seq // tq
    h_n = ATTN_N_HEADS
    vec = lambda b, h, i: (0, 0)
    return pl.pallas_call(
        functools.partial(_attn_kernel, lambda_init=lambda_init),
        grid=(batch, h_n, nq),
        in_specs=[pl.BlockSpec((tq, LANES), lambda b, h, i: (b * nq + i, h)),
                  pl.BlockSpec((seq, LANES), lambda b, h, i: (b, h_n + h)),
                  pl.BlockSpec((seq, LANES), lambda b, h, i: (b, 2 * h_n + h)),
                  pl.BlockSpec((1, ATTN_HEAD_DIM), vec), pl.BlockSpec((1, ATTN_HEAD_DIM), vec),
                  pl.BlockSpec((1, ATTN_HEAD_DIM), vec), pl.BlockSpec((1, ATTN_HEAD_DIM), vec),
                  pl.BlockSpec((1, ATTN_V_DIM), vec)],
        out_specs=pl.BlockSpec((tq, LANES), lambda b, h, i: (b * nq + i, h)),
        out_shape=jax.ShapeDtypeStruct((t, h_n * ATTN_V_DIM), BF16),
        compiler_params=_arb(3),
        name="diff_attn",
    )(qkv, qkv, qkv, lq1[None, :], lk1[None, :], lq2[None, :], lk2[None, :], subln_w[None, :])


def kernel(x, positions, ln_mix_g, ln_mix_b, ln_ffn_g, ln_ffn_b, ssm_w_in, ssm_conv_w, ssm_conv_b, ssm_dt_bias, ssm_a_log, ssm_d, ssm_norm_w, ssm_w_out, attn_w_qkv, attn_lam_q1, attn_lam_k1, attn_lam_q2, attn_lam_k2, attn_subln_w, attn_w_o, moe_w_group, moe_w_expert, moe_w_gate, moe_w_up, moe_w_down):
    batch, seq, d = x.shape
    t = batch * seq
    h = x.reshape(t, d)

    zx = _matmul(h, ssm_w_in[0], SSM_ZX_DIM, F32)
    yg = _ssd(zx, h, ssm_w_in[0][:, SSM_ZX_DIM:], ssm_conv_w[0], ssm_conv_b[0], ssm_dt_bias[0],
              ssm_a_log[0], ssm_d[0], ssm_norm_w[0], batch, seq)
    h, eid, wt = _mm_ln_router(yg, ssm_w_out[0], h, ln_mix_g[0], ln_mix_b[0],
                               _router_weights(moe_w_group[0], moe_w_expert[0]), "mm_ssm_out_ln_router")
    h = _hier_moe_ln(h, eid, wt, moe_w_gate[0], moe_w_up[0], moe_w_down[0], ln_ffn_g[0], ln_ffn_b[0], 0)

    lambda_init = 0.8 - 0.6 * math.exp(-0.3 * 1)
    qkv = _qkv_rope(h, attn_w_qkv[0], positions)
    o = _diff_attention(qkv, attn_lam_q1[0], attn_lam_k1[0], attn_lam_q2[0], attn_lam_k2[0],
                        attn_subln_w[0], lambda_init, batch, seq)
    h, eid, wt = _mm_ln_router(o, attn_w_o[0], h, ln_mix_g[1], ln_mix_b[1],
                               _router_weights(moe_w_group[1], moe_w_expert[1]), "mm_attn_out_ln_router")
    h = _hier_moe_ln(h, eid, wt, moe_w_gate[1], moe_w_up[1], moe_w_down[1], ln_ffn_g[1], ln_ffn_b[1], 1)
    return h.reshape(batch, seq, d)
```

```python
import functools
import math

import jax
import jax.numpy as jnp
from jax import lax
from jax.experimental import pallas as pl
from jax.experimental.pallas import tpu as pltpu

F32 = jnp.float32
BF16 = jnp.bfloat16
I32 = jnp.int32

D_MODEL = 1024
DEPTH = 2
SSM_D_INNER = 2048
SSM_HEADDIM = 64
SSM_N_HEADS = 32
SSM_N_GROUPS = 8
SSM_HEADS_PER_GROUP = 4
SSM_D_STATE = 128
SSM_CONV_K = 4
SSM_CHUNK = 128
SSM_GROUP_WIDTH = SSM_HEADS_PER_GROUP * SSM_HEADDIM
SSM_ZX_DIM = 2 * SSM_D_INNER + 2 * SSM_N_GROUPS * SSM_D_STATE
ATTN_HEAD_DIM = 64
ATTN_N_HEADS = 8
ATTN_V_DIM = 128
ROT_DIM = 16
ROPE_THETA = 500000.0
MOE_GROUPS = 4
MOE_EXPERTS_PER_GROUP = 8
MOE_N_EXPERTS = 32
MOE_TOP_K = 2
MOE_D_FF = 512
DEEPNORM_ALPHA = (2 * DEPTH) ** 0.25
NORM_EPS = 1e-5

LANES = 128
CONV_TAIL = 8

MM_TM = 512
MM_TN = 1024
FFN_BLK = 256
DSP_TM = 512
CMB_TM = 256
ATT_TQ = 512


def _arb(n):
    return pltpu.CompilerParams(dimension_semantics=("arbitrary",) * n,
                                vmem_limit_bytes=56 * 1024 * 1024)


def _sigmoid(x):
    return 1.0 / (1.0 + jnp.exp(-x))


def _silu(x):
    return x * _sigmoid(x)


def _softplus(x):
    return jnp.maximum(x, 0.0) + jnp.log(1.0 + jnp.exp(-jnp.abs(x)))


def _layer_norm(y, g, b):
    mu = jnp.mean(y, axis=-1, keepdims=True)
    d = y - mu
    var = jnp.mean(d * d, axis=-1, keepdims=True)
    return d * lax.rsqrt(var + NORM_EPS) * g + b


def _split3(a):
    a1 = a.astype(BF16)
    r1 = a - a1.astype(F32)
    a2 = r1.astype(BF16)
    a3 = (r1 - a2.astype(F32)).astype(BF16)
    return a1, a2, a3


def _dot(a, b):
    return jnp.dot(a, b, preferred_element_type=F32)


def _dot_sel(a, sel):
    a1, a2, a3 = _split3(a)
    return _dot(a1, sel) + _dot(a2, sel) + _dot(a3, sel)


def _dot_f32(a, b):
    a1, a2, a3 = _split3(a)
    b1, b2, b3 = _split3(b)
    return (_dot(a1, b1) + _dot(a1, b2) + _dot(a2, b1)
            + _dot(a2, b2) + _dot(a1, b3) + _dot(a3, b1))


def _mm_kernel(x_ref, w_ref, o_ref, wb_ref):
    @pl.when(pl.program_id(1) == 0)
    def _():
        wb_ref[...] = w_ref[...].astype(BF16)

    o_ref[...] = _dot(x_ref[...].astype(BF16), wb_ref[...]).astype(o_ref.dtype)


def _matmul(x, w, n_cols, out_dtype):
    m, k = x.shape
    tm = min(MM_TM, m)
    return pl.pallas_call(
        _mm_kernel,
        grid=(n_cols // MM_TN, m // tm),
        in_specs=[pl.BlockSpec((tm, k), lambda j, i: (i, 0)),
                  pl.BlockSpec((k, MM_TN), lambda j, i: (0, j))],
        out_specs=pl.BlockSpec((tm, MM_TN), lambda j, i: (i, j)),
        out_shape=jax.ShapeDtypeStruct((m, n_cols), out_dtype),
        scratch_shapes=[pltpu.VMEM((k, MM_TN), BF16)],
        compiler_params=_arb(2),
        name="mm_inproj",
    )(x, w)


def _ssd_kernel(z_ref, xs_ref, bc_ref, x_ref, wdt_ref, cw_ref, cb_ref, dtb_ref, alog_ref, dskip_ref,
                nw_ref, e64_ref, e128_ref, o_ref,
                convbuf, xc, bcs, ccs, state, ybuf, dt_s, ac_s, act_s):
    G, W, N = SSM_N_GROUPS, SSM_GROUP_WIDTH, SSM_D_STATE
    L = z_ref.shape[0]
    c = pl.program_id(1)

    @pl.when(c == 0)
    def _():
        convbuf[0:CONV_TAIL, :] = jnp.zeros((CONV_TAIL, convbuf.shape[1]), F32)
        state[...] = jnp.zeros(state.shape, F32)

    convbuf[CONV_TAIL:CONV_TAIL + L, 0:SSM_D_INNER] = xs_ref[...]
    convbuf[CONV_TAIL:CONV_TAIL + L, SSM_D_INNER:] = bc_ref[...]

    def conv(c0, w):
        acc = cb_ref[:, c0:c0 + w]
        for k in range(SSM_CONV_K):
            r0 = CONV_TAIL - (SSM_CONV_K - 1) + k
            acc = acc + cw_ref[k:k + 1, c0:c0 + w] * convbuf[r0:r0 + L, c0:c0 + w]
        return _silu(acc)

    for g in range(G):
        xc[g] = conv(g * W, W)
        bcs[g] = conv(SSM_D_INNER + g * N, N).astype(BF16)
        ccs[g] = conv(SSM_D_INNER + G * N + g * N, N).astype(BF16)
    convbuf[0:CONV_TAIL, :] = convbuf[L:L + CONV_TAIL, :]

    dt = _softplus(_dot(x_ref[...].astype(BF16), wdt_ref[...].astype(BF16)) + dtb_ref[...])
    a = -jnp.exp(alog_ref[...])
    row_i = lax.broadcasted_iota(I32, (L, L), 0)
    col_i = lax.broadcasted_iota(I32, (L, L), 1)
    causal = row_i >= col_i
    a1, a2, a3 = _split3(dt * a)
    tri = causal.astype(BF16)
    ac = _dot(tri, a1) + _dot(tri, a2) + _dot(tri, a3)
    dt_s[...] = dt
    ac_s[...] = ac
    act_s[...] = ac.T
    lane = lax.broadcasted_iota(I32, (L, LANES), 1)
    dsk8 = jnp.broadcast_to(dskip_ref[...], (8, LANES))

    def group_body(g, carry):
        xg = xc[g]
        bg = bcs[g]
        cg = ccs[g]
        e64 = e64_ref[g]
        a_e = _dot_sel(ac_s[...], e64)
        dt_e = _dot_sel(dt_s[...], e64)
        dsk_e = _dot_sel(dsk8, e64)[0:1, :]
        al_e = a_e[L - 1:L, :]
        xdt = xg * dt_e
        cb = lax.dot_general(cg, bg, (((1,), (1,)), ((), ())), preferred_element_type=F32)
        col4 = _dot_sel(ac_s[...], e128_ref[g])
        xdt_b = xdt.astype(BF16)
        halves = []
        for p in range(2):
            xp = xdt_b[:, LANES * p:LANES * (p + 1)]
            ys = []
            for q in range(2):
                r = 2 * p + q
                row = act_s[pl.ds(SSM_HEADS_PER_GROUP * g + r, 1), :]
                seg = col4[:, LANES * r:LANES * (r + 1)] - row
                dec = jnp.where(causal, jnp.exp(seg), 0.0)
                ys.append(_dot((cb * dec).astype(BF16), xp))
            halves.append(jnp.where(lane < SSM_HEADDIM, ys[0], ys[1]))
        y_diag = jnp.concatenate(halves, axis=1)
        s_prev = state[g]
        y_off = _dot(cg, s_prev.astype(BF16)) * jnp.exp(a_e)
        new = lax.dot_general(bg, (xdt * jnp.exp(al_e - a_e)).astype(BF16),
                              (((0,), (0,)), ((), ())), preferred_element_type=F32)
        state[g] = s_prev * jnp.exp(al_e) + new
        ybuf[g] = y_diag + y_off + xg * dsk_e
        return carry

    lax.fori_loop(0, G, group_body, 0)

    for g in range(G):
        yg = ybuf[g] * _silu(z_ref[:, g * W:(g + 1) * W])
        ms = jnp.mean(yg * yg, axis=-1, keepdims=True)
        o_ref[:, g * W:(g + 1) * W] = (yg * lax.rsqrt(ms + NORM_EPS)
                                       * nw_ref[:, g * W:(g + 1) * W]).astype(o_ref.dtype)


def _head_selectors():
    G, R = SSM_N_GROUPS, SSM_HEADS_PER_GROUP
    h = jnp.arange(LANES)[None, :, None]
    g = jnp.arange(G)[:, None, None]
    j64 = jnp.arange(SSM_GROUP_WIDTH)[None, None, :]
    j128 = jnp.arange(R * LANES)[None, None, :]
    e64 = (h == g * R + j64 // SSM_HEADDIM).astype(BF16)
    e128 = (h == g * R + j128 // LANES).astype(BF16)
    return e64, e128


def _ssd(zx, x2, w_dt, conv_w, conv_b, dt_bias, a_log, d_skip, norm_w, batch, seq):
    L, G, W, N = SSM_CHUNK, SSM_N_GROUPS, SSM_GROUP_WIDTH, SSM_D_STATE
    nc = seq // L
    t = batch * seq
    pad = LANES - SSM_N_HEADS
    wdt = jnp.pad(w_dt, ((0, 0), (0, pad)))
    dtb = jnp.pad(dt_bias, (0, pad))[None, :]
    alog = jnp.pad(a_log, (0, pad))[None, :]
    dsk = jnp.pad(d_skip, (0, pad))[None, :]
    e64, e128 = _head_selectors()
    row = lambda b, c: (b * nc + c, 0)
    full2 = lambda b, c: (0, 0)
    full3 = lambda b, c: (0, 0, 0)
    conv_dim = conv_w.shape[1]
    return pl.pallas_call(
        _ssd_kernel,
        grid=(batch, nc),
        in_specs=[pl.BlockSpec((L, SSM_D_INNER), row),
                  pl.BlockSpec((L, SSM_D_INNER), lambda b, c: (b * nc + c, 1)),
                  pl.BlockSpec((L, SSM_D_INNER), lambda b, c: (b * nc + c, 2)),
                  pl.BlockSpec((L, D_MODEL), row),
                  pl.BlockSpec((D_MODEL, LANES), full2),
                  pl.BlockSpec((SSM_CONV_K, conv_dim), full2),
                  pl.BlockSpec((1, conv_dim), full2),
                  pl.BlockSpec((1, LANES), full2),
                  pl.BlockSpec((1, LANES), full2),
                  pl.BlockSpec((1, LANES), full2),
                  pl.BlockSpec((1, SSM_D_INNER), full2),
                  pl.BlockSpec((G, LANES, W), full3),
                  pl.BlockSpec((G, LANES, SSM_HEADS_PER_GROUP * LANES), full3)],
        out_specs=pl.BlockSpec((L, SSM_D_INNER), row),
        out_shape=jax.ShapeDtypeStruct((t, SSM_D_INNER), BF16),
        scratch_shapes=[pltpu.VMEM((CONV_TAIL + L, conv_dim), F32),
                        pltpu.VMEM((G, L, W), F32),
                        pltpu.VMEM((G, L, N), BF16),
                        pltpu.VMEM((G, L, N), BF16),
                        pltpu.VMEM((G, N, W), F32),
                        pltpu.VMEM((G, L, W), F32),
                        pltpu.VMEM((L, LANES), F32),
                        pltpu.VMEM((L, LANES), F32),
                        pltpu.VMEM((LANES, L), F32)],
        compiler_params=_arb(2),
        name="ssd_scan",
    )(zx, zx, zx, x2, wdt, conv_w, conv_b[None, :], dtb, alog, dsk, norm_w[None, :], e64, e128)


def _route(h, wr):
    tm = h.shape[0]
    logits = _dot_f32(h, wr)
    lane_i = lax.broadcasted_iota(I32, (tm, LANES), 1)
    lane = lane_i.astype(F32)
    neg = -jnp.inf
    big = float(LANES)

    def first_argmax(v, vmax):
        return jnp.min(jnp.where(v == vmax, lane, big), axis=-1, keepdims=True)

    gl = jnp.where((lane_i >= MOE_N_EXPERTS) & (lane_i < MOE_N_EXPERTS + MOE_GROUPS), logits, neg)
    gm = jnp.max(gl, axis=-1, keepdims=True)
    g_sel = first_argmax(gl, gm) - float(MOE_N_EXPERTS)
    g_gate = 1.0 / jnp.sum(jnp.exp(gl - gm), axis=-1, keepdims=True)
    lo = g_sel * float(MOE_EXPERTS_PER_GROUP)
    el = jnp.where((lane >= lo) & (lane < lo + float(MOE_EXPERTS_PER_GROUP)), logits, neg)
    m1 = jnp.max(el, axis=-1, keepdims=True)
    i1 = first_argmax(el, m1)
    el2 = jnp.where(lane == i1, neg, el)
    m2 = jnp.max(el2, axis=-1, keepdims=True)
    i2 = first_argmax(el2, m2)
    p2 = jnp.exp(m2 - m1)
    t1 = 1.0 / (1.0 + p2)
    t2 = p2 / (1.0 + p2)
    eid = jnp.where(lane_i == 0, i1, jnp.where(lane_i == 1, i2, 0.0)).astype(I32)
    wt = jnp.where(lane_i == 0, g_gate * t1, jnp.where(lane_i == 1, g_gate * t2, 0.0))
    return eid, wt


def _router_weights(w_group, w_expert):
    pad = LANES - MOE_N_EXPERTS - MOE_GROUPS
    return jnp.pad(jnp.concatenate([w_expert, w_group], axis=1), ((0, 0), (0, pad)))


def _mm_ln_router_kernel(x_ref, w_ref, r_ref, g_ref, b_ref, wr_ref, h_ref, eid_ref, wt_ref, wb_ref):
    @pl.when(pl.program_id(0) == 0)
    def _():
        wb_ref[...] = w_ref[...].astype(BF16)

    y = DEEPNORM_ALPHA * r_ref[...] + _dot(x_ref[...].astype(BF16), wb_ref[...])
    h = _layer_norm(y, g_ref[...], b_ref[...])
    h_ref[...] = h
    eid, wt = _route(h, wr_ref[...])
    eid_ref[...] = eid
    wt_ref[...] = wt


def _mm_ln_router(x, w, resid, g, b, wr, name):
    m, k = x.shape
    d = w.shape[1]
    tm = min(MM_TM, m)
    row = lambda i: (i, 0)
    full = lambda i: (0, 0)
    return pl.pallas_call(
        _mm_ln_router_kernel,
        grid=(m // tm,),
        in_specs=[pl.BlockSpec((tm, k), row), pl.BlockSpec((k, d), full), pl.BlockSpec((tm, d), row),
                  pl.BlockSpec((1, d), full), pl.BlockSpec((1, d), full), pl.BlockSpec((d, LANES), full)],
        out_specs=[pl.BlockSpec((tm, d), row), pl.BlockSpec((tm, LANES), row),
                   pl.BlockSpec((tm, LANES), row)],
        out_shape=[jax.ShapeDtypeStruct((m, d), F32), jax.ShapeDtypeStruct((m, LANES), I32),
                   jax.ShapeDtypeStruct((m, LANES), F32)],
        scratch_shapes=[pltpu.VMEM((k, d), BF16)],
        compiler_params=_arb(1),
        name=name,
    )(x, w, resid, g[None, :], b[None, :], wr)


def _dispatch_kernel(eid_ref, h_ref, dest_ref, be_ref, nu_ref, xs_hbm,
                     cnt, base, upper, zbuf, dst_v, dst_s, pe_v, pe_s, sem_z, sem_r, sem_s, *, blk):
    ph = pl.program_id(0)
    i = pl.program_id(1)
    tm = eid_ref.shape[0]
    n_e = LANES
    eid_t = eid_ref[...].astype(F32).T
    sub = lax.broadcasted_iota(I32, (n_e, tm), 0).astype(F32)
    oh = [(sub == eid_t[k:k + 1, :]).astype(F32) for k in range(MOE_TOP_K)]
    tot = [jnp.sum(o, axis=1, keepdims=True) for o in oh]

    @pl.when((ph == 0) & (i == 0))
    def _():
        cnt[...] = jnp.zeros(cnt.shape, F32)
        r_i = lax.broadcasted_iota(I32, (tm, tm), 0)
        c_i = lax.broadcasted_iota(I32, (tm, tm), 1)
        upper[...] = (r_i < c_i).astype(BF16)

    @pl.when(ph == 0)
    def _():
        cnt[...] += jnp.broadcast_to(tot[0] + tot[1], cnt.shape)

    @pl.when((ph == 1) & (i == 0))
    def _():
        counts = cnt[...]
        padded = jnp.floor((counts + float(blk - 1)) * (1.0 / blk)) * float(blk)
        r_i = lax.broadcasted_iota(I32, (n_e, n_e), 0)
        c_i = lax.broadcasted_iota(I32, (n_e, n_e), 1)
        tril = (r_i >= c_i).astype(BF16)
        p1, p2, p3 = _split3(padded)
        pends = _dot(tril, p1) + _dot(tril, p2) + _dot(tril, p3)
        base[...] = pends - padded
        nbp = be_ref.shape[1]
        blk_start = lax.broadcasted_iota(I32, (n_e, nbp), 1).astype(F32) * float(blk)
        is_e = lax.broadcasted_iota(I32, (n_e, nbp), 0) < MOE_N_EXPERTS
        done = jnp.where(is_e & (jnp.tile(pends, (1, nbp // LANES)) <= blk_start), 1.0, 0.0)
        be = jnp.minimum(jnp.sum(done, axis=0, keepdims=True), float(MOE_N_EXPERTS - 1))
        be_ref[...] = jnp.broadcast_to(be, be_ref.shape).astype(I32)
        last = pends[MOE_N_EXPERTS - 1:MOE_N_EXPERTS, :] * (1.0 / blk)
        nu_ref[...] = jnp.broadcast_to(last, nu_ref.shape).astype(I32)
        zbuf[...] = jnp.zeros(zbuf.shape, F32)
        row8 = lax.broadcasted_iota(I32, (8, LANES), 0)
        pe_v[...] = jnp.where(row8 == 0, pends.T[0:8, :], counts.T[0:8, :]).astype(I32)
        cp = pltpu.make_async_copy(pe_v, pe_s, sem_s)
        cp.start()
        cp.wait()

        def zero_copy(e):
            start = pl.multiple_of(pe_s[0, e] - blk, blk)
            return pltpu.make_async_copy(zbuf, xs_hbm.at[pl.ds(start, blk)], sem_z)

        def tail_copy(b):
            return pltpu.make_async_copy(zbuf, xs_hbm.at[pl.ds(pl.multiple_of(b * blk, blk), blk)], sem_z)

        n_used = lax.shift_right_logical(pe_s[0, MOE_N_EXPERTS - 1], blk.bit_length() - 1)
        n_blocks = xs_hbm.shape[0] // blk
        for e in range(MOE_N_EXPERTS):
            @pl.when(pe_s[1, e] > 0)
            def _():
                zero_copy(e).start()
        lax.fori_loop(n_used, n_blocks, lambda b, c: (tail_copy(b).start(), c)[1], 0)
        for e in range(MOE_N_EXPERTS):
            @pl.when(pe_s[1, e] > 0)
            def _():
                zero_copy(e).wait()
        lax.fori_loop(n_used, n_blocks, lambda b, c: (tail_copy(b).wait(), c)[1], 0)

    @pl.when(ph == 1)
    def _():
        b0 = base[:, 0:1]
        c0 = _dot(oh[0].astype(BF16), upper[...])
        c1 = _dot(oh[1].astype(BF16), upper[...])
        d0 = jnp.sum(oh[0] * (b0 + c0), axis=0, keepdims=True)
        d1 = jnp.sum(oh[1] * (b0 + tot[0] + c1), axis=0, keepdims=True)
        base[...] += jnp.broadcast_to(tot[0] + tot[1], base.shape)
        row8 = lax.broadcasted_iota(I32, (8, tm), 0)
        dst = jnp.where(row8 == 0, d0, jnp.where(row8 == 1, d1, 0.0)).astype(I32)
        dest_ref[...] = dst
        dst_v[...] = dst
        cp = pltpu.make_async_copy(dst_v, dst_s, sem_s)
        cp.start()
        cp.wait()

        def body(r, carry):
            for k in range(MOE_TOP_K):
                pltpu.make_async_copy(h_ref.at[pl.ds(r, 1)], xs_hbm.at[pl.ds(dst_s[k, r], 1)], sem_r).start()
            return carry
        lax.fori_loop(0, tm, body, 0, unroll=4)
        for k in range(MOE_TOP_K):
            pltpu.make_async_copy(h_ref, xs_hbm.at[pl.ds(0, tm)], sem_r).wait()


def _moe_dispatch(h, eid, blk, name):
    t, d = h.shape
    assert blk & (blk - 1) == 0, "block size must be a power of two"
    tm = min(DSP_TM, t)
    nt = t // tm
    n_blocks = -(-(t * MOE_TOP_K) // blk) + MOE_N_EXPERTS
    nbp = -(-n_blocks // LANES) * LANES
    tile = lambda ph, i: (i * ph, 0)
    dest, be, nu, xs = pl.pallas_call(
        functools.partial(_dispatch_kernel, blk=blk),
        grid=(2, nt),
        in_specs=[pl.BlockSpec((tm, LANES), lambda ph, i: (i, 0)),
                  pl.BlockSpec((tm, d), tile)],
        out_specs=[pl.BlockSpec((8, tm), lambda ph, i: (0, i * ph)),
                   pl.BlockSpec((8, nbp), lambda ph, i: (0, 0)),
                   pl.BlockSpec((8, LANES), lambda ph, i: (0, 0)),
                   pl.BlockSpec(memory_space=pl.ANY)],
        out_shape=[jax.ShapeDtypeStruct((8, t), I32), jax.ShapeDtypeStruct((8, nbp), I32),
                   jax.ShapeDtypeStruct((8, LANES), I32), jax.ShapeDtypeStruct((n_blocks * blk, d), F32)],
        scratch_shapes=[pltpu.VMEM((LANES, LANES), F32), pltpu.VMEM((LANES, LANES), F32),
                        pltpu.VMEM((tm, tm), BF16), pltpu.VMEM((blk, d), F32),
                        pltpu.VMEM((8, tm), I32), pltpu.SMEM((8, tm), I32),
                        pltpu.VMEM((8, LANES), I32), pltpu.SMEM((8, LANES), I32),
                        pltpu.SemaphoreType.DMA(()), pltpu.SemaphoreType.DMA(()), pltpu.SemaphoreType.DMA(())],
        compiler_params=_arb(2),
        name=name,
    )(eid, h)
    return dest, be[0, :n_blocks], nu[0, :1], xs, n_blocks


def _ffn_kernel(be_ref, nu_ref, x_ref, wg_ref, wu_ref, wd_ref, o_ref, wgb, wub, wdb):
    i = pl.program_id(0)

    @pl.when(i < nu_ref[0])
    def _():
        prev = be_ref[jnp.maximum(i - 1, 0)]

        @pl.when((i == 0) | (be_ref[i] != prev))
        def _():
            wgb[...] = wg_ref[0, 0].astype(BF16)
            wub[...] = wu_ref[0, 0].astype(BF16)
            wdb[...] = wd_ref[0, 0].astype(BF16)

        xb = x_ref[...].astype(BF16)
        hid = _silu(_dot(xb, wgb[...])) * _dot(xb, wub[...])
        o_ref[...] = _dot(hid.astype(BF16), wdb[...])

    @pl.when(i >= nu_ref[0])
    def _():
        o_ref[...] = jnp.zeros(o_ref.shape, o_ref.dtype)


def _moe_ffn(xs, block_expert, n_used, n_blocks, w_gate, w_up, w_down, layer, blk, name):
    d = xs.shape[1]
    f = w_gate.shape[3]
    used = lambda i, be, nu: (jnp.minimum(i, nu[0] - 1), 0)
    every = lambda i, be, nu: (i, 0)
    grid_spec = pltpu.PrefetchScalarGridSpec(
        num_scalar_prefetch=2,
        grid=(n_blocks,),
        in_specs=[pl.BlockSpec((blk, d), used),
                  pl.BlockSpec((1, 1, d, f), lambda i, be, nu: (layer, be[i], 0, 0)),
                  pl.BlockSpec((1, 1, d, f), lambda i, be, nu: (layer, be[i], 0, 0)),
                  pl.BlockSpec((1, 1, f, d), lambda i, be, nu: (layer, be[i], 0, 0))],
        out_specs=pl.BlockSpec((blk, d), every),
        scratch_shapes=[pltpu.VMEM((d, f), BF16), pltpu.VMEM((d, f), BF16), pltpu.VMEM((f, d), BF16)])
    return pl.pallas_call(
        _ffn_kernel,
        grid_spec=grid_spec,
        out_shape=jax.ShapeDtypeStruct((n_blocks * blk, d), F32),
        compiler_params=_arb(1),
        name=name,
    )(block_expert, n_used, xs, w_gate, w_up, w_down)


def _combine_ln_kernel(dst_ref, dstn_ref, h_ref, wt_ref, g_ref, b_ref, yb_hbm, o_ref, ybuf, sem):
    i = pl.program_id(0)
    n = pl.num_programs(0)
    tm = h_ref.shape[0]

    def start_gather(dref, slot):
        def body(r, carry):
            for k in range(MOE_TOP_K):
                pltpu.make_async_copy(yb_hbm.at[pl.ds(dref[k, r], 1)],
                                      ybuf.at[slot, k, pl.ds(r, 1)], sem.at[slot]).start()
            return carry
        lax.fori_loop(0, tm, body, 0, unroll=4)

    @pl.when(i == 0)
    def _():
        start_gather(dst_ref, 0)

    @pl.when(i + 1 < n)
    def _():
        start_gather(dstn_ref, (i + 1) % 2)

    slot = i % 2
    for k in range(MOE_TOP_K):
        pltpu.make_async_copy(yb_hbm.at[pl.ds(0, tm)], ybuf.at[slot, k], sem.at[slot]).wait()
    wt = wt_ref[...]
    ffn = wt[:, 0:1] * ybuf[slot, 0] + wt[:, 1:2] * ybuf[slot, 1]
    o_ref[...] = _layer_norm(DEEPNORM_ALPHA * h_ref[...] + ffn, g_ref[...], b_ref[...])


def _combine_ln(h, wt, dest, yb, g, b, name):
    t, d = h.shape
    tm = min(CMB_TM, t)
    nblk = t // tm
    row = lambda i: (i, 0)
    full = lambda i: (0, 0)
    return pl.pallas_call(
        _combine_ln_kernel,
        grid=(nblk,),
        in_specs=[pl.BlockSpec((8, tm), lambda i: (0, i), memory_space=pltpu.SMEM),
                  pl.BlockSpec((8, tm), lambda i: (0, jnp.minimum(i + 1, nblk - 1)),
                               memory_space=pltpu.SMEM),
                  pl.BlockSpec((tm, d), row), pl.BlockSpec((tm, LANES), row),
                  pl.BlockSpec((1, d), full), pl.BlockSpec((1, d), full),
                  pl.BlockSpec(memory_space=pl.ANY)],
        out_specs=pl.BlockSpec((tm, d), row),
        out_shape=jax.ShapeDtypeStruct((t, d), F32),
        scratch_shapes=[pltpu.VMEM((2, MOE_TOP_K, tm, d), F32), pltpu.SemaphoreType.DMA((2,))],
        compiler_params=_arb(1),
        name=name,
    )(dest, dest, h, wt, g[None, :], b[None, :], yb)


def _hier_moe_ln(h, eid, wt, w_gate, w_up, w_down, g, b, layer):
    dest, block_expert, n_used, xs, n_blocks = _moe_dispatch(h, eid, FFN_BLK, f"moe_dispatch{layer}")
    yb = _moe_ffn(xs, block_expert, n_used, n_blocks, w_gate, w_up, w_down, layer, FFN_BLK, f"moe_ffn{layer}")
    return _combine_ln(h, wt, dest, yb, g, b, f"moe_combine_ln{layer}")


def _qkv_rope_kernel(x_ref, w_ref, pos_ref, inv_ref, o_ref, wb_ref):
    j = pl.program_id(0)

    @pl.when(pl.program_id(1) == 0)
    def _():
        wb_ref[...] = w_ref[...].astype(BF16)

    acc = _dot(x_ref[...].astype(BF16), wb_ref[...])
    tm, n = acc.shape

    @pl.when(j == 2)
    def _():
        o_ref[...] = acc.astype(o_ref.dtype)

    @pl.when(j < 2)
    def _():
        sc = jnp.where(j == 0, ATTN_HEAD_DIM ** -0.5 * math.log2(math.e), 1.0).astype(F32)
        ang = pos_ref[...].astype(F32) * inv_ref[...]
        lane = lax.broadcasted_iota(I32, (tm, LANES), 1)
        dd = lane & (ATTN_HEAD_DIM - 1)
        half = ROT_DIM // 2
        cosv = jnp.cos(ang) * sc
        sinv = jnp.sin(ang) * sc
        c_t = jnp.where(dd < ROT_DIM, cosv, sc)
        s_up = jnp.where(dd < half, -sinv, 0.0)
        s_dn = jnp.where((dd >= half) & (dd < ROT_DIM), sinv, 0.0)
        for blk in range(n // LANES):
            tt = acc[:, blk * LANES:(blk + 1) * LANES]
            out = (tt * c_t + pltpu.roll(tt, LANES - half, 1) * s_up + pltpu.roll(tt, half, 1) * s_dn)
            o_ref[:, blk * LANES:(blk + 1) * LANES] = out.astype(o_ref.dtype)


def _rope_inv_table():
    inv = ROPE_THETA ** (-jnp.arange(0, ROT_DIM, 2, dtype=F32) / ROT_DIM)
    head = jnp.concatenate([inv, inv, jnp.zeros((ATTN_HEAD_DIM - ROT_DIM,), F32)])
    return jnp.tile(head, LANES // ATTN_HEAD_DIM)[None, :]


def _qkv_rope(h, w_qkv, positions):
    m, k = h.shape
    n = w_qkv.shape[1]
    tm = min(MM_TM, m)
    pos = positions.reshape(m, 1)
    return pl.pallas_call(
        _qkv_rope_kernel,
        grid=(n // MM_TN, m // tm),
        in_specs=[pl.BlockSpec((tm, k), lambda j, i: (i, 0)),
                  pl.BlockSpec((k, MM_TN), lambda j, i: (0, j)),
                  pl.BlockSpec((tm, 1), lambda j, i: (i, 0)),
                  pl.BlockSpec((1, LANES), lambda j, i: (0, 0))],
        out_specs=pl.BlockSpec((tm, MM_TN), lambda j, i: (i, j)),
        out_shape=jax.ShapeDtypeStruct((m, n), BF16),
        scratch_shapes=[pltpu.VMEM((k, MM_TN), BF16)],
        compiler_params=_arb(2),
        name="mm_qkv_rope",
    )(h, w_qkv, pos, _rope_inv_table())


def _attn_kernel(q_ref, k_ref, v_ref, lq1_ref, lk1_ref, lq2_ref, lk2_ref, sw_ref, o_ref, *, lambda_init):
    i = pl.program_id(2)
    tq = q_ref.shape[0]
    q = q_ref[...]
    lane = lax.broadcasted_iota(I32, (tq, LANES), 1)
    zero = jnp.zeros((), q.dtype)
    qs = (jnp.where(lane < ATTN_HEAD_DIM, q, zero), jnp.where(lane >= ATTN_HEAD_DIM, q, zero))
    row_i = lax.broadcasted_iota(I32, (tq, tq), 0)
    col_i = lax.broadcasted_iota(I32, (tq, tq), 1)
    causal = row_i >= col_i

    def step(kb, vb, carry, masked):
        out = []
        for c in range(2):
            m, l, acc = carry[c]
            s = lax.dot_general(qs[c], kb, (((1,), (1,)), ((), ())), preferred_element_type=F32)
            if masked:
                s = jnp.where(causal, s, -jnp.inf)
            mn = jnp.maximum(m, jnp.max(s, axis=-1, keepdims=True))
            alpha = jnp.exp2(m - mn)
            p = jnp.exp2(s - mn)
            l = alpha * l + jnp.sum(p, axis=-1, keepdims=True)
            acc = alpha * acc + _dot(p.astype(BF16), vb)
            out.append((mn, l, acc))
        return tuple(out)

    def kv_step(width, masked):
        def body(off, carry):
            off = pl.multiple_of(off, tq)
            return step(k_ref[pl.ds(off, width), :], v_ref[pl.ds(off, width), :], carry, masked)
        return body

    init = tuple((jnp.full((tq, 1), -jnp.inf, F32), jnp.zeros((tq, 1), F32), jnp.zeros((tq, LANES), F32))
                 for _ in range(2))
    wide = kv_step(2 * tq, False)
    carry = lax.fori_loop(0, i // 2, lambda j, c: wide(j * (2 * tq), c), init)
    single = kv_step(tq, False)
    carry = lax.fori_loop(0, i % 2, lambda j, c: single((i - 1) * tq, c), carry)
    carry = kv_step(tq, True)(i * tq, carry)

    lam = (jnp.exp(jnp.sum(lq1_ref[...] * lk1_ref[...], axis=-1, keepdims=True))
           - jnp.exp(jnp.sum(lq2_ref[...] * lk2_ref[...], axis=-1, keepdims=True)) + lambda_init)
    (_, l1, a1), (_, l2, a2) = carry
    o = a1 / l1 - lam * (a2 / l2)
    o = o * lax.rsqrt(jnp.mean(o * o, axis=-1, keepdims=True) + NORM_EPS)
    o_ref[...] = (o * sw_ref[...] * (1.0 - lambda_init)).astype(o_ref.dtype)


def _diff_attention(qkv, lq1, lk1, lq2, lk2, subln_w, lambda_init, batch, seq):
    t = batch * seq
    tq = min(ATT_TQ, seq)
    nq = seq // tq
    h_n = ATTN_N_HEADS
    vec = lambda b, h, i: (0, 0)
    return pl.pallas_call(
        functools.partial(_attn_kernel, lambda_init=lambda_init),
        grid=(batch, h_n, nq),
        in_specs=[pl.BlockSpec((tq, LANES), lambda b, h, i: (b * nq + i, h)),
                  pl.BlockSpec((seq, LANES), lambda b, h, i: (b, h_n + h)),
                  pl.BlockSpec((seq, LANES), lambda b, h, i: (b, 2 * h_n + h)),
                  pl.BlockSpec((1, ATTN_HEAD_DIM), vec), pl.BlockSpec((1, ATTN_HEAD_DIM), vec),
                  pl.BlockSpec((1, ATTN_HEAD_DIM), vec), pl.BlockSpec((1, ATTN_HEAD_DIM), vec),
                  pl.BlockSpec((1, ATTN_V_DIM), vec)],
        out_specs=pl.BlockSpec((tq, LANES), lambda b, h, i: (b * nq + i, h)),
        out_shape=jax.ShapeDtypeStruct((t, h_n * ATTN_V_DIM), BF16),
        compiler_params=_arb(3),
        name="diff_attn",
    )(qkv, qkv, qkv, lq1[None, :], lk1[None, :], lq2[None, :], lk2[None, :], subln_w[None, :])


def kernel(x, positions, ln_mix_g, ln_mix_b, ln_ffn_g, ln_ffn_b, ssm_w_in, ssm_conv_w, ssm_conv_b, ssm_dt_bias, ssm_a_log, ssm_d, ssm_norm_w, ssm_w_out, attn_w_qkv, attn_lam_q1, attn_lam_k1, attn_lam_q2, attn_lam_k2, attn_subln_w, attn_w_o, moe_w_group, moe_w_expert, moe_w_gate, moe_w_up, moe_w_down):
    batch, seq, d = x.shape
    t = batch * seq
    h = x.reshape(t, d)

    zx = _matmul(h, ssm_w_in[0], SSM_ZX_DIM, F32)
    yg = _ssd(zx, h, ssm_w_in[0][:, SSM_ZX_DIM:], ssm_conv_w[0], ssm_conv_b[0], ssm_dt_bias[0],
              ssm_a_log[0], ssm_d[0], ssm_norm_w[0], batch, seq)
    h, eid, wt = _mm_ln_router(yg, ssm_w_out[0], h, ln_mix_g[0], ln_mix_b[0],
                               _router_weights(moe_w_group[0], moe_w_expert[0]), "mm_ssm_out_ln_router")
    h = _hier_moe_ln(h, eid, wt, moe_w_gate, moe_w_up, moe_w_down, ln_ffn_g[0], ln_ffn_b[0], 0)

    lambda_init = 0.8 - 0.6 * math.exp(-0.3 * 1)
    qkv = _qkv_rope(h, attn_w_qkv[0], positions)
    o = _diff_attention(qkv, attn_lam_q1[0], attn_lam_k1[0], attn_lam_q2[0], attn_lam_k2[0],
                        attn_subln_w[0], lambda_init, batch, seq)
    h, eid, wt = _mm_ln_router(o, attn_w_o[0], h, ln_mix_g[1], ln_mix_b[1],
                               _router_weights(moe_w_group[1], moe_w_expert[1]), "mm_attn_out_ln_router")
    h = _hier_moe_ln(h, eid, wt, moe_w_gate, moe_w_up, moe_w_down, ln_ffn_g[1], ln_ffn_b[1], 1)
    return h.reshape(batch, seq, d)
```

```python
import functools
import math

import jax
import jax.numpy as jnp
from jax import lax
from jax.experimental import pallas as pl
from jax.experimental.pallas import tpu as pltpu

F32 = jnp.float32
BF16 = jnp.bfloat16
I32 = jnp.int32

D_MODEL = 1024
DEPTH = 2
SSM_D_INNER = 2048
SSM_HEADDIM = 64
SSM_N_HEADS = 32
SSM_N_GROUPS = 8
SSM_HEADS_PER_GROUP = 4
SSM_D_STATE = 128
SSM_CONV_K = 4
SSM_CHUNK = 128
SSM_GROUP_WIDTH = SSM_HEADS_PER_GROUP * SSM_HEADDIM
SSM_ZX_DIM = 2 * SSM_D_INNER + 2 * SSM_N_GROUPS * SSM_D_STATE
ATTN_HEAD_DIM = 64
ATTN_N_HEADS = 8
ATTN_V_DIM = 128
ROT_DIM = 16
ROPE_THETA = 500000.0
MOE_GROUPS = 4
MOE_EXPERTS_PER_GROUP = 8
MOE_N_EXPERTS = 32
MOE_TOP_K = 2
MOE_D_FF = 512
DEEPNORM_ALPHA = (2 * DEPTH) ** 0.25
NORM_EPS = 1e-5

LANES = 128
CONV_TAIL = 8

MM_TM = 512
MM_TM_WIDE = 256
MM_TN = 1024
FFN_BLK = 512
DSP_TM = 512
CMB_TM = 256
ATT_TQ = 512


def _arb(n):
    return pltpu.CompilerParams(dimension_semantics=("arbitrary",) * n,
                                vmem_limit_bytes=56 * 1024 * 1024)


def _sigmoid(x):
    return 1.0 / (1.0 + jnp.exp(-x))


def _silu(x):
    return x * _sigmoid(x)


def _softplus(x):
    return jnp.maximum(x, 0.0) + jnp.log(1.0 + jnp.exp(-jnp.abs(x)))


def _layer_norm(y, g, b):
    mu = jnp.mean(y, axis=-1, keepdims=True)
    d = y - mu
    var = jnp.mean(d * d, axis=-1, keepdims=True)
    return d * lax.rsqrt(var + NORM_EPS) * g + b


def _split3(a):
    a1 = a.astype(BF16)
    r1 = a - a1.astype(F32)
    a2 = r1.astype(BF16)
    a3 = (r1 - a2.astype(F32)).astype(BF16)
    return a1, a2, a3


def _dot(a, b):
    return jnp.dot(a, b, preferred_element_type=F32)


def _dot_sel(a, sel):
    a1, a2, a3 = _split3(a)
    return _dot(a1, sel) + _dot(a2, sel) + _dot(a3, sel)


def _dot_f32(a, b):
    a1, a2, a3 = _split3(a)
    b1, b2, b3 = _split3(b)
    return (_dot(a1, b1) + _dot(a1, b2) + _dot(a2, b1)
            + _dot(a2, b2) + _dot(a1, b3) + _dot(a3, b1))


def _mm_kernel(x_ref, w_ref, o_ref):
    xb = x_ref[...].astype(BF16)
    for j in range(o_ref.shape[1] // MM_TN):
        cols = slice(j * MM_TN, (j + 1) * MM_TN)
        o_ref[:, cols] = _dot(xb, w_ref[:, cols]).astype(o_ref.dtype)


def _matmul(x, w, n_cols, out_dtype):
    m, k = x.shape
    tm = min(MM_TM_WIDE, m)
    return pl.pallas_call(
        _mm_kernel,
        grid=(m // tm,),
        in_specs=[pl.BlockSpec((tm, k), lambda i: (i, 0)),
                  pl.BlockSpec((k, n_cols), lambda i: (0, 0))],
        out_specs=pl.BlockSpec((tm, n_cols), lambda i: (i, 0)),
        out_shape=jax.ShapeDtypeStruct((m, n_cols), out_dtype),
        compiler_params=_arb(1),
        name="mm_inproj",
    )(x, w)


def _ssd_kernel(z_ref, xs_ref, bc_ref, x_ref, wdt_ref, cw_ref, cb_ref, dtb_ref, alog_ref, dskip_ref,
                nw_ref, e64_ref, e128_ref, o_ref,
                convbuf, xc, bcs, ccs, state, ybuf, dt_s, ac_s, act_s):
    G, W, N = SSM_N_GROUPS, SSM_GROUP_WIDTH, SSM_D_STATE
    L = z_ref.shape[0]
    c = pl.program_id(1)

    @pl.when(c == 0)
    def _():
        convbuf[0:CONV_TAIL, :] = jnp.zeros((CONV_TAIL, convbuf.shape[1]), F32)
        state[...] = jnp.zeros(state.shape, F32)

    convbuf[CONV_TAIL:CONV_TAIL + L, 0:SSM_D_INNER] = xs_ref[...]
    convbuf[CONV_TAIL:CONV_TAIL + L, SSM_D_INNER:] = bc_ref[...]

    def conv(c0, w):
        acc = cb_ref[:, c0:c0 + w]
        for k in range(SSM_CONV_K):
            r0 = CONV_TAIL - (SSM_CONV_K - 1) + k
            acc = acc + cw_ref[k:k + 1, c0:c0 + w] * convbuf[r0:r0 + L, c0:c0 + w]
        return _silu(acc)

    for g in range(G):
        xc[g] = conv(g * W, W)
        bcs[g] = conv(SSM_D_INNER + g * N, N).astype(BF16)
        ccs[g] = conv(SSM_D_INNER + G * N + g * N, N).astype(BF16)
    convbuf[0:CONV_TAIL, :] = convbuf[L:L + CONV_TAIL, :]

    dt = _softplus(_dot(x_ref[...].astype(BF16), wdt_ref[...].astype(BF16)) + dtb_ref[...])
    a = -jnp.exp(alog_ref[...])
    row_i = lax.broadcasted_iota(I32, (L, L), 0)
    col_i = lax.broadcasted_iota(I32, (L, L), 1)
    causal = row_i >= col_i
    a1, a2, a3 = _split3(dt * a)
    tri = causal.astype(BF16)
    ac = _dot(tri, a1) + _dot(tri, a2) + _dot(tri, a3)
    dt_s[...] = dt
    ac_s[...] = ac
    act_s[...] = ac.T
    lane = lax.broadcasted_iota(I32, (L, LANES), 1)
    dsk8 = jnp.broadcast_to(dskip_ref[...], (8, LANES))

    def group_body(g, carry):
        xg = xc[g]
        bg = bcs[g]
        cg = ccs[g]
        e64 = e64_ref[g]
        a_e = _dot_sel(ac_s[...], e64)
        dt_e = _dot_sel(dt_s[...], e64)
        dsk_e = _dot_sel(dsk8, e64)[0:1, :]
        al_e = a_e[L - 1:L, :]
        xdt = xg * dt_e
        cb = lax.dot_general(cg, bg, (((1,), (1,)), ((), ())), preferred_element_type=F32)
        col4 = _dot_sel(ac_s[...], e128_ref[g])
        xdt_b = xdt.astype(BF16)
        halves = []
        for p in range(2):
            xp = xdt_b[:, LANES * p:LANES * (p + 1)]
            ys = []
            for q in range(2):
                r = 2 * p + q
                row = act_s[pl.ds(SSM_HEADS_PER_GROUP * g + r, 1), :]
                seg = col4[:, LANES * r:LANES * (r + 1)] - row
                dec = jnp.where(causal, jnp.exp(seg), 0.0)
                ys.append(_dot((cb * dec).astype(BF16), xp))
            halves.append(jnp.where(lane < SSM_HEADDIM, ys[0], ys[1]))
        y_diag = jnp.concatenate(halves, axis=1)
        s_prev = state[g]
        y_off = _dot(cg, s_prev.astype(BF16)) * jnp.exp(a_e)
        new = lax.dot_general(bg, (xdt * jnp.exp(al_e - a_e)).astype(BF16),
                              (((0,), (0,)), ((), ())), preferred_element_type=F32)
        state[g] = s_prev * jnp.exp(al_e) + new
        ybuf[g] = y_diag + y_off + xg * dsk_e
        return carry

    lax.fori_loop(0, G, group_body, 0)

    for g in range(G):
        yg = ybuf[g] * _silu(z_ref[:, g * W:(g + 1) * W])
        ms = jnp.mean(yg * yg, axis=-1, keepdims=True)
        o_ref[:, g * W:(g + 1) * W] = (yg * lax.rsqrt(ms + NORM_EPS)
                                       * nw_ref[:, g * W:(g + 1) * W]).astype(o_ref.dtype)


def _head_selectors():
    G, R = SSM_N_GROUPS, SSM_HEADS_PER_GROUP
    h = jnp.arange(LANES)[None, :, None]
    g = jnp.arange(G)[:, None, None]
    j64 = jnp.arange(SSM_GROUP_WIDTH)[None, None, :]
    j128 = jnp.arange(R * LANES)[None, None, :]
    e64 = (h == g * R + j64 // SSM_HEADDIM).astype(BF16)
    e128 = (h == g * R + j128 // LANES).astype(BF16)
    return e64, e128


def _ssd(zx, x2, w_dt, conv_w, conv_b, dt_bias, a_log, d_skip, norm_w, batch, seq):
    L, G, W, N = SSM_CHUNK, SSM_N_GROUPS, SSM_GROUP_WIDTH, SSM_D_STATE
    nc = seq // L
    t = batch * seq
    pad = LANES - SSM_N_HEADS
    wdt = jnp.pad(w_dt, ((0, 0), (0, pad)))
    dtb = jnp.pad(dt_bias, (0, pad))[None, :]
    alog = jnp.pad(a_log, (0, pad))[None, :]
    dsk = jnp.pad(d_skip, (0, pad))[None, :]
    e64, e128 = _head_selectors()
    row = lambda b, c: (b * nc + c, 0)
    full2 = lambda b, c: (0, 0)
    full3 = lambda b, c: (0, 0, 0)
    conv_dim = conv_w.shape[1]
    return pl.pallas_call(
        _ssd_kernel,
        grid=(batch, nc),
        in_specs=[pl.BlockSpec((L, SSM_D_INNER), row),
                  pl.BlockSpec((L, SSM_D_INNER), lambda b, c: (b * nc + c, 1)),
                  pl.BlockSpec((L, SSM_D_INNER), lambda b, c: (b * nc + c, 2)),
                  pl.BlockSpec((L, D_MODEL), row),
                  pl.BlockSpec((D_MODEL, LANES), full2),
                  pl.BlockSpec((SSM_CONV_K, conv_dim), full2),
                  pl.BlockSpec((1, conv_dim), full2),
                  pl.BlockSpec((1, LANES), full2),
                  pl.BlockSpec((1, LANES), full2),
                  pl.BlockSpec((1, LANES), full2),
                  pl.BlockSpec((1, SSM_D_INNER), full2),
                  pl.BlockSpec((G, LANES, W), full3),
                  pl.BlockSpec((G, LANES, SSM_HEADS_PER_GROUP * LANES), full3)],
        out_specs=pl.BlockSpec((L, SSM_D_INNER), row),
        out_shape=jax.ShapeDtypeStruct((t, SSM_D_INNER), BF16),
        scratch_shapes=[pltpu.VMEM((CONV_TAIL + L, conv_dim), F32),
                        pltpu.VMEM((G, L, W), F32),
                        pltpu.VMEM((G, L, N), BF16),
                        pltpu.VMEM((G, L, N), BF16),
                        pltpu.VMEM((G, N, W), F32),
                        pltpu.VMEM((G, L, W), F32),
                        pltpu.VMEM((L, LANES), F32),
                        pltpu.VMEM((L, LANES), F32),
                        pltpu.VMEM((LANES, L), F32)],
        compiler_params=_arb(2),
        name="ssd_scan",
    )(zx, zx, zx, x2, wdt, conv_w, conv_b[None, :], dtb, alog, dsk, norm_w[None, :], e64, e128)


def _route(h, wr):
    tm = h.shape[0]
    logits = _dot(h.astype(BF16), wr)
    lane_i = lax.broadcasted_iota(I32, (tm, LANES), 1)
    lane = lane_i.astype(F32)
    neg = -jnp.inf
    big = float(LANES)

    def first_argmax(v, vmax):
        return jnp.min(jnp.where(v == vmax, lane, big), axis=-1, keepdims=True)

    gl = jnp.where((lane_i >= MOE_N_EXPERTS) & (lane_i < MOE_N_EXPERTS + MOE_GROUPS), logits, neg)
    gm = jnp.max(gl, axis=-1, keepdims=True)
    g_sel = first_argmax(gl, gm) - float(MOE_N_EXPERTS)
    g_gate = 1.0 / jnp.sum(jnp.exp(gl - gm), axis=-1, keepdims=True)
    lo = g_sel * float(MOE_EXPERTS_PER_GROUP)
    el = jnp.where((lane >= lo) & (lane < lo + float(MOE_EXPERTS_PER_GROUP)), logits, neg)
    m1 = jnp.max(el, axis=-1, keepdims=True)
    i1 = first_argmax(el, m1)
    el2 = jnp.where(lane == i1, neg, el)
    m2 = jnp.max(el2, axis=-1, keepdims=True)
    i2 = first_argmax(el2, m2)
    p2 = jnp.exp(m2 - m1)
    t1 = 1.0 / (1.0 + p2)
    t2 = p2 / (1.0 + p2)
    eid = jnp.where(lane_i == 0, i1, jnp.where(lane_i == 1, i2, 0.0)).astype(I32)
    wt = jnp.where(lane_i == 0, g_gate * t1, jnp.where(lane_i == 1, g_gate * t2, 0.0))
    return eid, wt


def _router_weights(w_group, w_expert):
    pad = LANES - MOE_N_EXPERTS - MOE_GROUPS
    return jnp.pad(jnp.concatenate([w_expert, w_group], axis=1), ((0, 0), (0, pad))).astype(BF16)


def _mm_ln_router_kernel(x_ref, w_ref, r_ref, g_ref, b_ref, wr_ref, h_ref, eid_ref, wt_ref):
    y = DEEPNORM_ALPHA * r_ref[...] + _dot(x_ref[...].astype(BF16), w_ref[...])
    h = _layer_norm(y, g_ref[...], b_ref[...])
    h_ref[...] = h
    eid, wt = _route(h, wr_ref[...])
    eid_ref[...] = eid
    wt_ref[...] = wt


def _mm_ln_router(x, w, resid, g, b, wr, name):
    m, k = x.shape
    d = w.shape[1]
    tm = min(MM_TM, m)
    row = lambda i: (i, 0)
    full = lambda i: (0, 0)
    return pl.pallas_call(
        _mm_ln_router_kernel,
        grid=(m // tm,),
        in_specs=[pl.BlockSpec((tm, k), row), pl.BlockSpec((k, d), full), pl.BlockSpec((tm, d), row),
                  pl.BlockSpec((1, d), full), pl.BlockSpec((1, d), full), pl.BlockSpec((d, LANES), full)],
        out_specs=[pl.BlockSpec((tm, d), row), pl.BlockSpec((tm, LANES), row),
                   pl.BlockSpec((tm, LANES), row)],
        out_shape=[jax.ShapeDtypeStruct((m, d), F32), jax.ShapeDtypeStruct((m, LANES), I32),
                   jax.ShapeDtypeStruct((m, LANES), F32)],
        compiler_params=_arb(1),
        name=name,
    )(x, w, resid, g[None, :], b[None, :], wr)


def _dispatch_kernel(eid_ref, h_ref, dest_ref, be_ref, nu_ref, xs_hbm,
                     cnt, base, upper, zbuf, dst_v, dst_s, pe_v, pe_s, sem_z, sem_r, sem_s, *, blk):
    ph = pl.program_id(0)
    i = pl.program_id(1)
    tm = eid_ref.shape[0]
    n_e = LANES
    eid_t = eid_ref[...].astype(F32).T
    sub = lax.broadcasted_iota(I32, (n_e, tm), 0).astype(F32)
    oh = [(sub == eid_t[k:k + 1, :]).astype(F32) for k in range(MOE_TOP_K)]
    tot = [jnp.sum(o, axis=1, keepdims=True) for o in oh]

    @pl.when((ph == 0) & (i == 0))
    def _():
        cnt[...] = jnp.zeros(cnt.shape, F32)
        r_i = lax.broadcasted_iota(I32, (tm, tm), 0)
        c_i = lax.broadcasted_iota(I32, (tm, tm), 1)
        upper[...] = (r_i < c_i).astype(BF16)

    @pl.when(ph == 0)
    def _():
        cnt[...] += jnp.broadcast_to(tot[0] + tot[1], cnt.shape)

    @pl.when((ph == 1) & (i == 0))
    def _():
        counts = cnt[...]
        padded = jnp.floor((counts + float(blk - 1)) * (1.0 / blk)) * float(blk)
        r_i = lax.broadcasted_iota(I32, (n_e, n_e), 0)
        c_i = lax.broadcasted_iota(I32, (n_e, n_e), 1)
        tril = (r_i >= c_i).astype(BF16)
        p1, p2, p3 = _split3(padded)
        pends = _dot(tril, p1) + _dot(tril, p2) + _dot(tril, p3)
        base[...] = pends - padded
        nbp = be_ref.shape[1]
        blk_start = lax.broadcasted_iota(I32, (n_e, nbp), 1).astype(F32) * float(blk)
        is_e = lax.broadcasted_iota(I32, (n_e, nbp), 0) < MOE_N_EXPERTS
        done = jnp.where(is_e & (jnp.tile(pends, (1, nbp // LANES)) <= blk_start), 1.0, 0.0)
        be = jnp.minimum(jnp.sum(done, axis=0, keepdims=True), float(MOE_N_EXPERTS - 1))
        be_ref[...] = jnp.broadcast_to(be, be_ref.shape).astype(I32)
        last = pends[MOE_N_EXPERTS - 1:MOE_N_EXPERTS, :] * (1.0 / blk)
        nu_ref[...] = jnp.broadcast_to(last, nu_ref.shape).astype(I32)
        zbuf[...] = jnp.zeros(zbuf.shape, F32)
        row8 = lax.broadcasted_iota(I32, (8, LANES), 0)
        pe_v[...] = jnp.where(row8 == 0, pends.T[0:8, :], counts.T[0:8, :]).astype(I32)
        cp = pltpu.make_async_copy(pe_v, pe_s, sem_s)
        cp.start()
        cp.wait()

        def zero_copy(e):
            start = pl.multiple_of(pe_s[0, e] - blk, blk)
            return pltpu.make_async_copy(zbuf, xs_hbm.at[pl.ds(start, blk)], sem_z)

        def tail_copy(b):
            return pltpu.make_async_copy(zbuf, xs_hbm.at[pl.ds(pl.multiple_of(b * blk, blk), blk)], sem_z)

        n_used = lax.shift_right_logical(pe_s[0, MOE_N_EXPERTS - 1], blk.bit_length() - 1)
        n_blocks = xs_hbm.shape[0] // blk
        for e in range(MOE_N_EXPERTS):
            @pl.when(pe_s[1, e] > 0)
            def _():
                zero_copy(e).start()
        lax.fori_loop(n_used, n_blocks, lambda b, c: (tail_copy(b).start(), c)[1], 0)
        for e in range(MOE_N_EXPERTS):
            @pl.when(pe_s[1, e] > 0)
            def _():
                zero_copy(e).wait()
        lax.fori_loop(n_used, n_blocks, lambda b, c: (tail_copy(b).wait(), c)[1], 0)

    @pl.when(ph == 1)
    def _():
        b0 = base[:, 0:1]
        c0 = _dot(oh[0].astype(BF16), upper[...])
        c1 = _dot(oh[1].astype(BF16), upper[...])
        d0 = jnp.sum(oh[0] * (b0 + c0), axis=0, keepdims=True)
        d1 = jnp.sum(oh[1] * (b0 + tot[0] + c1), axis=0, keepdims=True)
        base[...] += jnp.broadcast_to(tot[0] + tot[1], base.shape)
        row8 = lax.broadcasted_iota(I32, (8, tm), 0)
        dst = jnp.where(row8 == 0, d0, jnp.where(row8 == 1, d1, 0.0)).astype(I32)
        dest_ref[...] = dst
        dst_v[...] = dst
        cp = pltpu.make_async_copy(dst_v, dst_s, sem_s)
        cp.start()
        cp.wait()

        def body(r, carry):
            for k in range(MOE_TOP_K):
                pltpu.make_async_copy(h_ref.at[pl.ds(r, 1)], xs_hbm.at[pl.ds(dst_s[k, r], 1)], sem_r).start()
            return carry
        lax.fori_loop(0, tm, body, 0, unroll=4)
        for k in range(MOE_TOP_K):
            pltpu.make_async_copy(h_ref, xs_hbm.at[pl.ds(0, tm)], sem_r).wait()


def _moe_dispatch(h, eid, blk, name):
    t, d = h.shape
    assert blk & (blk - 1) == 0, "block size must be a power of two"
    tm = min(DSP_TM, t)
    nt = t // tm
    n_blocks = -(-(t * MOE_TOP_K) // blk) + MOE_N_EXPERTS
    nbp = -(-n_blocks // LANES) * LANES
    tile = lambda ph, i: (i * ph, 0)
    dest, be, nu, xs = pl.pallas_call(
        functools.partial(_dispatch_kernel, blk=blk),
        grid=(2, nt),
        in_specs=[pl.BlockSpec((tm, LANES), lambda ph, i: (i, 0)),
                  pl.BlockSpec((tm, d), tile)],
        out_specs=[pl.BlockSpec((8, tm), lambda ph, i: (0, i * ph)),
                   pl.BlockSpec((8, nbp), lambda ph, i: (0, 0)),
                   pl.BlockSpec((8, LANES), lambda ph, i: (0, 0)),
                   pl.BlockSpec(memory_space=pl.ANY)],
        out_shape=[jax.ShapeDtypeStruct((8, t), I32), jax.ShapeDtypeStruct((8, nbp), I32),
                   jax.ShapeDtypeStruct((8, LANES), I32), jax.ShapeDtypeStruct((n_blocks * blk, d), F32)],
        scratch_shapes=[pltpu.VMEM((LANES, LANES), F32), pltpu.VMEM((LANES, LANES), F32),
                        pltpu.VMEM((tm, tm), BF16), pltpu.VMEM((blk, d), F32),
                        pltpu.VMEM((8, tm), I32), pltpu.SMEM((8, tm), I32),
                        pltpu.VMEM((8, LANES), I32), pltpu.SMEM((8, LANES), I32),
                        pltpu.SemaphoreType.DMA(()), pltpu.SemaphoreType.DMA(()), pltpu.SemaphoreType.DMA(())],
        compiler_params=_arb(2),
        name=name,
    )(eid, h)
    return dest, be[0, :n_blocks], nu[0, :1], xs, n_blocks


def _ffn_kernel(be_ref, nu_ref, x_ref, wg_ref, wu_ref, wd_ref, o_ref, wgb, wub, wdb):
    i = pl.program_id(0)

    @pl.when(i < nu_ref[0])
    def _():
        prev = be_ref[jnp.maximum(i - 1, 0)]

        @pl.when((i == 0) | (be_ref[i] != prev))
        def _():
            wgb[...] = wg_ref[0, 0].astype(BF16)
            wub[...] = wu_ref[0, 0].astype(BF16)
            wdb[...] = wd_ref[0, 0].astype(BF16)

        xb = x_ref[...].astype(BF16)
        hid = _silu(_dot(xb, wgb[...])) * _dot(xb, wub[...])
        o_ref[...] = _dot(hid.astype(BF16), wdb[...])

    @pl.when(i >= nu_ref[0])
    def _():
        o_ref[...] = jnp.zeros(o_ref.shape, o_ref.dtype)


def _moe_ffn(xs, block_expert, n_used, n_blocks, w_gate, w_up, w_down, layer, blk, name):
    d = xs.shape[1]
    f = w_gate.shape[3]
    used = lambda i, be, nu: (jnp.minimum(i, nu[0] - 1), 0)
    every = lambda i, be, nu: (i, 0)
    grid_spec = pltpu.PrefetchScalarGridSpec(
        num_scalar_prefetch=2,
        grid=(n_blocks,),
        in_specs=[pl.BlockSpec((blk, d), used),
                  pl.BlockSpec((1, 1, d, f), lambda i, be, nu: (layer, be[i], 0, 0)),
                  pl.BlockSpec((1, 1, d, f), lambda i, be, nu: (layer, be[i], 0, 0)),
                  pl.BlockSpec((1, 1, f, d), lambda i, be, nu: (layer, be[i], 0, 0))],
        out_specs=pl.BlockSpec((blk, d), every),
        scratch_shapes=[pltpu.VMEM((d, f), BF16), pltpu.VMEM((d, f), BF16), pltpu.VMEM((f, d), BF16)])
    return pl.pallas_call(
        _ffn_kernel,
        grid_spec=grid_spec,
        out_shape=jax.ShapeDtypeStruct((n_blocks * blk, d), F32),
        compiler_params=_arb(1),
        name=name,
    )(block_expert, n_used, xs, w_gate, w_up, w_down)


def _combine_ln_kernel(dst_ref, dstn_ref, h_ref, wt_ref, g_ref, b_ref, yb_hbm, o_ref, ybuf, sem):
    i = pl.program_id(0)
    n = pl.num_programs(0)
    tm = h_ref.shape[0]

    def start_gather(dref, slot):
        for r in range(tm):
            for k in range(MOE_TOP_K):
                pltpu.make_async_copy(yb_hbm.at[pl.ds(dref[k, r], 1)],
                                      ybuf.at[slot, k, pl.ds(r, 1)], sem.at[slot]).start(priority=k)

    @pl.when(i == 0)
    def _():
        start_gather(dst_ref, 0)

    @pl.when(i + 1 < n)
    def _():
        start_gather(dstn_ref, (i + 1) % 2)

    slot = i % 2
    for k in range(MOE_TOP_K):
        pltpu.make_async_copy(yb_hbm.at[pl.ds(0, tm)], ybuf.at[slot, k], sem.at[slot]).wait()
    wt = wt_ref[...]
    ffn = wt[:, 0:1] * ybuf[slot, 0] + wt[:, 1:2] * ybuf[slot, 1]
    o_ref[...] = _layer_norm(DEEPNORM_ALPHA * h_ref[...] + ffn, g_ref[...], b_ref[...])


def _combine_ln(h, wt, dest, yb, g, b, name):
    t, d = h.shape
    tm = min(CMB_TM, t)
    nblk = t // tm
    row = lambda i: (i, 0)
    full = lambda i: (0, 0)
    return pl.pallas_call(
        _combine_ln_kernel,
        grid=(nblk,),
        in_specs=[pl.BlockSpec((8, tm), lambda i: (0, i), memory_space=pltpu.SMEM),
                  pl.BlockSpec((8, tm), lambda i: (0, jnp.minimum(i + 1, nblk - 1)),
                               memory_space=pltpu.SMEM),
                  pl.BlockSpec((tm, d), row), pl.BlockSpec((tm, LANES), row),
                  pl.BlockSpec((1, d), full), pl.BlockSpec((1, d), full),
                  pl.BlockSpec(memory_space=pl.ANY)],
        out_specs=pl.BlockSpec((tm, d), row),
        out_shape=jax.ShapeDtypeStruct((t, d), F32),
        scratch_shapes=[pltpu.VMEM((2, MOE_TOP_K, tm, d), F32), pltpu.SemaphoreType.DMA((2,))],
        compiler_params=_arb(1),
        name=name,
    )(dest, dest, h, wt, g[None, :], b[None, :], yb)


def _hier_moe_ln(h, eid, wt, w_gate, w_up, w_down, g, b, layer):
    dest, block_expert, n_used, xs, n_blocks = _moe_dispatch(h, eid, FFN_BLK, f"moe_dispatch{layer}")
    yb = _moe_ffn(xs, block_expert, n_used, n_blocks, w_gate, w_up, w_down, layer, FFN_BLK, f"moe_ffn{layer}")
    return _combine_ln(h, wt, dest, yb, g, b, f"moe_combine_ln{layer}")


def _qkv_rope_kernel(x_ref, w_ref, pos_ref, inv_ref, o_ref):
    xb = x_ref[...].astype(BF16)
    tm = xb.shape[0]
    n = D_MODEL
    ang = pos_ref[...].astype(F32) * inv_ref[...]
    lane = lax.broadcasted_iota(I32, (tm, LANES), 1)
    dd = lane & (ATTN_HEAD_DIM - 1)
    half = ROT_DIM // 2
    cosv = jnp.cos(ang)
    sinv = jnp.sin(ang)
    c_t = jnp.where(dd < ROT_DIM, cosv, 1.0)
    s_up = jnp.where(dd < half, -sinv, 0.0)
    s_dn = jnp.where((dd >= half) & (dd < ROT_DIM), sinv, 0.0)
    for j, sc in ((0, ATTN_HEAD_DIM ** -0.5 * math.log2(math.e)), (1, 1.0)):
        acc = _dot(xb, w_ref[:, j * n:(j + 1) * n])
        c_j, up_j, dn_j = c_t * sc, s_up * sc, s_dn * sc
        for blk in range(n // LANES):
            tt = acc[:, blk * LANES:(blk + 1) * LANES]
            out = tt * c_j + pltpu.roll(tt, LANES - half, 1) * up_j + pltpu.roll(tt, half, 1) * dn_j
            o_ref[:, j * n + blk * LANES:j * n + (blk + 1) * LANES] = out.astype(o_ref.dtype)
    o_ref[:, 2 * n:3 * n] = _dot(xb, w_ref[:, 2 * n:3 * n]).astype(o_ref.dtype)


def _rope_inv_table():
    inv = ROPE_THETA ** (-jnp.arange(0, ROT_DIM, 2, dtype=F32) / ROT_DIM)
    head = jnp.concatenate([inv, inv, jnp.zeros((ATTN_HEAD_DIM - ROT_DIM,), F32)])
    return jnp.tile(head, LANES // ATTN_HEAD_DIM)[None, :]


def _qkv_rope(h, w_qkv, positions):
    m, k = h.shape
    n = w_qkv.shape[1]
    tm = min(MM_TM, m)
    pos = positions.reshape(m, 1)
    return pl.pallas_call(
        _qkv_rope_kernel,
        grid=(m // tm,),
        in_specs=[pl.BlockSpec((tm, k), lambda i: (i, 0)),
                  pl.BlockSpec((k, n), lambda i: (0, 0)),
                  pl.BlockSpec((tm, 1), lambda i: (i, 0)),
                  pl.BlockSpec((1, LANES), lambda i: (0, 0))],
        out_specs=pl.BlockSpec((tm, n), lambda i: (i, 0)),
        out_shape=jax.ShapeDtypeStruct((m, n), BF16),
        compiler_params=_arb(1),
        name="mm_qkv_rope",
    )(h, w_qkv, pos, _rope_inv_table())


def _attn_kernel(q_ref, k_ref, v_ref, lq1_ref, lk1_ref, lq2_ref, lk2_ref, sw_ref, o_ref, *, lambda_init):
    i = pl.program_id(2)
    tq = q_ref.shape[0]
    q = q_ref[...]
    lane = lax.broadcasted_iota(I32, (tq, LANES), 1)
    zero = jnp.zeros((), q.dtype)
    qs = (jnp.where(lane < ATTN_HEAD_DIM, q, zero), jnp.where(lane >= ATTN_HEAD_DIM, q, zero))
    row_i = lax.broadcasted_iota(I32, (tq, tq), 0)
    col_i = lax.broadcasted_iota(I32, (tq, tq), 1)
    causal = row_i >= col_i

    def step(kb, vb, carry, masked):
        out = []
        for c in range(2):
            m, l, acc = carry[c]
            s = lax.dot_general(qs[c], kb, (((1,), (1,)), ((), ())), preferred_element_type=F32)
            if masked:
                s = jnp.where(causal, s, -jnp.inf)
            mn = jnp.maximum(m, jnp.max(s, axis=-1, keepdims=True))
            alpha = jnp.exp2(m - mn)
            p = jnp.exp2(s - mn)
            l = alpha * l + jnp.sum(p, axis=-1, keepdims=True)
            acc = alpha * acc + _dot(p.astype(BF16), vb)
            out.append((mn, l, acc))
        return tuple(out)

    def kv_step(width, masked):
        def body(off, carry):
            off = pl.multiple_of(off, tq)
            return step(k_ref[pl.ds(off, width), :], v_ref[pl.ds(off, width), :], carry, masked)
        return body

    init = tuple((jnp.full((tq, 1), -jnp.inf, F32), jnp.zeros((tq, 1), F32), jnp.zeros((tq, LANES), F32))
                 for _ in range(2))
    wide = kv_step(2 * tq, False)
    carry = lax.fori_loop(0, i // 2, lambda j, c: wide(j * (2 * tq), c), init)
    single = kv_step(tq, False)
    carry = lax.fori_loop(0, i % 2, lambda j, c: single((i - 1) * tq, c), carry)
    carry = kv_step(tq, True)(i * tq, carry)

    lam = (jnp.exp(jnp.sum(lq1_ref[...] * lk1_ref[...], axis=-1, keepdims=True))
           - jnp.exp(jnp.sum(lq2_ref[...] * lk2_ref[...], axis=-1, keepdims=True)) + lambda_init)
    (_, l1, a1), (_, l2, a2) = carry
    o = a1 / l1 - lam * (a2 / l2)
    o = o * lax.rsqrt(jnp.mean(o * o, axis=-1, keepdims=True) + NORM_EPS)
    o_ref[...] = (o * sw_ref[...] * (1.0 - lambda_init)).astype(o_ref.dtype)


def _diff_attention(qkv, lq1, lk1, lq2, lk2, subln_w, lambda_init, batch, seq):
    t = batch * seq
    tq = min(ATT_TQ, seq)
    nq = seq // tq
    h_n = ATTN_N_HEADS
    vec = lambda b, h, i: (0, 0)
    return pl.pallas_call(
        functools.partial(_attn_kernel, lambda_init=lambda_init),
        grid=(batch, h_n, nq),
        in_specs=[pl.BlockSpec((tq, LANES), lambda b, h, i: (b * nq + i, h)),
                  pl.BlockSpec((seq, LANES), lambda b, h, i: (b, h_n + h)),
                  pl.BlockSpec((seq, LANES), lambda b, h, i: (b, 2 * h_n + h)),
                  pl.BlockSpec((1, ATTN_HEAD_DIM), vec), pl.BlockSpec((1, ATTN_HEAD_DIM), vec),
                  pl.BlockSpec((1, ATTN_HEAD_DIM), vec), pl.BlockSpec((1, ATTN_HEAD_DIM), vec),
                  pl.BlockSpec((1, ATTN_V_DIM), vec)],
        out_specs=pl.BlockSpec((tq, LANES), lambda b, h, i: (b * nq + i, h)),
        out_shape=jax.ShapeDtypeStruct((t, h_n * ATTN_V_DIM), BF16),
        compiler_params=_arb(3),
        name="diff_attn",
    )(qkv, qkv, qkv, lq1[None, :], lk1[None, :], lq2[None, :], lk2[None, :], subln_w[None, :])


def kernel(x, positions, ln_mix_g, ln_mix_b, ln_ffn_g, ln_ffn_b, ssm_w_in, ssm_conv_w, ssm_conv_b, ssm_dt_bias, ssm_a_log, ssm_d, ssm_norm_w, ssm_w_out, attn_w_qkv, attn_lam_q1, attn_lam_k1, attn_lam_q2, attn_lam_k2, attn_subln_w, attn_w_o, moe_w_group, moe_w_expert, moe_w_gate, moe_w_up, moe_w_down):
    batch, seq, d = x.shape
    t = batch * seq
    h = x.reshape(t, d)

    w_in = ssm_w_in[0].astype(BF16)
    w_zx = w_in[:, :SSM_ZX_DIM]

    zx = _matmul(h, w_zx, SSM_ZX_DIM, F32)
    yg = _ssd(zx, h, w_in[:, SSM_ZX_DIM:], ssm_conv_w[0], ssm_conv_b[0], ssm_dt_bias[0],
              ssm_a_log[0], ssm_d[0], ssm_norm_w[0], batch, seq)
    h, eid, wt = _mm_ln_router(yg, ssm_w_out[0].astype(BF16), h, ln_mix_g[0], ln_mix_b[0],
                               _router_weights(moe_w_group[0], moe_w_expert[0]), "mm_ssm_out_ln_router")
    h = _hier_moe_ln(h, eid, wt, moe_w_gate, moe_w_up, moe_w_down, ln_ffn_g[0], ln_ffn_b[0], 0)

    lambda_init = 0.8 - 0.6 * math.exp(-0.3 * 1)
    qkv = _qkv_rope(h, attn_w_qkv[0].astype(BF16), positions)
    o = _diff_attention(qkv, attn_lam_q1[0], attn_lam_k1[0], attn_lam_q2[0], attn_lam_k2[0],
                        attn_subln_w[0], lambda_init, batch, seq)
    h, eid, wt = _mm_ln_router(o, attn_w_o[0].astype(BF16), h, ln_mix_g[1], ln_mix_b[1],
                               _router_weights(moe_w_group[1], moe_w_expert[1]), "mm_attn_out_ln_router")
    h = _hier_moe_ln(h, eid, wt, moe_w_gate, moe_w_up, moe_w_down, ln_ffn_g[1], ln_ffn_b[1], 1)
    return h.reshape(batch, seq, d)
```

```python
import functools
import math

import jax
import jax.numpy as jnp
from jax import lax
from jax.experimental import pallas as pl
from jax.experimental.pallas import tpu as pltpu

F32 = jnp.float32
BF16 = jnp.bfloat16
I32 = jnp.int32

D_MODEL = 1024
DEPTH = 2
SSM_D_INNER = 2048
SSM_HEADDIM = 64
SSM_N_HEADS = 32
SSM_N_GROUPS = 8
SSM_HEADS_PER_GROUP = 4
SSM_D_STATE = 128
SSM_CONV_K = 4
SSM_CHUNK = 128
SSM_GROUP_WIDTH = SSM_HEADS_PER_GROUP * SSM_HEADDIM
SSM_ZX_DIM = 2 * SSM_D_INNER + 2 * SSM_N_GROUPS * SSM_D_STATE
ATTN_HEAD_DIM = 64
ATTN_N_HEADS = 8
ATTN_V_DIM = 128
ROT_DIM = 16
ROPE_THETA = 500000.0
MOE_GROUPS = 4
MOE_EXPERTS_PER_GROUP = 8
MOE_N_EXPERTS = 32
MOE_TOP_K = 2
MOE_D_FF = 512
DEEPNORM_ALPHA = (2 * DEPTH) ** 0.25
NORM_EPS = 1e-5

LANES = 128
ROW_SLABS = D_MODEL // LANES
CONV_TAIL = 8

MM_TM = 512
MM_TM_WIDE = 256
MM_TN = 1024
FFN_BLK = 512
DSP_TM = 512
CMB_TM = 256
ATT_TQ = 512


def _arb(n):
    return pltpu.CompilerParams(dimension_semantics=("arbitrary",) * n,
                                vmem_limit_bytes=56 * 1024 * 1024)


def _sigmoid(x):
    return 1.0 / (1.0 + jnp.exp(-x))


def _silu(x):
    return x * _sigmoid(x)


def _softplus(x):
    return jnp.maximum(x, 0.0) + jnp.log(1.0 + jnp.exp(-jnp.abs(x)))


def _layer_norm(y, g, b):
    mu = jnp.mean(y, axis=-1, keepdims=True)
    d = y - mu
    var = jnp.mean(d * d, axis=-1, keepdims=True)
    return d * lax.rsqrt(var + NORM_EPS) * g + b


def _split3(a):
    a1 = a.astype(BF16)
    r1 = a - a1.astype(F32)
    a2 = r1.astype(BF16)
    a3 = (r1 - a2.astype(F32)).astype(BF16)
    return a1, a2, a3


def _dot(a, b):
    return jnp.dot(a, b, preferred_element_type=F32)


def _dot_sel(a, sel):
    a1, a2, a3 = _split3(a)
    return _dot(a1, sel) + _dot(a2, sel) + _dot(a3, sel)


def _dot_f32(a, b):
    a1, a2, a3 = _split3(a)
    b1, b2, b3 = _split3(b)
    return (_dot(a1, b1) + _dot(a1, b2) + _dot(a2, b1)
            + _dot(a2, b2) + _dot(a1, b3) + _dot(a3, b1))


def _mm_kernel(x_ref, w_ref, o_ref):
    xb = x_ref[...].astype(BF16)
    for j in range(o_ref.shape[1] // MM_TN):
        cols = slice(j * MM_TN, (j + 1) * MM_TN)
        o_ref[:, cols] = _dot(xb, w_ref[:, cols]).astype(o_ref.dtype)


def _matmul(x, w, n_cols, out_dtype):
    m, k = x.shape
    tm = min(MM_TM_WIDE, m)
    return pl.pallas_call(
        _mm_kernel,
        grid=(m // tm,),
        in_specs=[pl.BlockSpec((tm, k), lambda i: (i, 0)),
                  pl.BlockSpec((k, n_cols), lambda i: (0, 0))],
        out_specs=pl.BlockSpec((tm, n_cols), lambda i: (i, 0)),
        out_shape=jax.ShapeDtypeStruct((m, n_cols), out_dtype),
        compiler_params=_arb(1),
        name="mm_inproj",
    )(x, w)


def _ssd_kernel(z_ref, xs_ref, bc_ref, x_ref, wdt_ref, cw_ref, cb_ref, dtb_ref, alog_ref, dskip_ref,
                nw_ref, e64_ref, e128_ref, o_ref,
                convbuf, xc, bcs, ccs, state, ybuf, dt_s, ac_s, act_s):
    G, W, N = SSM_N_GROUPS, SSM_GROUP_WIDTH, SSM_D_STATE
    L = z_ref.shape[0]
    c = pl.program_id(1)

    @pl.when(c == 0)
    def _():
        convbuf[0:CONV_TAIL, :] = jnp.zeros((CONV_TAIL, convbuf.shape[1]), F32)
        state[...] = jnp.zeros(state.shape, F32)

    convbuf[CONV_TAIL:CONV_TAIL + L, 0:SSM_D_INNER] = xs_ref[...]
    convbuf[CONV_TAIL:CONV_TAIL + L, SSM_D_INNER:] = bc_ref[...]

    def conv(c0, w):
        acc = cb_ref[:, c0:c0 + w]
        for k in range(SSM_CONV_K):
            r0 = CONV_TAIL - (SSM_CONV_K - 1) + k
            acc = acc + cw_ref[k:k + 1, c0:c0 + w] * convbuf[r0:r0 + L, c0:c0 + w]
        return _silu(acc)

    for g in range(G):
        xc[g] = conv(g * W, W)
        bcs[g] = conv(SSM_D_INNER + g * N, N).astype(BF16)
        ccs[g] = conv(SSM_D_INNER + G * N + g * N, N).astype(BF16)
    convbuf[0:CONV_TAIL, :] = convbuf[L:L + CONV_TAIL, :]

    dt = _softplus(_dot(x_ref[...].astype(BF16), wdt_ref[...].astype(BF16)) + dtb_ref[...])
    a = -jnp.exp(alog_ref[...])
    row_i = lax.broadcasted_iota(I32, (L, L), 0)
    col_i = lax.broadcasted_iota(I32, (L, L), 1)
    causal = row_i >= col_i
    a1, a2, a3 = _split3(dt * a)
    tri = causal.astype(BF16)
    ac = _dot(tri, a1) + _dot(tri, a2) + _dot(tri, a3)
    dt_s[...] = dt
    ac_s[...] = ac
    act_s[...] = ac.T
    lane = lax.broadcasted_iota(I32, (L, LANES), 1)
    dsk8 = jnp.broadcast_to(dskip_ref[...], (8, LANES))

    def group_body(g, carry):
        xg = xc[g]
        bg = bcs[g]
        cg = ccs[g]
        e64 = e64_ref[g]
        a_e = _dot_sel(ac_s[...], e64)
        dt_e = _dot_sel(dt_s[...], e64)
        dsk_e = _dot_sel(dsk8, e64)[0:1, :]
        al_e = a_e[L - 1:L, :]
        xdt = xg * dt_e
        cb = lax.dot_general(cg, bg, (((1,), (1,)), ((), ())), preferred_element_type=F32)
        col4 = _dot_sel(ac_s[...], e128_ref[g])
        xdt_b = xdt.astype(BF16)
        halves = []
        for p in range(2):
            xp = xdt_b[:, LANES * p:LANES * (p + 1)]
            ys = []
            for q in range(2):
                r = 2 * p + q
                row = act_s[pl.ds(SSM_HEADS_PER_GROUP * g + r, 1), :]
                seg = col4[:, LANES * r:LANES * (r + 1)] - row
                dec = jnp.where(causal, jnp.exp(seg), 0.0)
                ys.append(_dot((cb * dec).astype(BF16), xp))
            halves.append(jnp.where(lane < SSM_HEADDIM, ys[0], ys[1]))
        y_diag = jnp.concatenate(halves, axis=1)
        s_prev = state[g]
        y_off = _dot(cg, s_prev.astype(BF16)) * jnp.exp(a_e)
        new = lax.dot_general(bg, (xdt * jnp.exp(al_e - a_e)).astype(BF16),
                              (((0,), (0,)), ((), ())), preferred_element_type=F32)
        state[g] = s_prev * jnp.exp(al_e) + new
        ybuf[g] = y_diag + y_off + xg * dsk_e
        return carry

    lax.fori_loop(0, G, group_body, 0)

    for g in range(G):
        yg = ybuf[g] * _silu(z_ref[:, g * W:(g + 1) * W])
        ms = jnp.mean(yg * yg, axis=-1, keepdims=True)
        o_ref[:, g * W:(g + 1) * W] = (yg * lax.rsqrt(ms + NORM_EPS)
                                       * nw_ref[:, g * W:(g + 1) * W]).astype(o_ref.dtype)


def _head_selectors():
    G, R = SSM_N_GROUPS, SSM_HEADS_PER_GROUP
    h = jnp.arange(LANES)[None, :, None]
    g = jnp.arange(G)[:, None, None]
    j64 = jnp.arange(SSM_GROUP_WIDTH)[None, None, :]
    j128 = jnp.arange(R * LANES)[None, None, :]
    e64 = (h == g * R + j64 // SSM_HEADDIM).astype(BF16)
    e128 = (h == g * R + j128 // LANES).astype(BF16)
    return e64, e128


def _ssd(zx, x2, w_dt, conv_w, conv_b, dt_bias, a_log, d_skip, norm_w, batch, seq):
    L, G, W, N = SSM_CHUNK, SSM_N_GROUPS, SSM_GROUP_WIDTH, SSM_D_STATE
    nc = seq // L
    t = batch * seq
    pad = LANES - SSM_N_HEADS
    wdt = jnp.pad(w_dt, ((0, 0), (0, pad)))
    dtb = jnp.pad(dt_bias, (0, pad))[None, :]
    alog = jnp.pad(a_log, (0, pad))[None, :]
    dsk = jnp.pad(d_skip, (0, pad))[None, :]
    e64, e128 = _head_selectors()
    row = lambda b, c: (b * nc + c, 0)
    full2 = lambda b, c: (0, 0)
    full3 = lambda b, c: (0, 0, 0)
    conv_dim = conv_w.shape[1]
    return pl.pallas_call(
        _ssd_kernel,
        grid=(batch, nc),
        in_specs=[pl.BlockSpec((L, SSM_D_INNER), row),
                  pl.BlockSpec((L, SSM_D_INNER), lambda b, c: (b * nc + c, 1)),
                  pl.BlockSpec((L, SSM_D_INNER), lambda b, c: (b * nc + c, 2)),
                  pl.BlockSpec((L, D_MODEL), row),
                  pl.BlockSpec((D_MODEL, LANES), full2),
                  pl.BlockSpec((SSM_CONV_K, conv_dim), full2),
                  pl.BlockSpec((1, conv_dim), full2),
                  pl.BlockSpec((1, LANES), full2),
                  pl.BlockSpec((1, LANES), full2),
                  pl.BlockSpec((1, LANES), full2),
                  pl.BlockSpec((1, SSM_D_INNER), full2),
                  pl.BlockSpec((G, LANES, W), full3),
                  pl.BlockSpec((G, LANES, SSM_HEADS_PER_GROUP * LANES), full3)],
        out_specs=pl.BlockSpec((L, SSM_D_INNER), row),
        out_shape=jax.ShapeDtypeStruct((t, SSM_D_INNER), BF16),
        scratch_shapes=[pltpu.VMEM((CONV_TAIL + L, conv_dim), F32),
                        pltpu.VMEM((G, L, W), F32),
                        pltpu.VMEM((G, L, N), BF16),
                        pltpu.VMEM((G, L, N), BF16),
                        pltpu.VMEM((G, N, W), F32),
                        pltpu.VMEM((G, L, W), F32),
                        pltpu.VMEM((L, LANES), F32),
                        pltpu.VMEM((L, LANES), F32),
                        pltpu.VMEM((LANES, L), F32)],
        compiler_params=_arb(2),
        name="ssd_scan",
    )(zx, zx, zx, x2, wdt, conv_w, conv_b[None, :], dtb, alog, dsk, norm_w[None, :], e64, e128)


def _route(h, wr):
    tm = h.shape[0]
    logits = _dot(h.astype(BF16), wr)
    lane_i = lax.broadcasted_iota(I32, (tm, LANES), 1)
    lane = lane_i.astype(F32)
    neg = -jnp.inf
    big = float(LANES)

    def first_argmax(v, vmax):
        return jnp.min(jnp.where(v == vmax, lane, big), axis=-1, keepdims=True)

    gl = jnp.where((lane_i >= MOE_N_EXPERTS) & (lane_i < MOE_N_EXPERTS + MOE_GROUPS), logits, neg)
    gm = jnp.max(gl, axis=-1, keepdims=True)
    g_sel = first_argmax(gl, gm) - float(MOE_N_EXPERTS)
    g_gate = 1.0 / jnp.sum(jnp.exp(gl - gm), axis=-1, keepdims=True)
    lo = g_sel * float(MOE_EXPERTS_PER_GROUP)
    el = jnp.where((lane >= lo) & (lane < lo + float(MOE_EXPERTS_PER_GROUP)), logits, neg)
    m1 = jnp.max(el, axis=-1, keepdims=True)
    i1 = first_argmax(el, m1)
    el2 = jnp.where(lane == i1, neg, el)
    m2 = jnp.max(el2, axis=-1, keepdims=True)
    i2 = first_argmax(el2, m2)
    p2 = jnp.exp(m2 - m1)
    t1 = 1.0 / (1.0 + p2)
    t2 = p2 / (1.0 + p2)
    eid = jnp.where(lane_i == 0, i1, jnp.where(lane_i == 1, i2, 0.0)).astype(I32)
    wt = jnp.where(lane_i == 0, g_gate * t1, jnp.where(lane_i == 1, g_gate * t2, 0.0))
    return eid, wt


def _router_weights(w_group, w_expert):
    pad = LANES - MOE_N_EXPERTS - MOE_GROUPS
    return jnp.pad(jnp.concatenate([w_expert, w_group], axis=1), ((0, 0), (0, pad))).astype(BF16)


def _store_token_tiles(ref, v):
    rows = v.shape[0]
    for s in range(ROW_SLABS):
        ref[pl.ds(s, rows, stride=ROW_SLABS), :] = v[:, s * LANES:(s + 1) * LANES]


def _load_token_tiles(ref, rows):
    return jnp.concatenate([ref[pl.ds(s, rows, stride=ROW_SLABS), :] for s in range(ROW_SLABS)], axis=1)


def _mm_ln_router_kernel(x_ref, w_ref, r_ref, g_ref, b_ref, wr_ref, h_ref, ht_ref, eid_ref, wt_ref):
    y = DEEPNORM_ALPHA * r_ref[...] + _dot(x_ref[...].astype(BF16), w_ref[...])
    h = _layer_norm(y, g_ref[...], b_ref[...])
    h_ref[...] = h
    _store_token_tiles(ht_ref, h)
    eid, wt = _route(h, wr_ref[...])
    eid_ref[...] = eid
    wt_ref[...] = wt


def _mm_ln_router(x, w, resid, g, b, wr, name):
    m, k = x.shape
    d = w.shape[1]
    tm = min(MM_TM, m)
    row = lambda i: (i, 0)
    full = lambda i: (0, 0)
    return pl.pallas_call(
        _mm_ln_router_kernel,
        grid=(m // tm,),
        in_specs=[pl.BlockSpec((tm, k), row), pl.BlockSpec((k, d), full), pl.BlockSpec((tm, d), row),
                  pl.BlockSpec((1, d), full), pl.BlockSpec((1, d), full), pl.BlockSpec((d, LANES), full)],
        out_specs=[pl.BlockSpec((tm, d), row), pl.BlockSpec((tm * ROW_SLABS, LANES), row),
                   pl.BlockSpec((tm, LANES), row), pl.BlockSpec((tm, LANES), row)],
        out_shape=[jax.ShapeDtypeStruct((m, d), F32), jax.ShapeDtypeStruct((m * ROW_SLABS, LANES), F32),
                   jax.ShapeDtypeStruct((m, LANES), I32), jax.ShapeDtypeStruct((m, LANES), F32)],
        compiler_params=_arb(1),
        name=name,
    )(x, w, resid, g[None, :], b[None, :], wr)


def _dispatch_kernel(eid_ref, h_ref, dest_ref, be_ref, nu_ref, xs_hbm,
                     cnt, base, upper, zbuf, dst_v, dst_s, pe_v, pe_s, sem_z, sem_r, sem_s, *, blk):
    ph = pl.program_id(0)
    i = pl.program_id(1)
    tm = eid_ref.shape[0]
    n_e = LANES
    eid_t = eid_ref[...].astype(F32).T
    sub = lax.broadcasted_iota(I32, (n_e, tm), 0).astype(F32)
    oh = [(sub == eid_t[k:k + 1, :]).astype(F32) for k in range(MOE_TOP_K)]
    tot = [jnp.sum(o, axis=1, keepdims=True) for o in oh]

    @pl.when((ph == 0) & (i == 0))
    def _():
        cnt[...] = jnp.zeros(cnt.shape, F32)
        r_i = lax.broadcasted_iota(I32, (tm, tm), 0)
        c_i = lax.broadcasted_iota(I32, (tm, tm), 1)
        upper[...] = (r_i < c_i).astype(BF16)

    @pl.when(ph == 0)
    def _():
        cnt[...] += jnp.broadcast_to(tot[0] + tot[1], cnt.shape)

    @pl.when((ph == 1) & (i == 0))
    def _():
        counts = cnt[...]
        padded = jnp.floor((counts + float(blk - 1)) * (1.0 / blk)) * float(blk)
        r_i = lax.broadcasted_iota(I32, (n_e, n_e), 0)
        c_i = lax.broadcasted_iota(I32, (n_e, n_e), 1)
        tril = (r_i >= c_i).astype(BF16)
        p1, p2, p3 = _split3(padded)
        pends = _dot(tril, p1) + _dot(tril, p2) + _dot(tril, p3)
        base[...] = pends - padded
        nbp = be_ref.shape[1]
        blk_start = lax.broadcasted_iota(I32, (n_e, nbp), 1).astype(F32) * float(blk)
        is_e = lax.broadcasted_iota(I32, (n_e, nbp), 0) < MOE_N_EXPERTS
        done = jnp.where(is_e & (jnp.tile(pends, (1, nbp // LANES)) <= blk_start), 1.0, 0.0)
        be = jnp.minimum(jnp.sum(done, axis=0, keepdims=True), float(MOE_N_EXPERTS - 1))
        be_ref[...] = jnp.broadcast_to(be, be_ref.shape).astype(I32)
        last = pends[MOE_N_EXPERTS - 1:MOE_N_EXPERTS, :] * (1.0 / blk)
        nu_ref[...] = jnp.broadcast_to(last, nu_ref.shape).astype(I32)
        zbuf[...] = jnp.zeros(zbuf.shape, F32)
        row8 = lax.broadcasted_iota(I32, (8, LANES), 0)
        pe_v[...] = jnp.where(row8 == 0, pends.T[0:8, :], counts.T[0:8, :]).astype(I32)
        cp = pltpu.make_async_copy(pe_v, pe_s, sem_s)
        cp.start()
        cp.wait()

        brows = blk * ROW_SLABS

        def zero_copy(e):
            start = pl.multiple_of((pe_s[0, e] - blk) * ROW_SLABS, brows)
            return pltpu.make_async_copy(zbuf, xs_hbm.at[pl.ds(start, brows)], sem_z)

        def tail_copy(b):
            return pltpu.make_async_copy(zbuf, xs_hbm.at[pl.ds(pl.multiple_of(b * brows, brows), brows)], sem_z)

        n_used = lax.shift_right_logical(pe_s[0, MOE_N_EXPERTS - 1], blk.bit_length() - 1)
        n_blocks = xs_hbm.shape[0] // brows
        for e in range(MOE_N_EXPERTS):
            @pl.when(pe_s[1, e] > 0)
            def _():
                zero_copy(e).start()
        lax.fori_loop(n_used, n_blocks, lambda b, c: (tail_copy(b).start(), c)[1], 0)
        for e in range(MOE_N_EXPERTS):
            @pl.when(pe_s[1, e] > 0)
            def _():
                zero_copy(e).wait()
        lax.fori_loop(n_used, n_blocks, lambda b, c: (tail_copy(b).wait(), c)[1], 0)

    @pl.when(ph == 1)
    def _():
        b0 = base[:, 0:1]
        c0 = _dot(oh[0].astype(BF16), upper[...])
        c1 = _dot(oh[1].astype(BF16), upper[...])
        d0 = jnp.sum(oh[0] * (b0 + c0), axis=0, keepdims=True)
        d1 = jnp.sum(oh[1] * (b0 + tot[0] + c1), axis=0, keepdims=True)
        base[...] += jnp.broadcast_to(tot[0] + tot[1], base.shape)
        row8 = lax.broadcasted_iota(I32, (8, tm), 0)
        dst = jnp.where(row8 == 0, d0, jnp.where(row8 == 1, d1, 0.0)).astype(I32)
        dest_ref[...] = dst
        dst_v[...] = dst
        cp = pltpu.make_async_copy(dst_v, dst_s, sem_s)
        cp.start()
        cp.wait()

        for r in range(tm):
            for k in range(MOE_TOP_K):
                slot = pl.multiple_of(dst_s[k, r] * ROW_SLABS, ROW_SLABS)
                pltpu.make_async_copy(h_ref.at[pl.ds(r * ROW_SLABS, ROW_SLABS)],
                                      xs_hbm.at[pl.ds(slot, ROW_SLABS)], sem_r).start(priority=k)
        for k in range(MOE_TOP_K):
            pltpu.make_async_copy(h_ref, xs_hbm.at[pl.ds(0, tm * ROW_SLABS)], sem_r).wait()


def _moe_dispatch(ht, eid, blk, name):
    t = ht.shape[0] // ROW_SLABS
    assert blk & (blk - 1) == 0, "block size must be a power of two"
    tm = min(DSP_TM, t)
    nt = t // tm
    n_blocks = -(-(t * MOE_TOP_K) // blk) + MOE_N_EXPERTS
    nbp = -(-n_blocks // LANES) * LANES
    tile = lambda ph, i: (i * ph, 0)
    dest, be, nu, xs = pl.pallas_call(
        functools.partial(_dispatch_kernel, blk=blk),
        grid=(2, nt),
        in_specs=[pl.BlockSpec((tm, LANES), lambda ph, i: (i, 0)),
                  pl.BlockSpec((tm * ROW_SLABS, LANES), tile)],
        out_specs=[pl.BlockSpec((8, tm), lambda ph, i: (0, i * ph)),
                   pl.BlockSpec((8, nbp), lambda ph, i: (0, 0)),
                   pl.BlockSpec((8, LANES), lambda ph, i: (0, 0)),
                   pl.BlockSpec(memory_space=pl.ANY)],
        out_shape=[jax.ShapeDtypeStruct((8, t), I32), jax.ShapeDtypeStruct((8, nbp), I32),
                   jax.ShapeDtypeStruct((8, LANES), I32),
                   jax.ShapeDtypeStruct((n_blocks * blk * ROW_SLABS, LANES), F32)],
        scratch_shapes=[pltpu.VMEM((LANES, LANES), F32), pltpu.VMEM((LANES, LANES), F32),
                        pltpu.VMEM((tm, tm), BF16), pltpu.VMEM((blk * ROW_SLABS, LANES), F32),
                        pltpu.VMEM((8, tm), I32), pltpu.SMEM((8, tm), I32),
                        pltpu.VMEM((8, LANES), I32), pltpu.SMEM((8, LANES), I32),
                        pltpu.SemaphoreType.DMA(()), pltpu.SemaphoreType.DMA(()), pltpu.SemaphoreType.DMA(())],
        compiler_params=_arb(2),
        name=name,
    )(eid, ht)
    return dest, be[0, :n_blocks], nu[0, :1], xs, n_blocks


def _ffn_kernel(be_ref, nu_ref, x_ref, wg_ref, wu_ref, wd_ref, o_ref, wgb, wub, wdb):
    i = pl.program_id(0)

    @pl.when(i < nu_ref[0])
    def _():
        prev = be_ref[jnp.maximum(i - 1, 0)]

        @pl.when((i == 0) | (be_ref[i] != prev))
        def _():
            wgb[...] = wg_ref[0, 0].astype(BF16)
            wub[...] = wu_ref[0, 0].astype(BF16)
            wdb[...] = wd_ref[0, 0].astype(BF16)

        blk = x_ref.shape[0] // ROW_SLABS
        xb = _load_token_tiles(x_ref, blk).astype(BF16)
        hid = _silu(_dot(xb, wgb[...])) * _dot(xb, wub[...])
        _store_token_tiles(o_ref, _dot(hid.astype(BF16), wdb[...]))

    @pl.when(i >= nu_ref[0])
    def _():
        o_ref[...] = jnp.zeros(o_ref.shape, o_ref.dtype)


def _moe_ffn(xs, block_expert, n_used, n_blocks, w_gate, w_up, w_down, layer, blk, name):
    d, f = w_gate.shape[2], w_gate.shape[3]
    brows = blk * ROW_SLABS
    used = lambda i, be, nu: (jnp.minimum(i, nu[0] - 1), 0)
    every = lambda i, be, nu: (i, 0)
    grid_spec = pltpu.PrefetchScalarGridSpec(
        num_scalar_prefetch=2,
        grid=(n_blocks,),
        in_specs=[pl.BlockSpec((brows, LANES), used),
                  pl.BlockSpec((1, 1, d, f), lambda i, be, nu: (layer, be[i], 0, 0)),
                  pl.BlockSpec((1, 1, d, f), lambda i, be, nu: (layer, be[i], 0, 0)),
                  pl.BlockSpec((1, 1, f, d), lambda i, be, nu: (layer, be[i], 0, 0))],
        out_specs=pl.BlockSpec((brows, LANES), every),
        scratch_shapes=[pltpu.VMEM((d, f), BF16), pltpu.VMEM((d, f), BF16), pltpu.VMEM((f, d), BF16)])
    return pl.pallas_call(
        _ffn_kernel,
        grid_spec=grid_spec,
        out_shape=jax.ShapeDtypeStruct((n_blocks * brows, LANES), F32),
        compiler_params=_arb(1),
        name=name,
    )(block_expert, n_used, xs, w_gate, w_up, w_down)


def _combine_ln_kernel(dst_ref, dstn_ref, h_ref, wt_ref, g_ref, b_ref, yb_hbm, o_ref, ybuf, sem):
    i = pl.program_id(0)
    n = pl.num_programs(0)
    tm = h_ref.shape[0]

    def start_gather(dref, slot):
        for r in range(tm):
            for k in range(MOE_TOP_K):
                src = pl.multiple_of(dref[k, r] * ROW_SLABS, ROW_SLABS)
                pltpu.make_async_copy(yb_hbm.at[pl.ds(src, ROW_SLABS)],
                                      ybuf.at[slot, k, pl.ds(r * ROW_SLABS, ROW_SLABS)],
                                      sem.at[slot]).start(priority=k)

    @pl.when(i == 0)
    def _():
        start_gather(dst_ref, 0)

    @pl.when(i + 1 < n)
    def _():
        start_gather(dstn_ref, (i + 1) % 2)

    slot = i % 2
    for k in range(MOE_TOP_K):
        pltpu.make_async_copy(yb_hbm.at[pl.ds(0, tm * ROW_SLABS)], ybuf.at[slot, k], sem.at[slot]).wait()
    wt = wt_ref[...]
    ffn = (wt[:, 0:1] * _load_token_tiles(ybuf.at[slot, 0], tm)
           + wt[:, 1:2] * _load_token_tiles(ybuf.at[slot, 1], tm))
    o_ref[...] = _layer_norm(DEEPNORM_ALPHA * h_ref[...] + ffn, g_ref[...], b_ref[...])


def _combine_ln(h, wt, dest, yb, g, b, name):
    t, d = h.shape
    tm = min(CMB_TM, t)
    nblk = t // tm
    row = lambda i: (i, 0)
    full = lambda i: (0, 0)
    return pl.pallas_call(
        _combine_ln_kernel,
        grid=(nblk,),
        in_specs=[pl.BlockSpec((8, tm), lambda i: (0, i), memory_space=pltpu.SMEM),
                  pl.BlockSpec((8, tm), lambda i: (0, jnp.minimum(i + 1, nblk - 1)),
                               memory_space=pltpu.SMEM),
                  pl.BlockSpec((tm, d), row), pl.BlockSpec((tm, LANES), row),
                  pl.BlockSpec((1, d), full), pl.BlockSpec((1, d), full),
                  pl.BlockSpec(memory_space=pl.ANY)],
        out_specs=pl.BlockSpec((tm, d), row),
        out_shape=jax.ShapeDtypeStruct((t, d), F32),
        scratch_shapes=[pltpu.VMEM((2, MOE_TOP_K, tm * ROW_SLABS, LANES), F32), pltpu.SemaphoreType.DMA((2,))],
        compiler_params=_arb(1),
        name=name,
    )(dest, dest, h, wt, g[None, :], b[None, :], yb)


def _hier_moe_ln(h, ht, eid, wt, w_gate, w_up, w_down, g, b, layer):
    dest, block_expert, n_used, xs, n_blocks = _moe_dispatch(ht, eid, FFN_BLK, f"moe_dispatch{layer}")
    yb = _moe_ffn(xs, block_expert, n_used, n_blocks, w_gate, w_up, w_down, layer, FFN_BLK, f"moe_ffn{layer}")
    return _combine_ln(h, wt, dest, yb, g, b, f"moe_combine_ln{layer}")


def _qkv_rope_kernel(x_ref, w_ref, pos_ref, inv_ref, o_ref):
    xb = x_ref[...].astype(BF16)
    tm = xb.shape[0]
    n = D_MODEL
    ang = pos_ref[...].astype(F32) * inv_ref[...]
    lane = lax.broadcasted_iota(I32, (tm, LANES), 1)
    dd = lane & (ATTN_HEAD_DIM - 1)
    half = ROT_DIM // 2
    cosv = jnp.cos(ang)
    sinv = jnp.sin(ang)
    c_t = jnp.where(dd < ROT_DIM, cosv, 1.0)
    s_up = jnp.where(dd < half, -sinv, 0.0)
    s_dn = jnp.where((dd >= half) & (dd < ROT_DIM), sinv, 0.0)
    for j, sc in ((0, ATTN_HEAD_DIM ** -0.5 * math.log2(math.e)), (1, 1.0)):
        acc = _dot(xb, w_ref[:, j * n:(j + 1) * n])
        c_j, up_j, dn_j = c_t * sc, s_up * sc, s_dn * sc
        for blk in range(n // LANES):
            tt = acc[:, blk * LANES:(blk + 1) * LANES]
            out = tt * c_j + pltpu.roll(tt, LANES - half, 1) * up_j + pltpu.roll(tt, half, 1) * dn_j
            o_ref[:, j * n + blk * LANES:j * n + (blk + 1) * LANES] = out.astype(o_ref.dtype)
    o_ref[:, 2 * n:3 * n] = _dot(xb, w_ref[:, 2 * n:3 * n]).astype(o_ref.dtype)


def _rope_inv_table():
    inv = ROPE_THETA ** (-jnp.arange(0, ROT_DIM, 2, dtype=F32) / ROT_DIM)
    head = jnp.concatenate([inv, inv, jnp.zeros((ATTN_HEAD_DIM - ROT_DIM,), F32)])
    return jnp.tile(head, LANES // ATTN_HEAD_DIM)[None, :]


def _qkv_rope(h, w_qkv, positions):
    m, k = h.shape
    n = w_qkv.shape[1]
    tm = min(MM_TM, m)
    pos = positions.reshape(m, 1)
    return pl.pallas_call(
        _qkv_rope_kernel,
        grid=(m // tm,),
        in_specs=[pl.BlockSpec((tm, k), lambda i: (i, 0)),
                  pl.BlockSpec((k, n), lambda i: (0, 0)),
                  pl.BlockSpec((tm, 1), lambda i: (i, 0)),
                  pl.BlockSpec((1, LANES), lambda i: (0, 0))],
        out_specs=pl.BlockSpec((tm, n), lambda i: (i, 0)),
        out_shape=jax.ShapeDtypeStruct((m, n), BF16),
        compiler_params=_arb(1),
        name="mm_qkv_rope",
    )(h, w_qkv, pos, _rope_inv_table())


def _attn_kernel(q_ref, k_ref, v_ref, lq1_ref, lk1_ref, lq2_ref, lk2_ref, sw_ref, o_ref, *, lambda_init):
    i = pl.program_id(2)
    tq = q_ref.shape[0]
    q = q_ref[...]
    lane = lax.broadcasted_iota(I32, (tq, LANES), 1)
    zero = jnp.zeros((), q.dtype)
    qs = (jnp.where(lane < ATTN_HEAD_DIM, q, zero), jnp.where(lane >= ATTN_HEAD_DIM, q, zero))
    row_i = lax.broadcasted_iota(I32, (tq, tq), 0)
    col_i = lax.broadcasted_iota(I32, (tq, tq), 1)
    causal = row_i >= col_i

    def step(kb, vb, carry, masked):
        out = []
        for c in range(2):
            m, l, acc = carry[c]
            s = lax.dot_general(qs[c], kb, (((1,), (1,)), ((), ())), preferred_element_type=F32)
            if masked:
                s = jnp.where(causal, s, -jnp.inf)
            mn = jnp.maximum(m, jnp.max(s, axis=-1, keepdims=True))
            alpha = jnp.exp2(m - mn)
            p = jnp.exp2(s - mn)
            l = alpha * l + jnp.sum(p, axis=-1, keepdims=True)
            acc = alpha * acc + _dot(p.astype(BF16), vb)
            out.append((mn, l, acc))
        return tuple(out)

    def kv_step(width, masked):
        def body(off, carry):
            off = pl.multiple_of(off, tq)
            return step(k_ref[pl.ds(off, width), :], v_ref[pl.ds(off, width), :], carry, masked)
        return body

    init = tuple((jnp.full((tq, 1), -jnp.inf, F32), jnp.zeros((tq, 1), F32), jnp.zeros((tq, LANES), F32))
                 for _ in range(2))
    wide = kv_step(2 * tq, False)
    carry = lax.fori_loop(0, i // 2, lambda j, c: wide(j * (2 * tq), c), init)
    single = kv_step(tq, False)
    carry = lax.fori_loop(0, i % 2, lambda j, c: single((i - 1) * tq, c), carry)
    carry = kv_step(tq, True)(i * tq, carry)

    lam = (jnp.exp(jnp.sum(lq1_ref[...] * lk1_ref[...], axis=-1, keepdims=True))
           - jnp.exp(jnp.sum(lq2_ref[...] * lk2_ref[...], axis=-1, keepdims=True)) + lambda_init)
    (_, l1, a1), (_, l2, a2) = carry
    o = a1 / l1 - lam * (a2 / l2)
    o = o * lax.rsqrt(jnp.mean(o * o, axis=-1, keepdims=True) + NORM_EPS)
    o_ref[...] = (o * sw_ref[...] * (1.0 - lambda_init)).astype(o_ref.dtype)


def _diff_attention(qkv, lq1, lk1, lq2, lk2, subln_w, lambda_init, batch, seq):
    t = batch * seq
    tq = min(ATT_TQ, seq)
    nq = seq // tq
    h_n = ATTN_N_HEADS
    vec = lambda b, h, i: (0, 0)
    return pl.pallas_call(
        functools.partial(_attn_kernel, lambda_init=lambda_init),
        grid=(batch, h_n, nq),
        in_specs=[pl.BlockSpec((tq, LANES), lambda b, h, i: (b * nq + i, h)),
                  pl.BlockSpec((seq, LANES), lambda b, h, i: (b, h_n + h)),
                  pl.BlockSpec((seq, LANES), lambda b, h, i: (b, 2 * h_n + h)),
                  pl.BlockSpec((1, ATTN_HEAD_DIM), vec), pl.BlockSpec((1, ATTN_HEAD_DIM), vec),
                  pl.BlockSpec((1, ATTN_HEAD_DIM), vec), pl.BlockSpec((1, ATTN_HEAD_DIM), vec),
                  pl.BlockSpec((1, ATTN_V_DIM), vec)],
        out_specs=pl.BlockSpec((tq, LANES), lambda b, h, i: (b * nq + i, h)),
        out_shape=jax.ShapeDtypeStruct((t, h_n * ATTN_V_DIM), BF16),
        compiler_params=_arb(3),
        name="diff_attn",
    )(qkv, qkv, qkv, lq1[None, :], lk1[None, :], lq2[None, :], lk2[None, :], subln_w[None, :])


def kernel(x, positions, ln_mix_g, ln_mix_b, ln_ffn_g, ln_ffn_b, ssm_w_in, ssm_conv_w, ssm_conv_b, ssm_dt_bias, ssm_a_log, ssm_d, ssm_norm_w, ssm_w_out, attn_w_qkv, attn_lam_q1, attn_lam_k1, attn_lam_q2, attn_lam_k2, attn_subln_w, attn_w_o, moe_w_group, moe_w_expert, moe_w_gate, moe_w_up, moe_w_down):
    batch, seq, d = x.shape
    t = batch * seq
    h = x.reshape(t, d)

    w_in = ssm_w_in[0].astype(BF16)
    w_zx = w_in[:, :SSM_ZX_DIM]

    zx = _matmul(h, w_zx, SSM_ZX_DIM, F32)
    yg = _ssd(zx, h, w_in[:, SSM_ZX_DIM:], ssm_conv_w[0], ssm_conv_b[0], ssm_dt_bias[0],
              ssm_a_log[0], ssm_d[0], ssm_norm_w[0], batch, seq)
    h, ht, eid, wt = _mm_ln_router(yg, ssm_w_out[0].astype(BF16), h, ln_mix_g[0], ln_mix_b[0],
                               _router_weights(moe_w_group[0], moe_w_expert[0]), "mm_ssm_out_ln_router")
    h = _hier_moe_ln(h, ht, eid, wt, moe_w_gate, moe_w_up, moe_w_down, ln_ffn_g[0], ln_ffn_b[0], 0)

    lambda_init = 0.8 - 0.6 * math.exp(-0.3 * 1)
    qkv = _qkv_rope(h, attn_w_qkv[0].astype(BF16), positions)
    o = _diff_attention(qkv, attn_lam_q1[0], attn_lam_k1[0], attn_lam_q2[0], attn_lam_k2[0],
                        attn_subln_w[0], lambda_init, batch, seq)
    h, ht, eid, wt = _mm_ln_router(o, attn_w_o[0].astype(BF16), h, ln_mix_g[1], ln_mix_b[1],
                               _router_weights(moe_w_group[1], moe_w_expert[1]), "mm_attn_out_ln_router")
    h = _hier_moe_ln(h, ht, eid, wt, moe_w_gate, moe_w_up, moe_w_down, ln_ffn_g[1], ln_ffn_b[1], 1)
    return h.reshape(batch, seq, d)
```

```python
import functools
import math

import jax
import jax.numpy as jnp
from jax import lax
from jax.experimental import pallas as pl
from jax.experimental.pallas import tpu as pltpu

F32 = jnp.float32
BF16 = jnp.bfloat16
I32 = jnp.int32

D_MODEL = 1024
DEPTH = 2
SSM_D_INNER = 2048
SSM_HEADDIM = 64
SSM_N_HEADS = 32
SSM_N_GROUPS = 8
SSM_HEADS_PER_GROUP = 4
SSM_D_STATE = 128
SSM_CONV_K = 4
SSM_CHUNK = 128
SSM_GROUP_WIDTH = SSM_HEADS_PER_GROUP * SSM_HEADDIM
SSM_ZX_DIM = 2 * SSM_D_INNER + 2 * SSM_N_GROUPS * SSM_D_STATE
ATTN_HEAD_DIM = 64
ATTN_N_HEADS = 8
ATTN_V_DIM = 128
ROT_DIM = 16
ROPE_THETA = 500000.0
MOE_GROUPS = 4
MOE_EXPERTS_PER_GROUP = 8
MOE_N_EXPERTS = 32
MOE_TOP_K = 2
MOE_D_FF = 512
DEEPNORM_ALPHA = (2 * DEPTH) ** 0.25
NORM_EPS = 1e-5

LANES = 128
ROW_SLABS = D_MODEL // LANES
CONV_TAIL = 8
CONV_ROWS = 32

MM_TM = 512
MM_TM_WIDE = 256
MM_TN = 1024
FFN_BLK = 512
DSP_TM = 512
CMB_TM = 256
ATT_TQ = 512


def _arb(n):
    return pltpu.CompilerParams(dimension_semantics=("arbitrary",) * n,
                                vmem_limit_bytes=56 * 1024 * 1024)


def _sigmoid(x):
    return 1.0 / (1.0 + jnp.exp(-x))


def _silu(x):
    return x * _sigmoid(x)


def _softplus(x):
    return jnp.maximum(x, 0.0) + jnp.log(1.0 + jnp.exp(-jnp.abs(x)))


def _layer_norm(y, g, b):
    mu = jnp.mean(y, axis=-1, keepdims=True)
    d = y - mu
    var = jnp.mean(d * d, axis=-1, keepdims=True)
    return d * lax.rsqrt(var + NORM_EPS) * g + b


def _split3(a):
    a1 = a.astype(BF16)
    r1 = a - a1.astype(F32)
    a2 = r1.astype(BF16)
    a3 = (r1 - a2.astype(F32)).astype(BF16)
    return a1, a2, a3


def _dot(a, b):
    return jnp.dot(a, b, preferred_element_type=F32)


def _dot_sel(a, sel):
    a1, a2, a3 = _split3(a)
    return _dot(a1, sel) + _dot(a2, sel) + _dot(a3, sel)


def _dot_f32(a, b):
    a1, a2, a3 = _split3(a)
    b1, b2, b3 = _split3(b)
    return (_dot(a1, b1) + _dot(a1, b2) + _dot(a2, b1)
            + _dot(a2, b2) + _dot(a1, b3) + _dot(a3, b1))


def _inproj_conv_kernel(x_ref, w_ref, cw_ref, cb_ref, z_ref, xs_ref, bc_ref, *scratch, tiles_per_seq):
    n_conv = (SSM_ZX_DIM - SSM_D_INNER) // MM_TN
    stages = scratch
    tm = x_ref.shape[0]
    xb = x_ref[...].astype(BF16)

    first = pl.program_id(0) % tiles_per_seq == 0

    @pl.when(first)
    def _():
        for stage in stages:
            stage[0:CONV_TAIL, :] = jnp.zeros((CONV_TAIL, MM_TN), F32)

    @pl.when(jnp.logical_not(first))
    def _():
        for stage in stages:
            stage[0:CONV_TAIL, :] = stage[tm:tm + CONV_TAIL, :]

    for j in range(n_conv):
        c0 = SSM_D_INNER + j * MM_TN
        stages[j][CONV_TAIL:CONV_TAIL + tm, :] = _dot(xb, w_ref[:, c0:c0 + MM_TN])
    for j in range(SSM_D_INNER // MM_TN):
        cols = slice(j * MM_TN, (j + 1) * MM_TN)
        z_ref[:, cols] = _dot(xb, w_ref[:, cols])
    for j in range(n_conv):
        c0 = j * MM_TN
        stage = stages[j]
        w_taps = [cw_ref[k:k + 1, c0:c0 + MM_TN] for k in range(SSM_CONV_K)]
        bias = cb_ref[:, c0:c0 + MM_TN]
        for rb in range(tm // CONV_ROWS):
            r0 = rb * CONV_ROWS
            u = stage[r0:r0 + CONV_ROWS + CONV_TAIL, :]
            out = bias + w_taps[SSM_CONV_K - 1] * u[CONV_TAIL:, :]
            for k in range(SSM_CONV_K - 1):
                back = SSM_CONV_K - 1 - k
                shifted = pltpu.roll(u, back, 0)[CONV_TAIL:, :]
                out = out + w_taps[k] * shifted
            out = _silu(out)
            if c0 < SSM_D_INNER:
                xs_ref[r0:r0 + CONV_ROWS, c0:c0 + MM_TN] = out
            else:
                bc_ref[r0:r0 + CONV_ROWS, c0 - SSM_D_INNER:c0 - SSM_D_INNER + MM_TN] = out.astype(bc_ref.dtype)


def _inproj_conv(x, w, conv_w, conv_b, seq):
    m, k = x.shape
    tm = min(MM_TM_WIDE, seq)
    n_conv = (SSM_ZX_DIM - SSM_D_INNER) // MM_TN
    conv_dim = conv_w.shape[1]
    row = lambda i: (i, 0)
    full = lambda i: (0, 0)
    return pl.pallas_call(
        functools.partial(_inproj_conv_kernel, tiles_per_seq=seq // tm),
        grid=(m // tm,),
        in_specs=[pl.BlockSpec((tm, k), row), pl.BlockSpec((k, SSM_ZX_DIM), full),
                  pl.BlockSpec((SSM_CONV_K, conv_dim), full), pl.BlockSpec((1, conv_dim), full)],
        out_specs=[pl.BlockSpec((tm, SSM_D_INNER), row), pl.BlockSpec((tm, SSM_D_INNER), row),
                   pl.BlockSpec((tm, conv_dim - SSM_D_INNER), row)],
        out_shape=[jax.ShapeDtypeStruct((m, SSM_D_INNER), F32), jax.ShapeDtypeStruct((m, SSM_D_INNER), F32),
                   jax.ShapeDtypeStruct((m, conv_dim - SSM_D_INNER), BF16)],
        scratch_shapes=[pltpu.VMEM((CONV_TAIL + tm, MM_TN), F32) for _ in range(n_conv)],
        compiler_params=_arb(1),
        name="mm_inproj_conv",
    )(x, w, conv_w, conv_b[None, :])


def _ssd_kernel(z_ref, xs_ref, bc_ref, x_ref, wdt_ref, dtb_ref, alog_ref, dskip_ref,
                nw_ref, e64_ref, e128_ref, o_ref, state, acp_s, dtp_s, act_s):
    G, W, N = SSM_N_GROUPS, SSM_GROUP_WIDTH, SSM_D_STATE
    L = z_ref.shape[0]
    c = pl.program_id(1)

    @pl.when(c == 0)
    def _():
        state[...] = jnp.zeros(state.shape, F32)

    def x_of(g):
        return xs_ref[:, g * W:(g + 1) * W]

    def b_of(g):
        return bc_ref[:, g * N:(g + 1) * N]

    def c_of(g):
        return bc_ref[:, G * N + g * N:G * N + (g + 1) * N]

    dt = _softplus(_dot(x_ref[...].astype(BF16), wdt_ref[...].astype(BF16)) + dtb_ref[...])
    a = -jnp.exp(alog_ref[...])
    row_i = lax.broadcasted_iota(I32, (L, L), 0)
    col_i = lax.broadcasted_iota(I32, (L, L), 1)
    causal = row_i >= col_i
    a1, a2, a3 = _split3(dt * a)
    tri = causal.astype(BF16)
    ac = _dot(tri, a1) + _dot(tri, a2) + _dot(tri, a3)
    act_s[...] = ac.T
    for n, part in enumerate(_split3(ac)):
        acp_s[n] = part
    for n, part in enumerate(_split3(dt)[:2]):
        dtp_s[n] = part
    lane = lax.broadcasted_iota(I32, (L, LANES), 1)
    first_half = lane < SSM_HEADDIM

    def select(parts_ref, n_parts, sel):
        out = _dot(parts_ref[0], sel)
        for n in range(1, n_parts):
            out = out + _dot(parts_ref[n], sel)
        return out

    gs = range(G)
    col4 = [select(acp_s, 3, e128_ref[g]) for g in gs]
    dt_e = [select(dtp_s, 2, e64_ref[g]) for g in gs]
    cb = [lax.dot_general(c_of(g), b_of(g), (((1,), (1,)), ((), ())), preferred_element_type=F32)
          for g in gs]
    y_cs = [_dot(c_of(g), state[g].astype(BF16)) for g in gs]
    a_e = [jnp.concatenate(
        [jnp.where(first_half, c4[:, 2 * p * LANES:(2 * p + 1) * LANES],
                   c4[:, (2 * p + 1) * LANES:(2 * p + 2) * LANES]) for p in range(2)], axis=1)
        for c4 in col4]
    xdt = [x_of(g) * dt_e[g] for g in gs]
    xdt_b = [v.astype(BF16) for v in xdt]
    ys = [[None] * SSM_HEADS_PER_GROUP for _ in gs]
    for r in range(SSM_HEADS_PER_GROUP):
        for g in gs:
            row = act_s[SSM_HEADS_PER_GROUP * g + r:SSM_HEADS_PER_GROUP * g + r + 1, :]
            seg = col4[g][:, LANES * r:LANES * (r + 1)] - row
            dec = jnp.where(causal, jnp.exp(seg), 0.0)
            xp = xdt_b[g][:, LANES * (r // 2):LANES * (r // 2 + 1)]
            ys[g][r] = _dot((cb[g] * dec).astype(BF16), xp)
    for g in gs:
        y_diag = jnp.concatenate([jnp.where(first_half, ys[g][2 * p], ys[g][2 * p + 1])
                                  for p in range(2)], axis=1)
        al_e = a_e[g][L - 1:L, :]
        new = lax.dot_general(b_of(g), (xdt[g] * jnp.exp(al_e - a_e[g])).astype(BF16),
                              (((0,), (0,)), ((), ())), preferred_element_type=F32)
        y = y_diag + y_cs[g] * jnp.exp(a_e[g]) + x_of(g) * dskip_ref[g]
        state[g] = state[g] * jnp.exp(al_e) + new
        yg = y * _silu(z_ref[:, g * W:(g + 1) * W])
        ms = jnp.mean(yg * yg, axis=-1, keepdims=True)
        o_ref[:, g * W:(g + 1) * W] = (yg * lax.rsqrt(ms + NORM_EPS)
                                       * nw_ref[:, g * W:(g + 1) * W]).astype(o_ref.dtype)


def _head_selectors():
    G, R = SSM_N_GROUPS, SSM_HEADS_PER_GROUP
    h = jnp.arange(LANES)[None, :, None]
    g = jnp.arange(G)[:, None, None]
    j64 = jnp.arange(SSM_GROUP_WIDTH)[None, None, :]
    j128 = jnp.arange(R * LANES)[None, None, :]
    e64 = (h == g * R + j64 // SSM_HEADDIM).astype(BF16)
    e128 = (h == g * R + j128 // LANES).astype(BF16)
    return e64, e128


def _ssd(zs, xs, bc, x2, w_dt, dt_bias, a_log, d_skip, norm_w, batch, seq):
    L, G, W, N = SSM_CHUNK, SSM_N_GROUPS, SSM_GROUP_WIDTH, SSM_D_STATE
    nc = seq // L
    t = batch * seq
    pad = LANES - SSM_N_HEADS
    wdt = jnp.pad(w_dt, ((0, 0), (0, pad)))
    dtb = jnp.pad(dt_bias, (0, pad))[None, :]
    alog = jnp.pad(a_log, (0, pad))[None, :]
    dsk = jnp.repeat(d_skip, SSM_HEADDIM).reshape(G, 1, W)
    e64, e128 = _head_selectors()
    row = lambda b, c: (b * nc + c, 0)
    full2 = lambda b, c: (0, 0)
    full3 = lambda b, c: (0, 0, 0)
    return pl.pallas_call(
        _ssd_kernel,
        grid=(batch, nc),
        in_specs=[pl.BlockSpec((L, SSM_D_INNER), row),
                  pl.BlockSpec((L, SSM_D_INNER), row),
                  pl.BlockSpec((L, 2 * G * N), row),
                  pl.BlockSpec((L, D_MODEL), row),
                  pl.BlockSpec((D_MODEL, LANES), full2),
                  pl.BlockSpec((1, LANES), full2),
                  pl.BlockSpec((1, LANES), full2),
                  pl.BlockSpec((G, 1, W), full3),
                  pl.BlockSpec((1, SSM_D_INNER), full2),
                  pl.BlockSpec((G, LANES, W), full3),
                  pl.BlockSpec((G, LANES, SSM_HEADS_PER_GROUP * LANES), full3)],
        out_specs=pl.BlockSpec((L, SSM_D_INNER), row),
        out_shape=jax.ShapeDtypeStruct((t, SSM_D_INNER), BF16),
        scratch_shapes=[pltpu.VMEM((G, N, W), F32),
                        pltpu.VMEM((3, L, LANES), BF16),
                        pltpu.VMEM((2, L, LANES), BF16),
                        pltpu.VMEM((LANES, L), F32)],
        compiler_params=_arb(2),
        name="ssd_scan",
    )(zs, xs, bc, x2, wdt, dtb, alog, dsk, norm_w[None, :], e64, e128)


def _route(h, wr):
    tm = h.shape[0]
    logits = _dot(h.astype(BF16), wr)
    lane_i = lax.broadcasted_iota(I32, (tm, LANES), 1)
    lane = lane_i.astype(F32)
    neg = -jnp.inf
    big = float(LANES)

    def first_argmax(v, vmax):
        return jnp.min(jnp.where(v == vmax, lane, big), axis=-1, keepdims=True)

    gl = jnp.where((lane_i >= MOE_N_EXPERTS) & (lane_i < MOE_N_EXPERTS + MOE_GROUPS), logits, neg)
    gm = jnp.max(gl, axis=-1, keepdims=True)
    g_sel = first_argmax(gl, gm) - float(MOE_N_EXPERTS)
    g_gate = 1.0 / jnp.sum(jnp.exp(gl - gm), axis=-1, keepdims=True)
    lo = g_sel * float(MOE_EXPERTS_PER_GROUP)
    el = jnp.where((lane >= lo) & (lane < lo + float(MOE_EXPERTS_PER_GROUP)), logits, neg)
    m1 = jnp.max(el, axis=-1, keepdims=True)
    i1 = first_argmax(el, m1)
    el2 = jnp.where(lane == i1, neg, el)
    m2 = jnp.max(el2, axis=-1, keepdims=True)
    i2 = first_argmax(el2, m2)
    p2 = jnp.exp(m2 - m1)
    t1 = 1.0 / (1.0 + p2)
    t2 = p2 / (1.0 + p2)
    eid = jnp.where(lane_i == 0, i1, jnp.where(lane_i == 1, i2, 0.0)).astype(I32)
    wt = jnp.where(lane_i == 0, g_gate * t1, jnp.where(lane_i == 1, g_gate * t2, 0.0))
    return eid, wt


def _router_weights(w_group, w_expert):
    pad = LANES - MOE_N_EXPERTS - MOE_GROUPS
    return jnp.pad(jnp.concatenate([w_expert, w_group], axis=1), ((0, 0), (0, pad))).astype(BF16)


def _store_token_tiles(ref, v):
    rows = v.shape[0]
    for s in range(ROW_SLABS):
        ref[pl.ds(s, rows, stride=ROW_SLABS), :] = v[:, s * LANES:(s + 1) * LANES]


def _load_token_tiles(ref, rows):
    return jnp.concatenate([ref[pl.ds(s, rows, stride=ROW_SLABS), :] for s in range(ROW_SLABS)], axis=1)


def _mm_ln_router_kernel(x_ref, w_ref, r_ref, g_ref, b_ref, wr_ref, h_ref, ht_ref, eid_ref, wt_ref):
    y = DEEPNORM_ALPHA * r_ref[...] + _dot(x_ref[...].astype(BF16), w_ref[...])
    h = _layer_norm(y, g_ref[...], b_ref[...])
    h_ref[...] = h
    _store_token_tiles(ht_ref, h)
    eid, wt = _route(h, wr_ref[...])
    eid_ref[...] = eid
    wt_ref[...] = wt


def _mm_ln_router(x, w, resid, g, b, wr, name):
    m, k = x.shape
    d = w.shape[1]
    tm = min(MM_TM, m)
    row = lambda i: (i, 0)
    full = lambda i: (0, 0)
    return pl.pallas_call(
        _mm_ln_router_kernel,
        grid=(m // tm,),
        in_specs=[pl.BlockSpec((tm, k), row), pl.BlockSpec((k, d), full), pl.BlockSpec((tm, d), row),
                  pl.BlockSpec((1, d), full), pl.BlockSpec((1, d), full), pl.BlockSpec((d, LANES), full)],
        out_specs=[pl.BlockSpec((tm, d), row), pl.BlockSpec((tm * ROW_SLABS, LANES), row),
                   pl.BlockSpec((tm, LANES), row), pl.BlockSpec((tm, LANES), row)],
        out_shape=[jax.ShapeDtypeStruct((m, d), F32), jax.ShapeDtypeStruct((m * ROW_SLABS, LANES), F32),
                   jax.ShapeDtypeStruct((m, LANES), I32), jax.ShapeDtypeStruct((m, LANES), F32)],
        compiler_params=_arb(1),
        name=name,
    )(x, w, resid, g[None, :], b[None, :], wr)


def _dispatch_kernel(eid_ref, h_ref, dest_ref, be_ref, nu_ref, xs_hbm,
                     cnt, base, upper, zbuf, dst_v, dst_s, pe_v, pe_s, sem_z, sem_r, sem_s, *, blk):
    ph = pl.program_id(0)
    i = pl.program_id(1)
    tm = eid_ref.shape[0]
    n_e = LANES
    eid_t = eid_ref[...].astype(F32).T
    sub = lax.broadcasted_iota(I32, (n_e, tm), 0).astype(F32)
    oh = [(sub == eid_t[k:k + 1, :]).astype(F32) for k in range(MOE_TOP_K)]
    tot = [jnp.sum(o, axis=1, keepdims=True) for o in oh]

    @pl.when((ph == 0) & (i == 0))
    def _():
        cnt[...] = jnp.zeros(cnt.shape, F32)
        r_i = lax.broadcasted_iota(I32, (tm, tm), 0)
        c_i = lax.broadcasted_iota(I32, (tm, tm), 1)
        upper[...] = (r_i < c_i).astype(BF16)

    @pl.when(ph == 0)
    def _():
        cnt[...] += jnp.broadcast_to(tot[0] + tot[1], cnt.shape)

    @pl.when((ph == 1) & (i == 0))
    def _():
        counts = cnt[...]
        padded = jnp.floor((counts + float(blk - 1)) * (1.0 / blk)) * float(blk)
        r_i = lax.broadcasted_iota(I32, (n_e, n_e), 0)
        c_i = lax.broadcasted_iota(I32, (n_e, n_e), 1)
        tril = (r_i >= c_i).astype(BF16)
        p1, p2, p3 = _split3(padded)
        pends = _dot(tril, p1) + _dot(tril, p2) + _dot(tril, p3)
        base[...] = pends - padded
        nbp = be_ref.shape[1]
        blk_start = lax.broadcasted_iota(I32, (n_e, nbp), 1).astype(F32) * float(blk)
        is_e = lax.broadcasted_iota(I32, (n_e, nbp), 0) < MOE_N_EXPERTS
        done = jnp.where(is_e & (jnp.tile(pends, (1, nbp // LANES)) <= blk_start), 1.0, 0.0)
        be = jnp.minimum(jnp.sum(done, axis=0, keepdims=True), float(MOE_N_EXPERTS - 1))
        be_ref[...] = jnp.broadcast_to(be, be_ref.shape).astype(I32)
        last = pends[MOE_N_EXPERTS - 1:MOE_N_EXPERTS, :] * (1.0 / blk)
        nu_ref[...] = jnp.broadcast_to(last, nu_ref.shape).astype(I32)
        zbuf[...] = jnp.zeros(zbuf.shape, F32)
        row8 = lax.broadcasted_iota(I32, (8, LANES), 0)
        pe_v[...] = jnp.where(row8 == 0, pends.T[0:8, :], counts.T[0:8, :]).astype(I32)
        cp = pltpu.make_async_copy(pe_v, pe_s, sem_s)
        cp.start()
        cp.wait()

        brows = blk * ROW_SLABS

        def zero_copy(e):
            start = pl.multiple_of((pe_s[0, e] - blk) * ROW_SLABS, brows)
            return pltpu.make_async_copy(zbuf, xs_hbm.at[pl.ds(start, brows)], sem_z)

        def tail_copy(b):
            return pltpu.make_async_copy(zbuf, xs_hbm.at[pl.ds(pl.multiple_of(b * brows, brows), brows)], sem_z)

        n_used = lax.shift_right_logical(pe_s[0, MOE_N_EXPERTS - 1], blk.bit_length() - 1)
        n_blocks = xs_hbm.shape[0] // brows
        for e in range(MOE_N_EXPERTS):
            @pl.when(pe_s[1, e] > 0)
            def _():
                zero_copy(e).start()
        lax.fori_loop(n_used, n_blocks, lambda b, c: (tail_copy(b).start(), c)[1], 0)
        for e in range(MOE_N_EXPERTS):
            @pl.when(pe_s[1, e] > 0)
            def _():
                zero_copy(e).wait()
        lax.fori_loop(n_used, n_blocks, lambda b, c: (tail_copy(b).wait(), c)[1], 0)

    @pl.when(ph == 1)
    def _():
        b0 = base[:, 0:1]
        c0 = _dot(oh[0].astype(BF16), upper[...])
        c1 = _dot(oh[1].astype(BF16), upper[...])
        d0 = jnp.sum(oh[0] * (b0 + c0), axis=0, keepdims=True)
        d1 = jnp.sum(oh[1] * (b0 + tot[0] + c1), axis=0, keepdims=True)
        base[...] += jnp.broadcast_to(tot[0] + tot[1], base.shape)
        row8 = lax.broadcasted_iota(I32, (8, tm), 0)
        dst = jnp.where(row8 == 0, d0, jnp.where(row8 == 1, d1, 0.0)).astype(I32)
        dest_ref[...] = dst
        dst_v[...] = dst
        cp = pltpu.make_async_copy(dst_v, dst_s, sem_s)
        cp.start()
        cp.wait()

        for r in range(tm):
            for k in range(MOE_TOP_K):
                slot = pl.multiple_of(dst_s[k, r] * ROW_SLABS, ROW_SLABS)
                pltpu.make_async_copy(h_ref.at[pl.ds(r * ROW_SLABS, ROW_SLABS)],
                                      xs_hbm.at[pl.ds(slot, ROW_SLABS)], sem_r).start(priority=k)
        for k in range(MOE_TOP_K):
            pltpu.make_async_copy(h_ref, xs_hbm.at[pl.ds(0, tm * ROW_SLABS)], sem_r).wait()


def _moe_dispatch(ht, eid, blk, name):
    t = ht.shape[0] // ROW_SLABS
    assert blk & (blk - 1) == 0, "block size must be a power of two"
    tm = min(DSP_TM, t)
    nt = t // tm
    n_blocks = -(-(t * MOE_TOP_K) // blk) + MOE_N_EXPERTS
    nbp = -(-n_blocks // LANES) * LANES
    tile = lambda ph, i: (i * ph, 0)
    dest, be, nu, xs = pl.pallas_call(
        functools.partial(_dispatch_kernel, blk=blk),
        grid=(2, nt),
        in_specs=[pl.BlockSpec((tm, LANES), lambda ph, i: (i, 0)),
                  pl.BlockSpec((tm * ROW_SLABS, LANES), tile)],
        out_specs=[pl.BlockSpec((8, tm), lambda ph, i: (0, i * ph)),
                   pl.BlockSpec((8, nbp), lambda ph, i: (0, 0)),
                   pl.BlockSpec((8, LANES), lambda ph, i: (0, 0)),
                   pl.BlockSpec(memory_space=pl.ANY)],
        out_shape=[jax.ShapeDtypeStruct((8, t), I32), jax.ShapeDtypeStruct((8, nbp), I32),
                   jax.ShapeDtypeStruct((8, LANES), I32),
                   jax.ShapeDtypeStruct((n_blocks * blk * ROW_SLABS, LANES), F32)],
        scratch_shapes=[pltpu.VMEM((LANES, LANES), F32), pltpu.VMEM((LANES, LANES), F32),
                        pltpu.VMEM((tm, tm), BF16), pltpu.VMEM((blk * ROW_SLABS, LANES), F32),
                        pltpu.VMEM((8, tm), I32), pltpu.SMEM((8, tm), I32),
                        pltpu.VMEM((8, LANES), I32), pltpu.SMEM((8, LANES), I32),
                        pltpu.SemaphoreType.DMA(()), pltpu.SemaphoreType.DMA(()), pltpu.SemaphoreType.DMA(())],
        compiler_params=_arb(2),
        name=name,
    )(eid, ht)
    return dest, be[0, :n_blocks], nu[0, :1], xs, n_blocks


def _ffn_kernel(be_ref, nu_ref, x_ref, wg_ref, wu_ref, wd_ref, o_ref, wgb, wub, wdb):
    i = pl.program_id(0)

    @pl.when(i < nu_ref[0])
    def _():
        prev = be_ref[jnp.maximum(i - 1, 0)]

        @pl.when((i == 0) | (be_ref[i] != prev))
        def _():
            wgb[...] = wg_ref[0, 0].astype(BF16)
            wub[...] = wu_ref[0, 0].astype(BF16)
            wdb[...] = wd_ref[0, 0].astype(BF16)

        blk = x_ref.shape[0] // ROW_SLABS
        xb = _load_token_tiles(x_ref, blk).astype(BF16)
        hid = _silu(_dot(xb, wgb[...])) * _dot(xb, wub[...])
        _store_token_tiles(o_ref, _dot(hid.astype(BF16), wdb[...]))

    @pl.when(i >= nu_ref[0])
    def _():
        o_ref[...] = jnp.zeros(o_ref.shape, o_ref.dtype)


def _moe_ffn(xs, block_expert, n_used, n_blocks, w_gate, w_up, w_down, layer, blk, name):
    d, f = w_gate.shape[2], w_gate.shape[3]
    brows = blk * ROW_SLABS
    used = lambda i, be, nu: (jnp.minimum(i, nu[0] - 1), 0)
    every = lambda i, be, nu: (i, 0)
    grid_spec = pltpu.PrefetchScalarGridSpec(
        num_scalar_prefetch=2,
        grid=(n_blocks,),
        in_specs=[pl.BlockSpec((brows, LANES), used),
                  pl.BlockSpec((1, 1, d, f), lambda i, be, nu: (layer, be[i], 0, 0)),
                  pl.BlockSpec((1, 1, d, f), lambda i, be, nu: (layer, be[i], 0, 0)),
                  pl.BlockSpec((1, 1, f, d), lambda i, be, nu: (layer, be[i], 0, 0))],
        out_specs=pl.BlockSpec((brows, LANES), every),
        scratch_shapes=[pltpu.VMEM((d, f), BF16), pltpu.VMEM((d, f), BF16), pltpu.VMEM((f, d), BF16)])
    return pl.pallas_call(
        _ffn_kernel,
        grid_spec=grid_spec,
        out_shape=jax.ShapeDtypeStruct((n_blocks * brows, LANES), F32),
        compiler_params=_arb(1),
        name=name,
    )(block_expert, n_used, xs, w_gate, w_up, w_down)


def _combine_ln_kernel(dst_ref, dstn_ref, h_ref, wt_ref, g_ref, b_ref, yb_hbm, o_ref, ybuf, sem):
    i = pl.program_id(0)
    n = pl.num_programs(0)
    tm = h_ref.shape[0]

    def start_gather(dref, slot):
        for r in range(tm):
            for k in range(MOE_TOP_K):
                src = pl.multiple_of(dref[k, r] * ROW_SLABS, ROW_SLABS)
                pltpu.make_async_copy(yb_hbm.at[pl.ds(src, ROW_SLABS)],
                                      ybuf.at[slot, k, pl.ds(r * ROW_SLABS, ROW_SLABS)],
                                      sem.at[slot]).start(priority=k)

    @pl.when(i == 0)
    def _():
        start_gather(dst_ref, 0)

    @pl.when(i + 1 < n)
    def _():
        start_gather(dstn_ref, (i + 1) % 2)

    slot = i % 2
    for k in range(MOE_TOP_K):
        pltpu.make_async_copy(yb_hbm.at[pl.ds(0, tm * ROW_SLABS)], ybuf.at[slot, k], sem.at[slot]).wait()
    wt = wt_ref[...]
    ffn = (wt[:, 0:1] * _load_token_tiles(ybuf.at[slot, 0], tm)
           + wt[:, 1:2] * _load_token_tiles(ybuf.at[slot, 1], tm))
    o_ref[...] = _layer_norm(DEEPNORM_ALPHA * h_ref[...] + ffn, g_ref[...], b_ref[...])


def _combine_ln(h, wt, dest, yb, g, b, name):
    t, d = h.shape
    tm = min(CMB_TM, t)
    nblk = t // tm
    row = lambda i: (i, 0)
    full = lambda i: (0, 0)
    return pl.pallas_call(
        _combine_ln_kernel,
        grid=(nblk,),
        in_specs=[pl.BlockSpec((8, tm), lambda i: (0, i), memory_space=pltpu.SMEM),
                  pl.BlockSpec((8, tm), lambda i: (0, jnp.minimum(i + 1, nblk - 1)),
                               memory_space=pltpu.SMEM),
                  pl.BlockSpec((tm, d), row), pl.BlockSpec((tm, LANES), row),
                  pl.BlockSpec((1, d), full), pl.BlockSpec((1, d), full),
                  pl.BlockSpec(memory_space=pl.ANY)],
        out_specs=pl.BlockSpec((tm, d), row),
        out_shape=jax.ShapeDtypeStruct((t, d), F32),
        scratch_shapes=[pltpu.VMEM((2, MOE_TOP_K, tm * ROW_SLABS, LANES), F32), pltpu.SemaphoreType.DMA((2,))],
        compiler_params=_arb(1),
        name=name,
    )(dest, dest, h, wt, g[None, :], b[None, :], yb)


def _hier_moe_ln(h, ht, eid, wt, w_gate, w_up, w_down, g, b, layer):
    dest, block_expert, n_used, xs, n_blocks = _moe_dispatch(ht, eid, FFN_BLK, f"moe_dispatch{layer}")
    yb = _moe_ffn(xs, block_expert, n_used, n_blocks, w_gate, w_up, w_down, layer, FFN_BLK, f"moe_ffn{layer}")
    return _combine_ln(h, wt, dest, yb, g, b, f"moe_combine_ln{layer}")


def _qkv_rope_kernel(x_ref, w_ref, pos_ref, inv_ref, o_ref):
    xb = x_ref[...].astype(BF16)
    tm = xb.shape[0]
    n = D_MODEL
    ang = pos_ref[...].astype(F32) * inv_ref[...]
    lane = lax.broadcasted_iota(I32, (tm, LANES), 1)
    dd = lane & (ATTN_HEAD_DIM - 1)
    half = ROT_DIM // 2
    cosv = jnp.cos(ang)
    sinv = jnp.sin(ang)
    c_t = jnp.where(dd < ROT_DIM, cosv, 1.0)
    s_up = jnp.where(dd < half, -sinv, 0.0)
    s_dn = jnp.where((dd >= half) & (dd < ROT_DIM), sinv, 0.0)
    for j, sc in ((0, ATTN_HEAD_DIM ** -0.5 * math.log2(math.e)), (1, 1.0)):
        acc = _dot(xb, w_ref[:, j * n:(j + 1) * n])
        c_j, up_j, dn_j = c_t * sc, s_up * sc, s_dn * sc
        for blk in range(n // LANES):
            tt = acc[:, blk * LANES:(blk + 1) * LANES]
            out = tt * c_j + pltpu.roll(tt, LANES - half, 1) * up_j + pltpu.roll(tt, half, 1) * dn_j
            o_ref[:, j * n + blk * LANES:j * n + (blk + 1) * LANES] = out.astype(o_ref.dtype)
    o_ref[:, 2 * n:3 * n] = _dot(xb, w_ref[:, 2 * n:3 * n]).astype(o_ref.dtype)


def _rope_inv_table():
    inv = ROPE_THETA ** (-jnp.arange(0, ROT_DIM, 2, dtype=F32) / ROT_DIM)
    head = jnp.concatenate([inv, inv, jnp.zeros((ATTN_HEAD_DIM - ROT_DIM,), F32)])
    return jnp.tile(head, LANES // ATTN_HEAD_DIM)[None, :]


def _qkv_rope(h, w_qkv, positions):
    m, k = h.shape
    n = w_qkv.shape[1]
    tm = min(MM_TM, m)
    pos = positions.reshape(m, 1)
    return pl.pallas_call(
        _qkv_rope_kernel,
        grid=(m // tm,),
        in_specs=[pl.BlockSpec((tm, k), lambda i: (i, 0)),
                  pl.BlockSpec((k, n), lambda i: (0, 0)),
                  pl.BlockSpec((tm, 1), lambda i: (i, 0)),
                  pl.BlockSpec((1, LANES), lambda i: (0, 0))],
        out_specs=pl.BlockSpec((tm, n), lambda i: (i, 0)),
        out_shape=jax.ShapeDtypeStruct((m, n), BF16),
        compiler_params=_arb(1),
        name="mm_qkv_rope",
    )(h, w_qkv, pos, _rope_inv_table())


def _attn_kernel(q_ref, k_ref, v_ref, lq1_ref, lk1_ref, lq2_ref, lk2_ref, sw_ref, o_ref,
                 vx_ref, m0_ref, m1_ref, acc0_ref, acc1_ref, *, lambda_init):
    m_refs = (m0_ref, m1_ref)
    acc_refs = (acc0_ref, acc1_ref)
    i = pl.program_id(2)
    tq = q_ref.shape[0]

    @pl.when(i == 0)
    def _():
        vx_ref[:, 0:LANES] = v_ref[...]
        vx_ref[:, LANES:] = jnp.ones((vx_ref.shape[0], LANES), vx_ref.dtype)

    q = q_ref[...]
    lane = lax.broadcasted_iota(I32, (tq, LANES), 1)
    zero = jnp.zeros((), q.dtype)
    qs = (jnp.where(lane < ATTN_HEAD_DIM, q, zero), jnp.where(lane >= ATTN_HEAD_DIM, q, zero))
    for c in range(2):
        m_refs[c][...] = jnp.full(m_refs[c].shape, -jnp.inf, F32)
        acc_refs[c][...] = jnp.zeros(acc_refs[c].shape, F32)

    def step(off, width, masked):
        off = pl.multiple_of(off, tq)
        kb = k_ref[pl.ds(off, width), :]
        vb = vx_ref[pl.ds(off, width), :]
        ss = [lax.dot_general(qs[c], kb, (((1,), (1,)), ((), ())), preferred_element_type=F32)
              for c in range(2)]
        if masked:
            row_i = lax.broadcasted_iota(I32, (tq, width), 0)
            col_i = lax.broadcasted_iota(I32, (tq, width), 1)
            ss = [jnp.where(row_i >= col_i, s, -jnp.inf) for s in ss]
        m_prev = [m_refs[c][...] for c in range(2)]
        mn = [jnp.maximum(m_prev[c], jnp.max(ss[c], axis=-1, keepdims=True)) for c in range(2)]
        ps = [jnp.exp2(ss[c] - jnp.tile(mn[c], (1, width // LANES))).astype(BF16) for c in range(2)]
        for c in range(2):
            alpha = jnp.exp2(m_prev[c] - mn[c])
            acc_refs[c][...] = jnp.tile(alpha, (1, 2)) * acc_refs[c][...] + _dot(ps[c], vb)
            m_refs[c][...] = mn[c]

    def loop(n, fn):
        lax.fori_loop(0, n, lambda j, carry: (fn(j), carry)[1], 0)

    loop(i // 2, lambda j: step(j * (2 * tq), 2 * tq, False))
    loop(i % 2, lambda j: step((i - 1) * tq, tq, False))
    step(i * tq, tq, True)

    lam = (jnp.exp(jnp.sum(lq1_ref[...] * lk1_ref[...], axis=-1, keepdims=True))
           - jnp.exp(jnp.sum(lq2_ref[...] * lk2_ref[...], axis=-1, keepdims=True)) + lambda_init)
    a1 = acc0_ref[...]
    a2 = acc1_ref[...]
    o = a1[:, :LANES] / a1[:, LANES:] - lam * (a2[:, :LANES] / a2[:, LANES:])
    o = o * lax.rsqrt(jnp.mean(o * o, axis=-1, keepdims=True) + NORM_EPS)
    o_ref[...] = (o * sw_ref[...] * (1.0 - lambda_init)).astype(o_ref.dtype)


def _diff_attention(qkv, lq1, lk1, lq2, lk2, subln_w, lambda_init, batch, seq):
    t = batch * seq
    tq = min(ATT_TQ, seq)
    nq = seq // tq
    h_n = ATTN_N_HEADS
    vec = lambda b, h, i: (0, 0)
    return pl.pallas_call(
        functools.partial(_attn_kernel, lambda_init=lambda_init),
        grid=(batch, h_n, nq),
        in_specs=[pl.BlockSpec((tq, LANES), lambda b, h, i: (b * nq + i, h)),
                  pl.BlockSpec((seq, LANES), lambda b, h, i: (b, h_n + h)),
                  pl.BlockSpec((seq, LANES), lambda b, h, i: (b, 2 * h_n + h)),
                  pl.BlockSpec((1, ATTN_HEAD_DIM), vec), pl.BlockSpec((1, ATTN_HEAD_DIM), vec),
                  pl.BlockSpec((1, ATTN_HEAD_DIM), vec), pl.BlockSpec((1, ATTN_HEAD_DIM), vec),
                  pl.BlockSpec((1, ATTN_V_DIM), vec)],
        out_specs=pl.BlockSpec((tq, LANES), lambda b, h, i: (b * nq + i, h)),
        out_shape=jax.ShapeDtypeStruct((t, h_n * ATTN_V_DIM), BF16),
        scratch_shapes=[pltpu.VMEM((seq, 2 * LANES), BF16),
                        pltpu.VMEM((tq, LANES), F32), pltpu.VMEM((tq, LANES), F32),
                        pltpu.VMEM((tq, 2 * LANES), F32), pltpu.VMEM((tq, 2 * LANES), F32)],
        compiler_params=_arb(3),
        name="diff_attn",
    )(qkv, qkv, qkv, lq1[None, :], lk1[None, :], lq2[None, :], lk2[None, :], subln_w[None, :])


def kernel(x, positions, ln_mix_g, ln_mix_b, ln_ffn_g, ln_ffn_b, ssm_w_in, ssm_conv_w, ssm_conv_b, ssm_dt_bias, ssm_a_log, ssm_d, ssm_norm_w, ssm_w_out, attn_w_qkv, attn_lam_q1, attn_lam_k1, attn_lam_q2, attn_lam_k2, attn_subln_w, attn_w_o, moe_w_group, moe_w_expert, moe_w_gate, moe_w_up, moe_w_down):
    batch, seq, d = x.shape
    t = batch * seq
    h = x.reshape(t, d)

    w_in = ssm_w_in[0].astype(BF16)
    w_zx = w_in[:, :SSM_ZX_DIM]

    zs, xs, bc = _inproj_conv(h, w_zx, ssm_conv_w[0], ssm_conv_b[0], seq)
    yg = _ssd(zs, xs, bc, h, w_in[:, SSM_ZX_DIM:], ssm_dt_bias[0],
              ssm_a_log[0], ssm_d[0], ssm_norm_w[0], batch, seq)
    h, ht, eid, wt = _mm_ln_router(yg, ssm_w_out[0].astype(BF16), h, ln_mix_g[0], ln_mix_b[0],
                               _router_weights(moe_w_group[0], moe_w_expert[0]), "mm_ssm_out_ln_router")
    h = _hier_moe_ln(h, ht, eid, wt, moe_w_gate, moe_w_up, moe_w_down, ln_ffn_g[0], ln_ffn_b[0], 0)

    lambda_init = 0.8 - 0.6 * math.exp(-0.3 * 1)
    qkv = _qkv_rope(h, attn_w_qkv[0].astype(BF16), positions)
    o = _diff_attention(qkv, attn_lam_q1[0], attn_lam_k1[0], attn_lam_q2[0], attn_lam_k2[0],
                        attn_subln_w[0], lambda_init, batch, seq)
    h, ht, eid, wt = _mm_ln_router(o, attn_w_o[0].astype(BF16), h, ln_mix_g[1], ln_mix_b[1],
                               _router_weights(moe_w_group[1], moe_w_expert[1]), "mm_attn_out_ln_router")
    h = _hier_moe_ln(h, ht, eid, wt, moe_w_gate, moe_w_up, moe_w_down, ln_ffn_g[1], ln_ffn_b[1], 1)
    return h.reshape(batch, seq, d)
```

```python
import functools
import math

import jax
import jax.numpy as jnp
from jax import lax
from jax.experimental import pallas as pl
from jax.experimental.pallas import tpu as pltpu

F32 = jnp.float32
BF16 = jnp.bfloat16
I32 = jnp.int32

D_MODEL = 1024
DEPTH = 2
SSM_D_INNER = 2048
SSM_HEADDIM = 64
SSM_N_HEADS = 32
SSM_N_GROUPS = 8
SSM_HEADS_PER_GROUP = 4
SSM_D_STATE = 128
SSM_CONV_K = 4
SSM_CHUNK = 128
SSM_GROUP_WIDTH = SSM_HEADS_PER_GROUP * SSM_HEADDIM
SSM_ZX_DIM = 2 * SSM_D_INNER + 2 * SSM_N_GROUPS * SSM_D_STATE
ATTN_HEAD_DIM = 64
ATTN_N_HEADS = 8
ATTN_V_DIM = 128
ROT_DIM = 16
ROPE_THETA = 500000.0
MOE_GROUPS = 4
MOE_EXPERTS_PER_GROUP = 8
MOE_N_EXPERTS = 32
MOE_TOP_K = 2
MOE_D_FF = 512
DEEPNORM_ALPHA = (2 * DEPTH) ** 0.25
NORM_EPS = 1e-5

LANES = 128
ROW_SLABS = D_MODEL // LANES
CONV_TAIL = 8
CONV_ROWS = 32

MM_TM = 512
MM_TM_WIDE = 256
MM_TN = 1024
FFN_BLK = 512
DSP_TM = 512
CMB_TM = 256
ATT_TQ = 512


def _arb(n):
    return pltpu.CompilerParams(dimension_semantics=("arbitrary",) * n,
                                vmem_limit_bytes=56 * 1024 * 1024)


def _sigmoid(x):
    return 1.0 / (1.0 + jnp.exp(-x))


def _silu(x):
    return x * _sigmoid(x)


def _softplus(x):
    return jnp.maximum(x, 0.0) + jnp.log(1.0 + jnp.exp(-jnp.abs(x)))


def _layer_norm(y, g, b):
    mu = jnp.mean(y, axis=-1, keepdims=True)
    d = y - mu
    var = jnp.mean(d * d, axis=-1, keepdims=True)
    return d * lax.rsqrt(var + NORM_EPS) * g + b


def _split3(a):
    a1 = a.astype(BF16)
    r1 = a - a1.astype(F32)
    a2 = r1.astype(BF16)
    a3 = (r1 - a2.astype(F32)).astype(BF16)
    return a1, a2, a3


def _dot(a, b):
    return jnp.dot(a, b, preferred_element_type=F32)


def _dot_sel(a, sel):
    a1, a2, a3 = _split3(a)
    return _dot(a1, sel) + _dot(a2, sel) + _dot(a3, sel)


def _dot_f32(a, b):
    a1, a2, a3 = _split3(a)
    b1, b2, b3 = _split3(b)
    return (_dot(a1, b1) + _dot(a1, b2) + _dot(a2, b1)
            + _dot(a2, b2) + _dot(a1, b3) + _dot(a3, b1))


def _inproj_conv_kernel(x_ref, w_ref, cw_ref, cb_ref, z_ref, xs_ref, bc_ref, *scratch, tiles_per_seq):
    n_conv = (SSM_ZX_DIM - SSM_D_INNER) // MM_TN
    stages = scratch
    tm = x_ref.shape[0]
    xb = x_ref[...].astype(BF16)

    first = pl.program_id(0) % tiles_per_seq == 0

    @pl.when(first)
    def _():
        for stage in stages:
            stage[0:CONV_TAIL, :] = jnp.zeros((CONV_TAIL, MM_TN), F32)

    @pl.when(jnp.logical_not(first))
    def _():
        for stage in stages:
            stage[0:CONV_TAIL, :] = stage[tm:tm + CONV_TAIL, :]

    for j in range(n_conv):
        c0 = SSM_D_INNER + j * MM_TN
        stages[j][CONV_TAIL:CONV_TAIL + tm, :] = _dot(xb, w_ref[:, c0:c0 + MM_TN])
    for j in range(SSM_D_INNER // MM_TN):
        cols = slice(j * MM_TN, (j + 1) * MM_TN)
        z_ref[:, cols] = _dot(xb, w_ref[:, cols])
    for j in range(n_conv):
        c0 = j * MM_TN
        stage = stages[j]
        w_taps = [cw_ref[k:k + 1, c0:c0 + MM_TN] for k in range(SSM_CONV_K)]
        bias = cb_ref[:, c0:c0 + MM_TN]
        for rb in range(tm // CONV_ROWS):
            r0 = rb * CONV_ROWS
            u = stage[r0:r0 + CONV_ROWS + CONV_TAIL, :]
            out = bias + w_taps[SSM_CONV_K - 1] * u[CONV_TAIL:, :]
            for k in range(SSM_CONV_K - 1):
                back = SSM_CONV_K - 1 - k
                shifted = pltpu.roll(u, back, 0)[CONV_TAIL:, :]
                out = out + w_taps[k] * shifted
            out = _silu(out)
            if c0 < SSM_D_INNER:
                xs_ref[r0:r0 + CONV_ROWS, c0:c0 + MM_TN] = out
            else:
                bc_ref[r0:r0 + CONV_ROWS, c0 - SSM_D_INNER:c0 - SSM_D_INNER + MM_TN] = out.astype(bc_ref.dtype)


def _inproj_conv(x, w, conv_w, conv_b, seq):
    m, k = x.shape
    tm = min(MM_TM_WIDE, seq)
    n_conv = (SSM_ZX_DIM - SSM_D_INNER) // MM_TN
    conv_dim = conv_w.shape[1]
    row = lambda i: (i, 0)
    full = lambda i: (0, 0)
    return pl.pallas_call(
        functools.partial(_inproj_conv_kernel, tiles_per_seq=seq // tm),
        grid=(m // tm,),
        in_specs=[pl.BlockSpec((tm, k), row), pl.BlockSpec((k, SSM_ZX_DIM), full),
                  pl.BlockSpec((SSM_CONV_K, conv_dim), full), pl.BlockSpec((1, conv_dim), full)],
        out_specs=[pl.BlockSpec((tm, SSM_D_INNER), row), pl.BlockSpec((tm, SSM_D_INNER), row),
                   pl.BlockSpec((tm, conv_dim - SSM_D_INNER), row)],
        out_shape=[jax.ShapeDtypeStruct((m, SSM_D_INNER), F32), jax.ShapeDtypeStruct((m, SSM_D_INNER), F32),
                   jax.ShapeDtypeStruct((m, conv_dim - SSM_D_INNER), BF16)],
        scratch_shapes=[pltpu.VMEM((CONV_TAIL + tm, MM_TN), F32) for _ in range(n_conv)],
        compiler_params=_arb(1),
        name="mm_inproj_conv",
    )(x, w, conv_w, conv_b[None, :])


def _ssd_kernel(z_ref, xs_ref, bc_ref, x_ref, wdt_ref, dtb_ref, alog_ref, dskip_ref,
                nw_ref, e64_ref, e128_ref, o_ref, state, acp_s, dtp_s, act_s):
    G, W, N = SSM_N_GROUPS, SSM_GROUP_WIDTH, SSM_D_STATE
    L = z_ref.shape[0]
    c = pl.program_id(1)

    @pl.when(c == 0)
    def _():
        state[...] = jnp.zeros(state.shape, F32)

    def x_of(g):
        return xs_ref[:, g * W:(g + 1) * W]

    def b_of(g):
        return bc_ref[:, g * N:(g + 1) * N]

    def c_of(g):
        return bc_ref[:, G * N + g * N:G * N + (g + 1) * N]

    dt = _softplus(_dot(x_ref[...].astype(BF16), wdt_ref[...].astype(BF16)) + dtb_ref[...])
    a = -jnp.exp(alog_ref[...])
    row_i = lax.broadcasted_iota(I32, (L, L), 0)
    col_i = lax.broadcasted_iota(I32, (L, L), 1)
    causal = row_i >= col_i
    a1, a2, a3 = _split3(dt * a)
    tri = causal.astype(BF16)
    ac = _dot(tri, a1) + _dot(tri, a2) + _dot(tri, a3)
    act_s[...] = ac.T
    for n, part in enumerate(_split3(ac)):
        acp_s[n] = part
    for n, part in enumerate(_split3(dt)[:2]):
        dtp_s[n] = part
    lane = lax.broadcasted_iota(I32, (L, LANES), 1)
    first_half = lane < SSM_HEADDIM

    def select(parts_ref, n_parts, sel):
        out = _dot(parts_ref[0], sel)
        for n in range(1, n_parts):
            out = out + _dot(parts_ref[n], sel)
        return out

    gs = range(G)
    col4 = [select(acp_s, 3, e128_ref[g]) for g in gs]
    dt_e = [select(dtp_s, 2, e64_ref[g]) for g in gs]
    cb = [lax.dot_general(c_of(g), b_of(g), (((1,), (1,)), ((), ())), preferred_element_type=F32)
          for g in gs]
    y_cs = [_dot(c_of(g), state[g].astype(BF16)) for g in gs]
    a_e = [jnp.concatenate(
        [jnp.where(first_half, c4[:, 2 * p * LANES:(2 * p + 1) * LANES],
                   c4[:, (2 * p + 1) * LANES:(2 * p + 2) * LANES]) for p in range(2)], axis=1)
        for c4 in col4]
    xdt = [x_of(g) * dt_e[g] for g in gs]
    xdt_b = [v.astype(BF16) for v in xdt]
    ys = [[None] * SSM_HEADS_PER_GROUP for _ in gs]
    for r in range(SSM_HEADS_PER_GROUP):
        for g in gs:
            row = act_s[SSM_HEADS_PER_GROUP * g + r:SSM_HEADS_PER_GROUP * g + r + 1, :]
            seg = col4[g][:, LANES * r:LANES * (r + 1)] - row
            dec = jnp.where(causal, jnp.exp(seg), 0.0)
            xp = xdt_b[g][:, LANES * (r // 2):LANES * (r // 2 + 1)]
            ys[g][r] = _dot((cb[g] * dec).astype(BF16), xp)
    for g in gs:
        y_diag = jnp.concatenate([jnp.where(first_half, ys[g][2 * p], ys[g][2 * p + 1])
                                  for p in range(2)], axis=1)
        al_e = a_e[g][L - 1:L, :]
        new = lax.dot_general(b_of(g), (xdt[g] * jnp.exp(al_e - a_e[g])).astype(BF16),
                              (((0,), (0,)), ((), ())), preferred_element_type=F32)
        y = y_diag + y_cs[g] * jnp.exp(a_e[g]) + x_of(g) * dskip_ref[g]
        state[g] = state[g] * jnp.exp(al_e) + new
        yg = y * _silu(z_ref[:, g * W:(g + 1) * W])
        ms = jnp.mean(yg * yg, axis=-1, keepdims=True)
        o_ref[:, g * W:(g + 1) * W] = (yg * lax.rsqrt(ms + NORM_EPS)
                                       * nw_ref[:, g * W:(g + 1) * W]).astype(o_ref.dtype)


def _head_selectors():
    G, R = SSM_N_GROUPS, SSM_HEADS_PER_GROUP
    h = jnp.arange(LANES)[None, :, None]
    g = jnp.arange(G)[:, None, None]
    j64 = jnp.arange(SSM_GROUP_WIDTH)[None, None, :]
    j128 = jnp.arange(R * LANES)[None, None, :]
    e64 = (h == g * R + j64 // SSM_HEADDIM).astype(BF16)
    e128 = (h == g * R + j128 // LANES).astype(BF16)
    return e64, e128


def _ssd(zs, xs, bc, x2, w_dt, dt_bias, a_log, d_skip, norm_w, batch, seq):
    L, G, W, N = SSM_CHUNK, SSM_N_GROUPS, SSM_GROUP_WIDTH, SSM_D_STATE
    nc = seq // L
    t = batch * seq
    pad = LANES - SSM_N_HEADS
    wdt = jnp.pad(w_dt, ((0, 0), (0, pad)))
    dtb = jnp.pad(dt_bias, (0, pad))[None, :]
    alog = jnp.pad(a_log, (0, pad))[None, :]
    dsk = jnp.repeat(d_skip, SSM_HEADDIM).reshape(G, 1, W)
    e64, e128 = _head_selectors()
    row = lambda b, c: (b * nc + c, 0)
    full2 = lambda b, c: (0, 0)
    full3 = lambda b, c: (0, 0, 0)
    return pl.pallas_call(
        _ssd_kernel,
        grid=(batch, nc),
        in_specs=[pl.BlockSpec((L, SSM_D_INNER), row),
                  pl.BlockSpec((L, SSM_D_INNER), row),
                  pl.BlockSpec((L, 2 * G * N), row),
                  pl.BlockSpec((L, D_MODEL), row),
                  pl.BlockSpec((D_MODEL, LANES), full2),
                  pl.BlockSpec((1, LANES), full2),
                  pl.BlockSpec((1, LANES), full2),
                  pl.BlockSpec((G, 1, W), full3),
                  pl.BlockSpec((1, SSM_D_INNER), full2),
                  pl.BlockSpec((G, LANES, W), full3),
                  pl.BlockSpec((G, LANES, SSM_HEADS_PER_GROUP * LANES), full3)],
        out_specs=pl.BlockSpec((L, SSM_D_INNER), row),
        out_shape=jax.ShapeDtypeStruct((t, SSM_D_INNER), BF16),
        scratch_shapes=[pltpu.VMEM((G, N, W), F32),
                        pltpu.VMEM((3, L, LANES), BF16),
                        pltpu.VMEM((2, L, LANES), BF16),
                        pltpu.VMEM((LANES, L), F32)],
        compiler_params=_arb(2),
        name="ssd_scan",
    )(zs, xs, bc, x2, wdt, dtb, alog, dsk, norm_w[None, :], e64, e128)


def _route(h, wr):
    tm = h.shape[0]
    logits = _dot(h.astype(BF16), wr)
    lane_i = lax.broadcasted_iota(I32, (tm, LANES), 1)
    lane = lane_i.astype(F32)
    neg = -jnp.inf
    big = float(LANES)

    def first_argmax(v, vmax):
        return jnp.min(jnp.where(v == vmax, lane, big), axis=-1, keepdims=True)

    gl = jnp.where((lane_i >= MOE_N_EXPERTS) & (lane_i < MOE_N_EXPERTS + MOE_GROUPS), logits, neg)
    gm = jnp.max(gl, axis=-1, keepdims=True)
    g_sel = first_argmax(gl, gm) - float(MOE_N_EXPERTS)
    g_gate = 1.0 / jnp.sum(jnp.exp(gl - gm), axis=-1, keepdims=True)
    lo = g_sel * float(MOE_EXPERTS_PER_GROUP)
    el = jnp.where((lane >= lo) & (lane < lo + float(MOE_EXPERTS_PER_GROUP)), logits, neg)
    m1 = jnp.max(el, axis=-1, keepdims=True)
    i1 = first_argmax(el, m1)
    el2 = jnp.where(lane == i1, neg, el)
    m2 = jnp.max(el2, axis=-1, keepdims=True)
    i2 = first_argmax(el2, m2)
    p2 = jnp.exp(m2 - m1)
    t1 = 1.0 / (1.0 + p2)
    t2 = p2 / (1.0 + p2)
    eid = jnp.where(lane_i == 0, i1, jnp.where(lane_i == 1, i2, 0.0)).astype(I32)
    wt = jnp.where(lane_i == 0, g_gate * t1, jnp.where(lane_i == 1, g_gate * t2, 0.0))
    return eid, wt


def _router_weights(w_group, w_expert):
    pad = LANES - MOE_N_EXPERTS - MOE_GROUPS
    return jnp.pad(jnp.concatenate([w_expert, w_group], axis=1), ((0, 0), (0, pad))).astype(BF16)


def _store_token_tiles(ref, v):
    rows = v.shape[0]
    for s in range(ROW_SLABS):
        ref[pl.ds(s, rows, stride=ROW_SLABS), :] = v[:, s * LANES:(s + 1) * LANES]


def _load_token_tiles(ref, rows):
    return jnp.concatenate([ref[pl.ds(s, rows, stride=ROW_SLABS), :] for s in range(ROW_SLABS)], axis=1)


def _mm_ln_router_kernel(x_ref, w_ref, r_ref, g_ref, b_ref, wr_ref, h_ref, ht_ref, eid_ref, wt_ref):
    y = DEEPNORM_ALPHA * r_ref[...] + _dot(x_ref[...].astype(BF16), w_ref[...])
    h = _layer_norm(y, g_ref[...], b_ref[...])
    h_ref[...] = h
    _store_token_tiles(ht_ref, h)
    eid, wt = _route(h, wr_ref[...])
    eid_ref[...] = eid
    wt_ref[...] = wt


def _mm_ln_router(x, w, resid, g, b, wr, name):
    m, k = x.shape
    d = w.shape[1]
    tm = min(MM_TM, m)
    row = lambda i: (i, 0)
    full = lambda i: (0, 0)
    return pl.pallas_call(
        _mm_ln_router_kernel,
        grid=(m // tm,),
        in_specs=[pl.BlockSpec((tm, k), row), pl.BlockSpec((k, d), full), pl.BlockSpec((tm, d), row),
                  pl.BlockSpec((1, d), full), pl.BlockSpec((1, d), full), pl.BlockSpec((d, LANES), full)],
        out_specs=[pl.BlockSpec((tm, d), row), pl.BlockSpec((tm * ROW_SLABS, LANES), row),
                   pl.BlockSpec((tm, LANES), row), pl.BlockSpec((tm, LANES), row)],
        out_shape=[jax.ShapeDtypeStruct((m, d), F32), jax.ShapeDtypeStruct((m * ROW_SLABS, LANES), F32),
                   jax.ShapeDtypeStruct((m, LANES), I32), jax.ShapeDtypeStruct((m, LANES), F32)],
        compiler_params=_arb(1),
        name=name,
    )(x, w, resid, g[None, :], b[None, :], wr)


def _dispatch_kernel(eid_ref, h_ref, wg_ref, wu_ref, wd_ref, dest_ref, be_ref, nu_ref, wgb_ref, wub_ref, wdb_ref,
                     xs_hbm, cnt, base, upper, zbuf, dst_v, dst_s, pe_v, pe_s, sem_z, sem_r, sem_s,
                     *, blk, cast_steps):
    ph = pl.program_id(0)
    i = pl.program_id(1)

    @pl.when((ph == 1) & (i < cast_steps))
    def _():
        wgb_ref[...] = wg_ref[0].astype(BF16)
        wub_ref[...] = wu_ref[0].astype(BF16)
        wdb_ref[...] = wd_ref[0].astype(BF16)

    tm = eid_ref.shape[0]
    n_e = LANES
    eid_t = eid_ref[...].astype(F32).T
    sub = lax.broadcasted_iota(I32, (n_e, tm), 0).astype(F32)
    oh = [(sub == eid_t[k:k + 1, :]).astype(F32) for k in range(MOE_TOP_K)]
    tot = [jnp.sum(o, axis=1, keepdims=True) for o in oh]

    @pl.when((ph == 0) & (i == 0))
    def _():
        cnt[...] = jnp.zeros(cnt.shape, F32)
        r_i = lax.broadcasted_iota(I32, (tm, tm), 0)
        c_i = lax.broadcasted_iota(I32, (tm, tm), 1)
        upper[...] = (r_i < c_i).astype(BF16)

    @pl.when(ph == 0)
    def _():
        cnt[...] += jnp.broadcast_to(tot[0] + tot[1], cnt.shape)

    @pl.when((ph == 1) & (i == 0))
    def _():
        counts = cnt[...]
        padded = jnp.floor((counts + float(blk - 1)) * (1.0 / blk)) * float(blk)
        r_i = lax.broadcasted_iota(I32, (n_e, n_e), 0)
        c_i = lax.broadcasted_iota(I32, (n_e, n_e), 1)
        tril = (r_i >= c_i).astype(BF16)
        p1, p2, p3 = _split3(padded)
        pends = _dot(tril, p1) + _dot(tril, p2) + _dot(tril, p3)
        base[...] = pends - padded
        nbp = be_ref.shape[1]
        blk_start = lax.broadcasted_iota(I32, (n_e, nbp), 1).astype(F32) * float(blk)
        is_e = lax.broadcasted_iota(I32, (n_e, nbp), 0) < MOE_N_EXPERTS
        done = jnp.where(is_e & (jnp.tile(pends, (1, nbp // LANES)) <= blk_start), 1.0, 0.0)
        be = jnp.minimum(jnp.sum(done, axis=0, keepdims=True), float(MOE_N_EXPERTS - 1))
        be_ref[...] = jnp.broadcast_to(be, be_ref.shape).astype(I32)
        last = pends[MOE_N_EXPERTS - 1:MOE_N_EXPERTS, :] * (1.0 / blk)
        nu_ref[...] = jnp.broadcast_to(last, nu_ref.shape).astype(I32)
        zbuf[...] = jnp.zeros(zbuf.shape, F32)
        row8 = lax.broadcasted_iota(I32, (8, LANES), 0)
        pe_v[...] = jnp.where(row8 == 0, pends.T[0:8, :], counts.T[0:8, :]).astype(I32)
        cp = pltpu.make_async_copy(pe_v, pe_s, sem_s)
        cp.start()
        cp.wait()

        brows = blk * ROW_SLABS

        def zero_copy(e):
            start = pl.multiple_of((pe_s[0, e] - blk) * ROW_SLABS, brows)
            return pltpu.make_async_copy(zbuf, xs_hbm.at[pl.ds(start, brows)], sem_z)

        def tail_copy(b):
            return pltpu.make_async_copy(zbuf, xs_hbm.at[pl.ds(pl.multiple_of(b * brows, brows), brows)], sem_z)

        n_used = lax.shift_right_logical(pe_s[0, MOE_N_EXPERTS - 1], blk.bit_length() - 1)
        n_blocks = xs_hbm.shape[0] // brows
        for e in range(MOE_N_EXPERTS):
            @pl.when(pe_s[1, e] > 0)
            def _():
                zero_copy(e).start()
        lax.fori_loop(n_used, n_blocks, lambda b, c: (tail_copy(b).start(), c)[1], 0)
        for e in range(MOE_N_EXPERTS):
            @pl.when(pe_s[1, e] > 0)
            def _():
                zero_copy(e).wait()
        lax.fori_loop(n_used, n_blocks, lambda b, c: (tail_copy(b).wait(), c)[1], 0)

    @pl.when(ph == 1)
    def _():
        b0 = base[:, 0:1]
        c0 = _dot(oh[0].astype(BF16), upper[...])
        c1 = _dot(oh[1].astype(BF16), upper[...])
        d0 = jnp.sum(oh[0] * (b0 + c0), axis=0, keepdims=True)
        d1 = jnp.sum(oh[1] * (b0 + tot[0] + c1), axis=0, keepdims=True)
        base[...] += jnp.broadcast_to(tot[0] + tot[1], base.shape)
        row8 = lax.broadcasted_iota(I32, (8, tm), 0)
        dst = jnp.where(row8 == 0, d0, jnp.where(row8 == 1, d1, 0.0)).astype(I32)
        dest_ref[...] = dst
        dst_v[...] = dst
        cp = pltpu.make_async_copy(dst_v, dst_s, sem_s)
        cp.start()
        cp.wait()

        for r in range(tm):
            for k in range(MOE_TOP_K):
                slot = pl.multiple_of(dst_s[k, r] * ROW_SLABS, ROW_SLABS)
                pltpu.make_async_copy(h_ref.at[pl.ds(r * ROW_SLABS, ROW_SLABS)],
                                      xs_hbm.at[pl.ds(slot, ROW_SLABS)], sem_r).start(priority=k)
        for k in range(MOE_TOP_K):
            pltpu.make_async_copy(h_ref, xs_hbm.at[pl.ds(0, tm * ROW_SLABS)], sem_r).wait()


def _moe_dispatch(ht, eid, w_gate, w_up, w_down, layer, blk, name):
    t = ht.shape[0] // ROW_SLABS
    assert blk & (blk - 1) == 0, "block size must be a power of two"
    tm = min(DSP_TM, t)
    nt = t // tm
    n_blocks = -(-(t * MOE_TOP_K) // blk) + MOE_N_EXPERTS
    nbp = -(-n_blocks // LANES) * LANES
    n_e, d, f = w_gate.shape[1:]
    eps = -(-n_e // nt)
    assert n_e % eps == 0
    cast_steps = n_e // eps
    tile = lambda ph, i: (i * ph, 0)
    w_in = lambda ph, i: (layer, jnp.minimum(i * ph, cast_steps - 1), 0, 0)
    w_out = lambda ph, i: (jnp.minimum(i * ph, cast_steps - 1), 0, 0)
    dest, be, nu, wgb, wub, wdb, xs = pl.pallas_call(
        functools.partial(_dispatch_kernel, blk=blk, cast_steps=cast_steps),
        grid=(2, nt),
        in_specs=[pl.BlockSpec((tm, LANES), lambda ph, i: (i, 0)),
                  pl.BlockSpec((tm * ROW_SLABS, LANES), tile),
                  pl.BlockSpec((1, eps, d, f), w_in), pl.BlockSpec((1, eps, d, f), w_in),
                  pl.BlockSpec((1, eps, f, d), w_in)],
        out_specs=[pl.BlockSpec((8, tm), lambda ph, i: (0, i * ph)),
                   pl.BlockSpec((8, nbp), lambda ph, i: (0, 0)),
                   pl.BlockSpec((8, LANES), lambda ph, i: (0, 0)),
                   pl.BlockSpec((eps, d, f), w_out), pl.BlockSpec((eps, d, f), w_out),
                   pl.BlockSpec((eps, f, d), w_out),
                   pl.BlockSpec(memory_space=pl.ANY)],
        out_shape=[jax.ShapeDtypeStruct((8, t), I32), jax.ShapeDtypeStruct((8, nbp), I32),
                   jax.ShapeDtypeStruct((8, LANES), I32),
                   jax.ShapeDtypeStruct((n_e, d, f), BF16), jax.ShapeDtypeStruct((n_e, d, f), BF16),
                   jax.ShapeDtypeStruct((n_e, f, d), BF16),
                   jax.ShapeDtypeStruct((n_blocks * blk * ROW_SLABS, LANES), F32)],
        scratch_shapes=[pltpu.VMEM((LANES, LANES), F32), pltpu.VMEM((LANES, LANES), F32),
                        pltpu.VMEM((tm, tm), BF16), pltpu.VMEM((blk * ROW_SLABS, LANES), F32),
                        pltpu.VMEM((8, tm), I32), pltpu.SMEM((8, tm), I32),
                        pltpu.VMEM((8, LANES), I32), pltpu.SMEM((8, LANES), I32),
                        pltpu.SemaphoreType.DMA(()), pltpu.SemaphoreType.DMA(()), pltpu.SemaphoreType.DMA(())],
        compiler_params=_arb(2),
        name=name,
    )(eid, ht, w_gate, w_up, w_down)
    return dest, be[0, :n_blocks], nu[0, :1], xs, n_blocks, (wgb, wub, wdb)


def _ffn_kernel(be_ref, nu_ref, x_ref, wg_ref, wu_ref, wd_ref, o_ref):
    i = pl.program_id(0)

    @pl.when(i < nu_ref[0])
    def _():
        blk = x_ref.shape[0] // ROW_SLABS
        xb = _load_token_tiles(x_ref, blk).astype(BF16)
        hid = _silu(_dot(xb, wg_ref[0])) * _dot(xb, wu_ref[0])
        _store_token_tiles(o_ref, _dot(hid.astype(BF16), wd_ref[0]))

    @pl.when(i >= nu_ref[0])
    def _():
        o_ref[...] = jnp.zeros(o_ref.shape, o_ref.dtype)


def _moe_ffn(xs, block_expert, n_used, n_blocks, weights, blk, name):
    w_gate, w_up, w_down = weights
    d, f = w_gate.shape[1], w_gate.shape[2]
    brows = blk * ROW_SLABS
    used = lambda i, be, nu: (jnp.minimum(i, nu[0] - 1), 0)
    every = lambda i, be, nu: (i, 0)
    expert = lambda i, be, nu: (be[i], 0, 0)
    grid_spec = pltpu.PrefetchScalarGridSpec(
        num_scalar_prefetch=2,
        grid=(n_blocks,),
        in_specs=[pl.BlockSpec((brows, LANES), used),
                  pl.BlockSpec((1, d, f), expert), pl.BlockSpec((1, d, f), expert),
                  pl.BlockSpec((1, f, d), expert)],
        out_specs=pl.BlockSpec((brows, LANES), every))
    return pl.pallas_call(
        _ffn_kernel,
        grid_spec=grid_spec,
        out_shape=jax.ShapeDtypeStruct((n_blocks * brows, LANES), F32),
        compiler_params=_arb(1),
        name=name,
    )(block_expert, n_used, xs, w_gate, w_up, w_down)


def _combine_ln_kernel(dst_ref, dstn_ref, h_ref, wt_ref, g_ref, b_ref, yb_hbm, o_ref, ybuf, sem):
    i = pl.program_id(0)
    n = pl.num_programs(0)
    tm = h_ref.shape[0]

    def start_gather(dref, slot):
        for r in range(tm):
            for k in range(MOE_TOP_K):
                src = pl.multiple_of(dref[k, r] * ROW_SLABS, ROW_SLABS)
                pltpu.make_async_copy(yb_hbm.at[pl.ds(src, ROW_SLABS)],
                                      ybuf.at[slot, k, pl.ds(r * ROW_SLABS, ROW_SLABS)],
                                      sem.at[slot]).start(priority=k)

    @pl.when(i == 0)
    def _():
        start_gather(dst_ref, 0)

    @pl.when(i + 1 < n)
    def _():
        start_gather(dstn_ref, (i + 1) % 2)

    slot = i % 2
    for k in range(MOE_TOP_K):
        pltpu.make_async_copy(yb_hbm.at[pl.ds(0, tm * ROW_SLABS)], ybuf.at[slot, k], sem.at[slot]).wait()
    wt = wt_ref[...]
    ffn = (wt[:, 0:1] * _load_token_tiles(ybuf.at[slot, 0], tm)
           + wt[:, 1:2] * _load_token_tiles(ybuf.at[slot, 1], tm))
    o_ref[...] = _layer_norm(DEEPNORM_ALPHA * h_ref[...] + ffn, g_ref[...], b_ref[...])


def _combine_ln(h, wt, dest, yb, g, b, name):
    t, d = h.shape
    tm = min(CMB_TM, t)
    nblk = t // tm
    row = lambda i: (i, 0)
    full = lambda i: (0, 0)
    return pl.pallas_call(
        _combine_ln_kernel,
        grid=(nblk,),
        in_specs=[pl.BlockSpec((8, tm), lambda i: (0, i), memory_space=pltpu.SMEM),
                  pl.BlockSpec((8, tm), lambda i: (0, jnp.minimum(i + 1, nblk - 1)),
                               memory_space=pltpu.SMEM),
                  pl.BlockSpec((tm, d), row), pl.BlockSpec((tm, LANES), row),
                  pl.BlockSpec((1, d), full), pl.BlockSpec((1, d), full),
                  pl.BlockSpec(memory_space=pl.ANY)],
        out_specs=pl.BlockSpec((tm, d), row),
        out_shape=jax.ShapeDtypeStruct((t, d), F32),
        scratch_shapes=[pltpu.VMEM((2, MOE_TOP_K, tm * ROW_SLABS, LANES), F32), pltpu.SemaphoreType.DMA((2,))],
        compiler_params=_arb(1),
        name=name,
    )(dest, dest, h, wt, g[None, :], b[None, :], yb)


def _hier_moe_ln(h, ht, eid, wt, w_gate, w_up, w_down, g, b, layer):
    dest, block_expert, n_used, xs, n_blocks, w_bf16 = _moe_dispatch(
        ht, eid, w_gate, w_up, w_down, layer, FFN_BLK, f"moe_dispatch{layer}")
    yb = _moe_ffn(xs, block_expert, n_used, n_blocks, w_bf16, FFN_BLK, f"moe_ffn{layer}")
    return _combine_ln(h, wt, dest, yb, g, b, f"moe_combine_ln{layer}")


def _qkv_rope_kernel(x_ref, w_ref, pos_ref, inv_ref, o_ref):
    xb = x_ref[...].astype(BF16)
    tm = xb.shape[0]
    n = D_MODEL
    ang = pos_ref[...].astype(F32) * inv_ref[...]
    lane = lax.broadcasted_iota(I32, (tm, LANES), 1)
    dd = lane & (ATTN_HEAD_DIM - 1)
    half = ROT_DIM // 2
    cosv = jnp.cos(ang)
    sinv = jnp.sin(ang)
    c_t = jnp.where(dd < ROT_DIM, cosv, 1.0)
    s_up = jnp.where(dd < half, -sinv, 0.0)
    s_dn = jnp.where((dd >= half) & (dd < ROT_DIM), sinv, 0.0)
    for j, sc in ((0, ATTN_HEAD_DIM ** -0.5 * math.log2(math.e)), (1, 1.0)):
        acc = _dot(xb, w_ref[:, j * n:(j + 1) * n])
        c_j, up_j, dn_j = c_t * sc, s_up * sc, s_dn * sc
        for blk in range(n // LANES):
            tt = acc[:, blk * LANES:(blk + 1) * LANES]
            out = tt * c_j + pltpu.roll(tt, LANES - half, 1) * up_j + pltpu.roll(tt, half, 1) * dn_j
            o_ref[:, j * n + blk * LANES:j * n + (blk + 1) * LANES] = out.astype(o_ref.dtype)
    o_ref[:, 2 * n:3 * n] = _dot(xb, w_ref[:, 2 * n:3 * n]).astype(o_ref.dtype)


def _rope_inv_table():
    inv = ROPE_THETA ** (-jnp.arange(0, ROT_DIM, 2, dtype=F32) / ROT_DIM)
    head = jnp.concatenate([inv, inv, jnp.zeros((ATTN_HEAD_DIM - ROT_DIM,), F32)])
    return jnp.tile(head, LANES // ATTN_HEAD_DIM)[None, :]


def _qkv_rope(h, w_qkv, positions):
    m, k = h.shape
    n = w_qkv.shape[1]
    tm = min(MM_TM, m)
    pos = positions.reshape(m, 1)
    return pl.pallas_call(
        _qkv_rope_kernel,
        grid=(m // tm,),
        in_specs=[pl.BlockSpec((tm, k), lambda i: (i, 0)),
                  pl.BlockSpec((k, n), lambda i: (0, 0)),
                  pl.BlockSpec((tm, 1), lambda i: (i, 0)),
                  pl.BlockSpec((1, LANES), lambda i: (0, 0))],
        out_specs=pl.BlockSpec((tm, n), lambda i: (i, 0)),
        out_shape=jax.ShapeDtypeStruct((m, n), BF16),
        compiler_params=_arb(1),
        name="mm_qkv_rope",
    )(h, w_qkv, pos, _rope_inv_table())


def _attn_kernel(q_ref, k_ref, v_ref, lq1_ref, lk1_ref, lq2_ref, lk2_ref, sw_ref, o_ref,
                 vx_ref, m0_ref, m1_ref, acc0_ref, acc1_ref, *, lambda_init):
    m_refs = (m0_ref, m1_ref)
    acc_refs = (acc0_ref, acc1_ref)
    i = pl.program_id(2)
    tq = q_ref.shape[0]

    @pl.when(i == 0)
    def _():
        vx_ref[:, 0:LANES] = v_ref[...]
        vx_ref[:, LANES:] = jnp.ones((vx_ref.shape[0], LANES), vx_ref.dtype)

    q = q_ref[...]
    lane = lax.broadcasted_iota(I32, (tq, LANES), 1)
    zero = jnp.zeros((), q.dtype)
    qs = (jnp.where(lane < ATTN_HEAD_DIM, q, zero), jnp.where(lane >= ATTN_HEAD_DIM, q, zero))
    for c in range(2):
        m_refs[c][...] = jnp.full(m_refs[c].shape, -jnp.inf, F32)
        acc_refs[c][...] = jnp.zeros(acc_refs[c].shape, F32)

    def step(off, width, masked):
        off = pl.multiple_of(off, tq)
        kb = k_ref[pl.ds(off, width), :]
        vb = vx_ref[pl.ds(off, width), :]
        ss = [lax.dot_general(qs[c], kb, (((1,), (1,)), ((), ())), preferred_element_type=F32)
              for c in range(2)]
        if masked:
            row_i = lax.broadcasted_iota(I32, (tq, width), 0)
            col_i = lax.broadcasted_iota(I32, (tq, width), 1)
            ss = [jnp.where(row_i >= col_i, s, -jnp.inf) for s in ss]
        m_prev = [m_refs[c][...] for c in range(2)]
        mn = [jnp.maximum(m_prev[c], jnp.max(ss[c], axis=-1, keepdims=True)) for c in range(2)]
        ps = [jnp.exp2(ss[c] - jnp.tile(mn[c], (1, width // LANES))).astype(BF16) for c in range(2)]
        for c in range(2):
            alpha = jnp.exp2(m_prev[c] - mn[c])
            acc_refs[c][...] = jnp.tile(alpha, (1, 2)) * acc_refs[c][...] + _dot(ps[c], vb)
            m_refs[c][...] = mn[c]

    def loop(n, fn):
        lax.fori_loop(0, n, lambda j, carry: (fn(j), carry)[1], 0)

    loop(i // 2, lambda j: step(j * (2 * tq), 2 * tq, False))
    loop(i % 2, lambda j: step((i - 1) * tq, tq, False))
    step(i * tq, tq, True)

    lam = (jnp.exp(jnp.sum(lq1_ref[...] * lk1_ref[...], axis=-1, keepdims=True))
           - jnp.exp(jnp.sum(lq2_ref[...] * lk2_ref[...], axis=-1, keepdims=True)) + lambda_init)
    a1 = acc0_ref[...]
    a2 = acc1_ref[...]
    o = a1[:, :LANES] / a1[:, LANES:] - lam * (a2[:, :LANES] / a2[:, LANES:])
    o = o * lax.rsqrt(jnp.mean(o * o, axis=-1, keepdims=True) + NORM_EPS)
    o_ref[...] = (o * sw_ref[...] * (1.0 - lambda_init)).astype(o_ref.dtype)


def _diff_attention(qkv, lq1, lk1, lq2, lk2, subln_w, lambda_init, batch, seq):
    t = batch * seq
    tq = min(ATT_TQ, seq)
    nq = seq // tq
    h_n = ATTN_N_HEADS
    vec = lambda b, h, i: (0, 0)
    return pl.pallas_call(
        functools.partial(_attn_kernel, lambda_init=lambda_init),
        grid=(batch, h_n, nq),
        in_specs=[pl.BlockSpec((tq, LANES), lambda b, h, i: (b * nq + i, h)),
                  pl.BlockSpec((seq, LANES), lambda b, h, i: (b, h_n + h)),
                  pl.BlockSpec((seq, LANES), lambda b, h, i: (b, 2 * h_n + h)),
                  pl.BlockSpec((1, ATTN_HEAD_DIM), vec), pl.BlockSpec((1, ATTN_HEAD_DIM), vec),
                  pl.BlockSpec((1, ATTN_HEAD_DIM), vec), pl.BlockSpec((1, ATTN_HEAD_DIM), vec),
                  pl.BlockSpec((1, ATTN_V_DIM), vec)],
        out_specs=pl.BlockSpec((tq, LANES), lambda b, h, i: (b * nq + i, h)),
        out_shape=jax.ShapeDtypeStruct((t, h_n * ATTN_V_DIM), BF16),
        scratch_shapes=[pltpu.VMEM((seq, 2 * LANES), BF16),
                        pltpu.VMEM((tq, LANES), F32), pltpu.VMEM((tq, LANES), F32),
                        pltpu.VMEM((tq, 2 * LANES), F32), pltpu.VMEM((tq, 2 * LANES), F32)],
        compiler_params=_arb(3),
        name="diff_attn",
    )(qkv, qkv, qkv, lq1[None, :], lk1[None, :], lq2[None, :], lk2[None, :], subln_w[None, :])


def kernel(x, positions, ln_mix_g, ln_mix_b, ln_ffn_g, ln_ffn_b, ssm_w_in, ssm_conv_w, ssm_conv_b, ssm_dt_bias, ssm_a_log, ssm_d, ssm_norm_w, ssm_w_out, attn_w_qkv, attn_lam_q1, attn_lam_k1, attn_lam_q2, attn_lam_k2, attn_subln_w, attn_w_o, moe_w_group, moe_w_expert, moe_w_gate, moe_w_up, moe_w_down):
    batch, seq, d = x.shape
    t = batch * seq
    h = x.reshape(t, d)

    w_in = ssm_w_in[0].astype(BF16)
    w_zx = w_in[:, :SSM_ZX_DIM]

    zs, xs, bc = _inproj_conv(h, w_zx, ssm_conv_w[0], ssm_conv_b[0], seq)
    yg = _ssd(zs, xs, bc, h, w_in[:, SSM_ZX_DIM:], ssm_dt_bias[0],
              ssm_a_log[0], ssm_d[0], ssm_norm_w[0], batch, seq)
    h, ht, eid, wt = _mm_ln_router(yg, ssm_w_out[0].astype(BF16), h, ln_mix_g[0], ln_mix_b[0],
                               _router_weights(moe_w_group[0], moe_w_expert[0]), "mm_ssm_out_ln_router")
    h = _hier_moe_ln(h, ht, eid, wt, moe_w_gate, moe_w_up, moe_w_down, ln_ffn_g[0], ln_ffn_b[0], 0)

    lambda_init = 0.8 - 0.6 * math.exp(-0.3 * 1)
    qkv = _qkv_rope(h, attn_w_qkv[0].astype(BF16), positions)
    o = _diff_attention(qkv, attn_lam_q1[0], attn_lam_k1[0], attn_lam_q2[0], attn_lam_k2[0],
                        attn_subln_w[0], lambda_init, batch, seq)
    h, ht, eid, wt = _mm_ln_router(o, attn_w_o[0].astype(BF16), h, ln_mix_g[1], ln_mix_b[1],
                               _router_weights(moe_w_group[1], moe_w_expert[1]), "mm_attn_out_ln_router")
    h = _hier_moe_ln(h, ht, eid, wt, moe_w_gate, moe_w_up, moe_w_down, ln_ffn_g[1], ln_ffn_b[1], 1)
    return h.reshape(batch, seq, d)
```

```python
import functools
import math

import jax
import jax.numpy as jnp
from jax import lax
from jax.experimental import pallas as pl
from jax.experimental.pallas import tpu as pltpu

F32 = jnp.float32
BF16 = jnp.bfloat16
I32 = jnp.int32

D_MODEL = 1024
DEPTH = 2
SSM_D_INNER = 2048
SSM_HEADDIM = 64
SSM_N_HEADS = 32
SSM_N_GROUPS = 8
SSM_HEADS_PER_GROUP = 4
SSM_D_STATE = 128
SSM_CONV_K = 4
SSM_CHUNK = 128
SSM_GROUP_WIDTH = SSM_HEADS_PER_GROUP * SSM_HEADDIM
SSM_ZX_DIM = 2 * SSM_D_INNER + 2 * SSM_N_GROUPS * SSM_D_STATE
ATTN_HEAD_DIM = 64
ATTN_N_HEADS = 8
ATTN_V_DIM = 128
ROT_DIM = 16
ROPE_THETA = 500000.0
MOE_GROUPS = 4
MOE_EXPERTS_PER_GROUP = 8
MOE_N_EXPERTS = 32
MOE_TOP_K = 2
MOE_D_FF = 512
DEEPNORM_ALPHA = (2 * DEPTH) ** 0.25
NORM_EPS = 1e-5

LANES = 128
ROW_SLABS = D_MODEL // LANES
CONV_TAIL = 8
CONV_ROWS = 128

MM_TM = 512
MM_TM_WIDE = 256
MM_TN = 1024
FFN_BLK = 512
DSP_TM = 512
CMB_TM = 256
ATT_TQ = 512


def _arb(n):
    return pltpu.CompilerParams(dimension_semantics=("arbitrary",) * n,
                                vmem_limit_bytes=56 * 1024 * 1024)


def _sigmoid(x):
    return 1.0 / (1.0 + jnp.exp(-x))


def _silu(x):
    return x * _sigmoid(x)


def _softplus(x):
    return jnp.maximum(x, 0.0) + jnp.log(1.0 + jnp.exp(-jnp.abs(x)))


def _layer_norm(y, g, b):
    mu = jnp.mean(y, axis=-1, keepdims=True)
    d = y - mu
    var = jnp.mean(d * d, axis=-1, keepdims=True)
    return d * lax.rsqrt(var + NORM_EPS) * g + b


def _split3(a):
    a1 = a.astype(BF16)
    r1 = a - a1.astype(F32)
    a2 = r1.astype(BF16)
    a3 = (r1 - a2.astype(F32)).astype(BF16)
    return a1, a2, a3


def _dot(a, b):
    return jnp.dot(a, b, preferred_element_type=F32)


def _dot_sel(a, sel):
    a1, a2, a3 = _split3(a)
    return _dot(a1, sel) + _dot(a2, sel) + _dot(a3, sel)


def _dot_f32(a, b):
    a1, a2, a3 = _split3(a)
    b1, b2, b3 = _split3(b)
    return (_dot(a1, b1) + _dot(a1, b2) + _dot(a2, b1)
            + _dot(a2, b2) + _dot(a1, b3) + _dot(a3, b1))


def _mm_kernel(x_ref, w_ref, o_ref):
    xb = x_ref[...].astype(BF16)
    for j in range(o_ref.shape[1] // MM_TN):
        cols = slice(j * MM_TN, (j + 1) * MM_TN)
        o_ref[:, cols] = _dot(xb, w_ref[:, cols]).astype(o_ref.dtype)


def _matmul(x, w, out_dtype):
    m, k = x.shape
    n = w.shape[1]
    tm = min(MM_TM_WIDE, m)
    return pl.pallas_call(
        _mm_kernel,
        grid=(m // tm,),
        in_specs=[pl.BlockSpec((tm, k), lambda i: (i, 0)), pl.BlockSpec((k, n), lambda i: (0, 0))],
        out_specs=pl.BlockSpec((tm, n), lambda i: (i, 0)),
        out_shape=jax.ShapeDtypeStruct((m, n), out_dtype),
        compiler_params=_arb(1),
        name="mm_inproj",
    )(x, w)


def _ssd_kernel(z_ref, xs_ref, bc_ref, x_ref, wdt_ref, cw_ref, cb_ref, dtb_ref, alog_ref, dskip_ref,
                nw_ref, e64_ref, e128_ref, o_ref, stage_x, stage_bc, state, acp_s, dtp_s, act_s):
    G, W, N = SSM_N_GROUPS, SSM_GROUP_WIDTH, SSM_D_STATE
    L = z_ref.shape[0]
    c = pl.program_id(1)

    @pl.when(c == 0)
    def _():
        state[...] = jnp.zeros(state.shape, F32)
        for stage in (stage_x, stage_bc):
            stage[0:CONV_TAIL, :] = jnp.zeros((CONV_TAIL, stage.shape[1]), F32)

    @pl.when(c != 0)
    def _():
        for stage in (stage_x, stage_bc):
            stage[0:CONV_TAIL, :] = stage[L:L + CONV_TAIL, :]

    stage_x[CONV_TAIL:CONV_TAIL + L, :] = xs_ref[...]
    stage_bc[CONV_TAIL:CONV_TAIL + L, :] = bc_ref[...]

    def conv(stage, c0, width, w0):
        taps = [cw_ref[k:k + 1, w0:w0 + width] for k in range(SSM_CONV_K)]
        bias = cb_ref[:, w0:w0 + width]
        outs = []
        for rb in range(L // CONV_ROWS):
            r0 = rb * CONV_ROWS
            u = stage[r0:r0 + CONV_ROWS + CONV_TAIL, c0:c0 + width]
            out = bias + taps[SSM_CONV_K - 1] * u[CONV_TAIL:, :]
            for k in range(SSM_CONV_K - 1):
                back = SSM_CONV_K - 1 - k
                out = out + taps[k] * pltpu.roll(u, back, 0)[CONV_TAIL:, :]
            outs.append(_silu(out))
        return jnp.concatenate(outs, axis=0)

    xc = [conv(stage_x, g * W, W, g * W) for g in range(G)]
    bcv = [conv(stage_bc, g * N, N, SSM_D_INNER + g * N).astype(BF16) for g in range(G)]
    ccv = [conv(stage_bc, G * N + g * N, N, SSM_D_INNER + G * N + g * N).astype(BF16) for g in range(G)]

    def x_of(g):
        return xc[g]

    def b_of(g):
        return bcv[g]

    def c_of(g):
        return ccv[g]

    dt = _softplus(_dot(x_ref[...].astype(BF16), wdt_ref[...].astype(BF16)) + dtb_ref[...])
    a = -jnp.exp(alog_ref[...])
    row_i = lax.broadcasted_iota(I32, (L, L), 0)
    col_i = lax.broadcasted_iota(I32, (L, L), 1)
    causal = row_i >= col_i
    a1, a2, a3 = _split3(dt * a)
    tri = causal.astype(BF16)
    ac = _dot(tri, a1) + _dot(tri, a2) + _dot(tri, a3)
    act_s[...] = ac.T
    for n, part in enumerate(_split3(ac)):
        acp_s[n] = part
    for n, part in enumerate(_split3(dt)[:2]):
        dtp_s[n] = part
    lane = lax.broadcasted_iota(I32, (L, LANES), 1)
    first_half = lane < SSM_HEADDIM

    def select(parts_ref, n_parts, sel):
        out = _dot(parts_ref[0], sel)
        for n in range(1, n_parts):
            out = out + _dot(parts_ref[n], sel)
        return out

    gs = range(G)
    col4 = [select(acp_s, 3, e128_ref[g]) for g in gs]
    dt_e = [select(dtp_s, 2, e64_ref[g]) for g in gs]
    cb = [lax.dot_general(c_of(g), b_of(g), (((1,), (1,)), ((), ())), preferred_element_type=F32)
          for g in gs]
    y_cs = [_dot(c_of(g), state[g].astype(BF16)) for g in gs]
    a_e = [jnp.concatenate(
        [jnp.where(first_half, c4[:, 2 * p * LANES:(2 * p + 1) * LANES],
                   c4[:, (2 * p + 1) * LANES:(2 * p + 2) * LANES]) for p in range(2)], axis=1)
        for c4 in col4]
    xdt = [x_of(g) * dt_e[g] for g in gs]
    xdt_b = [v.astype(BF16) for v in xdt]
    ys = [[None] * SSM_HEADS_PER_GROUP for _ in gs]
    for r in range(SSM_HEADS_PER_GROUP):
        for g in gs:
            row = act_s[SSM_HEADS_PER_GROUP * g + r:SSM_HEADS_PER_GROUP * g + r + 1, :]
            seg = col4[g][:, LANES * r:LANES * (r + 1)] - row
            dec = jnp.where(causal, jnp.exp(seg), 0.0)
            xp = xdt_b[g][:, LANES * (r // 2):LANES * (r // 2 + 1)]
            ys[g][r] = _dot((cb[g] * dec).astype(BF16), xp)
    for g in gs:
        y_diag = jnp.concatenate([jnp.where(first_half, ys[g][2 * p], ys[g][2 * p + 1])
                                  for p in range(2)], axis=1)
        al_e = a_e[g][L - 1:L, :]
        new = lax.dot_general(b_of(g), (xdt[g] * jnp.exp(al_e - a_e[g])).astype(BF16),
                              (((0,), (0,)), ((), ())), preferred_element_type=F32)
        y = y_diag + y_cs[g] * jnp.exp(a_e[g]) + x_of(g) * dskip_ref[g]
        state[g] = state[g] * jnp.exp(al_e) + new
        yg = y * _silu(z_ref[:, g * W:(g + 1) * W])
        ms = jnp.mean(yg * yg, axis=-1, keepdims=True)
        o_ref[:, g * W:(g + 1) * W] = (yg * lax.rsqrt(ms + NORM_EPS)
                                       * nw_ref[:, g * W:(g + 1) * W]).astype(o_ref.dtype)


def _head_selectors():
    G, R = SSM_N_GROUPS, SSM_HEADS_PER_GROUP
    h = jnp.arange(LANES)[None, :, None]
    g = jnp.arange(G)[:, None, None]
    j64 = jnp.arange(SSM_GROUP_WIDTH)[None, None, :]
    j128 = jnp.arange(R * LANES)[None, None, :]
    e64 = (h == g * R + j64 // SSM_HEADDIM).astype(BF16)
    e128 = (h == g * R + j128 // LANES).astype(BF16)
    return e64, e128


def _ssd(zx, x2, w_dt, conv_w, conv_b, dt_bias, a_log, d_skip, norm_w, batch, seq):
    L, G, W, N = SSM_CHUNK, SSM_N_GROUPS, SSM_GROUP_WIDTH, SSM_D_STATE
    nc = seq // L
    t = batch * seq
    pad = LANES - SSM_N_HEADS
    wdt = jnp.pad(w_dt, ((0, 0), (0, pad)))
    dtb = jnp.pad(dt_bias, (0, pad))[None, :]
    alog = jnp.pad(a_log, (0, pad))[None, :]
    dsk = jnp.repeat(d_skip, SSM_HEADDIM).reshape(G, 1, W)
    e64, e128 = _head_selectors()
    conv_dim = conv_w.shape[1]
    row = lambda b, c: (b * nc + c, 0)
    full2 = lambda b, c: (0, 0)
    full3 = lambda b, c: (0, 0, 0)
    return pl.pallas_call(
        _ssd_kernel,
        grid=(batch, nc),
        in_specs=[pl.BlockSpec((L, SSM_D_INNER), row),
                  pl.BlockSpec((L, SSM_D_INNER), lambda b, c: (b * nc + c, 1)),
                  pl.BlockSpec((L, 2 * G * N), lambda b, c: (b * nc + c, 2)),
                  pl.BlockSpec((L, D_MODEL), row),
                  pl.BlockSpec((D_MODEL, LANES), full2),
                  pl.BlockSpec((SSM_CONV_K, conv_dim), full2),
                  pl.BlockSpec((1, conv_dim), full2),
                  pl.BlockSpec((1, LANES), full2),
                  pl.BlockSpec((1, LANES), full2),
                  pl.BlockSpec((G, 1, W), full3),
                  pl.BlockSpec((1, SSM_D_INNER), full2),
                  pl.BlockSpec((G, LANES, W), full3),
                  pl.BlockSpec((G, LANES, SSM_HEADS_PER_GROUP * LANES), full3)],
        out_specs=pl.BlockSpec((L, SSM_D_INNER), row),
        out_shape=jax.ShapeDtypeStruct((t, SSM_D_INNER), BF16),
        scratch_shapes=[pltpu.VMEM((CONV_TAIL + L, SSM_D_INNER), F32),
                        pltpu.VMEM((CONV_TAIL + L, 2 * G * N), F32),
                        pltpu.VMEM((G, N, W), F32),
                        pltpu.VMEM((3, L, LANES), BF16),
                        pltpu.VMEM((2, L, LANES), BF16),
                        pltpu.VMEM((LANES, L), F32)],
        compiler_params=_arb(2),
        name="ssd_scan",
    )(zx, zx, zx, x2, wdt, conv_w, conv_b[None, :], dtb, alog, dsk, norm_w[None, :], e64, e128)


def _route(h, wr):
    tm = h.shape[0]
    logits = _dot(h.astype(BF16), wr)
    lane_i = lax.broadcasted_iota(I32, (tm, LANES), 1)
    lane = lane_i.astype(F32)
    neg = -jnp.inf
    big = float(LANES)

    def first_argmax(v, vmax):
        return jnp.min(jnp.where(v == vmax, lane, big), axis=-1, keepdims=True)

    gl = jnp.where((lane_i >= MOE_N_EXPERTS) & (lane_i < MOE_N_EXPERTS + MOE_GROUPS), logits, neg)
    gm = jnp.max(gl, axis=-1, keepdims=True)
    g_sel = first_argmax(gl, gm) - float(MOE_N_EXPERTS)
    g_gate = 1.0 / jnp.sum(jnp.exp(gl - gm), axis=-1, keepdims=True)
    lo = g_sel * float(MOE_EXPERTS_PER_GROUP)
    el = jnp.where((lane >= lo) & (lane < lo + float(MOE_EXPERTS_PER_GROUP)), logits, neg)
    m1 = jnp.max(el, axis=-1, keepdims=True)
    i1 = first_argmax(el, m1)
    el2 = jnp.where(lane == i1, neg, el)
    m2 = jnp.max(el2, axis=-1, keepdims=True)
    i2 = first_argmax(el2, m2)
    p2 = jnp.exp(m2 - m1)
    t1 = 1.0 / (1.0 + p2)
    t2 = p2 / (1.0 + p2)
    eid = jnp.where(lane_i == 0, i1, jnp.where(lane_i == 1, i2, 0.0)).astype(I32)
    wt = jnp.where(lane_i == 0, g_gate * t1, jnp.where(lane_i == 1, g_gate * t2, 0.0))
    return eid, wt


def _router_weights(w_group, w_expert):
    pad = LANES - MOE_N_EXPERTS - MOE_GROUPS
    return jnp.pad(jnp.concatenate([w_expert, w_group], axis=1), ((0, 0), (0, pad))).astype(BF16)


def _store_token_tiles(ref, v):
    rows = v.shape[0]
    for s in range(ROW_SLABS):
        ref[pl.ds(s, rows, stride=ROW_SLABS), :] = v[:, s * LANES:(s + 1) * LANES]


def _load_token_tiles(ref, rows):
    return jnp.concatenate([ref[pl.ds(s, rows, stride=ROW_SLABS), :] for s in range(ROW_SLABS)], axis=1)


def _mm_ln_router_kernel(x_ref, w_ref, r_ref, g_ref, b_ref, wr_ref, h_ref, ht_ref, eid_ref, wt_ref):
    y = DEEPNORM_ALPHA * r_ref[...] + _dot(x_ref[...].astype(BF16), w_ref[...])
    h = _layer_norm(y, g_ref[...], b_ref[...])
    h_ref[...] = h
    _store_token_tiles(ht_ref, h)
    eid, wt = _route(h, wr_ref[...])
    eid_ref[...] = eid
    wt_ref[...] = wt


def _mm_ln_router(x, w, resid, g, b, wr, name):
    m, k = x.shape
    d = w.shape[1]
    tm = min(MM_TM, m)
    row = lambda i: (i, 0)
    full = lambda i: (0, 0)
    return pl.pallas_call(
        _mm_ln_router_kernel,
        grid=(m // tm,),
        in_specs=[pl.BlockSpec((tm, k), row), pl.BlockSpec((k, d), full), pl.BlockSpec((tm, d), row),
                  pl.BlockSpec((1, d), full), pl.BlockSpec((1, d), full), pl.BlockSpec((d, LANES), full)],
        out_specs=[pl.BlockSpec((tm, d), row), pl.BlockSpec((tm * ROW_SLABS, LANES), row),
                   pl.BlockSpec((tm, LANES), row), pl.BlockSpec((tm, LANES), row)],
        out_shape=[jax.ShapeDtypeStruct((m, d), F32), jax.ShapeDtypeStruct((m * ROW_SLABS, LANES), F32),
                   jax.ShapeDtypeStruct((m, LANES), I32), jax.ShapeDtypeStruct((m, LANES), F32)],
        compiler_params=_arb(1),
        name=name,
    )(x, w, resid, g[None, :], b[None, :], wr)


def _dispatch_kernel(eid_ref, h_ref, wg_ref, wu_ref, wd_ref, dest_ref, be_ref, nu_ref, wgb_ref, wub_ref, wdb_ref,
                     xs_hbm, cnt, base, upper, zbuf, dst_v, dst_s, pe_v, pe_s, sem_z, sem_r, sem_s,
                     *, blk, cast_steps):
    ph = pl.program_id(0)
    i = pl.program_id(1)

    tm = eid_ref.shape[0]
    n_e = LANES
    eid_t = eid_ref[...].astype(F32).T
    sub = lax.broadcasted_iota(I32, (n_e, tm), 0).astype(F32)
    oh = [(sub == eid_t[k:k + 1, :]).astype(F32) for k in range(MOE_TOP_K)]
    tot = [jnp.sum(o, axis=1, keepdims=True) for o in oh]

    @pl.when((ph == 0) & (i == 0))
    def _():
        cnt[...] = jnp.zeros(cnt.shape, F32)
        r_i = lax.broadcasted_iota(I32, (tm, tm), 0)
        c_i = lax.broadcasted_iota(I32, (tm, tm), 1)
        upper[...] = (r_i < c_i).astype(BF16)

    @pl.when(ph == 0)
    def _():
        cnt[...] += jnp.broadcast_to(tot[0] + tot[1], cnt.shape)

    @pl.when((ph == 1) & (i == 0))
    def _():
        counts = cnt[...]
        padded = jnp.floor((counts + float(blk - 1)) * (1.0 / blk)) * float(blk)
        r_i = lax.broadcasted_iota(I32, (n_e, n_e), 0)
        c_i = lax.broadcasted_iota(I32, (n_e, n_e), 1)
        tril = (r_i >= c_i).astype(BF16)
        p1, p2, p3 = _split3(padded)
        pends = _dot(tril, p1) + _dot(tril, p2) + _dot(tril, p3)
        base[...] = pends - padded
        nbp = be_ref.shape[1]
        blk_start = lax.broadcasted_iota(I32, (n_e, nbp), 1).astype(F32) * float(blk)
        is_e = lax.broadcasted_iota(I32, (n_e, nbp), 0) < MOE_N_EXPERTS
        done = jnp.where(is_e & (jnp.tile(pends, (1, nbp // LANES)) <= blk_start), 1.0, 0.0)
        be = jnp.minimum(jnp.sum(done, axis=0, keepdims=True), float(MOE_N_EXPERTS - 1))
        be_ref[...] = jnp.broadcast_to(be, be_ref.shape).astype(I32)
        last = pends[MOE_N_EXPERTS - 1:MOE_N_EXPERTS, :] * (1.0 / blk)
        nu_ref[...] = jnp.broadcast_to(last, nu_ref.shape).astype(I32)
        zbuf[...] = jnp.zeros(zbuf.shape, F32)
        row8 = lax.broadcasted_iota(I32, (8, LANES), 0)
        pe_v[...] = jnp.where(row8 == 0, pends.T[0:8, :], counts.T[0:8, :]).astype(I32)
        cp = pltpu.make_async_copy(pe_v, pe_s, sem_s)
        cp.start()
        cp.wait()

        brows = blk * ROW_SLABS

        def zero_copy(e):
            start = pl.multiple_of((pe_s[0, e] - blk) * ROW_SLABS, brows)
            return pltpu.make_async_copy(zbuf, xs_hbm.at[pl.ds(start, brows)], sem_z)

        def tail_copy(b):
            return pltpu.make_async_copy(zbuf, xs_hbm.at[pl.ds(pl.multiple_of(b * brows, brows), brows)], sem_z)

        n_used = lax.shift_right_logical(pe_s[0, MOE_N_EXPERTS - 1], blk.bit_length() - 1)
        n_blocks = xs_hbm.shape[0] // brows
        for e in range(MOE_N_EXPERTS):
            @pl.when(pe_s[1, e] > 0)
            def _():
                zero_copy(e).start()
        lax.fori_loop(n_used, n_blocks, lambda b, c: (tail_copy(b).start(), c)[1], 0)
        for e in range(MOE_N_EXPERTS):
            @pl.when(pe_s[1, e] > 0)
            def _():
                zero_copy(e).wait()
        lax.fori_loop(n_used, n_blocks, lambda b, c: (tail_copy(b).wait(), c)[1], 0)

    @pl.when(ph == 1)
    def _():
        b0 = base[:, 0:1]
        c0 = _dot(oh[0].astype(BF16), upper[...])
        c1 = _dot(oh[1].astype(BF16), upper[...])
        d0 = jnp.sum(oh[0] * (b0 + c0), axis=0, keepdims=True)
        d1 = jnp.sum(oh[1] * (b0 + tot[0] + c1), axis=0, keepdims=True)
        base[...] += jnp.broadcast_to(tot[0] + tot[1], base.shape)
        row8 = lax.broadcasted_iota(I32, (8, tm), 0)
        dst = jnp.where(row8 == 0, d0, jnp.where(row8 == 1, d1, 0.0)).astype(I32)
        dest_ref[...] = dst
        dst_v[...] = dst
        cp = pltpu.make_async_copy(dst_v, dst_s, sem_s)
        cp.start()
        cp.wait()

        for r in range(tm):
            for k in range(MOE_TOP_K):
                slot = pl.multiple_of(dst_s[k, r] * ROW_SLABS, ROW_SLABS)
                pltpu.make_async_copy(h_ref.at[pl.ds(r * ROW_SLABS, ROW_SLABS)],
                                      xs_hbm.at[pl.ds(slot, ROW_SLABS)], sem_r).start(priority=k)
        @pl.when(i < cast_steps)
        def _():
            wgb_ref[...] = wg_ref[0].astype(BF16)
            wub_ref[...] = wu_ref[0].astype(BF16)
            wdb_ref[...] = wd_ref[0].astype(BF16)

        for k in range(MOE_TOP_K):
            pltpu.make_async_copy(h_ref, xs_hbm.at[pl.ds(0, tm * ROW_SLABS)], sem_r).wait()


def _moe_dispatch(ht, eid, w_gate, w_up, w_down, layer, blk, name):
    t = ht.shape[0] // ROW_SLABS
    assert blk & (blk - 1) == 0, "block size must be a power of two"
    tm = min(DSP_TM, t)
    nt = t // tm
    n_blocks = -(-(t * MOE_TOP_K) // blk) + MOE_N_EXPERTS
    nbp = -(-n_blocks // LANES) * LANES
    n_e, d, f = w_gate.shape[1:]
    eps = -(-n_e // nt)
    assert n_e % eps == 0
    cast_steps = n_e // eps
    tile = lambda ph, i: (i * ph, 0)
    w_in = lambda ph, i: (layer, jnp.minimum(i * ph, cast_steps - 1), 0, 0)
    w_out = lambda ph, i: (jnp.minimum(i * ph, cast_steps - 1), 0, 0)
    dest, be, nu, wgb, wub, wdb, xs = pl.pallas_call(
        functools.partial(_dispatch_kernel, blk=blk, cast_steps=cast_steps),
        grid=(2, nt),
        in_specs=[pl.BlockSpec((tm, LANES), lambda ph, i: (i, 0)),
                  pl.BlockSpec((tm * ROW_SLABS, LANES), tile),
                  pl.BlockSpec((1, eps, d, f), w_in), pl.BlockSpec((1, eps, d, f), w_in),
                  pl.BlockSpec((1, eps, f, d), w_in)],
        out_specs=[pl.BlockSpec((8, tm), lambda ph, i: (0, i * ph)),
                   pl.BlockSpec((8, nbp), lambda ph, i: (0, 0)),
                   pl.BlockSpec((8, LANES), lambda ph, i: (0, 0)),
                   pl.BlockSpec((eps, d, f), w_out), pl.BlockSpec((eps, d, f), w_out),
                   pl.BlockSpec((eps, f, d), w_out),
                   pl.BlockSpec(memory_space=pl.ANY)],
        out_shape=[jax.ShapeDtypeStruct((8, t), I32), jax.ShapeDtypeStruct((8, nbp), I32),
                   jax.ShapeDtypeStruct((8, LANES), I32),
                   jax.ShapeDtypeStruct((n_e, d, f), BF16), jax.ShapeDtypeStruct((n_e, d, f), BF16),
                   jax.ShapeDtypeStruct((n_e, f, d), BF16),
                   jax.ShapeDtypeStruct((n_blocks * blk * ROW_SLABS, LANES), F32)],
        scratch_shapes=[pltpu.VMEM((LANES, LANES), F32), pltpu.VMEM((LANES, LANES), F32),
                        pltpu.VMEM((tm, tm), BF16), pltpu.VMEM((blk * ROW_SLABS, LANES), F32),
                        pltpu.VMEM((8, tm), I32), pltpu.SMEM((8, tm), I32),
                        pltpu.VMEM((8, LANES), I32), pltpu.SMEM((8, LANES), I32),
                        pltpu.SemaphoreType.DMA(()), pltpu.SemaphoreType.DMA(()), pltpu.SemaphoreType.DMA(())],
        compiler_params=_arb(2),
        name=name,
    )(eid, ht, w_gate, w_up, w_down)
    return dest, be[0, :n_blocks], nu[0, :1], xs, n_blocks, (wgb, wub, wdb)


def _ffn_kernel(be_ref, nu_ref, x_ref, wg_ref, wu_ref, wd_ref, o_ref):
    i = pl.program_id(0)

    @pl.when(i < nu_ref[0])
    def _():
        blk = x_ref.shape[0] // ROW_SLABS
        xb = _load_token_tiles(x_ref, blk).astype(BF16)
        hid = _silu(_dot(xb, wg_ref[0])) * _dot(xb, wu_ref[0])
        _store_token_tiles(o_ref, _dot(hid.astype(BF16), wd_ref[0]))

    @pl.when(i >= nu_ref[0])
    def _():
        o_ref[...] = jnp.zeros(o_ref.shape, o_ref.dtype)


def _moe_ffn(xs, block_expert, n_used, n_blocks, weights, blk, name):
    w_gate, w_up, w_down = weights
    d, f = w_gate.shape[1], w_gate.shape[2]
    brows = blk * ROW_SLABS
    used = lambda i, be, nu: (jnp.minimum(i, nu[0] - 1), 0)
    every = lambda i, be, nu: (i, 0)
    expert = lambda i, be, nu: (be[i], 0, 0)
    grid_spec = pltpu.PrefetchScalarGridSpec(
        num_scalar_prefetch=2,
        grid=(n_blocks,),
        in_specs=[pl.BlockSpec((brows, LANES), used),
                  pl.BlockSpec((1, d, f), expert), pl.BlockSpec((1, d, f), expert),
                  pl.BlockSpec((1, f, d), expert)],
        out_specs=pl.BlockSpec((brows, LANES), every))
    return pl.pallas_call(
        _ffn_kernel,
        grid_spec=grid_spec,
        out_shape=jax.ShapeDtypeStruct((n_blocks * brows, LANES), F32),
        compiler_params=_arb(1),
        name=name,
    )(block_expert, n_used, xs, w_gate, w_up, w_down)


def _combine_ln_kernel(dst_ref, dstn_ref, h_ref, wt_ref, g_ref, b_ref, yb_hbm, o_ref, ybuf, sem):
    i = pl.program_id(0)
    n = pl.num_programs(0)
    tm = h_ref.shape[0]

    def start_gather(dref, slot):
        for r in range(tm):
            for k in range(MOE_TOP_K):
                src = pl.multiple_of(dref[k, r] * ROW_SLABS, ROW_SLABS)
                pltpu.make_async_copy(yb_hbm.at[pl.ds(src, ROW_SLABS)],
                                      ybuf.at[slot, k, pl.ds(r * ROW_SLABS, ROW_SLABS)],
                                      sem.at[slot]).start(priority=k)

    @pl.when(i == 0)
    def _():
        start_gather(dst_ref, 0)

    @pl.when(i + 1 < n)
    def _():
        start_gather(dstn_ref, (i + 1) % 2)

    slot = i % 2
    for k in range(MOE_TOP_K):
        pltpu.make_async_copy(yb_hbm.at[pl.ds(0, tm * ROW_SLABS)], ybuf.at[slot, k], sem.at[slot]).wait()
    wt = wt_ref[...]
    ffn = (wt[:, 0:1] * _load_token_tiles(ybuf.at[slot, 0], tm)
           + wt[:, 1:2] * _load_token_tiles(ybuf.at[slot, 1], tm))
    o_ref[...] = _layer_norm(DEEPNORM_ALPHA * h_ref[...] + ffn, g_ref[...], b_ref[...])


def _combine_ln(h, wt, dest, yb, g, b, name):
    t, d = h.shape
    tm = min(CMB_TM, t)
    nblk = t // tm
    row = lambda i: (i, 0)
    full = lambda i: (0, 0)
    return pl.pallas_call(
        _combine_ln_kernel,
        grid=(nblk,),
        in_specs=[pl.BlockSpec((8, tm), lambda i: (0, i), memory_space=pltpu.SMEM),
                  pl.BlockSpec((8, tm), lambda i: (0, jnp.minimum(i + 1, nblk - 1)),
                               memory_space=pltpu.SMEM),
                  pl.BlockSpec((tm, d), row), pl.BlockSpec((tm, LANES), row),
                  pl.BlockSpec((1, d), full), pl.BlockSpec((1, d), full),
                  pl.BlockSpec(memory_space=pl.ANY)],
        out_specs=pl.BlockSpec((tm, d), row),
        out_shape=jax.ShapeDtypeStruct((t, d), F32),
        scratch_shapes=[pltpu.VMEM((2, MOE_TOP_K, tm * ROW_SLABS, LANES), F32), pltpu.SemaphoreType.DMA((2,))],
        compiler_params=_arb(1),
        name=name,
    )(dest, dest, h, wt, g[None, :], b[None, :], yb)


def _hier_moe_ln(h, ht, eid, wt, w_gate, w_up, w_down, g, b, layer):
    dest, block_expert, n_used, xs, n_blocks, w_bf16 = _moe_dispatch(
        ht, eid, w_gate, w_up, w_down, layer, FFN_BLK, f"moe_dispatch{layer}")
    yb = _moe_ffn(xs, block_expert, n_used, n_blocks, w_bf16, FFN_BLK, f"moe_ffn{layer}")
    return _combine_ln(h, wt, dest, yb, g, b, f"moe_combine_ln{layer}")


def _qkv_rope_kernel(x_ref, w_ref, pos_ref, inv_ref, o_ref):
    xb = x_ref[...].astype(BF16)
    tm = xb.shape[0]
    n = D_MODEL
    ang = pos_ref[...].astype(F32) * inv_ref[...]
    lane = lax.broadcasted_iota(I32, (tm, LANES), 1)
    dd = lane & (ATTN_HEAD_DIM - 1)
    half = ROT_DIM // 2
    cosv = jnp.cos(ang)
    sinv = jnp.sin(ang)
    c_t = jnp.where(dd < ROT_DIM, cosv, 1.0)
    s_up = jnp.where(dd < half, -sinv, 0.0)
    s_dn = jnp.where((dd >= half) & (dd < ROT_DIM), sinv, 0.0)
    for j, sc in ((0, ATTN_HEAD_DIM ** -0.5 * math.log2(math.e)), (1, 1.0)):
        acc = _dot(xb, w_ref[:, j * n:(j + 1) * n])
        c_j, up_j, dn_j = c_t * sc, s_up * sc, s_dn * sc
        for blk in range(n // LANES):
            tt = acc[:, blk * LANES:(blk + 1) * LANES]
            out = tt * c_j + pltpu.roll(tt, LANES - half, 1) * up_j + pltpu.roll(tt, half, 1) * dn_j
            o_ref[:, j * n + blk * LANES:j * n + (blk + 1) * LANES] = out.astype(o_ref.dtype)
    o_ref[:, 2 * n:3 * n] = _dot(xb, w_ref[:, 2 * n:3 * n]).astype(o_ref.dtype)


def _rope_inv_table():
    inv = ROPE_THETA ** (-jnp.arange(0, ROT_DIM, 2, dtype=F32) / ROT_DIM)
    head = jnp.concatenate([inv, inv, jnp.zeros((ATTN_HEAD_DIM - ROT_DIM,), F32)])
    return jnp.tile(head, LANES // ATTN_HEAD_DIM)[None, :]


def _qkv_rope(h, w_qkv, positions):
    m, k = h.shape
    n = w_qkv.shape[1]
    tm = min(MM_TM, m)
    pos = positions.reshape(m, 1)
    return pl.pallas_call(
        _qkv_rope_kernel,
        grid=(m // tm,),
        in_specs=[pl.BlockSpec((tm, k), lambda i: (i, 0)),
                  pl.BlockSpec((k, n), lambda i: (0, 0)),
                  pl.BlockSpec((tm, 1), lambda i: (i, 0)),
                  pl.BlockSpec((1, LANES), lambda i: (0, 0))],
        out_specs=pl.BlockSpec((tm, n), lambda i: (i, 0)),
        out_shape=jax.ShapeDtypeStruct((m, n), BF16),
        compiler_params=_arb(1),
        name="mm_qkv_rope",
    )(h, w_qkv, pos, _rope_inv_table())


def _attn_kernel(q_ref, k_ref, v_ref, lq1_ref, lk1_ref, lq2_ref, lk2_ref, sw_ref, o_ref,
                 vx_ref, m0_ref, m1_ref, acc0_ref, acc1_ref, *, lambda_init):
    m_refs = (m0_ref, m1_ref)
    acc_refs = (acc0_ref, acc1_ref)
    i = pl.program_id(2)
    tq = q_ref.shape[0]

    @pl.when(i == 0)
    def _():
        vx_ref[:, 0:LANES] = v_ref[...]
        vx_ref[:, LANES:] = jnp.ones((vx_ref.shape[0], LANES), vx_ref.dtype)

    q = q_ref[...]
    lane = lax.broadcasted_iota(I32, (tq, LANES), 1)
    zero = jnp.zeros((), q.dtype)
    qs = (jnp.where(lane < ATTN_HEAD_DIM, q, zero), jnp.where(lane >= ATTN_HEAD_DIM, q, zero))
    for c in range(2):
        m_refs[c][...] = jnp.full(m_refs[c].shape, -jnp.inf, F32)
        acc_refs[c][...] = jnp.zeros(acc_refs[c].shape, F32)

    def step(off, width, diag_col=None):
        off = pl.multiple_of(off, tq)
        kb = k_ref[pl.ds(off, width), :]
        vb = vx_ref[pl.ds(off, width), :]
        ss = [lax.dot_general(qs[c], kb, (((1,), (1,)), ((), ())), preferred_element_type=F32)
              for c in range(2)]
        if diag_col is not None:
            row_i = lax.broadcasted_iota(I32, (tq, width), 0)
            col_i = lax.broadcasted_iota(I32, (tq, width), 1)
            ss = [jnp.where(row_i + diag_col >= col_i, s, -jnp.inf) for s in ss]
        m_prev = [m_refs[c][...] for c in range(2)]
        mn = [jnp.maximum(m_prev[c], jnp.max(ss[c], axis=-1, keepdims=True)) for c in range(2)]
        ps = [jnp.exp2(ss[c] - jnp.tile(mn[c], (1, width // LANES))).astype(BF16) for c in range(2)]
        for c in range(2):
            alpha = jnp.exp2(m_prev[c] - mn[c])
            acc_refs[c][...] = jnp.tile(alpha, (1, 2)) * acc_refs[c][...] + _dot(ps[c], vb)
            m_refs[c][...] = mn[c]

    def loop(n, fn):
        lax.fori_loop(0, n, lambda j, carry: (fn(j), carry)[1], 0)

    loop(i // 2, lambda j: step(j * (2 * tq), 2 * tq))

    @pl.when(i % 2 == 0)
    def _():
        step(i * tq, tq, diag_col=0)

    @pl.when(i % 2 == 1)
    def _():
        step((i - 1) * tq, 2 * tq, diag_col=tq)

    lam = (jnp.exp(jnp.sum(lq1_ref[...] * lk1_ref[...], axis=-1, keepdims=True))
           - jnp.exp(jnp.sum(lq2_ref[...] * lk2_ref[...], axis=-1, keepdims=True)) + lambda_init)
    a1 = acc0_ref[...]
    a2 = acc1_ref[...]
    o = a1[:, :LANES] / a1[:, LANES:] - lam * (a2[:, :LANES] / a2[:, LANES:])
    o = o * lax.rsqrt(jnp.mean(o * o, axis=-1, keepdims=True) + NORM_EPS)
    o_ref[...] = (o * sw_ref[...] * (1.0 - lambda_init)).astype(o_ref.dtype)


def _diff_attention(qkv, lq1, lk1, lq2, lk2, subln_w, lambda_init, batch, seq):
    t = batch * seq
    tq = min(ATT_TQ, seq)
    nq = seq // tq
    h_n = ATTN_N_HEADS
    vec = lambda b, h, i: (0, 0)
    return pl.pallas_call(
        functools.partial(_attn_kernel, lambda_init=lambda_init),
        grid=(batch, h_n, nq),
        in_specs=[pl.BlockSpec((tq, LANES), lambda b, h, i: (b * nq + i, h)),
                  pl.BlockSpec((seq, LANES), lambda b, h, i: (b, h_n + h)),
                  pl.BlockSpec((seq, LANES), lambda b, h, i: (b, 2 * h_n + h)),
                  pl.BlockSpec((1, ATTN_HEAD_DIM), vec), pl.BlockSpec((1, ATTN_HEAD_DIM), vec),
                  pl.BlockSpec((1, ATTN_HEAD_DIM), vec), pl.BlockSpec((1, ATTN_HEAD_DIM), vec),
                  pl.BlockSpec((1, ATTN_V_DIM), vec)],
        out_specs=pl.BlockSpec((tq, LANES), lambda b, h, i: (b * nq + i, h)),
        out_shape=jax.ShapeDtypeStruct((t, h_n * ATTN_V_DIM), BF16),
        scratch_shapes=[pltpu.VMEM((seq, 2 * LANES), BF16),
                        pltpu.VMEM((tq, LANES), F32), pltpu.VMEM((tq, LANES), F32),
                        pltpu.VMEM((tq, 2 * LANES), F32), pltpu.VMEM((tq, 2 * LANES), F32)],
        compiler_params=_arb(3),
        name="diff_attn",
    )(qkv, qkv, qkv, lq1[None, :], lk1[None, :], lq2[None, :], lk2[None, :], subln_w[None, :])


def kernel(x, positions, ln_mix_g, ln_mix_b, ln_ffn_g, ln_ffn_b, ssm_w_in, ssm_conv_w, ssm_conv_b, ssm_dt_bias, ssm_a_log, ssm_d, ssm_norm_w, ssm_w_out, attn_w_qkv, attn_lam_q1, attn_lam_k1, attn_lam_q2, attn_lam_k2, attn_subln_w, attn_w_o, moe_w_group, moe_w_expert, moe_w_gate, moe_w_up, moe_w_down):
    batch, seq, d = x.shape
    t = batch * seq
    h = x.reshape(t, d)

    w_in = ssm_w_in[0].astype(BF16)
    w_zx = w_in[:, :SSM_ZX_DIM]

    zx = _matmul(h, w_zx, F32)
    yg = _ssd(zx, h, w_in[:, SSM_ZX_DIM:], ssm_conv_w[0], ssm_conv_b[0], ssm_dt_bias[0],
              ssm_a_log[0], ssm_d[0], ssm_norm_w[0], batch, seq)
    h, ht, eid, wt = _mm_ln_router(yg, ssm_w_out[0].astype(BF16), h, ln_mix_g[0], ln_mix_b[0],
                               _router_weights(moe_w_group[0], moe_w_expert[0]), "mm_ssm_out_ln_router")
    h = _hier_moe_ln(h, ht, eid, wt, moe_w_gate, moe_w_up, moe_w_down, ln_ffn_g[0], ln_ffn_b[0], 0)

    lambda_init = 0.8 - 0.6 * math.exp(-0.3 * 1)
    qkv = _qkv_rope(h, attn_w_qkv[0].astype(BF16), positions)
    o = _diff_attention(qkv, attn_lam_q1[0], attn_lam_k1[0], attn_lam_q2[0], attn_lam_k2[0],
                        attn_subln_w[0], lambda_init, batch, seq)
    h, ht, eid, wt = _mm_ln_router(o, attn_w_o[0].astype(BF16), h, ln_mix_g[1], ln_mix_b[1],
                               _router_weights(moe_w_group[1], moe_w_expert[1]), "mm_attn_out_ln_router")
    h = _hier_moe_ln(h, ht, eid, wt, moe_w_gate, moe_w_up, moe_w_down, ln_ffn_g[1], ln_ffn_b[1], 1)
    return h.reshape(batch, seq, d)
```

```python
import functools
import math

import jax
import jax.numpy as jnp
from jax import lax
from jax.experimental import pallas as pl
from jax.experimental.pallas import tpu as pltpu

F32 = jnp.float32
BF16 = jnp.bfloat16
I32 = jnp.int32

D_MODEL = 1024
DEPTH = 2
SSM_D_INNER = 2048
SSM_HEADDIM = 64
SSM_N_HEADS = 32
SSM_N_GROUPS = 8
SSM_HEADS_PER_GROUP = 4
SSM_D_STATE = 128
SSM_CONV_K = 4
SSM_CHUNK = 128
SSM_GROUP_WIDTH = SSM_HEADS_PER_GROUP * SSM_HEADDIM
SSM_ZX_DIM = 2 * SSM_D_INNER + 2 * SSM_N_GROUPS * SSM_D_STATE
ATTN_HEAD_DIM = 64
ATTN_N_HEADS = 8
ATTN_V_DIM = 128
ROT_DIM = 16
ROPE_THETA = 500000.0
MOE_GROUPS = 4
MOE_EXPERTS_PER_GROUP = 8
MOE_N_EXPERTS = 32
MOE_TOP_K = 2
MOE_D_FF = 512
DEEPNORM_ALPHA = (2 * DEPTH) ** 0.25
NORM_EPS = 1e-5

LANES = 128
ROW_SLABS = D_MODEL // LANES
CONV_TAIL = 8
CONV_ROWS = 128

MM_TM = 512
MM_TM_WIDE = 256
MM_TN = 1024
FFN_BLK = 512
DSP_TM = 512
LN_SUBTILES = 4
CMB_TM = 256
ATT_TQ = 512


def _arb(n):
    return pltpu.CompilerParams(dimension_semantics=("arbitrary",) * n,
                                vmem_limit_bytes=56 * 1024 * 1024)


def _sigmoid(x):
    return 1.0 / (1.0 + jnp.exp(-x))


def _silu(x):
    return x * _sigmoid(x)


def _softplus(x):
    return jnp.maximum(x, 0.0) + jnp.log(1.0 + jnp.exp(-jnp.abs(x)))


def _layer_norm(y, g, b):
    mu = jnp.mean(y, axis=-1, keepdims=True)
    d = y - mu
    var = jnp.mean(d * d, axis=-1, keepdims=True)
    return d * lax.rsqrt(var + NORM_EPS) * g + b


def _split3(a):
    a1 = a.astype(BF16)
    r1 = a - a1.astype(F32)
    a2 = r1.astype(BF16)
    a3 = (r1 - a2.astype(F32)).astype(BF16)
    return a1, a2, a3


def _dot(a, b):
    return jnp.dot(a, b, preferred_element_type=F32)


def _dot_sel(a, sel):
    a1, a2, a3 = _split3(a)
    return _dot(a1, sel) + _dot(a2, sel) + _dot(a3, sel)


def _dot_f32(a, b):
    a1, a2, a3 = _split3(a)
    b1, b2, b3 = _split3(b)
    return (_dot(a1, b1) + _dot(a1, b2) + _dot(a2, b1)
            + _dot(a2, b2) + _dot(a1, b3) + _dot(a3, b1))


class _CastPlan:
    def __init__(self, w_gate, w_up, w_down, layer, steps, step_of):
        self.weights = (w_gate, w_up, w_down)
        self.n_e = w_gate.shape[1]
        self.eps = -(-self.n_e // steps)
        assert self.n_e % self.eps == 0
        self.n_cast = self.n_e // self.eps
        self.stride = steps // self.n_cast
        block = lambda *ids: jnp.minimum(step_of(*ids) // self.stride, self.n_cast - 1)
        self.in_specs = [pl.BlockSpec((1, self.eps) + w.shape[2:], lambda *ids: (layer, block(*ids), 0, 0))
                         for w in self.weights]
        self.out_specs = [pl.BlockSpec((self.eps,) + w.shape[2:], lambda *ids: (block(*ids), 0, 0))
                          for w in self.weights]
        self.out_shape = [jax.ShapeDtypeStruct(w.shape[1:], BF16) for w in self.weights]

    def emit(self, step, w_refs, o_refs, extra=True):
        @pl.when(extra & (step % self.stride == 0) & (step // self.stride < self.n_cast))
        def _():
            for w_ref, o_ref in zip(w_refs, o_refs):
                o_ref[...] = w_ref[0].astype(BF16)


def _mm_kernel(x_ref, w_ref, o_ref):
    xb = x_ref[...].astype(BF16)
    for j in range(o_ref.shape[1] // MM_TN):
        cols = slice(j * MM_TN, (j + 1) * MM_TN)
        o_ref[:, cols] = _dot(xb, w_ref[:, cols]).astype(o_ref.dtype)


def _matmul(x, w, out_dtype):
    m, k = x.shape
    n = w.shape[1]
    tm = min(MM_TM_WIDE, m)
    return pl.pallas_call(
        _mm_kernel,
        grid=(m // tm,),
        in_specs=[pl.BlockSpec((tm, k), lambda i: (i, 0)), pl.BlockSpec((k, n), lambda i: (0, 0))],
        out_specs=pl.BlockSpec((tm, n), lambda i: (i, 0)),
        out_shape=jax.ShapeDtypeStruct((m, n), out_dtype),
        compiler_params=_arb(1),
        name="mm_inproj",
    )(x, w)


def _ssd_kernel(z_ref, xs_ref, bc_ref, x_ref, wdt_ref, cw_ref, cb_ref, dtb_ref, alog_ref, dskip_ref,
                nw_ref, e64_ref, e128_ref, wg_ref, wu_ref, wd_ref, o_ref, wgb_ref, wub_ref, wdb_ref,
                stage_x, stage_bc, state, acp_s, dtp_s, act_s, *, cast):
    G, W, N = SSM_N_GROUPS, SSM_GROUP_WIDTH, SSM_D_STATE
    L = z_ref.shape[0]
    c = pl.program_id(1)
    cast.emit(pl.program_id(0) * pl.num_programs(1) + c, (wg_ref, wu_ref, wd_ref), (wgb_ref, wub_ref, wdb_ref))

    @pl.when(c == 0)
    def _():
        state[...] = jnp.zeros(state.shape, F32)
        for stage in (stage_x, stage_bc):
            stage[0:CONV_TAIL, :] = jnp.zeros((CONV_TAIL, stage.shape[1]), F32)

    @pl.when(c != 0)
    def _():
        for stage in (stage_x, stage_bc):
            stage[0:CONV_TAIL, :] = stage[L:L + CONV_TAIL, :]

    stage_x[CONV_TAIL:CONV_TAIL + L, :] = xs_ref[...]
    stage_bc[CONV_TAIL:CONV_TAIL + L, :] = bc_ref[...]

    def conv(stage, c0, width, w0):
        taps = [cw_ref[k:k + 1, w0:w0 + width] for k in range(SSM_CONV_K)]
        bias = cb_ref[:, w0:w0 + width]
        outs = []
        for rb in range(L // CONV_ROWS):
            r0 = rb * CONV_ROWS
            u = stage[r0:r0 + CONV_ROWS + CONV_TAIL, c0:c0 + width]
            out = bias + taps[SSM_CONV_K - 1] * u[CONV_TAIL:, :]
            for k in range(SSM_CONV_K - 1):
                back = SSM_CONV_K - 1 - k
                out = out + taps[k] * pltpu.roll(u, back, 0)[CONV_TAIL:, :]
            outs.append(_silu(out))
        return jnp.concatenate(outs, axis=0)

    xc = [conv(stage_x, g * W, W, g * W) for g in range(G)]
    bcv = [conv(stage_bc, g * N, N, SSM_D_INNER + g * N).astype(BF16) for g in range(G)]
    ccv = [conv(stage_bc, G * N + g * N, N, SSM_D_INNER + G * N + g * N).astype(BF16) for g in range(G)]

    def x_of(g):
        return xc[g]

    def b_of(g):
        return bcv[g]

    def c_of(g):
        return ccv[g]

    dt = _softplus(_dot(x_ref[...].astype(BF16), wdt_ref[...].astype(BF16)) + dtb_ref[...])
    a = -jnp.exp(alog_ref[...])
    row_i = lax.broadcasted_iota(I32, (L, L), 0)
    col_i = lax.broadcasted_iota(I32, (L, L), 1)
    causal = row_i >= col_i
    a1, a2, a3 = _split3(dt * a)
    tri = causal.astype(BF16)
    ac = _dot(tri, a1) + _dot(tri, a2) + _dot(tri, a3)
    act_s[...] = ac.T
    for n, part in enumerate(_split3(ac)):
        acp_s[n] = part
    for n, part in enumerate(_split3(dt)[:2]):
        dtp_s[n] = part
    lane = lax.broadcasted_iota(I32, (L, LANES), 1)
    first_half = lane < SSM_HEADDIM

    def select(parts_ref, n_parts, sel):
        out = _dot(parts_ref[0], sel)
        for n in range(1, n_parts):
            out = out + _dot(parts_ref[n], sel)
        return out

    gs = range(G)
    col4 = [select(acp_s, 3, e128_ref[g]) for g in gs]
    dt_e = [select(dtp_s, 2, e64_ref[g]) for g in gs]
    cb = [lax.dot_general(c_of(g), b_of(g), (((1,), (1,)), ((), ())), preferred_element_type=F32)
          for g in gs]
    y_cs = [_dot(c_of(g), state[g].astype(BF16)) for g in gs]
    a_e = [jnp.concatenate(
        [jnp.where(first_half, c4[:, 2 * p * LANES:(2 * p + 1) * LANES],
                   c4[:, (2 * p + 1) * LANES:(2 * p + 2) * LANES]) for p in range(2)], axis=1)
        for c4 in col4]
    xdt = [x_of(g) * dt_e[g] for g in gs]
    xdt_b = [v.astype(BF16) for v in xdt]
    ys = [[None] * SSM_HEADS_PER_GROUP for _ in gs]
    for r in range(SSM_HEADS_PER_GROUP):
        for g in gs:
            row = act_s[SSM_HEADS_PER_GROUP * g + r:SSM_HEADS_PER_GROUP * g + r + 1, :]
            seg = col4[g][:, LANES * r:LANES * (r + 1)] - row
            dec = jnp.where(causal, jnp.exp(seg), 0.0)
            xp = xdt_b[g][:, LANES * (r // 2):LANES * (r // 2 + 1)]
            ys[g][r] = _dot((cb[g] * dec).astype(BF16), xp)
    for g in gs:
        y_diag = jnp.concatenate([jnp.where(first_half, ys[g][2 * p], ys[g][2 * p + 1])
                                  for p in range(2)], axis=1)
        al_e = a_e[g][L - 1:L, :]
        new = lax.dot_general(b_of(g), (xdt[g] * jnp.exp(al_e - a_e[g])).astype(BF16),
                              (((0,), (0,)), ((), ())), preferred_element_type=F32)
        y = y_diag + y_cs[g] * jnp.exp(a_e[g]) + x_of(g) * dskip_ref[g]
        state[g] = state[g] * jnp.exp(al_e) + new
        yg = y * _silu(z_ref[:, g * W:(g + 1) * W])
        ms = jnp.mean(yg * yg, axis=-1, keepdims=True)
        o_ref[:, g * W:(g + 1) * W] = (yg * lax.rsqrt(ms + NORM_EPS)
                                       * nw_ref[:, g * W:(g + 1) * W]).astype(o_ref.dtype)


def _head_selectors():
    G, R = SSM_N_GROUPS, SSM_HEADS_PER_GROUP
    h = jnp.arange(LANES)[None, :, None]
    g = jnp.arange(G)[:, None, None]
    j64 = jnp.arange(SSM_GROUP_WIDTH)[None, None, :]
    j128 = jnp.arange(R * LANES)[None, None, :]
    e64 = (h == g * R + j64 // SSM_HEADDIM).astype(BF16)
    e128 = (h == g * R + j128 // LANES).astype(BF16)
    return e64, e128


def _ssd(zx, x2, w_dt, conv_w, conv_b, dt_bias, a_log, d_skip, norm_w, batch, seq, moe_weights, moe_layer):
    L, G, W, N = SSM_CHUNK, SSM_N_GROUPS, SSM_GROUP_WIDTH, SSM_D_STATE
    nc = seq // L
    cast = _CastPlan(*moe_weights, moe_layer, batch * nc, lambda b, c: b * nc + c)
    t = batch * seq
    pad = LANES - SSM_N_HEADS
    wdt = jnp.pad(w_dt, ((0, 0), (0, pad)))
    dtb = jnp.pad(dt_bias, (0, pad))[None, :]
    alog = jnp.pad(a_log, (0, pad))[None, :]
    dsk = jnp.repeat(d_skip, SSM_HEADDIM).reshape(G, 1, W)
    e64, e128 = _head_selectors()
    conv_dim = conv_w.shape[1]
    row = lambda b, c: (b * nc + c, 0)
    full2 = lambda b, c: (0, 0)
    full3 = lambda b, c: (0, 0, 0)
    yg, *w_bf16 = pl.pallas_call(
        functools.partial(_ssd_kernel, cast=cast),
        grid=(batch, nc),
        in_specs=[pl.BlockSpec((L, SSM_D_INNER), row),
                  pl.BlockSpec((L, SSM_D_INNER), lambda b, c: (b * nc + c, 1)),
                  pl.BlockSpec((L, 2 * G * N), lambda b, c: (b * nc + c, 2)),
                  pl.BlockSpec((L, D_MODEL), row),
                  pl.BlockSpec((D_MODEL, LANES), full2),
                  pl.BlockSpec((SSM_CONV_K, conv_dim), full2),
                  pl.BlockSpec((1, conv_dim), full2),
                  pl.BlockSpec((1, LANES), full2),
                  pl.BlockSpec((1, LANES), full2),
                  pl.BlockSpec((G, 1, W), full3),
                  pl.BlockSpec((1, SSM_D_INNER), full2),
                  pl.BlockSpec((G, LANES, W), full3),
                  pl.BlockSpec((G, LANES, SSM_HEADS_PER_GROUP * LANES), full3)] + cast.in_specs,
        out_specs=[pl.BlockSpec((L, SSM_D_INNER), row)] + cast.out_specs,
        out_shape=[jax.ShapeDtypeStruct((t, SSM_D_INNER), BF16)] + cast.out_shape,
        scratch_shapes=[pltpu.VMEM((CONV_TAIL + L, SSM_D_INNER), F32),
                        pltpu.VMEM((CONV_TAIL + L, 2 * G * N), F32),
                        pltpu.VMEM((G, N, W), F32),
                        pltpu.VMEM((3, L, LANES), BF16),
                        pltpu.VMEM((2, L, LANES), BF16),
                        pltpu.VMEM((LANES, L), F32)],
        compiler_params=_arb(2),
        name="ssd_scan",
    )(zx, zx, zx, x2, wdt, conv_w, conv_b[None, :], dtb, alog, dsk, norm_w[None, :], e64, e128, *cast.weights)
    return yg, tuple(w_bf16)


def _route(h, wr):
    tm = h.shape[0]
    logits = _dot(h.astype(BF16), wr)
    lane_i = lax.broadcasted_iota(I32, (tm, LANES), 1)
    lane = lane_i.astype(F32)
    neg = -jnp.inf
    big = float(LANES)

    def first_argmax(v, vmax):
        return jnp.min(jnp.where(v == vmax, lane, big), axis=-1, keepdims=True)

    gl = jnp.where((lane_i >= MOE_N_EXPERTS) & (lane_i < MOE_N_EXPERTS + MOE_GROUPS), logits, neg)
    gm = jnp.max(gl, axis=-1, keepdims=True)
    g_sel = first_argmax(gl, gm) - float(MOE_N_EXPERTS)
    g_gate = 1.0 / jnp.sum(jnp.exp(gl - gm), axis=-1, keepdims=True)
    lo = g_sel * float(MOE_EXPERTS_PER_GROUP)
    el = jnp.where((lane >= lo) & (lane < lo + float(MOE_EXPERTS_PER_GROUP)), logits, neg)
    m1 = jnp.max(el, axis=-1, keepdims=True)
    i1 = first_argmax(el, m1)
    el2 = jnp.where(lane == i1, neg, el)
    m2 = jnp.max(el2, axis=-1, keepdims=True)
    i2 = first_argmax(el2, m2)
    p2 = jnp.exp(m2 - m1)
    t1 = 1.0 / (1.0 + p2)
    t2 = p2 / (1.0 + p2)
    eid = jnp.where(lane_i == 0, i1, jnp.where(lane_i == 1, i2, 0.0)).astype(I32)
    wt = jnp.where(lane_i == 0, g_gate * t1, jnp.where(lane_i == 1, g_gate * t2, 0.0))
    return eid, wt


def _router_weights(w_group, w_expert):
    pad = LANES - MOE_N_EXPERTS - MOE_GROUPS
    return jnp.pad(jnp.concatenate([w_expert, w_group], axis=1), ((0, 0), (0, pad))).astype(BF16)


def _store_token_tiles(ref, v, first_token=0):
    rows = v.shape[0]
    for s in range(ROW_SLABS):
        ref[pl.ds(first_token * ROW_SLABS + s, rows, stride=ROW_SLABS), :] = v[:, s * LANES:(s + 1) * LANES]


def _load_token_tiles(ref, rows):
    return jnp.concatenate([ref[pl.ds(s, rows, stride=ROW_SLABS), :] for s in range(ROW_SLABS)], axis=1)


def _mm_ln_router_kernel(x_ref, w_ref, r_ref, g_ref, b_ref, wr_ref, h_ref, ht_ref, eid_ref, wt_ref):
    sub = x_ref.shape[0] // LN_SUBTILES
    rows = [slice(u * sub, (u + 1) * sub) for u in range(LN_SUBTILES)]
    ys = [DEEPNORM_ALPHA * r_ref[rs, :] + _dot(x_ref[rs, :].astype(BF16), w_ref[...]) for rs in rows]
    hs = [_layer_norm(y, g_ref[...], b_ref[...]) for y in ys]
    routes = [_route(h, wr_ref[...]) for h in hs]
    for u, rs in enumerate(rows):
        h_ref[rs, :] = hs[u]
        _store_token_tiles(ht_ref, hs[u], first_token=u * sub)
        eid_ref[rs, :] = routes[u][0]
        wt_ref[rs, :] = routes[u][1]


def _mm_ln_router(x, w, resid, g, b, wr, name):
    m, k = x.shape
    d = w.shape[1]
    tm = min(MM_TM, m)
    row = lambda i: (i, 0)
    full = lambda i: (0, 0)
    return pl.pallas_call(
        _mm_ln_router_kernel,
        grid=(m // tm,),
        in_specs=[pl.BlockSpec((tm, k), row), pl.BlockSpec((k, d), full), pl.BlockSpec((tm, d), row),
                  pl.BlockSpec((1, d), full), pl.BlockSpec((1, d), full), pl.BlockSpec((d, LANES), full)],
        out_specs=[pl.BlockSpec((tm, d), row), pl.BlockSpec((tm * ROW_SLABS, LANES), row),
                   pl.BlockSpec((tm, LANES), row), pl.BlockSpec((tm, LANES), row)],
        out_shape=[jax.ShapeDtypeStruct((m, d), F32), jax.ShapeDtypeStruct((m * ROW_SLABS, LANES), F32),
                   jax.ShapeDtypeStruct((m, LANES), I32), jax.ShapeDtypeStruct((m, LANES), F32)],
        compiler_params=_arb(1),
        name=name,
    )(x, w, resid, g[None, :], b[None, :], wr)


def _dispatch_kernel(eid_ref, h_ref, dest_ref, be_ref, nu_ref, xs_hbm,
                     cnt, base, upper, zbuf, dst_v, dst_s, pe_v, pe_s, sem_z, sem_r, sem_s, *, blk):
    ph = pl.program_id(0)
    i = pl.program_id(1)

    tm = eid_ref.shape[0]
    n_e = LANES
    eid_t = eid_ref[...].astype(F32).T
    sub = lax.broadcasted_iota(I32, (n_e, tm), 0).astype(F32)
    oh = [(sub == eid_t[k:k + 1, :]).astype(F32) for k in range(MOE_TOP_K)]
    tot = [jnp.sum(o, axis=1, keepdims=True) for o in oh]

    @pl.when((ph == 0) & (i == 0))
    def _():
        cnt[...] = jnp.zeros(cnt.shape, F32)
        r_i = lax.broadcasted_iota(I32, (tm, tm), 0)
        c_i = lax.broadcasted_iota(I32, (tm, tm), 1)
        upper[...] = (r_i < c_i).astype(BF16)

    @pl.when(ph == 0)
    def _():
        cnt[...] += jnp.broadcast_to(tot[0] + tot[1], cnt.shape)

    @pl.when((ph == 1) & (i == 0))
    def _():
        counts = cnt[...]
        padded = jnp.floor((counts + float(blk - 1)) * (1.0 / blk)) * float(blk)
        r_i = lax.broadcasted_iota(I32, (n_e, n_e), 0)
        c_i = lax.broadcasted_iota(I32, (n_e, n_e), 1)
        tril = (r_i >= c_i).astype(BF16)
        p1, p2, p3 = _split3(padded)
        pends = _dot(tril, p1) + _dot(tril, p2) + _dot(tril, p3)
        base[...] = pends - padded
        nbp = be_ref.shape[1]
        blk_start = lax.broadcasted_iota(I32, (n_e, nbp), 1).astype(F32) * float(blk)
        is_e = lax.broadcasted_iota(I32, (n_e, nbp), 0) < MOE_N_EXPERTS
        done = jnp.where(is_e & (jnp.tile(pends, (1, nbp // LANES)) <= blk_start), 1.0, 0.0)
        be = jnp.minimum(jnp.sum(done, axis=0, keepdims=True), float(MOE_N_EXPERTS - 1))
        be_ref[...] = jnp.broadcast_to(be, be_ref.shape).astype(I32)
        last = pends[MOE_N_EXPERTS - 1:MOE_N_EXPERTS, :] * (1.0 / blk)
        nu_ref[...] = jnp.broadcast_to(last, nu_ref.shape).astype(I32)
        zbuf[...] = jnp.zeros(zbuf.shape, F32)
        row8 = lax.broadcasted_iota(I32, (8, LANES), 0)
        pe_v[...] = jnp.where(row8 == 0, pends.T[0:8, :], counts.T[0:8, :]).astype(I32)
        cp = pltpu.make_async_copy(pe_v, pe_s, sem_s)
        cp.start()
        cp.wait()

        brows = blk * ROW_SLABS

        def zero_copy(e):
            start = pl.multiple_of((pe_s[0, e] - blk) * ROW_SLABS, brows)
            return pltpu.make_async_copy(zbuf, xs_hbm.at[pl.ds(start, brows)], sem_z)

        def tail_copy(b):
            return pltpu.make_async_copy(zbuf, xs_hbm.at[pl.ds(pl.multiple_of(b * brows, brows), brows)], sem_z)

        n_used = lax.shift_right_logical(pe_s[0, MOE_N_EXPERTS - 1], blk.bit_length() - 1)
        n_blocks = xs_hbm.shape[0] // brows
        for e in range(MOE_N_EXPERTS):
            @pl.when(pe_s[1, e] > 0)
            def _():
                zero_copy(e).start()
        lax.fori_loop(n_used, n_blocks, lambda b, c: (tail_copy(b).start(), c)[1], 0)
        for e in range(MOE_N_EXPERTS):
            @pl.when(pe_s[1, e] > 0)
            def _():
                zero_copy(e).wait()
        lax.fori_loop(n_used, n_blocks, lambda b, c: (tail_copy(b).wait(), c)[1], 0)

    @pl.when(ph == 1)
    def _():
        b0 = base[:, 0:1]
        c0 = _dot(oh[0].astype(BF16), upper[...])
        c1 = _dot(oh[1].astype(BF16), upper[...])
        d0 = jnp.sum(oh[0] * (b0 + c0), axis=0, keepdims=True)
        d1 = jnp.sum(oh[1] * (b0 + tot[0] + c1), axis=0, keepdims=True)
        base[...] += jnp.broadcast_to(tot[0] + tot[1], base.shape)
        row8 = lax.broadcasted_iota(I32, (8, tm), 0)
        dst = jnp.where(row8 == 0, d0, jnp.where(row8 == 1, d1, 0.0)).astype(I32)
        dest_ref[...] = dst
        dst_v[...] = dst
        cp = pltpu.make_async_copy(dst_v, dst_s, sem_s)
        cp.start()
        cp.wait()

        for r in range(tm):
            for k in range(MOE_TOP_K):
                slot = pl.multiple_of(dst_s[k, r] * ROW_SLABS, ROW_SLABS)
                pltpu.make_async_copy(h_ref.at[pl.ds(r * ROW_SLABS, ROW_SLABS)],
                                      xs_hbm.at[pl.ds(slot, ROW_SLABS)], sem_r).start(priority=k)
        for k in range(MOE_TOP_K):
            pltpu.make_async_copy(h_ref, xs_hbm.at[pl.ds(0, tm * ROW_SLABS)], sem_r).wait()


def _moe_dispatch(ht, eid, blk, name):
    t = ht.shape[0] // ROW_SLABS
    assert blk & (blk - 1) == 0, "block size must be a power of two"
    tm = min(DSP_TM, t)
    nt = t // tm
    n_blocks = -(-(t * MOE_TOP_K) // blk) + MOE_N_EXPERTS
    nbp = -(-n_blocks // LANES) * LANES
    tile = lambda ph, i: (i * ph, 0)
    dest, be, nu, xs = pl.pallas_call(
        functools.partial(_dispatch_kernel, blk=blk),
        grid=(2, nt),
        in_specs=[pl.BlockSpec((tm, LANES), lambda ph, i: (i, 0)),
                  pl.BlockSpec((tm * ROW_SLABS, LANES), tile)],
        out_specs=[pl.BlockSpec((8, tm), lambda ph, i: (0, i * ph)),
                   pl.BlockSpec((8, nbp), lambda ph, i: (0, 0)),
                   pl.BlockSpec((8, LANES), lambda ph, i: (0, 0)),
                   pl.BlockSpec(memory_space=pl.ANY)],
        out_shape=[jax.ShapeDtypeStruct((8, t), I32), jax.ShapeDtypeStruct((8, nbp), I32),
                   jax.ShapeDtypeStruct((8, LANES), I32),
                   jax.ShapeDtypeStruct((n_blocks * blk * ROW_SLABS, LANES), F32)],
        scratch_shapes=[pltpu.VMEM((LANES, LANES), F32), pltpu.VMEM((LANES, LANES), F32),
                        pltpu.VMEM((tm, tm), BF16), pltpu.VMEM((blk * ROW_SLABS, LANES), F32),
                        pltpu.VMEM((8, tm), I32), pltpu.SMEM((8, tm), I32),
                        pltpu.VMEM((8, LANES), I32), pltpu.SMEM((8, LANES), I32),
                        pltpu.SemaphoreType.DMA(()), pltpu.SemaphoreType.DMA(()), pltpu.SemaphoreType.DMA(())],
        compiler_params=_arb(2),
        name=name,
    )(eid, ht)
    return dest, be[0, :n_blocks], nu[0, :1], xs, n_blocks


def _ffn_kernel(be_ref, nu_ref, x_ref, wg_ref, wu_ref, wd_ref, o_ref):
    i = pl.program_id(0)

    @pl.when(i < nu_ref[0])
    def _():
        blk = x_ref.shape[0] // ROW_SLABS
        xb = _load_token_tiles(x_ref, blk).astype(BF16)
        hid = _silu(_dot(xb, wg_ref[0])) * _dot(xb, wu_ref[0])
        _store_token_tiles(o_ref, _dot(hid.astype(BF16), wd_ref[0]))

    @pl.when(i >= nu_ref[0])
    def _():
        o_ref[...] = jnp.zeros(o_ref.shape, o_ref.dtype)


def _moe_ffn(xs, block_expert, n_used, n_blocks, weights, blk, name):
    w_gate, w_up, w_down = weights
    d, f = w_gate.shape[1], w_gate.shape[2]
    brows = blk * ROW_SLABS
    used = lambda i, be, nu: (jnp.minimum(i, nu[0] - 1), 0)
    every = lambda i, be, nu: (i, 0)
    expert = lambda i, be, nu: (be[i], 0, 0)
    grid_spec = pltpu.PrefetchScalarGridSpec(
        num_scalar_prefetch=2,
        grid=(n_blocks,),
        in_specs=[pl.BlockSpec((brows, LANES), used),
                  pl.BlockSpec((1, d, f), expert), pl.BlockSpec((1, d, f), expert),
                  pl.BlockSpec((1, f, d), expert)],
        out_specs=pl.BlockSpec((brows, LANES), every))
    return pl.pallas_call(
        _ffn_kernel,
        grid_spec=grid_spec,
        out_shape=jax.ShapeDtypeStruct((n_blocks * brows, LANES), F32),
        compiler_params=_arb(1),
        name=name,
    )(block_expert, n_used, xs, w_gate, w_up, w_down)


def _combine_ln_kernel(dst_ref, dstn_ref, h_ref, wt_ref, g_ref, b_ref, yb_hbm, o_ref, ybuf, sem):
    i = pl.program_id(0)
    n = pl.num_programs(0)
    tm = h_ref.shape[0]

    def start_gather(dref, slot):
        for r in range(tm):
            for k in range(MOE_TOP_K):
                src = pl.multiple_of(dref[k, r] * ROW_SLABS, ROW_SLABS)
                pltpu.make_async_copy(yb_hbm.at[pl.ds(src, ROW_SLABS)],
                                      ybuf.at[slot, k, pl.ds(r * ROW_SLABS, ROW_SLABS)],
                                      sem.at[slot]).start(priority=k)

    @pl.when(i == 0)
    def _():
        start_gather(dst_ref, 0)

    @pl.when(i + 1 < n)
    def _():
        start_gather(dstn_ref, (i + 1) % 2)

    slot = i % 2
    for k in range(MOE_TOP_K):
        pltpu.make_async_copy(yb_hbm.at[pl.ds(0, tm * ROW_SLABS)], ybuf.at[slot, k], sem.at[slot]).wait()
    wt = wt_ref[...]
    ffn = (wt[:, 0:1] * _load_token_tiles(ybuf.at[slot, 0], tm)
           + wt[:, 1:2] * _load_token_tiles(ybuf.at[slot, 1], tm))
    o_ref[...] = _layer_norm(DEEPNORM_ALPHA * h_ref[...] + ffn, g_ref[...], b_ref[...])


def _combine_ln(h, wt, dest, yb, g, b, name):
    t, d = h.shape
    tm = min(CMB_TM, t)
    nblk = t // tm
    row = lambda i: (i, 0)
    full = lambda i: (0, 0)
    return pl.pallas_call(
        _combine_ln_kernel,
        grid=(nblk,),
        in_specs=[pl.BlockSpec((8, tm), lambda i: (0, i), memory_space=pltpu.SMEM),
                  pl.BlockSpec((8, tm), lambda i: (0, jnp.minimum(i + 1, nblk - 1)),
                               memory_space=pltpu.SMEM),
                  pl.BlockSpec((tm, d), row), pl.BlockSpec((tm, LANES), row),
                  pl.BlockSpec((1, d), full), pl.BlockSpec((1, d), full),
                  pl.BlockSpec(memory_space=pl.ANY)],
        out_specs=pl.BlockSpec((tm, d), row),
        out_shape=jax.ShapeDtypeStruct((t, d), F32),
        scratch_shapes=[pltpu.VMEM((2, MOE_TOP_K, tm * ROW_SLABS, LANES), F32), pltpu.SemaphoreType.DMA((2,))],
        compiler_params=_arb(1),
        name=name,
    )(dest, dest, h, wt, g[None, :], b[None, :], yb)


def _hier_moe_ln(h, ht, eid, wt, w_bf16, g, b, layer):
    dest, block_expert, n_used, xs, n_blocks = _moe_dispatch(ht, eid, FFN_BLK, f"moe_dispatch{layer}")
    yb = _moe_ffn(xs, block_expert, n_used, n_blocks, w_bf16, FFN_BLK, f"moe_ffn{layer}")
    return _combine_ln(h, wt, dest, yb, g, b, f"moe_combine_ln{layer}")


def _qkv_rope_kernel(x_ref, w_ref, pos_ref, inv_ref, o_ref):
    xb = x_ref[...].astype(BF16)
    tm = xb.shape[0]
    n = D_MODEL
    ang = pos_ref[...].astype(F32) * inv_ref[...]
    lane = lax.broadcasted_iota(I32, (tm, LANES), 1)
    dd = lane & (ATTN_HEAD_DIM - 1)
    half = ROT_DIM // 2
    cosv = jnp.cos(ang)
    sinv = jnp.sin(ang)
    c_t = jnp.where(dd < ROT_DIM, cosv, 1.0)
    s_up = jnp.where(dd < half, -sinv, 0.0)
    s_dn = jnp.where((dd >= half) & (dd < ROT_DIM), sinv, 0.0)
    for j, sc in ((0, ATTN_HEAD_DIM ** -0.5 * math.log2(math.e)), (1, 1.0)):
        acc = _dot(xb, w_ref[:, j * n:(j + 1) * n])
        c_j, up_j, dn_j = c_t * sc, s_up * sc, s_dn * sc
        for blk in range(n // LANES):
            tt = acc[:, blk * LANES:(blk + 1) * LANES]
            out = tt * c_j + pltpu.roll(tt, LANES - half, 1) * up_j + pltpu.roll(tt, half, 1) * dn_j
            o_ref[:, j * n + blk * LANES:j * n + (blk + 1) * LANES] = out.astype(o_ref.dtype)
    o_ref[:, 2 * n:3 * n] = _dot(xb, w_ref[:, 2 * n:3 * n]).astype(o_ref.dtype)


def _rope_inv_table():
    inv = ROPE_THETA ** (-jnp.arange(0, ROT_DIM, 2, dtype=F32) / ROT_DIM)
    head = jnp.concatenate([inv, inv, jnp.zeros((ATTN_HEAD_DIM - ROT_DIM,), F32)])
    return jnp.tile(head, LANES // ATTN_HEAD_DIM)[None, :]


def _qkv_rope(h, w_qkv, positions):
    m, k = h.shape
    n = w_qkv.shape[1]
    tm = min(MM_TM, m)
    pos = positions.reshape(m, 1)
    return pl.pallas_call(
        _qkv_rope_kernel,
        grid=(m // tm,),
        in_specs=[pl.BlockSpec((tm, k), lambda i: (i, 0)),
                  pl.BlockSpec((k, n), lambda i: (0, 0)),
                  pl.BlockSpec((tm, 1), lambda i: (i, 0)),
                  pl.BlockSpec((1, LANES), lambda i: (0, 0))],
        out_specs=pl.BlockSpec((tm, n), lambda i: (i, 0)),
        out_shape=jax.ShapeDtypeStruct((m, n), BF16),
        compiler_params=_arb(1),
        name="mm_qkv_rope",
    )(h, w_qkv, pos, _rope_inv_table())


def _attn_kernel(q_ref, k_ref, v_ref, lq1_ref, lk1_ref, lq2_ref, lk2_ref, sw_ref, wg_ref, wu_ref, wd_ref,
                 o_ref, wgb_ref, wub_ref, wdb_ref,
                 vx_ref, m0_ref, m1_ref, acc0_ref, acc1_ref, *, lambda_init, cast):
    m_refs = (m0_ref, m1_ref)
    acc_refs = (acc0_ref, acc1_ref)
    i = pl.program_id(2)
    tq = q_ref.shape[0]
    cast.emit(pl.program_id(0) * pl.num_programs(1) + pl.program_id(1),
              (wg_ref, wu_ref, wd_ref), (wgb_ref, wub_ref, wdb_ref), extra=i == 0)

    @pl.when(i == 0)
    def _():
        vx_ref[:, 0:LANES] = v_ref[...]
        vx_ref[:, LANES:] = jnp.ones((vx_ref.shape[0], LANES), vx_ref.dtype)

    q = q_ref[...]
    lane = lax.broadcasted_iota(I32, (tq, LANES), 1)
    zero = jnp.zeros((), q.dtype)
    qs = (jnp.where(lane < ATTN_HEAD_DIM, q, zero), jnp.where(lane >= ATTN_HEAD_DIM, q, zero))
    for c in range(2):
        m_refs[c][...] = jnp.full(m_refs[c].shape, -jnp.inf, F32)
        acc_refs[c][...] = jnp.zeros(acc_refs[c].shape, F32)

    def step(off, width, diag_col=None):
        off = pl.multiple_of(off, tq)
        kb = k_ref[pl.ds(off, width), :]
        vb = vx_ref[pl.ds(off, width), :]
        ss = [lax.dot_general(qs[c], kb, (((1,), (1,)), ((), ())), preferred_element_type=F32)
              for c in range(2)]
        if diag_col is not None:
            row_i = lax.broadcasted_iota(I32, (tq, width), 0)
            col_i = lax.broadcasted_iota(I32, (tq, width), 1)
            ss = [jnp.where(row_i + diag_col >= col_i, s, -jnp.inf) for s in ss]
        m_prev = [m_refs[c][...] for c in range(2)]
        mn = [jnp.maximum(m_prev[c], jnp.max(ss[c], axis=-1, keepdims=True)) for c in range(2)]
        ps = [jnp.exp2(ss[c] - jnp.tile(mn[c], (1, width // LANES))).astype(BF16) for c in range(2)]
        for c in range(2):
            alpha = jnp.exp2(m_prev[c] - mn[c])
            acc_refs[c][...] = jnp.tile(alpha, (1, 2)) * acc_refs[c][...] + _dot(ps[c], vb)
            m_refs[c][...] = mn[c]

    def loop(n, fn):
        lax.fori_loop(0, n, lambda j, carry: (fn(j), carry)[1], 0)

    loop(i // 2, lambda j: step(j * (2 * tq), 2 * tq))

    @pl.when(i % 2 == 0)
    def _():
        step(i * tq, tq, diag_col=0)

    @pl.when(i % 2 == 1)
    def _():
        step((i - 1) * tq, 2 * tq, diag_col=tq)

    lam = (jnp.exp(jnp.sum(lq1_ref[...] * lk1_ref[...], axis=-1, keepdims=True))
           - jnp.exp(jnp.sum(lq2_ref[...] * lk2_ref[...], axis=-1, keepdims=True)) + lambda_init)
    a1 = acc0_ref[...]
    a2 = acc1_ref[...]
    o = a1[:, :LANES] / a1[:, LANES:] - lam * (a2[:, :LANES] / a2[:, LANES:])
    o = o * lax.rsqrt(jnp.mean(o * o, axis=-1, keepdims=True) + NORM_EPS)
    o_ref[...] = (o * sw_ref[...] * (1.0 - lambda_init)).astype(o_ref.dtype)


def _diff_attention(qkv, lq1, lk1, lq2, lk2, subln_w, lambda_init, batch, seq, moe_weights, moe_layer):
    t = batch * seq
    tq = min(ATT_TQ, seq)
    nq = seq // tq
    h_n = ATTN_N_HEADS
    vec = lambda b, h, i: (0, 0)
    cast = _CastPlan(*moe_weights, moe_layer, batch * h_n, lambda b, h, i: b * h_n + h)
    o, *w_bf16 = pl.pallas_call(
        functools.partial(_attn_kernel, lambda_init=lambda_init, cast=cast),
        grid=(batch, h_n, nq),
        in_specs=[pl.BlockSpec((tq, LANES), lambda b, h, i: (b * nq + i, h)),
                  pl.BlockSpec((seq, LANES), lambda b, h, i: (b, h_n + h)),
                  pl.BlockSpec((seq, LANES), lambda b, h, i: (b, 2 * h_n + h)),
                  pl.BlockSpec((1, ATTN_HEAD_DIM), vec), pl.BlockSpec((1, ATTN_HEAD_DIM), vec),
                  pl.BlockSpec((1, ATTN_HEAD_DIM), vec), pl.BlockSpec((1, ATTN_HEAD_DIM), vec),
                  pl.BlockSpec((1, ATTN_V_DIM), vec)] + cast.in_specs,
        out_specs=[pl.BlockSpec((tq, LANES), lambda b, h, i: (b * nq + i, h))] + cast.out_specs,
        out_shape=[jax.ShapeDtypeStruct((t, h_n * ATTN_V_DIM), BF16)] + cast.out_shape,
        scratch_shapes=[pltpu.VMEM((seq, 2 * LANES), BF16),
                        pltpu.VMEM((tq, LANES), F32), pltpu.VMEM((tq, LANES), F32),
                        pltpu.VMEM((tq, 2 * LANES), F32), pltpu.VMEM((tq, 2 * LANES), F32)],
        compiler_params=_arb(3),
        name="diff_attn",
    )(qkv, qkv, qkv, lq1[None, :], lk1[None, :], lq2[None, :], lk2[None, :], subln_w[None, :], *cast.weights)
    return o, tuple(w_bf16)


def kernel(x, positions, ln_mix_g, ln_mix_b, ln_ffn_g, ln_ffn_b, ssm_w_in, ssm_conv_w, ssm_conv_b, ssm_dt_bias, ssm_a_log, ssm_d, ssm_norm_w, ssm_w_out, attn_w_qkv, attn_lam_q1, attn_lam_k1, attn_lam_q2, attn_lam_k2, attn_subln_w, attn_w_o, moe_w_group, moe_w_expert, moe_w_gate, moe_w_up, moe_w_down):
    batch, seq, d = x.shape
    t = batch * seq
    h = x.reshape(t, d)

    w_in = ssm_w_in[0].astype(BF16)
    w_zx = w_in[:, :SSM_ZX_DIM]

    moe_weights = (moe_w_gate, moe_w_up, moe_w_down)
    zx = _matmul(h, w_zx, F32)
    yg, moe_w0 = _ssd(zx, h, w_in[:, SSM_ZX_DIM:], ssm_conv_w[0], ssm_conv_b[0], ssm_dt_bias[0],
                      ssm_a_log[0], ssm_d[0], ssm_norm_w[0], batch, seq, moe_weights, 0)
    h, ht, eid, wt = _mm_ln_router(yg, ssm_w_out[0].astype(BF16), h, ln_mix_g[0], ln_mix_b[0],
                               _router_weights(moe_w_group[0], moe_w_expert[0]), "mm_ssm_out_ln_router")
    h = _hier_moe_ln(h, ht, eid, wt, moe_w0, ln_ffn_g[0], ln_ffn_b[0], 0)

    lambda_init = 0.8 - 0.6 * math.exp(-0.3 * 1)
    qkv = _qkv_rope(h, attn_w_qkv[0].astype(BF16), positions)
    o, moe_w1 = _diff_attention(qkv, attn_lam_q1[0], attn_lam_k1[0], attn_lam_q2[0], attn_lam_k2[0],
                                attn_subln_w[0], lambda_init, batch, seq, moe_weights, 1)
    h, ht, eid, wt = _mm_ln_router(o, attn_w_o[0].astype(BF16), h, ln_mix_g[1], ln_mix_b[1],
                               _router_weights(moe_w_group[1], moe_w_expert[1]), "mm_attn_out_ln_router")
    h = _hier_moe_ln(h, ht, eid, wt, moe_w1, ln_ffn_g[1], ln_ffn_b[1], 1)
    return h.reshape(batch, seq, d)
```

```python
import functools
import math

import jax
import jax.numpy as jnp
from jax import lax
from jax.experimental import pallas as pl
from jax.experimental.pallas import tpu as pltpu

F32 = jnp.float32
BF16 = jnp.bfloat16
I32 = jnp.int32

D_MODEL = 1024
DEPTH = 2
SSM_D_INNER = 2048
SSM_HEADDIM = 64
SSM_N_HEADS = 32
SSM_N_GROUPS = 8
SSM_HEADS_PER_GROUP = 4
SSM_D_STATE = 128
SSM_CONV_K = 4
SSM_CHUNK = 128
SSM_GROUP_WIDTH = SSM_HEADS_PER_GROUP * SSM_HEADDIM
SSM_ZX_DIM = 2 * SSM_D_INNER + 2 * SSM_N_GROUPS * SSM_D_STATE
ATTN_HEAD_DIM = 64
ATTN_N_HEADS = 8
ATTN_V_DIM = 128
ROT_DIM = 16
ROPE_THETA = 500000.0
MOE_GROUPS = 4
MOE_EXPERTS_PER_GROUP = 8
MOE_N_EXPERTS = 32
MOE_TOP_K = 2
MOE_D_FF = 512
DEEPNORM_ALPHA = (2 * DEPTH) ** 0.25
NORM_EPS = 1e-5

LANES = 128
ROW_SLABS = D_MODEL // LANES
CONV_TAIL = 8
CONV_ROWS = 128

MM_TM = 512
MM_TM_WIDE = 256
MM_TN = 1024
FFN_BLK = 512
DSP_TM = 512
LN_SUBTILES = 4
CMB_TM = 256
ATT_TQ = 512


def _arb(n):
    return pltpu.CompilerParams(dimension_semantics=("arbitrary",) * n,
                                vmem_limit_bytes=56 * 1024 * 1024)


def _sigmoid(x):
    return 1.0 / (1.0 + jnp.exp(-x))


def _silu(x):
    return x * _sigmoid(x)


def _softplus(x):
    return jnp.maximum(x, 0.0) + jnp.log(1.0 + jnp.exp(-jnp.abs(x)))


def _layer_norm(y, g, b):
    mu = jnp.mean(y, axis=-1, keepdims=True)
    d = y - mu
    var = jnp.mean(d * d, axis=-1, keepdims=True)
    return d * lax.rsqrt(var + NORM_EPS) * g + b


def _split3(a):
    a1 = a.astype(BF16)
    r1 = a - a1.astype(F32)
    a2 = r1.astype(BF16)
    a3 = (r1 - a2.astype(F32)).astype(BF16)
    return a1, a2, a3


def _dot(a, b):
    return jnp.dot(a, b, preferred_element_type=F32)


def _dot_sel(a, sel):
    a1, a2, a3 = _split3(a)
    return _dot(a1, sel) + _dot(a2, sel) + _dot(a3, sel)


def _dot_f32(a, b):
    a1, a2, a3 = _split3(a)
    b1, b2, b3 = _split3(b)
    return (_dot(a1, b1) + _dot(a1, b2) + _dot(a2, b1)
            + _dot(a2, b2) + _dot(a1, b3) + _dot(a3, b1))


class _CastPlan:
    def __init__(self, w_gate, w_up, w_down, layer, steps, step_of):
        self.weights = (w_gate, w_up, w_down)
        self.n_e = w_gate.shape[1]
        self.eps = -(-self.n_e // steps)
        assert self.n_e % self.eps == 0
        self.n_cast = self.n_e // self.eps
        self.stride = steps // self.n_cast
        block = lambda *ids: jnp.minimum(step_of(*ids) // self.stride, self.n_cast - 1)
        self.in_specs = [pl.BlockSpec((1, self.eps) + w.shape[2:], lambda *ids: (layer, block(*ids), 0, 0))
                         for w in self.weights]
        self.out_specs = [pl.BlockSpec((self.eps,) + w.shape[2:], lambda *ids: (block(*ids), 0, 0))
                          for w in self.weights]
        self.out_shape = [jax.ShapeDtypeStruct(w.shape[1:], BF16) for w in self.weights]

    def emit(self, step, w_refs, o_refs, extra=True):
        @pl.when(extra & (step % self.stride == 0) & (step // self.stride < self.n_cast))
        def _():
            for w_ref, o_ref in zip(w_refs, o_refs):
                o_ref[...] = w_ref[0].astype(BF16)


def _mm_kernel(x_ref, w_ref, o_ref):
    xb = x_ref[...].astype(BF16)
    for j in range(o_ref.shape[1] // MM_TN):
        cols = slice(j * MM_TN, (j + 1) * MM_TN)
        o_ref[:, cols] = _dot(xb, w_ref[:, cols]).astype(o_ref.dtype)


def _matmul(x, w, n, out_dtype):
    m, k = x.shape
    tm = min(MM_TM_WIDE, m)
    return pl.pallas_call(
        _mm_kernel,
        grid=(m // tm,),
        in_specs=[pl.BlockSpec((tm, k), lambda i: (i, 0)), pl.BlockSpec((k, n), lambda i: (0, 0))],
        out_specs=pl.BlockSpec((tm, n), lambda i: (i, 0)),
        out_shape=jax.ShapeDtypeStruct((m, n), out_dtype),
        compiler_params=_arb(1),
        name="mm_inproj",
    )(x, w)


def _ssd_kernel(z_ref, xs_ref, bc_ref, x_ref, wdt_ref, cw_ref, cb_ref, dtb_ref, alog_ref, dskip_ref,
                nw_ref, e64_ref, e128_ref, wg_ref, wu_ref, wd_ref, o_ref, wgb_ref, wub_ref, wdb_ref,
                stage_x, stage_bc, state, acp_s, dtp_s, act_s, *, cast):
    G, W, N = SSM_N_GROUPS, SSM_GROUP_WIDTH, SSM_D_STATE
    L = z_ref.shape[0]
    c = pl.program_id(1)
    cast.emit(pl.program_id(0) * pl.num_programs(1) + c, (wg_ref, wu_ref, wd_ref), (wgb_ref, wub_ref, wdb_ref))

    @pl.when(c == 0)
    def _():
        state[...] = jnp.zeros(state.shape, F32)
        for stage in (stage_x, stage_bc):
            stage[0:CONV_TAIL, :] = jnp.zeros((CONV_TAIL, stage.shape[1]), F32)

    @pl.when(c != 0)
    def _():
        for stage in (stage_x, stage_bc):
            stage[0:CONV_TAIL, :] = stage[L:L + CONV_TAIL, :]

    stage_x[CONV_TAIL:CONV_TAIL + L, :] = xs_ref[...]
    stage_bc[CONV_TAIL:CONV_TAIL + L, :] = bc_ref[...]

    def conv(stage, c0, width, w0):
        taps = [cw_ref[k:k + 1, w0:w0 + width] for k in range(SSM_CONV_K)]
        bias = cb_ref[:, w0:w0 + width]
        outs = []
        for rb in range(L // CONV_ROWS):
            r0 = rb * CONV_ROWS
            u = stage[r0:r0 + CONV_ROWS + CONV_TAIL, c0:c0 + width]
            out = bias + taps[SSM_CONV_K - 1] * u[CONV_TAIL:, :]
            for k in range(SSM_CONV_K - 1):
                back = SSM_CONV_K - 1 - k
                out = out + taps[k] * pltpu.roll(u, back, 0)[CONV_TAIL:, :]
            outs.append(_silu(out))
        return jnp.concatenate(outs, axis=0)

    xc = [conv(stage_x, g * W, W, g * W) for g in range(G)]
    bcv = [conv(stage_bc, g * N, N, SSM_D_INNER + g * N).astype(BF16) for g in range(G)]
    ccv = [conv(stage_bc, G * N + g * N, N, SSM_D_INNER + G * N + g * N).astype(BF16) for g in range(G)]

    def x_of(g):
        return xc[g]

    def b_of(g):
        return bcv[g]

    def c_of(g):
        return ccv[g]

    dt = _softplus(_dot(x_ref[...].astype(BF16), wdt_ref[...].astype(BF16)) + dtb_ref[...])
    a = -jnp.exp(alog_ref[...])
    row_i = lax.broadcasted_iota(I32, (L, L), 0)
    col_i = lax.broadcasted_iota(I32, (L, L), 1)
    causal = row_i >= col_i
    a1, a2, a3 = _split3(dt * a)
    tri = causal.astype(BF16)
    ac = _dot(tri, a1) + _dot(tri, a2) + _dot(tri, a3)
    act_s[...] = ac.T
    for n, part in enumerate(_split3(ac)):
        acp_s[n] = part
    for n, part in enumerate(_split3(dt)[:2]):
        dtp_s[n] = part
    lane = lax.broadcasted_iota(I32, (L, LANES), 1)
    first_half = lane < SSM_HEADDIM

    def select(parts_ref, n_parts, sel):
        out = _dot(parts_ref[0], sel)
        for n in range(1, n_parts):
            out = out + _dot(parts_ref[n], sel)
        return out

    gs = range(G)
    col4 = [select(acp_s, 3, e128_ref[g]) for g in gs]
    dt_e = [select(dtp_s, 2, e64_ref[g]) for g in gs]
    cb = [lax.dot_general(c_of(g), b_of(g), (((1,), (1,)), ((), ())), preferred_element_type=F32)
          for g in gs]
    y_cs = [_dot(c_of(g), state[g].astype(BF16)) for g in gs]
    a_e = [jnp.concatenate(
        [jnp.where(first_half, c4[:, 2 * p * LANES:(2 * p + 1) * LANES],
                   c4[:, (2 * p + 1) * LANES:(2 * p + 2) * LANES]) for p in range(2)], axis=1)
        for c4 in col4]
    xdt = [x_of(g) * dt_e[g] for g in gs]
    xdt_b = [v.astype(BF16) for v in xdt]
    ys = [[None] * SSM_HEADS_PER_GROUP for _ in gs]
    for r in range(SSM_HEADS_PER_GROUP):
        for g in gs:
            row = act_s[SSM_HEADS_PER_GROUP * g + r:SSM_HEADS_PER_GROUP * g + r + 1, :]
            seg = col4[g][:, LANES * r:LANES * (r + 1)] - row
            dec = jnp.where(causal, jnp.exp(seg), 0.0)
            xp = xdt_b[g][:, LANES * (r // 2):LANES * (r // 2 + 1)]
            ys[g][r] = _dot((cb[g] * dec).astype(BF16), xp)
    for g in gs:
        y_diag = jnp.concatenate([jnp.where(first_half, ys[g][2 * p], ys[g][2 * p + 1])
                                  for p in range(2)], axis=1)
        al_e = a_e[g][L - 1:L, :]
        new = lax.dot_general(b_of(g), (xdt[g] * jnp.exp(al_e - a_e[g])).astype(BF16),
                              (((0,), (0,)), ((), ())), preferred_element_type=F32)
        y = y_diag + y_cs[g] * jnp.exp(a_e[g]) + x_of(g) * dskip_ref[g]
        state[g] = state[g] * jnp.exp(al_e) + new
        yg = y * _silu(z_ref[:, g * W:(g + 1) * W])
        ms = jnp.mean(yg * yg, axis=-1, keepdims=True)
        o_ref[:, g * W:(g + 1) * W] = (yg * lax.rsqrt(ms + NORM_EPS)
                                       * nw_ref[:, g * W:(g + 1) * W]).astype(o_ref.dtype)


def _head_selectors():
    G, R = SSM_N_GROUPS, SSM_HEADS_PER_GROUP
    h = jnp.arange(LANES)[None, :, None]
    g = jnp.arange(G)[:, None, None]
    j64 = jnp.arange(SSM_GROUP_WIDTH)[None, None, :]
    j128 = jnp.arange(R * LANES)[None, None, :]
    e64 = (h == g * R + j64 // SSM_HEADDIM).astype(BF16)
    e128 = (h == g * R + j128 // LANES).astype(BF16)
    return e64, e128


def _ssd(zx, x2, w_dt, conv_w, conv_b, dt_bias, a_log, d_skip, norm_w, batch, seq, moe_weights, moe_layer):
    L, G, W, N = SSM_CHUNK, SSM_N_GROUPS, SSM_GROUP_WIDTH, SSM_D_STATE
    nc = seq // L
    cast = _CastPlan(*moe_weights, moe_layer, batch * nc, lambda b, c: b * nc + c)
    t = batch * seq
    pad = LANES - SSM_N_HEADS
    wdt = jnp.pad(w_dt, ((0, 0), (0, pad)))
    dtb = jnp.pad(dt_bias, (0, pad))[None, :]
    alog = jnp.pad(a_log, (0, pad))[None, :]
    dsk = jnp.repeat(d_skip, SSM_HEADDIM).reshape(G, 1, W)
    e64, e128 = _head_selectors()
    conv_dim = conv_w.shape[1]
    row = lambda b, c: (b * nc + c, 0)
    full2 = lambda b, c: (0, 0)
    full3 = lambda b, c: (0, 0, 0)
    yg, *w_bf16 = pl.pallas_call(
        functools.partial(_ssd_kernel, cast=cast),
        grid=(batch, nc),
        in_specs=[pl.BlockSpec((L, SSM_D_INNER), row),
                  pl.BlockSpec((L, SSM_D_INNER), lambda b, c: (b * nc + c, 1)),
                  pl.BlockSpec((L, 2 * G * N), lambda b, c: (b * nc + c, 2)),
                  pl.BlockSpec((L, D_MODEL), row),
                  pl.BlockSpec((D_MODEL, LANES), full2),
                  pl.BlockSpec((SSM_CONV_K, conv_dim), full2),
                  pl.BlockSpec((1, conv_dim), full2),
                  pl.BlockSpec((1, LANES), full2),
                  pl.BlockSpec((1, LANES), full2),
                  pl.BlockSpec((G, 1, W), full3),
                  pl.BlockSpec((1, SSM_D_INNER), full2),
                  pl.BlockSpec((G, LANES, W), full3),
                  pl.BlockSpec((G, LANES, SSM_HEADS_PER_GROUP * LANES), full3)] + cast.in_specs,
        out_specs=[pl.BlockSpec((L, SSM_D_INNER), row)] + cast.out_specs,
        out_shape=[jax.ShapeDtypeStruct((t, SSM_D_INNER), BF16)] + cast.out_shape,
        scratch_shapes=[pltpu.VMEM((CONV_TAIL + L, SSM_D_INNER), F32),
                        pltpu.VMEM((CONV_TAIL + L, 2 * G * N), F32),
                        pltpu.VMEM((G, N, W), F32),
                        pltpu.VMEM((3, L, LANES), BF16),
                        pltpu.VMEM((2, L, LANES), BF16),
                        pltpu.VMEM((LANES, L), F32)],
        compiler_params=_arb(2),
        name="ssd_scan",
    )(zx, zx, zx, x2, wdt, conv_w, conv_b[None, :], dtb, alog, dsk, norm_w[None, :], e64, e128, *cast.weights)
    return yg, tuple(w_bf16)


def _route(h, wr):
    tm = h.shape[0]
    logits = _dot(h.astype(BF16), wr)
    lane_i = lax.broadcasted_iota(I32, (tm, LANES), 1)
    lane = lane_i.astype(F32)
    neg = -jnp.inf
    big = float(LANES)

    def first_argmax(v, vmax):
        return jnp.min(jnp.where(v == vmax, lane, big), axis=-1, keepdims=True)

    gl = jnp.where((lane_i >= MOE_N_EXPERTS) & (lane_i < MOE_N_EXPERTS + MOE_GROUPS), logits, neg)
    gm = jnp.max(gl, axis=-1, keepdims=True)
    g_sel = first_argmax(gl, gm) - float(MOE_N_EXPERTS)
    g_gate = 1.0 / jnp.sum(jnp.exp(gl - gm), axis=-1, keepdims=True)
    lo = g_sel * float(MOE_EXPERTS_PER_GROUP)
    el = jnp.where((lane >= lo) & (lane < lo + float(MOE_EXPERTS_PER_GROUP)), logits, neg)
    m1 = jnp.max(el, axis=-1, keepdims=True)
    i1 = first_argmax(el, m1)
    el2 = jnp.where(lane == i1, neg, el)
    m2 = jnp.max(el2, axis=-1, keepdims=True)
    i2 = first_argmax(el2, m2)
    p2 = jnp.exp(m2 - m1)
    t1 = 1.0 / (1.0 + p2)
    t2 = p2 / (1.0 + p2)
    eid = jnp.where(lane_i == 0, i1, jnp.where(lane_i == 1, i2, 0.0)).astype(I32)
    wt = jnp.where(lane_i == 0, g_gate * t1, jnp.where(lane_i == 1, g_gate * t2, 0.0))
    count = jnp.sum(jnp.where(lane == i1, 1.0, 0.0) + jnp.where(lane == i2, 1.0, 0.0), axis=0, keepdims=True)
    return eid, wt, count


def _router_weights(w_group, w_expert):
    pad = LANES - MOE_N_EXPERTS - MOE_GROUPS
    return jnp.pad(jnp.concatenate([w_expert, w_group], axis=1), ((0, 0), (0, pad))).astype(BF16)


def _store_token_tiles(ref, v, first_token=0):
    rows = v.shape[0]
    for s in range(ROW_SLABS):
        ref[pl.ds(first_token * ROW_SLABS + s, rows, stride=ROW_SLABS), :] = v[:, s * LANES:(s + 1) * LANES]


def _load_token_tiles(ref, rows):
    return jnp.concatenate([ref[pl.ds(s, rows, stride=ROW_SLABS), :] for s in range(ROW_SLABS)], axis=1)


def _mm_ln_router_kernel(x_ref, w_ref, r_ref, g_ref, b_ref, wr_ref, h_ref, ht_ref, eid_ref, wt_ref, cnt_ref):
    @pl.when(pl.program_id(0) == 0)
    def _():
        cnt_ref[...] = jnp.zeros(cnt_ref.shape, F32)

    sub = x_ref.shape[0] // LN_SUBTILES
    rows = [slice(u * sub, (u + 1) * sub) for u in range(LN_SUBTILES)]
    ys = [DEEPNORM_ALPHA * r_ref[rs, :] + _dot(x_ref[rs, :].astype(BF16), w_ref[...]) for rs in rows]
    hs = [_layer_norm(y, g_ref[...], b_ref[...]) for y in ys]
    routes = [_route(h, wr_ref[...]) for h in hs]
    for u, rs in enumerate(rows):
        h_ref[rs, :] = hs[u]
        _store_token_tiles(ht_ref, hs[u], first_token=u * sub)
        eid_ref[rs, :] = routes[u][0]
        wt_ref[rs, :] = routes[u][1]
    cnt_ref[...] += jnp.broadcast_to(sum(r[2] for r in routes), cnt_ref.shape)


def _mm_ln_router(x, w, resid, g, b, wr, name):
    m, k = x.shape
    d = w.shape[1]
    tm = min(MM_TM, m)
    row = lambda i: (i, 0)
    full = lambda i: (0, 0)
    return pl.pallas_call(
        _mm_ln_router_kernel,
        grid=(m // tm,),
        in_specs=[pl.BlockSpec((tm, k), row), pl.BlockSpec((k, d), full), pl.BlockSpec((tm, d), row),
                  pl.BlockSpec((1, d), full), pl.BlockSpec((1, d), full), pl.BlockSpec((d, LANES), full)],
        out_specs=[pl.BlockSpec((tm, d), row), pl.BlockSpec((tm * ROW_SLABS, LANES), row),
                   pl.BlockSpec((tm, LANES), row), pl.BlockSpec((tm, LANES), row),
                   pl.BlockSpec((8, LANES), full)],
        out_shape=[jax.ShapeDtypeStruct((m, d), F32), jax.ShapeDtypeStruct((m * ROW_SLABS, LANES), F32),
                   jax.ShapeDtypeStruct((m, LANES), I32), jax.ShapeDtypeStruct((m, LANES), F32),
                   jax.ShapeDtypeStruct((8, LANES), F32)],
        compiler_params=_arb(1),
        name=name,
    )(x, w, resid, g[None, :], b[None, :], wr)


def _dispatch_kernel(eid_ref, cnt_ref, h_ref, dest_ref, be_ref, nu_ref, xs_hbm,
                     base, upper, zbuf, dst_v, dst_s, pe_v, pe_s, sem_z, sem_r, sem_s, *, blk):
    i = pl.program_id(0)
    tm = eid_ref.shape[0]
    n_e = LANES
    eid_t = eid_ref[...].astype(F32).T
    sub = lax.broadcasted_iota(I32, (n_e, tm), 0).astype(F32)
    oh = [(sub == eid_t[k:k + 1, :]).astype(F32) for k in range(MOE_TOP_K)]
    tot = [jnp.sum(o, axis=1, keepdims=True) for o in oh]

    @pl.when(i == 0)
    def _():
        r_i = lax.broadcasted_iota(I32, (tm, tm), 0)
        c_i = lax.broadcasted_iota(I32, (tm, tm), 1)
        upper[...] = (r_i < c_i).astype(BF16)
        counts = jnp.broadcast_to(cnt_ref[0:1, :], (n_e, n_e)).T
        padded = jnp.floor((counts + float(blk - 1)) * (1.0 / blk)) * float(blk)
        r_i = lax.broadcasted_iota(I32, (n_e, n_e), 0)
        c_i = lax.broadcasted_iota(I32, (n_e, n_e), 1)
        tril = (r_i >= c_i).astype(BF16)
        p1, p2, p3 = _split3(padded)
        pends = _dot(tril, p1) + _dot(tril, p2) + _dot(tril, p3)
        base[...] = pends - padded
        nbp = be_ref.shape[1]
        blk_start = lax.broadcasted_iota(I32, (n_e, nbp), 1).astype(F32) * float(blk)
        is_e = lax.broadcasted_iota(I32, (n_e, nbp), 0) < MOE_N_EXPERTS
        done = jnp.where(is_e & (jnp.tile(pends, (1, nbp // LANES)) <= blk_start), 1.0, 0.0)
        be = jnp.minimum(jnp.sum(done, axis=0, keepdims=True), float(MOE_N_EXPERTS - 1))
        be_ref[...] = jnp.broadcast_to(be, be_ref.shape).astype(I32)
        last = pends[MOE_N_EXPERTS - 1:MOE_N_EXPERTS, :] * (1.0 / blk)
        nu_ref[...] = jnp.broadcast_to(last, nu_ref.shape).astype(I32)
        zbuf[...] = jnp.zeros(zbuf.shape, F32)
        row8 = lax.broadcasted_iota(I32, (8, LANES), 0)
        pe_v[...] = jnp.where(row8 == 0, pends.T[0:8, :], counts.T[0:8, :]).astype(I32)
        cp = pltpu.make_async_copy(pe_v, pe_s, sem_s)
        cp.start()
        cp.wait()

        brows = blk * ROW_SLABS

        def zero_copy(e):
            start = pl.multiple_of((pe_s[0, e] - blk) * ROW_SLABS, brows)
            return pltpu.make_async_copy(zbuf, xs_hbm.at[pl.ds(start, brows)], sem_z)

        def tail_copy(b):
            return pltpu.make_async_copy(zbuf, xs_hbm.at[pl.ds(pl.multiple_of(b * brows, brows), brows)], sem_z)

        n_used = lax.shift_right_logical(pe_s[0, MOE_N_EXPERTS - 1], blk.bit_length() - 1)
        n_blocks = xs_hbm.shape[0] // brows
        for e in range(MOE_N_EXPERTS):
            @pl.when(pe_s[1, e] > 0)
            def _():
                zero_copy(e).start()
        lax.fori_loop(n_used, n_blocks, lambda b, c: (tail_copy(b).start(), c)[1], 0)
        for e in range(MOE_N_EXPERTS):
            @pl.when(pe_s[1, e] > 0)
            def _():
                zero_copy(e).wait()
        lax.fori_loop(n_used, n_blocks, lambda b, c: (tail_copy(b).wait(), c)[1], 0)

    b0 = base[:, 0:1]
    c0 = _dot(oh[0].astype(BF16), upper[...])
    c1 = _dot(oh[1].astype(BF16), upper[...])
    d0 = jnp.sum(oh[0] * (b0 + c0), axis=0, keepdims=True)
    d1 = jnp.sum(oh[1] * (b0 + tot[0] + c1), axis=0, keepdims=True)
    base[...] += jnp.broadcast_to(tot[0] + tot[1], base.shape)
    row8 = lax.broadcasted_iota(I32, (8, tm), 0)
    dst = jnp.where(row8 == 0, d0, jnp.where(row8 == 1, d1, 0.0)).astype(I32)
    dest_ref[...] = dst
    dst_v[...] = dst
    cp = pltpu.make_async_copy(dst_v, dst_s, sem_s)
    cp.start()
    cp.wait()

    for r in range(tm):
        for k in range(MOE_TOP_K):
            slot = pl.multiple_of(dst_s[k, r] * ROW_SLABS, ROW_SLABS)
            pltpu.make_async_copy(h_ref.at[pl.ds(r * ROW_SLABS, ROW_SLABS)],
                                  xs_hbm.at[pl.ds(slot, ROW_SLABS)], sem_r).start(priority=k)
    for k in range(MOE_TOP_K):
        pltpu.make_async_copy(h_ref, xs_hbm.at[pl.ds(0, tm * ROW_SLABS)], sem_r).wait()


def _moe_dispatch(ht, eid, counts, blk, name):
    t = ht.shape[0] // ROW_SLABS
    assert blk & (blk - 1) == 0, "block size must be a power of two"
    tm = min(DSP_TM, t)
    nt = t // tm
    n_blocks = -(-(t * MOE_TOP_K) // blk) + MOE_N_EXPERTS
    nbp = -(-n_blocks // LANES) * LANES
    full = lambda i: (0, 0)
    dest, be, nu, xs = pl.pallas_call(
        functools.partial(_dispatch_kernel, blk=blk),
        grid=(nt,),
        in_specs=[pl.BlockSpec((tm, LANES), lambda i: (i, 0)),
                  pl.BlockSpec((8, LANES), full),
                  pl.BlockSpec((tm * ROW_SLABS, LANES), lambda i: (i, 0))],
        out_specs=[pl.BlockSpec((8, tm), lambda i: (0, i)),
                   pl.BlockSpec((8, nbp), full),
                   pl.BlockSpec((8, LANES), full),
                   pl.BlockSpec(memory_space=pl.ANY)],
        out_shape=[jax.ShapeDtypeStruct((8, t), I32), jax.ShapeDtypeStruct((8, nbp), I32),
                   jax.ShapeDtypeStruct((8, LANES), I32),
                   jax.ShapeDtypeStruct((n_blocks * blk * ROW_SLABS, LANES), F32)],
        scratch_shapes=[pltpu.VMEM((LANES, LANES), F32),
                        pltpu.VMEM((tm, tm), BF16), pltpu.VMEM((blk * ROW_SLABS, LANES), F32),
                        pltpu.VMEM((8, tm), I32), pltpu.SMEM((8, tm), I32),
                        pltpu.VMEM((8, LANES), I32), pltpu.SMEM((8, LANES), I32),
                        pltpu.SemaphoreType.DMA(()), pltpu.SemaphoreType.DMA(()), pltpu.SemaphoreType.DMA(())],
        compiler_params=_arb(1),
        name=name,
    )(eid, counts, ht)
    return dest, be[0, :n_blocks], nu[0, :1], xs, n_blocks


def _ffn_kernel(be_ref, nu_ref, x_ref, wg_ref, wu_ref, wd_ref, o_ref):
    i = pl.program_id(0)

    @pl.when(i < nu_ref[0])
    def _():
        blk = x_ref.shape[0] // ROW_SLABS
        xb = _load_token_tiles(x_ref, blk).astype(BF16)
        hid = _silu(_dot(xb, wg_ref[0])) * _dot(xb, wu_ref[0])
        _store_token_tiles(o_ref, _dot(hid.astype(BF16), wd_ref[0]))

    @pl.when(i >= nu_ref[0])
    def _():
        o_ref[...] = jnp.zeros(o_ref.shape, o_ref.dtype)


def _moe_ffn(xs, block_expert, n_used, n_blocks, weights, blk, name):
    w_gate, w_up, w_down = weights
    d, f = w_gate.shape[1], w_gate.shape[2]
    brows = blk * ROW_SLABS
    used = lambda i, be, nu: (jnp.minimum(i, nu[0] - 1), 0)
    every = lambda i, be, nu: (i, 0)
    expert = lambda i, be, nu: (be[i], 0, 0)
    grid_spec = pltpu.PrefetchScalarGridSpec(
        num_scalar_prefetch=2,
        grid=(n_blocks,),
        in_specs=[pl.BlockSpec((brows, LANES), used),
                  pl.BlockSpec((1, d, f), expert), pl.BlockSpec((1, d, f), expert),
                  pl.BlockSpec((1, f, d), expert)],
        out_specs=pl.BlockSpec((brows, LANES), every))
    return pl.pallas_call(
        _ffn_kernel,
        grid_spec=grid_spec,
        out_shape=jax.ShapeDtypeStruct((n_blocks * brows, LANES), F32),
        compiler_params=_arb(1),
        name=name,
    )(block_expert, n_used, xs, w_gate, w_up, w_down)


def _combine_ln_kernel(dst_ref, dstn_ref, h_ref, wt_ref, g_ref, b_ref, yb_hbm, o_ref, ybuf, sem):
    i = pl.program_id(0)
    n = pl.num_programs(0)
    tm = h_ref.shape[0]

    def start_gather(dref, slot):
        for r in range(tm):
            for k in range(MOE_TOP_K):
                src = pl.multiple_of(dref[k, r] * ROW_SLABS, ROW_SLABS)
                pltpu.make_async_copy(yb_hbm.at[pl.ds(src, ROW_SLABS)],
                                      ybuf.at[slot, k, pl.ds(r * ROW_SLABS, ROW_SLABS)],
                                      sem.at[slot]).start(priority=k)

    @pl.when(i == 0)
    def _():
        start_gather(dst_ref, 0)

    @pl.when(i + 1 < n)
    def _():
        start_gather(dstn_ref, (i + 1) % 2)

    slot = i % 2
    for k in range(MOE_TOP_K):
        pltpu.make_async_copy(yb_hbm.at[pl.ds(0, tm * ROW_SLABS)], ybuf.at[slot, k], sem.at[slot]).wait()
    wt = wt_ref[...]
    ffn = (wt[:, 0:1] * _load_token_tiles(ybuf.at[slot, 0], tm)
           + wt[:, 1:2] * _load_token_tiles(ybuf.at[slot, 1], tm))
    o_ref[...] = _layer_norm(DEEPNORM_ALPHA * h_ref[...] + ffn, g_ref[...], b_ref[...])


def _combine_ln(h, wt, dest, yb, g, b, name):
    t, d = h.shape
    tm = min(CMB_TM, t)
    nblk = t // tm
    row = lambda i: (i, 0)
    full = lambda i: (0, 0)
    return pl.pallas_call(
        _combine_ln_kernel,
        grid=(nblk,),
        in_specs=[pl.BlockSpec((8, tm), lambda i: (0, i), memory_space=pltpu.SMEM),
                  pl.BlockSpec((8, tm), lambda i: (0, jnp.minimum(i + 1, nblk - 1)),
                               memory_space=pltpu.SMEM),
                  pl.BlockSpec((tm, d), row), pl.BlockSpec((tm, LANES), row),
                  pl.BlockSpec((1, d), full), pl.BlockSpec((1, d), full),
                  pl.BlockSpec(memory_space=pl.ANY)],
        out_specs=pl.BlockSpec((tm, d), row),
        out_shape=jax.ShapeDtypeStruct((t, d), F32),
        scratch_shapes=[pltpu.VMEM((2, MOE_TOP_K, tm * ROW_SLABS, LANES), F32), pltpu.SemaphoreType.DMA((2,))],
        compiler_params=_arb(1),
        name=name,
    )(dest, dest, h, wt, g[None, :], b[None, :], yb)


def _hier_moe_ln(h, ht, eid, wt, counts, w_bf16, g, b, layer):
    dest, block_expert, n_used, xs, n_blocks = _moe_dispatch(ht, eid, counts, FFN_BLK, f"moe_dispatch{layer}")
    yb = _moe_ffn(xs, block_expert, n_used, n_blocks, w_bf16, FFN_BLK, f"moe_ffn{layer}")
    return _combine_ln(h, wt, dest, yb, g, b, f"moe_combine_ln{layer}")


def _qkv_rope_kernel(x_ref, w_ref, pos_ref, inv_ref, o_ref):
    xb = x_ref[...].astype(BF16)
    tm = xb.shape[0]
    n = D_MODEL
    ang = pos_ref[...].astype(F32) * inv_ref[...]
    lane = lax.broadcasted_iota(I32, (tm, LANES), 1)
    dd = lane & (ATTN_HEAD_DIM - 1)
    half = ROT_DIM // 2
    cosv = jnp.cos(ang)
    sinv = jnp.sin(ang)
    c_t = jnp.where(dd < ROT_DIM, cosv, 1.0)
    s_up = jnp.where(dd < half, -sinv, 0.0)
    s_dn = jnp.where((dd >= half) & (dd < ROT_DIM), sinv, 0.0)
    for j, sc in ((0, ATTN_HEAD_DIM ** -0.5 * math.log2(math.e)), (1, 1.0)):
        acc = _dot(xb, w_ref[:, j * n:(j + 1) * n])
        c_j, up_j, dn_j = c_t * sc, s_up * sc, s_dn * sc
        for blk in range(n // LANES):
            tt = acc[:, blk * LANES:(blk + 1) * LANES]
            out = tt * c_j + pltpu.roll(tt, LANES - half, 1) * up_j + pltpu.roll(tt, half, 1) * dn_j
            o_ref[:, j * n + blk * LANES:j * n + (blk + 1) * LANES] = out.astype(o_ref.dtype)
    o_ref[:, 2 * n:3 * n] = _dot(xb, w_ref[:, 2 * n:3 * n]).astype(o_ref.dtype)


def _rope_inv_table():
    inv = ROPE_THETA ** (-jnp.arange(0, ROT_DIM, 2, dtype=F32) / ROT_DIM)
    head = jnp.concatenate([inv, inv, jnp.zeros((ATTN_HEAD_DIM - ROT_DIM,), F32)])
    return jnp.tile(head, LANES // ATTN_HEAD_DIM)[None, :]


def _qkv_rope(h, w_qkv, positions):
    m, k = h.shape
    n = w_qkv.shape[1]
    tm = min(MM_TM, m)
    pos = positions.reshape(m, 1)
    return pl.pallas_call(
        _qkv_rope_kernel,
        grid=(m // tm,),
        in_specs=[pl.BlockSpec((tm, k), lambda i: (i, 0)),
                  pl.BlockSpec((k, n), lambda i: (0, 0)),
                  pl.BlockSpec((tm, 1), lambda i: (i, 0)),
                  pl.BlockSpec((1, LANES), lambda i: (0, 0))],
        out_specs=pl.BlockSpec((tm, n), lambda i: (i, 0)),
        out_shape=jax.ShapeDtypeStruct((m, n), BF16),
        compiler_params=_arb(1),
        name="mm_qkv_rope",
    )(h, w_qkv, pos, _rope_inv_table())


def _attn_kernel(q_ref, k_ref, v_ref, lq1_ref, lk1_ref, lq2_ref, lk2_ref, sw_ref, wg_ref, wu_ref, wd_ref,
                 o_ref, wgb_ref, wub_ref, wdb_ref,
                 vx_ref, m0_ref, m1_ref, acc0_ref, acc1_ref, *, lambda_init, cast):
    m_refs = (m0_ref, m1_ref)
    acc_refs = (acc0_ref, acc1_ref)
    i = pl.program_id(2)
    tq = q_ref.shape[0]
    cast.emit(pl.program_id(0) * pl.num_programs(1) + pl.program_id(1),
              (wg_ref, wu_ref, wd_ref), (wgb_ref, wub_ref, wdb_ref), extra=i == 0)

    @pl.when(i == 0)
    def _():
        vx_ref[:, 0:LANES] = v_ref[...]
        vx_ref[:, LANES:] = jnp.ones((vx_ref.shape[0], LANES), vx_ref.dtype)

    q = q_ref[...]
    lane = lax.broadcasted_iota(I32, (tq, LANES), 1)
    zero = jnp.zeros((), q.dtype)
    qs = (jnp.where(lane < ATTN_HEAD_DIM, q, zero), jnp.where(lane >= ATTN_HEAD_DIM, q, zero))
    for c in range(2):
        m_refs[c][...] = jnp.full(m_refs[c].shape, -jnp.inf, F32)
        acc_refs[c][...] = jnp.zeros(acc_refs[c].shape, F32)

    def step(off, width, diag_col=None):
        off = pl.multiple_of(off, tq)
        kb = k_ref[pl.ds(off, width), :]
        vb = vx_ref[pl.ds(off, width), :]
        ss = [lax.dot_general(qs[c], kb, (((1,), (1,)), ((), ())), preferred_element_type=F32)
              for c in range(2)]
        if diag_col is not None:
            row_i = lax.broadcasted_iota(I32, (tq, width), 0)
            col_i = lax.broadcasted_iota(I32, (tq, width), 1)
            ss = [jnp.where(row_i + diag_col >= col_i, s, -jnp.inf) for s in ss]
        m_prev = [m_refs[c][...] for c in range(2)]
        mn = [jnp.maximum(m_prev[c], jnp.max(ss[c], axis=-1, keepdims=True)) for c in range(2)]
        ps = [jnp.exp2(ss[c] - jnp.tile(mn[c], (1, width // LANES))).astype(BF16) for c in range(2)]
        for c in range(2):
            alpha = jnp.exp2(m_prev[c] - mn[c])
            acc_refs[c][...] = jnp.tile(alpha, (1, 2)) * acc_refs[c][...] + _dot(ps[c], vb)
            m_refs[c][...] = mn[c]

    def loop(n, fn):
        lax.fori_loop(0, n, lambda j, carry: (fn(j), carry)[1], 0)

    loop(i // 2, lambda j: step(j * (2 * tq), 2 * tq))

    @pl.when(i % 2 == 0)
    def _():
        step(i * tq, tq, diag_col=0)

    @pl.when(i % 2 == 1)
    def _():
        step((i - 1) * tq, 2 * tq, diag_col=tq)

    lam = (jnp.exp(jnp.sum(lq1_ref[...] * lk1_ref[...], axis=-1, keepdims=True))
           - jnp.exp(jnp.sum(lq2_ref[...] * lk2_ref[...], axis=-1, keepdims=True)) + lambda_init)
    a1 = acc0_ref[...]
    a2 = acc1_ref[...]
    o = a1[:, :LANES] / a1[:, LANES:] - lam * (a2[:, :LANES] / a2[:, LANES:])
    o = o * lax.rsqrt(jnp.mean(o * o, axis=-1, keepdims=True) + NORM_EPS)
    o_ref[...] = (o * sw_ref[...] * (1.0 - lambda_init)).astype(o_ref.dtype)


def _diff_attention(qkv, lq1, lk1, lq2, lk2, subln_w, lambda_init, batch, seq, moe_weights, moe_layer):
    t = batch * seq
    tq = min(ATT_TQ, seq)
    nq = seq // tq
    h_n = ATTN_N_HEADS
    vec = lambda b, h, i: (0, 0)
    cast = _CastPlan(*moe_weights, moe_layer, batch * h_n, lambda b, h, i: b * h_n + h)
    o, *w_bf16 = pl.pallas_call(
        functools.partial(_attn_kernel, lambda_init=lambda_init, cast=cast),
        grid=(batch, h_n, nq),
        in_specs=[pl.BlockSpec((tq, LANES), lambda b, h, i: (b * nq + i, h)),
                  pl.BlockSpec((seq, LANES), lambda b, h, i: (b, h_n + h)),
                  pl.BlockSpec((seq, LANES), lambda b, h, i: (b, 2 * h_n + h)),
                  pl.BlockSpec((1, ATTN_HEAD_DIM), vec), pl.BlockSpec((1, ATTN_HEAD_DIM), vec),
                  pl.BlockSpec((1, ATTN_HEAD_DIM), vec), pl.BlockSpec((1, ATTN_HEAD_DIM), vec),
                  pl.BlockSpec((1, ATTN_V_DIM), vec)] + cast.in_specs,
        out_specs=[pl.BlockSpec((tq, LANES), lambda b, h, i: (b * nq + i, h))] + cast.out_specs,
        out_shape=[jax.ShapeDtypeStruct((t, h_n * ATTN_V_DIM), BF16)] + cast.out_shape,
        scratch_shapes=[pltpu.VMEM((seq, 2 * LANES), BF16),
                        pltpu.VMEM((tq, LANES), F32), pltpu.VMEM((tq, LANES), F32),
                        pltpu.VMEM((tq, 2 * LANES), F32), pltpu.VMEM((tq, 2 * LANES), F32)],
        compiler_params=_arb(3),
        name="diff_attn",
    )(qkv, qkv, qkv, lq1[None, :], lk1[None, :], lq2[None, :], lk2[None, :], subln_w[None, :], *cast.weights)
    return o, tuple(w_bf16)


def kernel(x, positions, ln_mix_g, ln_mix_b, ln_ffn_g, ln_ffn_b, ssm_w_in, ssm_conv_w, ssm_conv_b, ssm_dt_bias, ssm_a_log, ssm_d, ssm_norm_w, ssm_w_out, attn_w_qkv, attn_lam_q1, attn_lam_k1, attn_lam_q2, attn_lam_k2, attn_subln_w, attn_w_o, moe_w_group, moe_w_expert, moe_w_gate, moe_w_up, moe_w_down):
    batch, seq, d = x.shape
    t = batch * seq
    h = x.reshape(t, d)

    w_in = ssm_w_in[0].astype(BF16)

    moe_weights = (moe_w_gate, moe_w_up, moe_w_down)
    zx = _matmul(h, w_in, SSM_ZX_DIM, F32)
    yg, moe_w0 = _ssd(zx, h, w_in[:, SSM_ZX_DIM:], ssm_conv_w[0], ssm_conv_b[0], ssm_dt_bias[0],
                      ssm_a_log[0], ssm_d[0], ssm_norm_w[0], batch, seq, moe_weights, 0)
    h, ht, eid, wt, cnt = _mm_ln_router(yg, ssm_w_out[0].astype(BF16), h, ln_mix_g[0], ln_mix_b[0],
                                        _router_weights(moe_w_group[0], moe_w_expert[0]), "mm_ssm_out_ln_router")
    h = _hier_moe_ln(h, ht, eid, wt, cnt, moe_w0, ln_ffn_g[0], ln_ffn_b[0], 0)

    lambda_init = 0.8 - 0.6 * math.exp(-0.3 * 1)
    qkv = _qkv_rope(h, attn_w_qkv[0].astype(BF16), positions)
    o, moe_w1 = _diff_attention(qkv, attn_lam_q1[0], attn_lam_k1[0], attn_lam_q2[0], attn_lam_k2[0],
                                attn_subln_w[0], lambda_init, batch, seq, moe_weights, 1)
    h, ht, eid, wt, cnt = _mm_ln_router(o, attn_w_o[0].astype(BF16), h, ln_mix_g[1], ln_mix_b[1],
                                        _router_weights(moe_w_group[1], moe_w_expert[1]), "mm_attn_out_ln_router")
    h = _hier_moe_ln(h, ht, eid, wt, cnt, moe_w1, ln_ffn_g[1], ln_ffn_b[1], 1)
    return h.reshape(batch, seq, d)
```

```python
import functools
import math

import jax
import jax.numpy as jnp
from jax import lax
from jax.experimental import pallas as pl
from jax.experimental.pallas import tpu as pltpu

F32 = jnp.float32
BF16 = jnp.bfloat16
I32 = jnp.int32

D_MODEL = 1024
DEPTH = 2
SSM_D_INNER = 2048
SSM_HEADDIM = 64
SSM_N_HEADS = 32
SSM_N_GROUPS = 8
SSM_HEADS_PER_GROUP = 4
SSM_D_STATE = 128
SSM_CONV_K = 4
SSM_CHUNK = 128
SSM_GROUP_WIDTH = SSM_HEADS_PER_GROUP * SSM_HEADDIM
SSM_ZX_DIM = 2 * SSM_D_INNER + 2 * SSM_N_GROUPS * SSM_D_STATE
ATTN_HEAD_DIM = 64
ATTN_N_HEADS = 8
ATTN_V_DIM = 128
ROT_DIM = 16
ROPE_THETA = 500000.0
MOE_GROUPS = 4
MOE_EXPERTS_PER_GROUP = 8
MOE_N_EXPERTS = 32
MOE_TOP_K = 2
MOE_D_FF = 512
DEEPNORM_ALPHA = (2 * DEPTH) ** 0.25
NORM_EPS = 1e-5

LANES = 128
ROW_SLABS = D_MODEL // LANES
CONV_TAIL = 8
CONV_ROWS = 128

MM_TM = 512
MM_TM_WIDE = 256
MM_TN = 1024
FFN_BLK = 512
DSP_TM = 512
LN_SUBTILES = 4
CMB_TM = 256
ATT_TQ = 512


def _arb(n):
    return pltpu.CompilerParams(dimension_semantics=("arbitrary",) * n,
                                vmem_limit_bytes=56 * 1024 * 1024)


def _sigmoid(x):
    return 1.0 / (1.0 + jnp.exp(-x))


def _silu(x):
    return x * _sigmoid(x)


def _softplus(x):
    return jnp.maximum(x, 0.0) + jnp.log(1.0 + jnp.exp(-jnp.abs(x)))


def _layer_norm(y, g, b):
    mu = jnp.mean(y, axis=-1, keepdims=True)
    d = y - mu
    var = jnp.mean(d * d, axis=-1, keepdims=True)
    return d * lax.rsqrt(var + NORM_EPS) * g + b


def _split3(a):
    a1 = a.astype(BF16)
    r1 = a - a1.astype(F32)
    a2 = r1.astype(BF16)
    a3 = (r1 - a2.astype(F32)).astype(BF16)
    return a1, a2, a3


def _dot(a, b):
    return jnp.dot(a, b, preferred_element_type=F32)


def _dot_sel(a, sel):
    a1, a2, a3 = _split3(a)
    return _dot(a1, sel) + _dot(a2, sel) + _dot(a3, sel)


def _dot_f32(a, b):
    a1, a2, a3 = _split3(a)
    b1, b2, b3 = _split3(b)
    return (_dot(a1, b1) + _dot(a1, b2) + _dot(a2, b1)
            + _dot(a2, b2) + _dot(a1, b3) + _dot(a3, b1))


class _CastPlan:
    def __init__(self, w_gate, w_up, w_down, layer, steps, step_of):
        self.weights = (w_gate, w_up, w_down)
        self.n_e = w_gate.shape[1]
        self.eps = -(-self.n_e // steps)
        assert self.n_e % self.eps == 0
        self.n_cast = self.n_e // self.eps
        self.stride = steps // self.n_cast
        block = lambda *ids: jnp.minimum(step_of(*ids) // self.stride, self.n_cast - 1)
        self.in_specs = [pl.BlockSpec((1, self.eps) + w.shape[2:], lambda *ids: (layer, block(*ids), 0, 0))
                         for w in self.weights]
        self.out_specs = [pl.BlockSpec((self.eps,) + w.shape[2:], lambda *ids: (block(*ids), 0, 0))
                          for w in self.weights]
        self.out_shape = [jax.ShapeDtypeStruct(w.shape[1:], BF16) for w in self.weights]

    def emit(self, step, w_refs, o_refs, extra=True):
        @pl.when(extra & (step % self.stride == 0) & (step // self.stride < self.n_cast))
        def _():
            for w_ref, o_ref in zip(w_refs, o_refs):
                o_ref[...] = w_ref[0].astype(BF16)


def _mm_kernel(x_ref, w_ref, o_ref):
    xb = x_ref[...].astype(BF16)
    for j in range(o_ref.shape[1] // MM_TN):
        cols = slice(j * MM_TN, (j + 1) * MM_TN)
        o_ref[:, cols] = _dot(xb, w_ref[:, cols]).astype(o_ref.dtype)


def _matmul(x, w, n, out_dtype):
    m, k = x.shape
    tm = min(MM_TM_WIDE, m)
    return pl.pallas_call(
        _mm_kernel,
        grid=(m // tm,),
        in_specs=[pl.BlockSpec((tm, k), lambda i: (i, 0)), pl.BlockSpec((k, n), lambda i: (0, 0))],
        out_specs=pl.BlockSpec((tm, n), lambda i: (i, 0)),
        out_shape=jax.ShapeDtypeStruct((m, n), out_dtype),
        compiler_params=_arb(1),
        name="mm_inproj",
    )(x, w)


def _ssd_kernel(z_ref, xs_ref, bc_ref, x_ref, wdt_ref, cw_ref, cb_ref, dtb_ref, alog_ref, dskip_ref,
                nw_ref, e64_ref, e128_ref, wg_ref, wu_ref, wd_ref, o_ref, wgb_ref, wub_ref, wdb_ref,
                stage_x, stage_bc, state, acp_s, dtp_s, act_s, *, cast):
    G, W, N = SSM_N_GROUPS, SSM_GROUP_WIDTH, SSM_D_STATE
    L = z_ref.shape[0]
    c = pl.program_id(1)
    cast.emit(pl.program_id(0) * pl.num_programs(1) + c, (wg_ref, wu_ref, wd_ref), (wgb_ref, wub_ref, wdb_ref))

    @pl.when(c == 0)
    def _():
        state[...] = jnp.zeros(state.shape, F32)
        for stage in (stage_x, stage_bc):
            stage[0:CONV_TAIL, :] = jnp.zeros((CONV_TAIL, stage.shape[1]), F32)

    @pl.when(c != 0)
    def _():
        for stage in (stage_x, stage_bc):
            stage[0:CONV_TAIL, :] = stage[L:L + CONV_TAIL, :]

    stage_x[CONV_TAIL:CONV_TAIL + L, :] = xs_ref[...]
    stage_bc[CONV_TAIL:CONV_TAIL + L, :] = bc_ref[...]

    def conv(stage, c0, width, w0):
        taps = [cw_ref[k:k + 1, w0:w0 + width] for k in range(SSM_CONV_K)]
        bias = cb_ref[:, w0:w0 + width]
        outs = []
        for rb in range(L // CONV_ROWS):
            r0 = rb * CONV_ROWS
            u = stage[r0:r0 + CONV_ROWS + CONV_TAIL, c0:c0 + width]
            out = bias + taps[SSM_CONV_K - 1] * u[CONV_TAIL:, :]
            for k in range(SSM_CONV_K - 1):
                back = SSM_CONV_K - 1 - k
                out = out + taps[k] * pltpu.roll(u, back, 0)[CONV_TAIL:, :]
            outs.append(_silu(out))
        return jnp.concatenate(outs, axis=0)

    xc = [conv(stage_x, g * W, W, g * W) for g in range(G)]
    bcv = [conv(stage_bc, g * N, N, SSM_D_INNER + g * N).astype(BF16) for g in range(G)]
    ccv = [conv(stage_bc, G * N + g * N, N, SSM_D_INNER + G * N + g * N).astype(BF16) for g in range(G)]

    def x_of(g):
        return xc[g]

    def b_of(g):
        return bcv[g]

    def c_of(g):
        return ccv[g]

    dt = _softplus(_dot(x_ref[...].astype(BF16), wdt_ref[...].astype(BF16)) + dtb_ref[...])
    a = -jnp.exp(alog_ref[...])
    row_i = lax.broadcasted_iota(I32, (L, L), 0)
    col_i = lax.broadcasted_iota(I32, (L, L), 1)
    causal = row_i >= col_i
    a1, a2, a3 = _split3(dt * a)
    tri = causal.astype(BF16)
    ac = _dot(tri, a1) + _dot(tri, a2) + _dot(tri, a3)
    act_s[...] = ac.T
    for n, part in enumerate(_split3(ac)):
        acp_s[n] = part
    for n, part in enumerate(_split3(dt)[:2]):
        dtp_s[n] = part
    lane = lax.broadcasted_iota(I32, (L, LANES), 1)
    first_half = lane < SSM_HEADDIM

    def select(parts_ref, n_parts, sel):
        out = _dot(parts_ref[0], sel)
        for n in range(1, n_parts):
            out = out + _dot(parts_ref[n], sel)
        return out

    gs = range(G)
    col4 = [select(acp_s, 3, e128_ref[g]) for g in gs]
    dt_e = [select(dtp_s, 2, e64_ref[g]) for g in gs]
    cb = [lax.dot_general(c_of(g), b_of(g), (((1,), (1,)), ((), ())), preferred_element_type=F32)
          for g in gs]
    y_cs = [_dot(c_of(g), state[g].astype(BF16)) for g in gs]
    a_e = [jnp.concatenate(
        [jnp.where(first_half, c4[:, 2 * p * LANES:(2 * p + 1) * LANES],
                   c4[:, (2 * p + 1) * LANES:(2 * p + 2) * LANES]) for p in range(2)], axis=1)
        for c4 in col4]
    xdt = [x_of(g) * dt_e[g] for g in gs]
    xdt_b = [v.astype(BF16) for v in xdt]
    ys = [[None] * SSM_HEADS_PER_GROUP for _ in gs]
    for r in range(SSM_HEADS_PER_GROUP):
        for g in gs:
            row = act_s[SSM_HEADS_PER_GROUP * g + r:SSM_HEADS_PER_GROUP * g + r + 1, :]
            seg = col4[g][:, LANES * r:LANES * (r + 1)] - row
            dec = jnp.where(causal, jnp.exp(seg), 0.0)
            xp = xdt_b[g][:, LANES * (r // 2):LANES * (r // 2 + 1)]
            ys[g][r] = _dot((cb[g] * dec).astype(BF16), xp)
    for g in gs:
        y_diag = jnp.concatenate([jnp.where(first_half, ys[g][2 * p], ys[g][2 * p + 1])
                                  for p in range(2)], axis=1)
        al_e = a_e[g][L - 1:L, :]
        new = lax.dot_general(b_of(g), (xdt[g] * jnp.exp(al_e - a_e[g])).astype(BF16),
                              (((0,), (0,)), ((), ())), preferred_element_type=F32)
        y = y_diag + y_cs[g] * jnp.exp(a_e[g]) + x_of(g) * dskip_ref[g]
        state[g] = state[g] * jnp.exp(al_e) + new
        yg = y * _silu(z_ref[:, g * W:(g + 1) * W])
        ms = jnp.mean(yg * yg, axis=-1, keepdims=True)
        o_ref[:, g * W:(g + 1) * W] = (yg * lax.rsqrt(ms + NORM_EPS)
                                       * nw_ref[:, g * W:(g + 1) * W]).astype(o_ref.dtype)


def _head_selectors():
    G, R = SSM_N_GROUPS, SSM_HEADS_PER_GROUP
    h = jnp.arange(LANES)[None, :, None]
    g = jnp.arange(G)[:, None, None]
    j64 = jnp.arange(SSM_GROUP_WIDTH)[None, None, :]
    j128 = jnp.arange(R * LANES)[None, None, :]
    e64 = (h == g * R + j64 // SSM_HEADDIM).astype(BF16)
    e128 = (h == g * R + j128 // LANES).astype(BF16)
    return e64, e128


def _ssd(zx, x2, w_dt, conv_w, conv_b, dt_bias, a_log, d_skip, norm_w, batch, seq, moe_weights, moe_layer):
    L, G, W, N = SSM_CHUNK, SSM_N_GROUPS, SSM_GROUP_WIDTH, SSM_D_STATE
    nc = seq // L
    cast = _CastPlan(*moe_weights, moe_layer, batch * nc, lambda b, c: b * nc + c)
    t = batch * seq
    pad = LANES - SSM_N_HEADS
    wdt = jnp.pad(w_dt, ((0, 0), (0, pad)))
    dtb = jnp.pad(dt_bias, (0, pad))[None, :]
    alog = jnp.pad(a_log, (0, pad))[None, :]
    dsk = jnp.repeat(d_skip, SSM_HEADDIM).reshape(G, 1, W)
    e64, e128 = _head_selectors()
    conv_dim = conv_w.shape[1]
    row = lambda b, c: (b * nc + c, 0)
    full2 = lambda b, c: (0, 0)
    full3 = lambda b, c: (0, 0, 0)
    yg, *w_bf16 = pl.pallas_call(
        functools.partial(_ssd_kernel, cast=cast),
        grid=(batch, nc),
        in_specs=[pl.BlockSpec((L, SSM_D_INNER), row),
                  pl.BlockSpec((L, SSM_D_INNER), lambda b, c: (b * nc + c, 1)),
                  pl.BlockSpec((L, 2 * G * N), lambda b, c: (b * nc + c, 2)),
                  pl.BlockSpec((L, D_MODEL), row),
                  pl.BlockSpec((D_MODEL, LANES), full2),
                  pl.BlockSpec((SSM_CONV_K, conv_dim), full2),
                  pl.BlockSpec((1, conv_dim), full2),
                  pl.BlockSpec((1, LANES), full2),
                  pl.BlockSpec((1, LANES), full2),
                  pl.BlockSpec((G, 1, W), full3),
                  pl.BlockSpec((1, SSM_D_INNER), full2),
                  pl.BlockSpec((G, LANES, W), full3),
                  pl.BlockSpec((G, LANES, SSM_HEADS_PER_GROUP * LANES), full3)] + cast.in_specs,
        out_specs=[pl.BlockSpec((L, SSM_D_INNER), row)] + cast.out_specs,
        out_shape=[jax.ShapeDtypeStruct((t, SSM_D_INNER), BF16)] + cast.out_shape,
        scratch_shapes=[pltpu.VMEM((CONV_TAIL + L, SSM_D_INNER), F32),
                        pltpu.VMEM((CONV_TAIL + L, 2 * G * N), F32),
                        pltpu.VMEM((G, N, W), F32),
                        pltpu.VMEM((3, L, LANES), BF16),
                        pltpu.VMEM((2, L, LANES), BF16),
                        pltpu.VMEM((LANES, L), F32)],
        compiler_params=_arb(2),
        name="ssd_scan",
    )(zx, zx, zx, x2, wdt, conv_w, conv_b[None, :], dtb, alog, dsk, norm_w[None, :], e64, e128, *cast.weights)
    return yg, tuple(w_bf16)


def _route(h, wr):
    tm = h.shape[0]
    logits = _dot(h.astype(BF16), wr)
    lane_i = lax.broadcasted_iota(I32, (tm, LANES), 1)
    lane = lane_i.astype(F32)
    neg = -jnp.inf
    big = float(LANES)

    def first_argmax(v, vmax):
        return jnp.min(jnp.where(v == vmax, lane, big), axis=-1, keepdims=True)

    gl = jnp.where((lane_i >= MOE_N_EXPERTS) & (lane_i < MOE_N_EXPERTS + MOE_GROUPS), logits, neg)
    gm = jnp.max(gl, axis=-1, keepdims=True)
    g_sel = first_argmax(gl, gm) - float(MOE_N_EXPERTS)
    g_gate = 1.0 / jnp.sum(jnp.exp(gl - gm), axis=-1, keepdims=True)
    lo = g_sel * float(MOE_EXPERTS_PER_GROUP)
    el = jnp.where((lane >= lo) & (lane < lo + float(MOE_EXPERTS_PER_GROUP)), logits, neg)
    m1 = jnp.max(el, axis=-1, keepdims=True)
    i1 = first_argmax(el, m1)
    el2 = jnp.where(lane == i1, neg, el)
    m2 = jnp.max(el2, axis=-1, keepdims=True)
    i2 = first_argmax(el2, m2)
    p2 = jnp.exp(m2 - m1)
    t1 = 1.0 / (1.0 + p2)
    t2 = p2 / (1.0 + p2)
    eid = jnp.where(lane_i == 0, i1, jnp.where(lane_i == 1, i2, 0.0)).astype(I32)
    wt = jnp.where(lane_i == 0, g_gate * t1, jnp.where(lane_i == 1, g_gate * t2, 0.0))
    count = jnp.sum(jnp.where(lane == i1, 1.0, 0.0) + jnp.where(lane == i2, 1.0, 0.0), axis=0, keepdims=True)
    return eid, wt, count


def _router_weights(w_group, w_expert):
    pad = LANES - MOE_N_EXPERTS - MOE_GROUPS
    return jnp.pad(jnp.concatenate([w_expert, w_group], axis=1), ((0, 0), (0, pad))).astype(BF16)


def _store_token_tiles(ref, v, first_token=0):
    rows = v.shape[0]
    for s in range(ROW_SLABS):
        ref[pl.ds(first_token * ROW_SLABS + s, rows, stride=ROW_SLABS), :] = v[:, s * LANES:(s + 1) * LANES]


def _load_token_tiles(ref, rows):
    return jnp.concatenate([ref[pl.ds(s, rows, stride=ROW_SLABS), :] for s in range(ROW_SLABS)], axis=1)


def _mm_ln_router_kernel(x_ref, w_ref, r_ref, g_ref, b_ref, wr_ref, h_ref, ht_ref, eid_ref, wt_ref, cnt_ref):
    @pl.when(pl.program_id(0) == 0)
    def _():
        cnt_ref[...] = jnp.zeros(cnt_ref.shape, F32)

    sub = x_ref.shape[0] // LN_SUBTILES
    rows = [slice(u * sub, (u + 1) * sub) for u in range(LN_SUBTILES)]
    ys = [DEEPNORM_ALPHA * r_ref[rs, :] + _dot(x_ref[rs, :].astype(BF16), w_ref[...]) for rs in rows]
    hs = [_layer_norm(y, g_ref[...], b_ref[...]) for y in ys]
    routes = [_route(h, wr_ref[...]) for h in hs]
    for u, rs in enumerate(rows):
        h_ref[rs, :] = hs[u]
        _store_token_tiles(ht_ref, hs[u], first_token=u * sub)
        eid_ref[rs, :] = routes[u][0]
        wt_ref[rs, :] = routes[u][1]
    cnt_ref[...] += jnp.broadcast_to(sum(r[2] for r in routes), cnt_ref.shape)


def _mm_ln_router(x, w, resid, g, b, wr, name):
    m, k = x.shape
    d = w.shape[1]
    tm = min(MM_TM, m)
    row = lambda i: (i, 0)
    full = lambda i: (0, 0)
    return pl.pallas_call(
        _mm_ln_router_kernel,
        grid=(m // tm,),
        in_specs=[pl.BlockSpec((tm, k), row), pl.BlockSpec((k, d), full), pl.BlockSpec((tm, d), row),
                  pl.BlockSpec((1, d), full), pl.BlockSpec((1, d), full), pl.BlockSpec((d, LANES), full)],
        out_specs=[pl.BlockSpec((tm, d), row), pl.BlockSpec((tm * ROW_SLABS, LANES), row),
                   pl.BlockSpec((tm, LANES), row), pl.BlockSpec((tm, LANES), row),
                   pl.BlockSpec((8, LANES), full)],
        out_shape=[jax.ShapeDtypeStruct((m, d), F32), jax.ShapeDtypeStruct((m * ROW_SLABS, LANES), F32),
                   jax.ShapeDtypeStruct((m, LANES), I32), jax.ShapeDtypeStruct((m, LANES), F32),
                   jax.ShapeDtypeStruct((8, LANES), F32)],
        compiler_params=_arb(1),
        name=name,
    )(x, w, resid, g[None, :], b[None, :], wr)


def _dispatch_kernel(eid_ref, cnt_ref, h_ref, dest_ref, be_ref, nu_ref, xs_hbm,
                     base, upper, zbuf, dst_v, dst_s, pe_v, pe_s, sem_z, sem_r, sem_s, *, blk):
    i = pl.program_id(0)
    tm = eid_ref.shape[0]
    n_e = LANES
    eid_t = eid_ref[...].astype(F32).T
    sub = lax.broadcasted_iota(I32, (n_e, tm), 0).astype(F32)
    oh = [(sub == eid_t[k:k + 1, :]).astype(F32) for k in range(MOE_TOP_K)]
    tot = [jnp.sum(o, axis=1, keepdims=True) for o in oh]

    @pl.when(i == 0)
    def _():
        r_i = lax.broadcasted_iota(I32, (tm, tm), 0)
        c_i = lax.broadcasted_iota(I32, (tm, tm), 1)
        upper[...] = (r_i < c_i).astype(BF16)
        counts = jnp.broadcast_to(cnt_ref[0:1, :], (n_e, n_e)).T
        padded = jnp.floor((counts + float(blk - 1)) * (1.0 / blk)) * float(blk)
        r_i = lax.broadcasted_iota(I32, (n_e, n_e), 0)
        c_i = lax.broadcasted_iota(I32, (n_e, n_e), 1)
        tril = (r_i >= c_i).astype(BF16)
        p1, p2, p3 = _split3(padded)
        pends = _dot(tril, p1) + _dot(tril, p2) + _dot(tril, p3)
        base[...] = pends - padded
        nbp = be_ref.shape[1]
        blk_start = lax.broadcasted_iota(I32, (n_e, nbp), 1).astype(F32) * float(blk)
        is_e = lax.broadcasted_iota(I32, (n_e, nbp), 0) < MOE_N_EXPERTS
        done = jnp.where(is_e & (jnp.tile(pends, (1, nbp // LANES)) <= blk_start), 1.0, 0.0)
        be = jnp.minimum(jnp.sum(done, axis=0, keepdims=True), float(MOE_N_EXPERTS - 1))
        be_ref[...] = jnp.broadcast_to(be, be_ref.shape).astype(I32)
        last = pends[MOE_N_EXPERTS - 1:MOE_N_EXPERTS, :] * (1.0 / blk)
        nu_ref[...] = jnp.broadcast_to(last, nu_ref.shape).astype(I32)
        zbuf[...] = jnp.zeros(zbuf.shape, F32)
        row8 = lax.broadcasted_iota(I32, (8, LANES), 0)
        pe_v[...] = jnp.where(row8 == 0, pends.T[0:8, :], counts.T[0:8, :]).astype(I32)
        cp = pltpu.make_async_copy(pe_v, pe_s, sem_s)
        cp.start()
        cp.wait()

        brows = blk * ROW_SLABS

        def zero_copy(e):
            start = pl.multiple_of((pe_s[0, e] - blk) * ROW_SLABS, brows)
            return pltpu.make_async_copy(zbuf, xs_hbm.at[pl.ds(start, brows)], sem_z)

        def tail_copy(b):
            return pltpu.make_async_copy(zbuf, xs_hbm.at[pl.ds(pl.multiple_of(b * brows, brows), brows)], sem_z)

        n_used = lax.shift_right_logical(pe_s[0, MOE_N_EXPERTS - 1], blk.bit_length() - 1)
        n_blocks = xs_hbm.shape[0] // brows
        for e in range(MOE_N_EXPERTS):
            @pl.when(pe_s[1, e] > 0)
            def _():
                zero_copy(e).start()
        lax.fori_loop(n_used, n_blocks, lambda b, c: (tail_copy(b).start(), c)[1], 0)
        for e in range(MOE_N_EXPERTS):
            @pl.when(pe_s[1, e] > 0)
            def _():
                zero_copy(e).wait()
        lax.fori_loop(n_used, n_blocks, lambda b, c: (tail_copy(b).wait(), c)[1], 0)

    b0 = base[:, 0:1]
    c0 = _dot(oh[0].astype(BF16), upper[...])
    c1 = _dot(oh[1].astype(BF16), upper[...])
    d0 = jnp.sum(oh[0] * (b0 + c0), axis=0, keepdims=True)
    d1 = jnp.sum(oh[1] * (b0 + tot[0] + c1), axis=0, keepdims=True)
    base[...] += jnp.broadcast_to(tot[0] + tot[1], base.shape)
    row8 = lax.broadcasted_iota(I32, (8, tm), 0)
    dst = jnp.where(row8 == 0, d0, jnp.where(row8 == 1, d1, 0.0)).astype(I32)
    dest_ref[...] = dst
    dst_v[...] = dst
    cp = pltpu.make_async_copy(dst_v, dst_s, sem_s)
    cp.start()
    cp.wait()

    for r in range(tm):
        for k in range(MOE_TOP_K):
            slot = pl.multiple_of(dst_s[k, r] * ROW_SLABS, ROW_SLABS)
            pltpu.make_async_copy(h_ref.at[pl.ds(r * ROW_SLABS, ROW_SLABS)],
                                  xs_hbm.at[pl.ds(slot, ROW_SLABS)], sem_r).start(priority=k)
    for k in range(MOE_TOP_K):
        pltpu.make_async_copy(h_ref, xs_hbm.at[pl.ds(0, tm * ROW_SLABS)], sem_r).wait()


def _moe_dispatch(ht, eid, counts, blk, name):
    t = ht.shape[0] // ROW_SLABS
    assert blk & (blk - 1) == 0, "block size must be a power of two"
    tm = min(DSP_TM, t)
    nt = t // tm
    n_blocks = -(-(t * MOE_TOP_K) // blk) + MOE_N_EXPERTS
    nbp = -(-n_blocks // LANES) * LANES
    full = lambda i: (0, 0)
    dest, be, nu, xs = pl.pallas_call(
        functools.partial(_dispatch_kernel, blk=blk),
        grid=(nt,),
        in_specs=[pl.BlockSpec((tm, LANES), lambda i: (i, 0)),
                  pl.BlockSpec((8, LANES), full),
                  pl.BlockSpec((tm * ROW_SLABS, LANES), lambda i: (i, 0))],
        out_specs=[pl.BlockSpec((8, tm), lambda i: (0, i)),
                   pl.BlockSpec((8, nbp), full),
                   pl.BlockSpec((8, LANES), full),
                   pl.BlockSpec(memory_space=pl.ANY)],
        out_shape=[jax.ShapeDtypeStruct((8, t), I32), jax.ShapeDtypeStruct((8, nbp), I32),
                   jax.ShapeDtypeStruct((8, LANES), I32),
                   jax.ShapeDtypeStruct((n_blocks * blk * ROW_SLABS, LANES), F32)],
        scratch_shapes=[pltpu.VMEM((LANES, LANES), F32),
                        pltpu.VMEM((tm, tm), BF16), pltpu.VMEM((blk * ROW_SLABS, LANES), F32),
                        pltpu.VMEM((8, tm), I32), pltpu.SMEM((8, tm), I32),
                        pltpu.VMEM((8, LANES), I32), pltpu.SMEM((8, LANES), I32),
                        pltpu.SemaphoreType.DMA(()), pltpu.SemaphoreType.DMA(()), pltpu.SemaphoreType.DMA(())],
        compiler_params=_arb(1),
        name=name,
    )(eid, counts, ht)
    return dest, be[0, :n_blocks], nu[0, :1], xs, n_blocks


def _ffn_kernel(be_ref, nu_ref, x_ref, wg_ref, wu_ref, wd_ref, o_ref):
    i = pl.program_id(0)

    @pl.when(i < nu_ref[0])
    def _():
        blk = x_ref.shape[0] // ROW_SLABS
        xb = _load_token_tiles(x_ref, blk).astype(BF16)
        hid = _silu(_dot(xb, wg_ref[0])) * _dot(xb, wu_ref[0])
        _store_token_tiles(o_ref, _dot(hid.astype(BF16), wd_ref[0]))

    @pl.when(i >= nu_ref[0])
    def _():
        o_ref[...] = jnp.zeros(o_ref.shape, o_ref.dtype)


def _moe_ffn(xs, block_expert, n_used, n_blocks, weights, blk, name):
    w_gate, w_up, w_down = weights
    d, f = w_gate.shape[1], w_gate.shape[2]
    brows = blk * ROW_SLABS
    used = lambda i, be, nu: (jnp.minimum(i, nu[0] - 1), 0)
    every = lambda i, be, nu: (i, 0)
    expert = lambda i, be, nu: (be[i], 0, 0)
    grid_spec = pltpu.PrefetchScalarGridSpec(
        num_scalar_prefetch=2,
        grid=(n_blocks,),
        in_specs=[pl.BlockSpec((brows, LANES), used),
                  pl.BlockSpec((1, d, f), expert), pl.BlockSpec((1, d, f), expert),
                  pl.BlockSpec((1, f, d), expert)],
        out_specs=pl.BlockSpec((brows, LANES), every))
    return pl.pallas_call(
        _ffn_kernel,
        grid_spec=grid_spec,
        out_shape=jax.ShapeDtypeStruct((n_blocks * brows, LANES), F32),
        compiler_params=_arb(1),
        name=name,
    )(block_expert, n_used, xs, w_gate, w_up, w_down)


def _combine_ln_kernel(dst_ref, dstn_ref, h_ref, wt_ref, g_ref, b_ref, yb_hbm, o_ref, ybuf, sem):
    i = pl.program_id(0)
    n = pl.num_programs(0)
    tm = h_ref.shape[0]

    def start_gather(dref, slot):
        for r in range(tm):
            for k in range(MOE_TOP_K):
                src = pl.multiple_of(dref[k, r] * ROW_SLABS, ROW_SLABS)
                pltpu.make_async_copy(yb_hbm.at[pl.ds(src, ROW_SLABS)],
                                      ybuf.at[slot, k, pl.ds(r * ROW_SLABS, ROW_SLABS)],
                                      sem.at[slot]).start(priority=k)

    @pl.when(i == 0)
    def _():
        start_gather(dst_ref, 0)

    @pl.when(i + 1 < n)
    def _():
        start_gather(dstn_ref, (i + 1) % 2)

    slot = i % 2
    for k in range(MOE_TOP_K):
        pltpu.make_async_copy(yb_hbm.at[pl.ds(0, tm * ROW_SLABS)], ybuf.at[slot, k], sem.at[slot]).wait()
    wt = wt_ref[...]
    ffn = (wt[:, 0:1] * _load_token_tiles(ybuf.at[slot, 0], tm)
           + wt[:, 1:2] * _load_token_tiles(ybuf.at[slot, 1], tm))
    o_ref[...] = _layer_norm(DEEPNORM_ALPHA * h_ref[...] + ffn, g_ref[...], b_ref[...])


def _combine_ln(h, wt, dest, yb, g, b, name):
    t, d = h.shape
    tm = min(CMB_TM, t)
    nblk = t // tm
    row = lambda i: (i, 0)
    full = lambda i: (0, 0)
    return pl.pallas_call(
        _combine_ln_kernel,
        grid=(nblk,),
        in_specs=[pl.BlockSpec((8, tm), lambda i: (0, i), memory_space=pltpu.SMEM),
                  pl.BlockSpec((8, tm), lambda i: (0, jnp.minimum(i + 1, nblk - 1)),
                               memory_space=pltpu.SMEM),
                  pl.BlockSpec((tm, d), row), pl.BlockSpec((tm, LANES), row),
                  pl.BlockSpec((1, d), full), pl.BlockSpec((1, d), full),
                  pl.BlockSpec(memory_space=pl.ANY)],
        out_specs=pl.BlockSpec((tm, d), row),
        out_shape=jax.ShapeDtypeStruct((t, d), F32),
        scratch_shapes=[pltpu.VMEM((2, MOE_TOP_K, tm * ROW_SLABS, LANES), F32), pltpu.SemaphoreType.DMA((2,))],
        compiler_params=_arb(1),
        name=name,
    )(dest, dest, h, wt, g[None, :], b[None, :], yb)


def _hier_moe_ln(h, ht, eid, wt, counts, w_bf16, g, b, layer):
    dest, block_expert, n_used, xs, n_blocks = _moe_dispatch(ht, eid, counts, FFN_BLK, f"moe_dispatch{layer}")
    yb = _moe_ffn(xs, block_expert, n_used, n_blocks, w_bf16, FFN_BLK, f"moe_ffn{layer}")
    return _combine_ln(h, wt, dest, yb, g, b, f"moe_combine_ln{layer}")


def _qkv_rope_kernel(x_ref, w_ref, pos_ref, inv_ref, o_ref):
    xb = x_ref[...].astype(BF16)
    tm = xb.shape[0]
    n = D_MODEL
    ang = pos_ref[...].astype(F32) * inv_ref[...]
    lane = lax.broadcasted_iota(I32, (tm, LANES), 1)
    dd = lane & (ATTN_HEAD_DIM - 1)
    half = ROT_DIM // 2
    cosv = jnp.cos(ang)
    sinv = jnp.sin(ang)
    c_t = jnp.where(dd < ROT_DIM, cosv, 1.0)
    s_up = jnp.where(dd < half, -sinv, 0.0)
    s_dn = jnp.where((dd >= half) & (dd < ROT_DIM), sinv, 0.0)
    for j, sc in ((0, ATTN_HEAD_DIM ** -0.5 * math.log2(math.e)), (1, 1.0)):
        acc = _dot(xb, w_ref[:, j * n:(j + 1) * n])
        c_j, up_j, dn_j = c_t * sc, s_up * sc, s_dn * sc
        for blk in range(n // LANES):
            tt = acc[:, blk * LANES:(blk + 1) * LANES]
            out = tt * c_j + pltpu.roll(tt, LANES - half, 1) * up_j + pltpu.roll(tt, half, 1) * dn_j
            o_ref[:, j * n + blk * LANES:j * n + (blk + 1) * LANES] = out.astype(o_ref.dtype)
    o_ref[:, 2 * n:3 * n] = _dot(xb, w_ref[:, 2 * n:3 * n]).astype(o_ref.dtype)


def _rope_inv_table():
    inv = ROPE_THETA ** (-jnp.arange(0, ROT_DIM, 2, dtype=F32) / ROT_DIM)
    head = jnp.concatenate([inv, inv, jnp.zeros((ATTN_HEAD_DIM - ROT_DIM,), F32)])
    return jnp.tile(head, LANES // ATTN_HEAD_DIM)[None, :]


def _qkv_rope(h, w_qkv, positions):
    m, k = h.shape
    n = w_qkv.shape[1]
    tm = min(MM_TM, m)
    pos = positions.reshape(m, 1)
    return pl.pallas_call(
        _qkv_rope_kernel,
        grid=(m // tm,),
        in_specs=[pl.BlockSpec((tm, k), lambda i: (i, 0)),
                  pl.BlockSpec((k, n), lambda i: (0, 0)),
                  pl.BlockSpec((tm, 1), lambda i: (i, 0)),
                  pl.BlockSpec((1, LANES), lambda i: (0, 0))],
        out_specs=pl.BlockSpec((tm, n), lambda i: (i, 0)),
        out_shape=jax.ShapeDtypeStruct((m, n), BF16),
        compiler_params=_arb(1),
        name="mm_qkv_rope",
    )(h, w_qkv, pos, _rope_inv_table())


def _attn_kernel(q_ref, k_ref, v_ref, lq1_ref, lk1_ref, lq2_ref, lk2_ref, sw_ref, wg_ref, wu_ref, wd_ref,
                 o_ref, wgb_ref, wub_ref, wdb_ref,
                 vx_ref, m0_ref, m1_ref, acc0_ref, acc1_ref, *, lambda_init, cast):
    m_refs = (m0_ref, m1_ref)
    acc_refs = (acc0_ref, acc1_ref)
    jp = pl.program_id(2)
    tq = q_ref.shape[0] // 2
    cast.emit(pl.program_id(0) * pl.num_programs(1) + pl.program_id(1),
              (wg_ref, wu_ref, wd_ref), (wgb_ref, wub_ref, wdb_ref), extra=jp == 0)

    @pl.when(jp == 0)
    def _():
        vx_ref[:, 0:LANES] = v_ref[...]
        vx_ref[:, LANES:] = jnp.ones((vx_ref.shape[0], LANES), vx_ref.dtype)

    q = q_ref[...]
    lane = lax.broadcasted_iota(I32, q.shape, 1)
    zero = jnp.zeros((), q.dtype)
    qs = (jnp.where(lane < ATTN_HEAD_DIM, q, zero), jnp.where(lane >= ATTN_HEAD_DIM, q, zero))
    for c in range(2):
        m_refs[c][...] = jnp.full(m_refs[c].shape, -jnp.inf, F32)
        acc_refs[c][...] = jnp.zeros(acc_refs[c].shape, F32)

    def step(off, width, tiles, diag_col=None):
        off = pl.multiple_of(off, tq)
        kb = k_ref[pl.ds(off, width), :]
        vb = vx_ref[pl.ds(off, width), :]
        chains = [(slice(r0, r0 + tq), c) for r0 in tiles for c in range(2)]
        ss = [lax.dot_general(qs[c][rows, :], kb, (((1,), (1,)), ((), ())), preferred_element_type=F32)
              for rows, c in chains]
        if diag_col is not None:
            row_i = lax.broadcasted_iota(I32, (tq, width), 0)
            col_i = lax.broadcasted_iota(I32, (tq, width), 1)
            ss = [jnp.where(row_i + diag_col >= col_i, s, -jnp.inf) for s in ss]
        m_prev = [m_refs[c][rows, :] for rows, c in chains]
        mn = [jnp.maximum(mp, jnp.max(s, axis=-1, keepdims=True)) for mp, s in zip(m_prev, ss)]
        ps = [jnp.exp2(s - jnp.tile(m, (1, width // LANES))).astype(BF16) for s, m in zip(ss, mn)]
        for n, (rows, c) in enumerate(chains):
            alpha = jnp.exp2(m_prev[n] - mn[n])
            acc_refs[c][rows, :] = jnp.tile(alpha, (1, 2)) * acc_refs[c][rows, :] + _dot(ps[n], vb)
            m_refs[c][rows, :] = mn[n]

    lax.fori_loop(0, jp, lambda j, carry: (step(j * (2 * tq), 2 * tq, (0, tq)), carry)[1], 0)
    step(jp * (2 * tq), tq, (0,), diag_col=0)
    step(jp * (2 * tq), 2 * tq, (tq,), diag_col=tq)

    lam = (jnp.exp(jnp.sum(lq1_ref[...] * lk1_ref[...], axis=-1, keepdims=True))
           - jnp.exp(jnp.sum(lq2_ref[...] * lk2_ref[...], axis=-1, keepdims=True)) + lambda_init)
    a1 = acc0_ref[...]
    a2 = acc1_ref[...]
    o = a1[:, :LANES] / a1[:, LANES:] - lam * (a2[:, :LANES] / a2[:, LANES:])
    o = o * lax.rsqrt(jnp.mean(o * o, axis=-1, keepdims=True) + NORM_EPS)
    o_ref[...] = (o * sw_ref[...] * (1.0 - lambda_init)).astype(o_ref.dtype)


def _diff_attention(qkv, lq1, lk1, lq2, lk2, subln_w, lambda_init, batch, seq, moe_weights, moe_layer):
    t = batch * seq
    tq = 2 * min(ATT_TQ, seq // 2)
    nq = seq // tq
    h_n = ATTN_N_HEADS
    vec = lambda b, h, i: (0, 0)
    cast = _CastPlan(*moe_weights, moe_layer, batch * h_n, lambda b, h, i: b * h_n + h)
    o, *w_bf16 = pl.pallas_call(
        functools.partial(_attn_kernel, lambda_init=lambda_init, cast=cast),
        grid=(batch, h_n, nq),
        in_specs=[pl.BlockSpec((tq, LANES), lambda b, h, i: (b * nq + i, h)),
                  pl.BlockSpec((seq, LANES), lambda b, h, i: (b, h_n + h)),
                  pl.BlockSpec((seq, LANES), lambda b, h, i: (b, 2 * h_n + h)),
                  pl.BlockSpec((1, ATTN_HEAD_DIM), vec), pl.BlockSpec((1, ATTN_HEAD_DIM), vec),
                  pl.BlockSpec((1, ATTN_HEAD_DIM), vec), pl.BlockSpec((1, ATTN_HEAD_DIM), vec),
                  pl.BlockSpec((1, ATTN_V_DIM), vec)] + cast.in_specs,
        out_specs=[pl.BlockSpec((tq, LANES), lambda b, h, i: (b * nq + i, h))] + cast.out_specs,
        out_shape=[jax.ShapeDtypeStruct((t, h_n * ATTN_V_DIM), BF16)] + cast.out_shape,
        scratch_shapes=[pltpu.VMEM((seq, 2 * LANES), BF16),
                        pltpu.VMEM((tq, LANES), F32), pltpu.VMEM((tq, LANES), F32),
                        pltpu.VMEM((tq, 2 * LANES), F32), pltpu.VMEM((tq, 2 * LANES), F32)],
        compiler_params=_arb(3),
        name="diff_attn",
    )(qkv, qkv, qkv, lq1[None, :], lk1[None, :], lq2[None, :], lk2[None, :], subln_w[None, :], *cast.weights)
    return o, tuple(w_bf16)


def kernel(x, positions, ln_mix_g, ln_mix_b, ln_ffn_g, ln_ffn_b, ssm_w_in, ssm_conv_w, ssm_conv_b, ssm_dt_bias, ssm_a_log, ssm_d, ssm_norm_w, ssm_w_out, attn_w_qkv, attn_lam_q1, attn_lam_k1, attn_lam_q2, attn_lam_k2, attn_subln_w, attn_w_o, moe_w_group, moe_w_expert, moe_w_gate, moe_w_up, moe_w_down):
    batch, seq, d = x.shape
    t = batch * seq
    h = x.reshape(t, d)

    w_in = ssm_w_in[0].astype(BF16)

    moe_weights = (moe_w_gate, moe_w_up, moe_w_down)
    zx = _matmul(h, w_in, SSM_ZX_DIM, F32)
    yg, moe_w0 = _ssd(zx, h, w_in[:, SSM_ZX_DIM:], ssm_conv_w[0], ssm_conv_b[0], ssm_dt_bias[0],
                      ssm_a_log[0], ssm_d[0], ssm_norm_w[0], batch, seq, moe_weights, 0)
    h, ht, eid, wt, cnt = _mm_ln_router(yg, ssm_w_out[0].astype(BF16), h, ln_mix_g[0], ln_mix_b[0],
                                        _router_weights(moe_w_group[0], moe_w_expert[0]), "mm_ssm_out_ln_router")
    h = _hier_moe_ln(h, ht, eid, wt, cnt, moe_w0, ln_ffn_g[0], ln_ffn_b[0], 0)

    lambda_init = 0.8 - 0.6 * math.exp(-0.3 * 1)
    qkv = _qkv_rope(h, attn_w_qkv[0].astype(BF16), positions)
    o, moe_w1 = _diff_attention(qkv, attn_lam_q1[0], attn_lam_k1[0], attn_lam_q2[0], attn_lam_k2[0],
                                attn_subln_w[0], lambda_init, batch, seq, moe_weights, 1)
    h, ht, eid, wt, cnt = _mm_ln_router(o, attn_w_o[0].astype(BF16), h, ln_mix_g[1], ln_mix_b[1],
                                        _router_weights(moe_w_group[1], moe_w_expert[1]), "mm_attn_out_ln_router")
    h = _hier_moe_ln(h, ht, eid, wt, cnt, moe_w1, ln_ffn_g[1], ln_ffn_b[1], 1)
    return h.reshape(batch, seq, d)
```

```python
import functools
import math

import jax
import jax.numpy as jnp
from jax import lax
from jax.experimental import pallas as pl
from jax.experimental.pallas import tpu as pltpu

F32 = jnp.float32
BF16 = jnp.bfloat16
I32 = jnp.int32

D_MODEL = 1024
DEPTH = 2
SSM_D_INNER = 2048
SSM_HEADDIM = 64
SSM_N_HEADS = 32
SSM_N_GROUPS = 8
SSM_HEADS_PER_GROUP = 4
SSM_D_STATE = 128
SSM_CONV_K = 4
SSM_CHUNK = 128
SSM_GROUP_WIDTH = SSM_HEADS_PER_GROUP * SSM_HEADDIM
SSM_ZX_DIM = 2 * SSM_D_INNER + 2 * SSM_N_GROUPS * SSM_D_STATE
ATTN_HEAD_DIM = 64
ATTN_N_HEADS = 8
ATTN_V_DIM = 128
ROT_DIM = 16
ROPE_THETA = 500000.0
MOE_GROUPS = 4
MOE_EXPERTS_PER_GROUP = 8
MOE_N_EXPERTS = 32
MOE_TOP_K = 2
MOE_D_FF = 512
DEEPNORM_ALPHA = (2 * DEPTH) ** 0.25
NORM_EPS = 1e-5

LANES = 128
ROW_SLABS = D_MODEL // LANES
CONV_TAIL = 8
CONV_ROWS = 128

MM_TM = 512
MM_TM_WIDE = 256
MM_TN = 1024
FFN_BLK = 512
DSP_TM = 512
SSD_GROUP_BATCH = 2
LN_SUBTILES = 4
CMB_TM = 256
ATT_TQ = 512


def _arb(n):
    return pltpu.CompilerParams(dimension_semantics=("arbitrary",) * n,
                                vmem_limit_bytes=56 * 1024 * 1024)


def _silu(x):
    hx = 0.5 * x
    return hx + hx * jnp.tanh(hx)


def _softplus(x):
    return jnp.maximum(x, 0.0) + jnp.log(1.0 + jnp.exp(-jnp.abs(x)))


def _layer_norm(y, g, b):
    mu = jnp.mean(y, axis=-1, keepdims=True)
    d = y - mu
    var = jnp.mean(d * d, axis=-1, keepdims=True)
    return d * lax.rsqrt(var + NORM_EPS) * g + b


def _split3(a):
    a1 = a.astype(BF16)
    r1 = a - a1.astype(F32)
    a2 = r1.astype(BF16)
    a3 = (r1 - a2.astype(F32)).astype(BF16)
    return a1, a2, a3


def _dot(a, b):
    return jnp.dot(a, b, preferred_element_type=F32)


def _dot_sel(a, sel):
    a1, a2, a3 = _split3(a)
    return _dot(a1, sel) + _dot(a2, sel) + _dot(a3, sel)


def _dot_f32(a, b):
    a1, a2, a3 = _split3(a)
    b1, b2, b3 = _split3(b)
    return (_dot(a1, b1) + _dot(a1, b2) + _dot(a2, b1)
            + _dot(a2, b2) + _dot(a1, b3) + _dot(a3, b1))


class _CastPlan:
    def __init__(self, w_gate, w_up, w_down, layer, steps, step_of):
        self.weights = (w_gate, w_up, w_down)
        self.n_e = w_gate.shape[1]
        self.eps = -(-self.n_e // steps)
        assert self.n_e % self.eps == 0
        self.n_cast = self.n_e // self.eps
        self.stride = steps // self.n_cast
        block = lambda *ids: jnp.minimum(step_of(*ids) // self.stride, self.n_cast - 1)
        self.in_specs = [pl.BlockSpec((1, self.eps) + w.shape[2:], lambda *ids: (layer, block(*ids), 0, 0))
                         for w in self.weights]
        self.out_specs = [pl.BlockSpec((self.eps,) + w.shape[2:], lambda *ids: (block(*ids), 0, 0))
                          for w in self.weights]
        self.out_shape = [jax.ShapeDtypeStruct(w.shape[1:], BF16) for w in self.weights]

    def emit(self, step, w_refs, o_refs, extra=True):
        @pl.when(extra & (step % self.stride == 0) & (step // self.stride < self.n_cast))
        def _():
            for w_ref, o_ref in zip(w_refs, o_refs):
                o_ref[...] = w_ref[0].astype(BF16)


def _mm_kernel(x_ref, w_ref, o_ref):
    xb = x_ref[...].astype(BF16)
    for j in range(o_ref.shape[1] // MM_TN):
        cols = slice(j * MM_TN, (j + 1) * MM_TN)
        o_ref[:, cols] = _dot(xb, w_ref[:, cols]).astype(o_ref.dtype)


def _matmul(x, w, n, out_dtype):
    m, k = x.shape
    tm = min(MM_TM_WIDE, m)
    return pl.pallas_call(
        _mm_kernel,
        grid=(m // tm,),
        in_specs=[pl.BlockSpec((tm, k), lambda i: (i, 0)), pl.BlockSpec((k, n), lambda i: (0, 0))],
        out_specs=pl.BlockSpec((tm, n), lambda i: (i, 0)),
        out_shape=jax.ShapeDtypeStruct((m, n), out_dtype),
        compiler_params=_arb(1),
        name="mm_inproj",
    )(x, w)


def _ssd_kernel(z_ref, xs_ref, bc_ref, x_ref, wdt_ref, cw_ref, cb_ref, dtb_ref, alog_ref, dskip_ref,
                nw_ref, e64_ref, e128_ref, wg_ref, wu_ref, wd_ref, o_ref, wgb_ref, wub_ref, wdb_ref,
                stage_x, stage_bc, state, acp_s, dtp_s, act_s, *, cast):
    G, W, N = SSM_N_GROUPS, SSM_GROUP_WIDTH, SSM_D_STATE
    L = z_ref.shape[0]
    c = pl.program_id(1)
    cast.emit(pl.program_id(0) * pl.num_programs(1) + c, (wg_ref, wu_ref, wd_ref), (wgb_ref, wub_ref, wdb_ref))

    @pl.when(c == 0)
    def _():
        state[...] = jnp.zeros(state.shape, F32)
        for stage in (stage_x, stage_bc):
            stage[0:CONV_TAIL, :] = jnp.zeros((CONV_TAIL, stage.shape[1]), F32)

    @pl.when(c != 0)
    def _():
        for stage in (stage_x, stage_bc):
            stage[0:CONV_TAIL, :] = stage[L:L + CONV_TAIL, :]

    stage_x[CONV_TAIL:CONV_TAIL + L, :] = xs_ref[...]
    stage_bc[CONV_TAIL:CONV_TAIL + L, :] = bc_ref[...]

    def conv(stage, c0, width, w0):
        taps = [cw_ref[k:k + 1, w0:w0 + width] for k in range(SSM_CONV_K)]
        bias = cb_ref[:, w0:w0 + width]
        outs = []
        for rb in range(L // CONV_ROWS):
            r0 = rb * CONV_ROWS
            u = stage[r0:r0 + CONV_ROWS + CONV_TAIL, c0:c0 + width]
            out = bias + taps[SSM_CONV_K - 1] * u[CONV_TAIL:, :]
            for k in range(SSM_CONV_K - 1):
                back = SSM_CONV_K - 1 - k
                out = out + taps[k] * pltpu.roll(u, back, 0)[CONV_TAIL:, :]
            outs.append(_silu(out))
        return jnp.concatenate(outs, axis=0)

    xc, bcv, ccv = {}, {}, {}

    def conv_groups(gs):
        for g in gs:
            xc[g] = conv(stage_x, g * W, W, g * W)
            bcv[g] = conv(stage_bc, g * N, N, SSM_D_INNER + g * N).astype(BF16)
            ccv[g] = conv(stage_bc, G * N + g * N, N, SSM_D_INNER + G * N + g * N).astype(BF16)

    x_of, b_of, c_of = xc.__getitem__, bcv.__getitem__, ccv.__getitem__

    dt = _softplus(_dot(x_ref[...].astype(BF16), wdt_ref[...].astype(BF16)) + dtb_ref[...])
    a = -jnp.exp(alog_ref[...])
    row_i = lax.broadcasted_iota(I32, (L, L), 0)
    col_i = lax.broadcasted_iota(I32, (L, L), 1)
    causal = row_i >= col_i
    a1, a2, a3 = _split3(dt * a)
    tri = causal.astype(BF16)
    ac = (_dot(tri, a1) + _dot(tri, a2) + _dot(tri, a3)) * math.log2(math.e)
    act_s[...] = ac.T
    for n, part in enumerate(_split3(ac)):
        acp_s[n] = part
    for n, part in enumerate(_split3(dt)[:2]):
        dtp_s[n] = part
    lane = lax.broadcasted_iota(I32, (L, LANES), 1)
    first_half = lane < SSM_HEADDIM

    def select(parts_ref, n_parts, sel):
        out = _dot(parts_ref[0], sel)
        for n in range(1, n_parts):
            out = out + _dot(parts_ref[n], sel)
        return out

    def run_groups(gs):
        conv_groups(gs)
        col4 = {g: select(acp_s, 3, e128_ref[g]) for g in gs}
        dt_e = {g: select(dtp_s, 2, e64_ref[g]) for g in gs}
        cb = {g: lax.dot_general(c_of(g), b_of(g), (((1,), (1,)), ((), ())), preferred_element_type=F32)
              for g in gs}
        y_cs = {g: _dot(c_of(g), state[g].astype(BF16)) for g in gs}
        a_e = {g: jnp.concatenate(
            [jnp.where(first_half, col4[g][:, 2 * p * LANES:(2 * p + 1) * LANES],
                       col4[g][:, (2 * p + 1) * LANES:(2 * p + 2) * LANES]) for p in range(2)], axis=1)
            for g in gs}
        xdt = {g: x_of(g) * dt_e[g] for g in gs}
        xdt_b = {g: xdt[g].astype(BF16) for g in gs}
        ys = {g: [None] * SSM_HEADS_PER_GROUP for g in gs}
        for r in range(SSM_HEADS_PER_GROUP):
            for g in gs:
                row = act_s[SSM_HEADS_PER_GROUP * g + r:SSM_HEADS_PER_GROUP * g + r + 1, :]
                seg = col4[g][:, LANES * r:LANES * (r + 1)] - row
                dec = jnp.where(causal, jnp.exp2(seg), 0.0)
                xp = xdt_b[g][:, LANES * (r // 2):LANES * (r // 2 + 1)]
                ys[g][r] = _dot((cb[g] * dec).astype(BF16), xp)
        for g in gs:
            y_diag = jnp.concatenate([jnp.where(first_half, ys[g][2 * p], ys[g][2 * p + 1])
                                      for p in range(2)], axis=1)
            al_e = a_e[g][L - 1:L, :]
            new = lax.dot_general(b_of(g), (xdt[g] * jnp.exp2(al_e - a_e[g])).astype(BF16),
                                  (((0,), (0,)), ((), ())), preferred_element_type=F32)
            y = y_diag + y_cs[g] * jnp.exp2(a_e[g]) + x_of(g) * dskip_ref[g]
            state[g] = state[g] * jnp.exp2(al_e) + new
            yg = y * _silu(z_ref[:, g * W:(g + 1) * W])
            ms = jnp.mean(yg * yg, axis=-1, keepdims=True)
            o_ref[:, g * W:(g + 1) * W] = (yg * lax.rsqrt(ms + NORM_EPS)
                                           * nw_ref[:, g * W:(g + 1) * W]).astype(o_ref.dtype)

    for g0 in range(0, G, SSD_GROUP_BATCH):
        run_groups(range(g0, g0 + SSD_GROUP_BATCH))


def _head_selectors():
    G, R = SSM_N_GROUPS, SSM_HEADS_PER_GROUP
    h = jnp.arange(LANES)[None, :, None]
    g = jnp.arange(G)[:, None, None]
    j64 = jnp.arange(SSM_GROUP_WIDTH)[None, None, :]
    j128 = jnp.arange(R * LANES)[None, None, :]
    e64 = (h == g * R + j64 // SSM_HEADDIM).astype(BF16)
    e128 = (h == g * R + j128 // LANES).astype(BF16)
    return e64, e128


def _ssd(zx, x2, w_dt, conv_w, conv_b, dt_bias, a_log, d_skip, norm_w, batch, seq, moe_weights, moe_layer):
    L, G, W, N = SSM_CHUNK, SSM_N_GROUPS, SSM_GROUP_WIDTH, SSM_D_STATE
    nc = seq // L
    cast = _CastPlan(*moe_weights, moe_layer, batch * nc, lambda b, c: b * nc + c)
    t = batch * seq
    pad = LANES - SSM_N_HEADS
    wdt = jnp.pad(w_dt, ((0, 0), (0, pad)))
    dtb = jnp.pad(dt_bias, (0, pad))[None, :]
    alog = jnp.pad(a_log, (0, pad))[None, :]
    dsk = jnp.repeat(d_skip, SSM_HEADDIM).reshape(G, 1, W)
    e64, e128 = _head_selectors()
    conv_dim = conv_w.shape[1]
    row = lambda b, c: (b * nc + c, 0)
    full2 = lambda b, c: (0, 0)
    full3 = lambda b, c: (0, 0, 0)
    yg, *w_bf16 = pl.pallas_call(
        functools.partial(_ssd_kernel, cast=cast),
        grid=(batch, nc),
        in_specs=[pl.BlockSpec((L, SSM_D_INNER), row),
                  pl.BlockSpec((L, SSM_D_INNER), lambda b, c: (b * nc + c, 1)),
                  pl.BlockSpec((L, 2 * G * N), lambda b, c: (b * nc + c, 2)),
                  pl.BlockSpec((L, D_MODEL), row),
                  pl.BlockSpec((D_MODEL, LANES), full2),
                  pl.BlockSpec((SSM_CONV_K, conv_dim), full2),
                  pl.BlockSpec((1, conv_dim), full2),
                  pl.BlockSpec((1, LANES), full2),
                  pl.BlockSpec((1, LANES), full2),
                  pl.BlockSpec((G, 1, W), full3),
                  pl.BlockSpec((1, SSM_D_INNER), full2),
                  pl.BlockSpec((G, LANES, W), full3),
                  pl.BlockSpec((G, LANES, SSM_HEADS_PER_GROUP * LANES), full3)] + cast.in_specs,
        out_specs=[pl.BlockSpec((L, SSM_D_INNER), row)] + cast.out_specs,
        out_shape=[jax.ShapeDtypeStruct((t, SSM_D_INNER), BF16)] + cast.out_shape,
        scratch_shapes=[pltpu.VMEM((CONV_TAIL + L, SSM_D_INNER), F32),
                        pltpu.VMEM((CONV_TAIL + L, 2 * G * N), F32),
                        pltpu.VMEM((G, N, W), F32),
                        pltpu.VMEM((3, L, LANES), BF16),
                        pltpu.VMEM((2, L, LANES), BF16),
                        pltpu.VMEM((LANES, L), F32)],
        compiler_params=_arb(2),
        name="ssd_scan",
    )(zx, zx, zx, x2, wdt, conv_w, conv_b[None, :], dtb, alog, dsk, norm_w[None, :], e64, e128, *cast.weights)
    return yg, tuple(w_bf16)


def _route(h, wr):
    tm = h.shape[0]
    logits = _dot(h.astype(BF16), wr)
    lane_i = lax.broadcasted_iota(I32, (tm, LANES), 1)
    lane = lane_i.astype(F32)
    neg = -jnp.inf
    big = float(LANES)

    def first_argmax(v, vmax):
        return jnp.min(jnp.where(v == vmax, lane, big), axis=-1, keepdims=True)

    gl = jnp.where((lane_i >= MOE_N_EXPERTS) & (lane_i < MOE_N_EXPERTS + MOE_GROUPS), logits, neg)
    gm = jnp.max(gl, axis=-1, keepdims=True)
    g_sel = first_argmax(gl, gm) - float(MOE_N_EXPERTS)
    g_gate = 1.0 / jnp.sum(jnp.exp(gl - gm), axis=-1, keepdims=True)
    lo = g_sel * float(MOE_EXPERTS_PER_GROUP)
    el = jnp.where((lane >= lo) & (lane < lo + float(MOE_EXPERTS_PER_GROUP)), logits, neg)
    m1 = jnp.max(el, axis=-1, keepdims=True)
    i1 = first_argmax(el, m1)
    el2 = jnp.where(lane == i1, neg, el)
    m2 = jnp.max(el2, axis=-1, keepdims=True)
    i2 = first_argmax(el2, m2)
    p2 = jnp.exp(m2 - m1)
    t1 = 1.0 / (1.0 + p2)
    t2 = p2 / (1.0 + p2)
    eid = jnp.where(lane_i == 0, i1, jnp.where(lane_i == 1, i2, 0.0)).astype(I32)
    wt = jnp.where(lane_i == 0, g_gate * t1, jnp.where(lane_i == 1, g_gate * t2, 0.0))
    count = jnp.sum(jnp.where(lane == i1, 1.0, 0.0) + jnp.where(lane == i2, 1.0, 0.0), axis=0, keepdims=True)
    return eid, wt, count


def _router_weights(w_group, w_expert):
    pad = LANES - MOE_N_EXPERTS - MOE_GROUPS
    return jnp.pad(jnp.concatenate([w_expert, w_group], axis=1), ((0, 0), (0, pad))).astype(BF16)


def _store_token_tiles(ref, v, first_token=0):
    rows = v.shape[0]
    for s in range(ROW_SLABS):
        ref[pl.ds(first_token * ROW_SLABS + s, rows, stride=ROW_SLABS), :] = v[:, s * LANES:(s + 1) * LANES]


def _load_token_tiles(ref, rows):
    return jnp.concatenate([ref[pl.ds(s, rows, stride=ROW_SLABS), :] for s in range(ROW_SLABS)], axis=1)


def _mm_ln_router_kernel(x_ref, w_ref, r_ref, g_ref, b_ref, wr_ref, h_ref, ht_ref, eid_ref, wt_ref, cnt_ref):
    @pl.when(pl.program_id(0) == 0)
    def _():
        cnt_ref[...] = jnp.zeros(cnt_ref.shape, F32)

    sub = x_ref.shape[0] // LN_SUBTILES
    rows = [slice(u * sub, (u + 1) * sub) for u in range(LN_SUBTILES)]
    ys = [DEEPNORM_ALPHA * r_ref[rs, :] + _dot(x_ref[rs, :].astype(BF16), w_ref[...]) for rs in rows]
    hs = [_layer_norm(y, g_ref[...], b_ref[...]) for y in ys]
    routes = [_route(h, wr_ref[...]) for h in hs]
    for u, rs in enumerate(rows):
        h_ref[rs, :] = hs[u]
        _store_token_tiles(ht_ref, hs[u], first_token=u * sub)
        eid_ref[rs, :] = routes[u][0]
        wt_ref[rs, :] = routes[u][1]
    cnt_ref[...] += jnp.broadcast_to(sum(r[2] for r in routes), cnt_ref.shape)


def _mm_ln_router(x, w, resid, g, b, wr, name):
    m, k = x.shape
    d = w.shape[1]
    tm = min(MM_TM, m)
    row = lambda i: (i, 0)
    full = lambda i: (0, 0)
    return pl.pallas_call(
        _mm_ln_router_kernel,
        grid=(m // tm,),
        in_specs=[pl.BlockSpec((tm, k), row), pl.BlockSpec((k, d), full), pl.BlockSpec((tm, d), row),
                  pl.BlockSpec((1, d), full), pl.BlockSpec((1, d), full), pl.BlockSpec((d, LANES), full)],
        out_specs=[pl.BlockSpec((tm, d), row), pl.BlockSpec((tm * ROW_SLABS, LANES), row),
                   pl.BlockSpec((tm, LANES), row), pl.BlockSpec((tm, LANES), row),
                   pl.BlockSpec((8, LANES), full)],
        out_shape=[jax.ShapeDtypeStruct((m, d), F32), jax.ShapeDtypeStruct((m * ROW_SLABS, LANES), F32),
                   jax.ShapeDtypeStruct((m, LANES), I32), jax.ShapeDtypeStruct((m, LANES), F32),
                   jax.ShapeDtypeStruct((8, LANES), F32)],
        compiler_params=_arb(1),
        name=name,
    )(x, w, resid, g[None, :], b[None, :], wr)


def _dispatch_kernel(eid_ref, cnt_ref, h_ref, dest_ref, be_ref, nu_ref, xs_hbm,
                     base, upper, zbuf, dst_v, dst_s, pe_v, pe_s, sem_z, sem_r, sem_s, *, blk):
    i = pl.program_id(0)
    tm = eid_ref.shape[0]
    n_e = LANES
    eid_t = eid_ref[...].astype(F32).T
    sub = lax.broadcasted_iota(I32, (n_e, tm), 0).astype(F32)
    oh = [(sub == eid_t[k:k + 1, :]).astype(F32) for k in range(MOE_TOP_K)]
    tot = [jnp.sum(o, axis=1, keepdims=True) for o in oh]

    @pl.when(i == 0)
    def _():
        r_i = lax.broadcasted_iota(I32, (tm, tm), 0)
        c_i = lax.broadcasted_iota(I32, (tm, tm), 1)
        upper[...] = (r_i < c_i).astype(BF16)
        counts = jnp.broadcast_to(cnt_ref[0:1, :], (n_e, n_e)).T
        padded = jnp.floor((counts + float(blk - 1)) * (1.0 / blk)) * float(blk)
        r_i = lax.broadcasted_iota(I32, (n_e, n_e), 0)
        c_i = lax.broadcasted_iota(I32, (n_e, n_e), 1)
        tril = (r_i >= c_i).astype(BF16)
        p1, p2, p3 = _split3(padded)
        pends = _dot(tril, p1) + _dot(tril, p2) + _dot(tril, p3)
        base[...] = pends - padded
        nbp = be_ref.shape[1]
        blk_start = lax.broadcasted_iota(I32, (n_e, nbp), 1).astype(F32) * float(blk)
        is_e = lax.broadcasted_iota(I32, (n_e, nbp), 0) < MOE_N_EXPERTS
        done = jnp.where(is_e & (jnp.tile(pends, (1, nbp // LANES)) <= blk_start), 1.0, 0.0)
        be = jnp.minimum(jnp.sum(done, axis=0, keepdims=True), float(MOE_N_EXPERTS - 1))
        be_ref[...] = jnp.broadcast_to(be, be_ref.shape).astype(I32)
        last = pends[MOE_N_EXPERTS - 1:MOE_N_EXPERTS, :] * (1.0 / blk)
        nu_ref[...] = jnp.broadcast_to(last, nu_ref.shape).astype(I32)
        zbuf[...] = jnp.zeros(zbuf.shape, F32)
        row8 = lax.broadcasted_iota(I32, (8, LANES), 0)
        pe_v[...] = jnp.where(row8 == 0, pends.T[0:8, :], counts.T[0:8, :]).astype(I32)
        cp = pltpu.make_async_copy(pe_v, pe_s, sem_s)
        cp.start()
        cp.wait()

        brows = blk * ROW_SLABS

        def zero_copy(e):
            start = pl.multiple_of((pe_s[0, e] - blk) * ROW_SLABS, brows)
            return pltpu.make_async_copy(zbuf, xs_hbm.at[pl.ds(start, brows)], sem_z)

        def tail_copy(b):
            return pltpu.make_async_copy(zbuf, xs_hbm.at[pl.ds(pl.multiple_of(b * brows, brows), brows)], sem_z)

        n_used = lax.shift_right_logical(pe_s[0, MOE_N_EXPERTS - 1], blk.bit_length() - 1)
        n_blocks = xs_hbm.shape[0] // brows
        for e in range(MOE_N_EXPERTS):
            @pl.when(pe_s[1, e] > 0)
            def _():
                zero_copy(e).start()
        lax.fori_loop(n_used, n_blocks, lambda b, c: (tail_copy(b).start(), c)[1], 0)
        for e in range(MOE_N_EXPERTS):
            @pl.when(pe_s[1, e] > 0)
            def _():
                zero_copy(e).wait()
        lax.fori_loop(n_used, n_blocks, lambda b, c: (tail_copy(b).wait(), c)[1], 0)

    b0 = base[:, 0:1]
    c0 = _dot(oh[0].astype(BF16), upper[...])
    c1 = _dot(oh[1].astype(BF16), upper[...])
    d0 = jnp.sum(oh[0] * (b0 + c0), axis=0, keepdims=True)
    d1 = jnp.sum(oh[1] * (b0 + tot[0] + c1), axis=0, keepdims=True)
    base[...] += jnp.broadcast_to(tot[0] + tot[1], base.shape)
    row8 = lax.broadcasted_iota(I32, (8, tm), 0)
    dst = jnp.where(row8 == 0, d0, jnp.where(row8 == 1, d1, 0.0)).astype(I32)
    dest_ref[...] = dst
    dst_v[...] = dst
    cp = pltpu.make_async_copy(dst_v, dst_s, sem_s)
    cp.start()
    cp.wait()

    for r in range(tm):
        for k in range(MOE_TOP_K):
            slot = pl.multiple_of(dst_s[k, r] * ROW_SLABS, ROW_SLABS)
            pltpu.make_async_copy(h_ref.at[pl.ds(r * ROW_SLABS, ROW_SLABS)],
                                  xs_hbm.at[pl.ds(slot, ROW_SLABS)], sem_r).start(priority=k)
    for k in range(MOE_TOP_K):
        pltpu.make_async_copy(h_ref, xs_hbm.at[pl.ds(0, tm * ROW_SLABS)], sem_r).wait()


def _moe_dispatch(ht, eid, counts, blk, name):
    t = ht.shape[0] // ROW_SLABS
    assert blk & (blk - 1) == 0, "block size must be a power of two"
    tm = min(DSP_TM, t)
    nt = t // tm
    n_blocks = -(-(t * MOE_TOP_K) // blk) + MOE_N_EXPERTS
    nbp = -(-n_blocks // LANES) * LANES
    full = lambda i: (0, 0)
    dest, be, nu, xs = pl.pallas_call(
        functools.partial(_dispatch_kernel, blk=blk),
        grid=(nt,),
        in_specs=[pl.BlockSpec((tm, LANES), lambda i: (i, 0)),
                  pl.BlockSpec((8, LANES), full),
                  pl.BlockSpec((tm * ROW_SLABS, LANES), lambda i: (i, 0))],
        out_specs=[pl.BlockSpec((8, tm), lambda i: (0, i)),
                   pl.BlockSpec((8, nbp), full),
                   pl.BlockSpec((8, LANES), full),
                   pl.BlockSpec(memory_space=pl.ANY)],
        out_shape=[jax.ShapeDtypeStruct((8, t), I32), jax.ShapeDtypeStruct((8, nbp), I32),
                   jax.ShapeDtypeStruct((8, LANES), I32),
                   jax.ShapeDtypeStruct((n_blocks * blk * ROW_SLABS, LANES), F32)],
        scratch_shapes=[pltpu.VMEM((LANES, LANES), F32),
                        pltpu.VMEM((tm, tm), BF16), pltpu.VMEM((blk * ROW_SLABS, LANES), F32),
                        pltpu.VMEM((8, tm), I32), pltpu.SMEM((8, tm), I32),
                        pltpu.VMEM((8, LANES), I32), pltpu.SMEM((8, LANES), I32),
                        pltpu.SemaphoreType.DMA(()), pltpu.SemaphoreType.DMA(()), pltpu.SemaphoreType.DMA(())],
        compiler_params=_arb(1),
        name=name,
    )(eid, counts, ht)
    return dest, be[0, :n_blocks], nu[0, :1], xs, n_blocks


def _ffn_kernel(be_ref, nu_ref, x_ref, wg_ref, wu_ref, wd_ref, o_ref):
    i = pl.program_id(0)

    @pl.when(i < nu_ref[0])
    def _():
        blk = x_ref.shape[0] // ROW_SLABS
        xb = _load_token_tiles(x_ref, blk).astype(BF16)
        hid = _silu(_dot(xb, wg_ref[0])) * _dot(xb, wu_ref[0])
        _store_token_tiles(o_ref, _dot(hid.astype(BF16), wd_ref[0]))

    @pl.when(i >= nu_ref[0])
    def _():
        o_ref[...] = jnp.zeros(o_ref.shape, o_ref.dtype)


def _moe_ffn(xs, block_expert, n_used, n_blocks, weights, blk, name):
    w_gate, w_up, w_down = weights
    d, f = w_gate.shape[1], w_gate.shape[2]
    brows = blk * ROW_SLABS
    used = lambda i, be, nu: (jnp.minimum(i, nu[0] - 1), 0)
    every = lambda i, be, nu: (i, 0)
    expert = lambda i, be, nu: (be[i], 0, 0)
    grid_spec = pltpu.PrefetchScalarGridSpec(
        num_scalar_prefetch=2,
        grid=(n_blocks,),
        in_specs=[pl.BlockSpec((brows, LANES), used),
                  pl.BlockSpec((1, d, f), expert), pl.BlockSpec((1, d, f), expert),
                  pl.BlockSpec((1, f, d), expert)],
        out_specs=pl.BlockSpec((brows, LANES), every))
    return pl.pallas_call(
        _ffn_kernel,
        grid_spec=grid_spec,
        out_shape=jax.ShapeDtypeStruct((n_blocks * brows, LANES), F32),
        compiler_params=_arb(1),
        name=name,
    )(block_expert, n_used, xs, w_gate, w_up, w_down)


def _combine_ln_kernel(dst_ref, dstn_ref, h_ref, wt_ref, g_ref, b_ref, yb_hbm, o_ref, ybuf, sem):
    i = pl.program_id(0)
    n = pl.num_programs(0)
    tm = h_ref.shape[0]

    def start_gather(dref, slot):
        for r in range(tm):
            for k in range(MOE_TOP_K):
                src = pl.multiple_of(dref[k, r] * ROW_SLABS, ROW_SLABS)
                pltpu.make_async_copy(yb_hbm.at[pl.ds(src, ROW_SLABS)],
                                      ybuf.at[slot, k, pl.ds(r * ROW_SLABS, ROW_SLABS)],
                                      sem.at[slot]).start(priority=k)

    @pl.when(i == 0)
    def _():
        start_gather(dst_ref, 0)

    @pl.when(i + 1 < n)
    def _():
        start_gather(dstn_ref, (i + 1) % 2)

    slot = i % 2
    for k in range(MOE_TOP_K):
        pltpu.make_async_copy(yb_hbm.at[pl.ds(0, tm * ROW_SLABS)], ybuf.at[slot, k], sem.at[slot]).wait()
    wt = wt_ref[...]
    ffn = (wt[:, 0:1] * _load_token_tiles(ybuf.at[slot, 0], tm)
           + wt[:, 1:2] * _load_token_tiles(ybuf.at[slot, 1], tm))
    o_ref[...] = _layer_norm(DEEPNORM_ALPHA * h_ref[...] + ffn, g_ref[...], b_ref[...])


def _combine_ln(h, wt, dest, yb, g, b, name):
    t, d = h.shape
    tm = min(CMB_TM, t)
    nblk = t // tm
    row = lambda i: (i, 0)
    full = lambda i: (0, 0)
    return pl.pallas_call(
        _combine_ln_kernel,
        grid=(nblk,),
        in_specs=[pl.BlockSpec((8, tm), lambda i: (0, i), memory_space=pltpu.SMEM),
                  pl.BlockSpec((8, tm), lambda i: (0, jnp.minimum(i + 1, nblk - 1)),
                               memory_space=pltpu.SMEM),
                  pl.BlockSpec((tm, d), row), pl.BlockSpec((tm, LANES), row),
                  pl.BlockSpec((1, d), full), pl.BlockSpec((1, d), full),
                  pl.BlockSpec(memory_space=pl.ANY)],
        out_specs=pl.BlockSpec((tm, d), row),
        out_shape=jax.ShapeDtypeStruct((t, d), F32),
        scratch_shapes=[pltpu.VMEM((2, MOE_TOP_K, tm * ROW_SLABS, LANES), F32), pltpu.SemaphoreType.DMA((2,))],
        compiler_params=_arb(1),
        name=name,
    )(dest, dest, h, wt, g[None, :], b[None, :], yb)


def _hier_moe_ln(h, ht, eid, wt, counts, w_bf16, g, b, layer):
    dest, block_expert, n_used, xs, n_blocks = _moe_dispatch(ht, eid, counts, FFN_BLK, f"moe_dispatch{layer}")
    yb = _moe_ffn(xs, block_expert, n_used, n_blocks, w_bf16, FFN_BLK, f"moe_ffn{layer}")
    return _combine_ln(h, wt, dest, yb, g, b, f"moe_combine_ln{layer}")


def _qkv_rope_kernel(x_ref, w_ref, pos_ref, inv_ref, o_ref):
    xb = x_ref[...].astype(BF16)
    tm = xb.shape[0]
    n = D_MODEL
    ang = pos_ref[...].astype(F32) * inv_ref[...]
    lane = lax.broadcasted_iota(I32, (tm, LANES), 1)
    dd = lane & (ATTN_HEAD_DIM - 1)
    half = ROT_DIM // 2
    cosv = jnp.cos(ang)
    sinv = jnp.sin(ang)
    c_t = jnp.where(dd < ROT_DIM, cosv, 1.0)
    s_up = jnp.where(dd < half, -sinv, 0.0)
    s_dn = jnp.where((dd >= half) & (dd < ROT_DIM), sinv, 0.0)
    for j, sc in ((0, ATTN_HEAD_DIM ** -0.5 * math.log2(math.e)), (1, 1.0)):
        acc = _dot(xb, w_ref[:, j * n:(j + 1) * n])
        c_j, up_j, dn_j = c_t * sc, s_up * sc, s_dn * sc
        for blk in range(n // LANES):
            tt = acc[:, blk * LANES:(blk + 1) * LANES]
            out = tt * c_j + pltpu.roll(tt, LANES - half, 1) * up_j + pltpu.roll(tt, half, 1) * dn_j
            o_ref[:, j * n + blk * LANES:j * n + (blk + 1) * LANES] = out.astype(o_ref.dtype)
    o_ref[:, 2 * n:3 * n] = _dot(xb, w_ref[:, 2 * n:3 * n]).astype(o_ref.dtype)


def _rope_inv_table():
    inv = ROPE_THETA ** (-jnp.arange(0, ROT_DIM, 2, dtype=F32) / ROT_DIM)
    head = jnp.concatenate([inv, inv, jnp.zeros((ATTN_HEAD_DIM - ROT_DIM,), F32)])
    return jnp.tile(head, LANES // ATTN_HEAD_DIM)[None, :]


def _qkv_rope(h, w_qkv, positions):
    m, k = h.shape
    n = w_qkv.shape[1]
    tm = min(MM_TM, m)
    pos = positions.reshape(m, 1)
    return pl.pallas_call(
        _qkv_rope_kernel,
        grid=(m // tm,),
        in_specs=[pl.BlockSpec((tm, k), lambda i: (i, 0)),
                  pl.BlockSpec((k, n), lambda i: (0, 0)),
                  pl.BlockSpec((tm, 1), lambda i: (i, 0)),
                  pl.BlockSpec((1, LANES), lambda i: (0, 0))],
        out_specs=pl.BlockSpec((tm, n), lambda i: (i, 0)),
        out_shape=jax.ShapeDtypeStruct((m, n), BF16),
        compiler_params=_arb(1),
        name="mm_qkv_rope",
    )(h, w_qkv, pos, _rope_inv_table())


def _attn_kernel(q_ref, k_ref, v_ref, lq1_ref, lk1_ref, lq2_ref, lk2_ref, sw_ref, wg_ref, wu_ref, wd_ref,
                 o_ref, wgb_ref, wub_ref, wdb_ref,
                 vx_ref, m0_ref, m1_ref, acc0_ref, acc1_ref, *, lambda_init, cast):
    m_refs = (m0_ref, m1_ref)
    acc_refs = (acc0_ref, acc1_ref)
    jp = pl.program_id(2)
    tq = q_ref.shape[0] // 2
    cast.emit(pl.program_id(0) * pl.num_programs(1) + pl.program_id(1),
              (wg_ref, wu_ref, wd_ref), (wgb_ref, wub_ref, wdb_ref), extra=jp == 0)

    @pl.when(jp == 0)
    def _():
        vx_ref[:, 0:LANES] = v_ref[...]
        vx_ref[:, LANES:] = jnp.ones((vx_ref.shape[0], LANES), vx_ref.dtype)

    q = q_ref[...]
    lane = lax.broadcasted_iota(I32, q.shape, 1)
    zero = jnp.zeros((), q.dtype)
    qs = (jnp.where(lane < ATTN_HEAD_DIM, q, zero), jnp.where(lane >= ATTN_HEAD_DIM, q, zero))
    for c in range(2):
        m_refs[c][...] = jnp.full(m_refs[c].shape, -jnp.inf, F32)
        acc_refs[c][...] = jnp.zeros(acc_refs[c].shape, F32)

    def step(off, width, tiles, diag_col=None):
        off = pl.multiple_of(off, tq)
        kb = k_ref[pl.ds(off, width), :]
        vb = vx_ref[pl.ds(off, width), :]
        chains = [(slice(r0, r0 + tq), c) for r0 in tiles for c in range(2)]
        ss = [lax.dot_general(qs[c][rows, :], kb, (((1,), (1,)), ((), ())), preferred_element_type=F32)
              for rows, c in chains]
        if diag_col is not None:
            row_i = lax.broadcasted_iota(I32, (tq, width), 0)
            col_i = lax.broadcasted_iota(I32, (tq, width), 1)
            ss = [jnp.where(row_i + diag_col >= col_i, s, -jnp.inf) for s in ss]
        m_prev = [m_refs[c][rows, :] for rows, c in chains]
        mn = [jnp.maximum(mp, jnp.max(s, axis=-1, keepdims=True)) for mp, s in zip(m_prev, ss)]
        ps = [jnp.exp2(s - jnp.tile(m, (1, width // LANES))).astype(BF16) for s, m in zip(ss, mn)]
        for n, (rows, c) in enumerate(chains):
            alpha = jnp.exp2(m_prev[n] - mn[n])
            acc_refs[c][rows, :] = jnp.tile(alpha, (1, 2)) * acc_refs[c][rows, :] + _dot(ps[n], vb)
            m_refs[c][rows, :] = mn[n]

    lax.fori_loop(0, jp, lambda j, carry: (step(j * (2 * tq), 2 * tq, (0, tq)), carry)[1], 0)
    step(jp * (2 * tq), tq, (0,), diag_col=0)
    step(jp * (2 * tq), 2 * tq, (tq,), diag_col=tq)

    lam = (jnp.exp(jnp.sum(lq1_ref[...] * lk1_ref[...], axis=-1, keepdims=True))
           - jnp.exp(jnp.sum(lq2_ref[...] * lk2_ref[...], axis=-1, keepdims=True)) + lambda_init)
    a1 = acc0_ref[...]
    a2 = acc1_ref[...]
    o = a1[:, :LANES] / a1[:, LANES:] - lam * (a2[:, :LANES] / a2[:, LANES:])
    o = o * lax.rsqrt(jnp.mean(o * o, axis=-1, keepdims=True) + NORM_EPS)
    o_ref[...] = (o * sw_ref[...] * (1.0 - lambda_init)).astype(o_ref.dtype)


def _diff_attention(qkv, lq1, lk1, lq2, lk2, subln_w, lambda_init, batch, seq, moe_weights, moe_layer):
    t = batch * seq
    tq = 2 * min(ATT_TQ, seq // 2)
    nq = seq // tq
    h_n = ATTN_N_HEADS
    vec = lambda b, h, i: (0, 0)
    cast = _CastPlan(*moe_weights, moe_layer, batch * h_n, lambda b, h, i: b * h_n + h)
    o, *w_bf16 = pl.pallas_call(
        functools.partial(_attn_kernel, lambda_init=lambda_init, cast=cast),
        grid=(batch, h_n, nq),
        in_specs=[pl.BlockSpec((tq, LANES), lambda b, h, i: (b * nq + i, h)),
                  pl.BlockSpec((seq, LANES), lambda b, h, i: (b, h_n + h)),
                  pl.BlockSpec((seq, LANES), lambda b, h, i: (b, 2 * h_n + h)),
                  pl.BlockSpec((1, ATTN_HEAD_DIM), vec), pl.BlockSpec((1, ATTN_HEAD_DIM), vec),
                  pl.BlockSpec((1, ATTN_HEAD_DIM), vec), pl.BlockSpec((1, ATTN_HEAD_DIM), vec),
                  pl.BlockSpec((1, ATTN_V_DIM), vec)] + cast.in_specs,
        out_specs=[pl.BlockSpec((tq, LANES), lambda b, h, i: (b * nq + i, h))] + cast.out_specs,
        out_shape=[jax.ShapeDtypeStruct((t, h_n * ATTN_V_DIM), BF16)] + cast.out_shape,
        scratch_shapes=[pltpu.VMEM((seq, 2 * LANES), BF16),
                        pltpu.VMEM((tq, LANES), F32), pltpu.VMEM((tq, LANES), F32),
                        pltpu.VMEM((tq, 2 * LANES), F32), pltpu.VMEM((tq, 2 * LANES), F32)],
        compiler_params=_arb(3),
        name="diff_attn",
    )(qkv, qkv, qkv, lq1[None, :], lk1[None, :], lq2[None, :], lk2[None, :], subln_w[None, :], *cast.weights)
    return o, tuple(w_bf16)


def kernel(x, positions, ln_mix_g, ln_mix_b, ln_ffn_g, ln_ffn_b, ssm_w_in, ssm_conv_w, ssm_conv_b, ssm_dt_bias, ssm_a_log, ssm_d, ssm_norm_w, ssm_w_out, attn_w_qkv, attn_lam_q1, attn_lam_k1, attn_lam_q2, attn_lam_k2, attn_subln_w, attn_w_o, moe_w_group, moe_w_expert, moe_w_gate, moe_w_up, moe_w_down):
    batch, seq, d = x.shape
    t = batch * seq
    h = x.reshape(t, d)

    w_in = ssm_w_in[0].astype(BF16)

    moe_weights = (moe_w_gate, moe_w_up, moe_w_down)
    zx = _matmul(h, w_in, SSM_ZX_DIM, F32)
    yg, moe_w0 = _ssd(zx, h, w_in[:, SSM_ZX_DIM:], ssm_conv_w[0], ssm_conv_b[0], ssm_dt_bias[0],
                      ssm_a_log[0], ssm_d[0], ssm_norm_w[0], batch, seq, moe_weights, 0)
    h, ht, eid, wt, cnt = _mm_ln_router(yg, ssm_w_out[0].astype(BF16), h, ln_mix_g[0], ln_mix_b[0],
                                        _router_weights(moe_w_group[0], moe_w_expert[0]), "mm_ssm_out_ln_router")
    h = _hier_moe_ln(h, ht, eid, wt, cnt, moe_w0, ln_ffn_g[0], ln_ffn_b[0], 0)

    lambda_init = 0.8 - 0.6 * math.exp(-0.3 * 1)
    qkv = _qkv_rope(h, attn_w_qkv[0].astype(BF16), positions)
    o, moe_w1 = _diff_attention(qkv, attn_lam_q1[0], attn_lam_k1[0], attn_lam_q2[0], attn_lam_k2[0],
                                attn_subln_w[0], lambda_init, batch, seq, moe_weights, 1)
    h, ht, eid, wt, cnt = _mm_ln_router(o, attn_w_o[0].astype(BF16), h, ln_mix_g[1], ln_mix_b[1],
                                        _router_weights(moe_w_group[1], moe_w_expert[1]), "mm_attn_out_ln_router")
    h = _hier_moe_ln(h, ht, eid, wt, cnt, moe_w1, ln_ffn_g[1], ln_ffn_b[1], 1)
    return h.reshape(batch, seq, d)
```

```python
import functools
import math

import jax
import jax.numpy as jnp
from jax import lax
from jax.experimental import pallas as pl
from jax.experimental.pallas import tpu as pltpu

F32 = jnp.float32
BF16 = jnp.bfloat16
I32 = jnp.int32

D_MODEL = 1024
DEPTH = 2
SSM_D_INNER = 2048
SSM_HEADDIM = 64
SSM_N_HEADS = 32
SSM_N_GROUPS = 8
SSM_HEADS_PER_GROUP = 4
SSM_D_STATE = 128
SSM_CONV_K = 4
SSM_CHUNK = 128
SSM_GROUP_WIDTH = SSM_HEADS_PER_GROUP * SSM_HEADDIM
SSM_ZX_DIM = 2 * SSM_D_INNER + 2 * SSM_N_GROUPS * SSM_D_STATE
ATTN_HEAD_DIM = 64
ATTN_N_HEADS = 8
ATTN_V_DIM = 128
ROT_DIM = 16
ROPE_THETA = 500000.0
MOE_GROUPS = 4
MOE_EXPERTS_PER_GROUP = 8
MOE_N_EXPERTS = 32
MOE_TOP_K = 2
MOE_D_FF = 512
DEEPNORM_ALPHA = (2 * DEPTH) ** 0.25
NORM_EPS = 1e-5

LANES = 128
ROW_SLABS = D_MODEL // LANES
CONV_TAIL = 8
CONV_ROWS = 128

MM_TM = 512
MM_TM_WIDE = 256
MM_TN = 1024
FFN_BLK = 512
DSP_TM = 512
SSD_GROUP_BATCH = 2
LN_SUBTILES = 4
CMB_TM = 256
ATT_TQ = 512


def _arb(n):
    return pltpu.CompilerParams(dimension_semantics=("arbitrary",) * n,
                                vmem_limit_bytes=56 * 1024 * 1024)


def _silu(x):
    hx = 0.5 * x
    return hx + hx * jnp.tanh(hx)


def _softplus(x):
    return jnp.maximum(x, 0.0) + jnp.log(1.0 + jnp.exp(-jnp.abs(x)))


def _layer_norm(y, g, b):
    mu = jnp.mean(y, axis=-1, keepdims=True)
    d = y - mu
    var = jnp.mean(d * d, axis=-1, keepdims=True)
    return d * lax.rsqrt(var + NORM_EPS) * g + b


def _split3(a):
    a1 = a.astype(BF16)
    r1 = a - a1.astype(F32)
    a2 = r1.astype(BF16)
    a3 = (r1 - a2.astype(F32)).astype(BF16)
    return a1, a2, a3


def _dot(a, b):
    return jnp.dot(a, b, preferred_element_type=F32)


def _dot_sel(a, sel):
    a1, a2, a3 = _split3(a)
    return _dot(a1, sel) + _dot(a2, sel) + _dot(a3, sel)


def _dot_f32(a, b):
    a1, a2, a3 = _split3(a)
    b1, b2, b3 = _split3(b)
    return (_dot(a1, b1) + _dot(a1, b2) + _dot(a2, b1)
            + _dot(a2, b2) + _dot(a1, b3) + _dot(a3, b1))


class _CastPlan:
    def __init__(self, w_gate, w_up, w_down, layer, steps, step_of):
        self.weights = (w_gate, w_up, w_down)
        self.n_e = w_gate.shape[1]
        self.eps = -(-self.n_e // steps)
        assert self.n_e % self.eps == 0
        self.n_cast = self.n_e // self.eps
        self.stride = steps // self.n_cast
        block = lambda *ids: jnp.minimum(step_of(*ids) // self.stride, self.n_cast - 1)
        self.in_specs = [pl.BlockSpec((1, self.eps) + w.shape[2:], lambda *ids: (layer, block(*ids), 0, 0))
                         for w in self.weights]
        self.out_specs = [pl.BlockSpec((self.eps,) + w.shape[2:], lambda *ids: (block(*ids), 0, 0))
                          for w in self.weights]
        self.out_shape = [jax.ShapeDtypeStruct(w.shape[1:], BF16) for w in self.weights]

    def emit(self, step, w_refs, o_refs, extra=True):
        @pl.when(extra & (step % self.stride == 0) & (step // self.stride < self.n_cast))
        def _():
            for w_ref, o_ref in zip(w_refs, o_refs):
                o_ref[...] = w_ref[0].astype(BF16)


def _mm_kernel(x_ref, w_ref, o_ref):
    xb = x_ref[...].astype(BF16)
    for j in range(o_ref.shape[1] // MM_TN):
        cols = slice(j * MM_TN, (j + 1) * MM_TN)
        o_ref[:, cols] = _dot(xb, w_ref[:, cols]).astype(o_ref.dtype)


def _matmul(x, w, n, out_dtype):
    m, k = x.shape
    tm = min(MM_TM_WIDE, m)
    return pl.pallas_call(
        _mm_kernel,
        grid=(m // tm,),
        in_specs=[pl.BlockSpec((tm, k), lambda i: (i, 0)), pl.BlockSpec((k, n), lambda i: (0, 0))],
        out_specs=pl.BlockSpec((tm, n), lambda i: (i, 0)),
        out_shape=jax.ShapeDtypeStruct((m, n), out_dtype),
        compiler_params=_arb(1),
        name="mm_inproj",
    )(x, w)


def _ssd_kernel(z_ref, xs_ref, bc_ref, x_ref, wdt_ref, cw_ref, cb_ref, dtb_ref, alog_ref, dskip_ref,
                nw_ref, e64_ref, e128_ref, wg_ref, wu_ref, wd_ref, o_ref, wgb_ref, wub_ref, wdb_ref,
                stage_x, stage_bc, state, acp_s, dtp_s, act_s, *, cast):
    G, W, N = SSM_N_GROUPS, SSM_GROUP_WIDTH, SSM_D_STATE
    L = z_ref.shape[0]
    c = pl.program_id(1)
    cast.emit(pl.program_id(0) * pl.num_programs(1) + c, (wg_ref, wu_ref, wd_ref), (wgb_ref, wub_ref, wdb_ref))

    @pl.when(c == 0)
    def _():
        state[...] = jnp.zeros(state.shape, F32)
        for stage in (stage_x, stage_bc):
            stage[0:CONV_TAIL, :] = jnp.zeros((CONV_TAIL, stage.shape[1]), F32)

    @pl.when(c != 0)
    def _():
        for stage in (stage_x, stage_bc):
            stage[0:CONV_TAIL, :] = stage[L:L + CONV_TAIL, :]

    stage_x[CONV_TAIL:CONV_TAIL + L, :] = xs_ref[...]
    stage_bc[CONV_TAIL:CONV_TAIL + L, :] = bc_ref[...]

    def conv(stage, c0, width, w0):
        taps = [cw_ref[k:k + 1, w0:w0 + width] for k in range(SSM_CONV_K)]
        bias = cb_ref[:, w0:w0 + width]
        outs = []
        for rb in range(L // CONV_ROWS):
            r0 = rb * CONV_ROWS
            u = stage[r0:r0 + CONV_ROWS + CONV_TAIL, c0:c0 + width]
            out = bias + taps[SSM_CONV_K - 1] * u[CONV_TAIL:, :]
            for k in range(SSM_CONV_K - 1):
                back = SSM_CONV_K - 1 - k
                out = out + taps[k] * pltpu.roll(u, back, 0)[CONV_TAIL:, :]
            outs.append(_silu(out))
        return jnp.concatenate(outs, axis=0)

    xc, bcv, ccv = {}, {}, {}

    def conv_groups(gs):
        for g in gs:
            xc[g] = conv(stage_x, g * W, W, g * W)
            bcv[g] = conv(stage_bc, g * N, N, SSM_D_INNER + g * N).astype(BF16)
            ccv[g] = conv(stage_bc, G * N + g * N, N, SSM_D_INNER + G * N + g * N).astype(BF16)

    x_of, b_of, c_of = xc.__getitem__, bcv.__getitem__, ccv.__getitem__

    dt = _softplus(_dot(x_ref[...].astype(BF16), wdt_ref[...].astype(BF16)) + dtb_ref[...])
    a = -jnp.exp(alog_ref[...])
    row_i = lax.broadcasted_iota(I32, (L, L), 0)
    col_i = lax.broadcasted_iota(I32, (L, L), 1)
    causal = row_i >= col_i
    a1, a2, a3 = _split3(dt * a)
    tri = causal.astype(BF16)
    ac = (_dot(tri, a1) + _dot(tri, a2) + _dot(tri, a3)) * math.log2(math.e)
    act_s[...] = ac.T
    for n, part in enumerate(_split3(ac)):
        acp_s[n] = part
    for n, part in enumerate(_split3(dt)[:2]):
        dtp_s[n] = part
    lane = lax.broadcasted_iota(I32, (L, LANES), 1)
    first_half = lane < SSM_HEADDIM

    def select(parts_ref, n_parts, sel):
        out = _dot(parts_ref[0], sel)
        for n in range(1, n_parts):
            out = out + _dot(parts_ref[n], sel)
        return out

    def run_groups(gs):
        conv_groups(gs)
        col4 = {g: select(acp_s, 3, e128_ref[g]) for g in gs}
        dt_e = {g: select(dtp_s, 2, e64_ref[g]) for g in gs}
        cb = {g: lax.dot_general(c_of(g), b_of(g), (((1,), (1,)), ((), ())), preferred_element_type=F32)
              for g in gs}
        y_cs = {g: _dot(c_of(g), state[g].astype(BF16)) for g in gs}
        a_e = {g: jnp.concatenate(
            [jnp.where(first_half, col4[g][:, 2 * p * LANES:(2 * p + 1) * LANES],
                       col4[g][:, (2 * p + 1) * LANES:(2 * p + 2) * LANES]) for p in range(2)], axis=1)
            for g in gs}
        xdt = {g: x_of(g) * dt_e[g] for g in gs}
        xdt_b = {g: xdt[g].astype(BF16) for g in gs}
        ys = {g: [None] * SSM_HEADS_PER_GROUP for g in gs}
        for r in range(SSM_HEADS_PER_GROUP):
            for g in gs:
                row = act_s[SSM_HEADS_PER_GROUP * g + r:SSM_HEADS_PER_GROUP * g + r + 1, :]
                seg = col4[g][:, LANES * r:LANES * (r + 1)] - row
                dec = jnp.where(causal, jnp.exp2(seg), 0.0)
                xp = xdt_b[g][:, LANES * (r // 2):LANES * (r // 2 + 1)]
                ys[g][r] = _dot((cb[g] * dec).astype(BF16), xp)
        for g in gs:
            y_diag = jnp.concatenate([jnp.where(first_half, ys[g][2 * p], ys[g][2 * p + 1])
                                      for p in range(2)], axis=1)
            al_e = a_e[g][L - 1:L, :]
            new = lax.dot_general(b_of(g), (xdt[g] * jnp.exp2(al_e - a_e[g])).astype(BF16),
                                  (((0,), (0,)), ((), ())), preferred_element_type=F32)
            y = y_diag + y_cs[g] * jnp.exp2(a_e[g]) + x_of(g) * dskip_ref[g]
            state[g] = state[g] * jnp.exp2(al_e) + new
            yg = y * _silu(z_ref[:, g * W:(g + 1) * W])
            ms = jnp.mean(yg * yg, axis=-1, keepdims=True)
            o_ref[:, g * W:(g + 1) * W] = (yg * lax.rsqrt(ms + NORM_EPS)
                                           * nw_ref[:, g * W:(g + 1) * W]).astype(o_ref.dtype)

    for g0 in range(0, G, SSD_GROUP_BATCH):
        run_groups(range(g0, g0 + SSD_GROUP_BATCH))


def _head_selectors():
    G, R = SSM_N_GROUPS, SSM_HEADS_PER_GROUP
    h = jnp.arange(LANES)[None, :, None]
    g = jnp.arange(G)[:, None, None]
    j64 = jnp.arange(SSM_GROUP_WIDTH)[None, None, :]
    j128 = jnp.arange(R * LANES)[None, None, :]
    e64 = (h == g * R + j64 // SSM_HEADDIM).astype(BF16)
    e128 = (h == g * R + j128 // LANES).astype(BF16)
    return e64, e128


def _ssd(zx, x2, w_dt, conv_w, conv_b, dt_bias, a_log, d_skip, norm_w, batch, seq, moe_weights, moe_layer):
    L, G, W, N = SSM_CHUNK, SSM_N_GROUPS, SSM_GROUP_WIDTH, SSM_D_STATE
    nc = seq // L
    cast = _CastPlan(*moe_weights, moe_layer, batch * nc, lambda b, c: b * nc + c)
    t = batch * seq
    pad = LANES - SSM_N_HEADS
    wdt = jnp.pad(w_dt, ((0, 0), (0, pad)))
    dtb = jnp.pad(dt_bias, (0, pad))[None, :]
    alog = jnp.pad(a_log, (0, pad))[None, :]
    dsk = jnp.repeat(d_skip, SSM_HEADDIM).reshape(G, 1, W)
    e64, e128 = _head_selectors()
    conv_dim = conv_w.shape[1]
    row = lambda b, c: (b * nc + c, 0)
    full2 = lambda b, c: (0, 0)
    full3 = lambda b, c: (0, 0, 0)
    yg, *w_bf16 = pl.pallas_call(
        functools.partial(_ssd_kernel, cast=cast),
        grid=(batch, nc),
        in_specs=[pl.BlockSpec((L, SSM_D_INNER), row),
                  pl.BlockSpec((L, SSM_D_INNER), lambda b, c: (b * nc + c, 1)),
                  pl.BlockSpec((L, 2 * G * N), lambda b, c: (b * nc + c, 2)),
                  pl.BlockSpec((L, D_MODEL), row),
                  pl.BlockSpec((D_MODEL, LANES), full2),
                  pl.BlockSpec((SSM_CONV_K, conv_dim), full2),
                  pl.BlockSpec((1, conv_dim), full2),
                  pl.BlockSpec((1, LANES), full2),
                  pl.BlockSpec((1, LANES), full2),
                  pl.BlockSpec((G, 1, W), full3),
                  pl.BlockSpec((1, SSM_D_INNER), full2),
                  pl.BlockSpec((G, LANES, W), full3),
                  pl.BlockSpec((G, LANES, SSM_HEADS_PER_GROUP * LANES), full3)] + cast.in_specs,
        out_specs=[pl.BlockSpec((L, SSM_D_INNER), row)] + cast.out_specs,
        out_shape=[jax.ShapeDtypeStruct((t, SSM_D_INNER), BF16)] + cast.out_shape,
        scratch_shapes=[pltpu.VMEM((CONV_TAIL + L, SSM_D_INNER), F32),
                        pltpu.VMEM((CONV_TAIL + L, 2 * G * N), F32),
                        pltpu.VMEM((G, N, W), F32),
                        pltpu.VMEM((3, L, LANES), BF16),
                        pltpu.VMEM((2, L, LANES), BF16),
                        pltpu.VMEM((LANES, L), F32)],
        compiler_params=_arb(2),
        name="ssd_scan",
    )(zx, zx, zx, x2, wdt, conv_w, conv_b[None, :], dtb, alog, dsk, norm_w[None, :], e64, e128, *cast.weights)
    return yg, tuple(w_bf16)


def _route(h, wr):
    tm = h.shape[0]
    logits = _dot(h.astype(BF16), wr)
    lane_i = lax.broadcasted_iota(I32, (tm, LANES), 1)
    lane = lane_i.astype(F32)
    neg = -jnp.inf
    big = float(LANES)

    def first_argmax(v, vmax):
        return jnp.min(jnp.where(v == vmax, lane, big), axis=-1, keepdims=True)

    gl = jnp.where((lane_i >= MOE_N_EXPERTS) & (lane_i < MOE_N_EXPERTS + MOE_GROUPS), logits, neg)
    gm = jnp.max(gl, axis=-1, keepdims=True)
    g_sel = first_argmax(gl, gm) - float(MOE_N_EXPERTS)
    g_gate = 1.0 / jnp.sum(jnp.exp(gl - gm), axis=-1, keepdims=True)
    lo = g_sel * float(MOE_EXPERTS_PER_GROUP)
    el = jnp.where((lane >= lo) & (lane < lo + float(MOE_EXPERTS_PER_GROUP)), logits, neg)
    m1 = jnp.max(el, axis=-1, keepdims=True)
    i1 = first_argmax(el, m1)
    el2 = jnp.where(lane == i1, neg, el)
    m2 = jnp.max(el2, axis=-1, keepdims=True)
    i2 = first_argmax(el2, m2)
    p2 = jnp.exp(m2 - m1)
    t1 = 1.0 / (1.0 + p2)
    t2 = p2 / (1.0 + p2)
    eid = jnp.where(lane_i == 0, i1, jnp.where(lane_i == 1, i2, 0.0)).astype(I32)
    wt = jnp.where(lane_i == 0, g_gate * t1, jnp.where(lane_i == 1, g_gate * t2, 0.0))
    count = jnp.sum(jnp.where(lane == i1, 1.0, 0.0) + jnp.where(lane == i2, 1.0, 0.0), axis=0, keepdims=True)
    return eid, wt, count


def _router_weights(w_group, w_expert):
    pad = LANES - MOE_N_EXPERTS - MOE_GROUPS
    return jnp.pad(jnp.concatenate([w_expert, w_group], axis=1), ((0, 0), (0, pad))).astype(BF16)


def _store_token_tiles(ref, v, first_token=0):
    rows = v.shape[0]
    for s in range(ROW_SLABS):
        ref[pl.ds(first_token * ROW_SLABS + s, rows, stride=ROW_SLABS), :] = v[:, s * LANES:(s + 1) * LANES]


def _load_token_tiles(ref, rows):
    return jnp.concatenate([ref[pl.ds(s, rows, stride=ROW_SLABS), :] for s in range(ROW_SLABS)], axis=1)


def _mm_ln_router_kernel(x_ref, w_ref, r_ref, g_ref, b_ref, wr_ref, h_ref, ht_ref, eid_ref, wt_ref, cnt_ref):
    @pl.when(pl.program_id(0) == 0)
    def _():
        cnt_ref[...] = jnp.zeros(cnt_ref.shape, F32)

    sub = x_ref.shape[0] // LN_SUBTILES
    rows = [slice(u * sub, (u + 1) * sub) for u in range(LN_SUBTILES)]
    ys = [DEEPNORM_ALPHA * r_ref[rs, :] + _dot(x_ref[rs, :].astype(BF16), w_ref[...]) for rs in rows]
    hs = [_layer_norm(y, g_ref[...], b_ref[...]) for y in ys]
    routes = [_route(h, wr_ref[...]) for h in hs]
    for u, rs in enumerate(rows):
        h_ref[rs, :] = hs[u]
        _store_token_tiles(ht_ref, hs[u], first_token=u * sub)
        eid_ref[rs, :] = routes[u][0]
        wt_ref[rs, :] = routes[u][1]
    cnt_ref[...] += jnp.broadcast_to(sum(r[2] for r in routes), cnt_ref.shape)


def _mm_ln_router(x, w, resid, g, b, wr, name):
    m, k = x.shape
    d = w.shape[1]
    tm = min(MM_TM, m)
    row = lambda i: (i, 0)
    full = lambda i: (0, 0)
    return pl.pallas_call(
        _mm_ln_router_kernel,
        grid=(m // tm,),
        in_specs=[pl.BlockSpec((tm, k), row), pl.BlockSpec((k, d), full), pl.BlockSpec((tm, d), row),
                  pl.BlockSpec((1, d), full), pl.BlockSpec((1, d), full), pl.BlockSpec((d, LANES), full)],
        out_specs=[pl.BlockSpec((tm, d), row), pl.BlockSpec((tm * ROW_SLABS, LANES), row),
                   pl.BlockSpec((tm, LANES), row), pl.BlockSpec((tm, LANES), row),
                   pl.BlockSpec((8, LANES), full)],
        out_shape=[jax.ShapeDtypeStruct((m, d), F32), jax.ShapeDtypeStruct((m * ROW_SLABS, LANES), F32),
                   jax.ShapeDtypeStruct((m, LANES), I32), jax.ShapeDtypeStruct((m, LANES), F32),
                   jax.ShapeDtypeStruct((8, LANES), F32)],
        compiler_params=_arb(1),
        name=name,
    )(x, w, resid, g[None, :], b[None, :], wr)


def _plan_kernel(eid_ref, cnt_ref, dest_ref, be_ref, nu_ref, rowtok_ref,
                 base, upper, zero_v, dst_v, dst_s, sem_s, *, blk):
    i = pl.program_id(0)
    tm = eid_ref.shape[0]
    n_e = LANES
    eid_t = eid_ref[...].astype(F32).T
    sub = lax.broadcasted_iota(I32, (n_e, tm), 0).astype(F32)
    oh = [(sub == eid_t[k:k + 1, :]).astype(F32) for k in range(MOE_TOP_K)]
    tot = [jnp.sum(o, axis=1, keepdims=True) for o in oh]

    @pl.when(i == 0)
    def _():
        r_i = lax.broadcasted_iota(I32, (tm, tm), 0)
        c_i = lax.broadcasted_iota(I32, (tm, tm), 1)
        upper[...] = (r_i < c_i).astype(BF16)
        counts = jnp.broadcast_to(cnt_ref[0:1, :], (n_e, n_e)).T
        padded = jnp.floor((counts + float(blk - 1)) * (1.0 / blk)) * float(blk)
        r_i = lax.broadcasted_iota(I32, (n_e, n_e), 0)
        c_i = lax.broadcasted_iota(I32, (n_e, n_e), 1)
        tril = (r_i >= c_i).astype(BF16)
        p1, p2, p3 = _split3(padded)
        pends = _dot(tril, p1) + _dot(tril, p2) + _dot(tril, p3)
        base[...] = pends - padded
        nbp = be_ref.shape[1]
        blk_start = lax.broadcasted_iota(I32, (n_e, nbp), 1).astype(F32) * float(blk)
        is_e = lax.broadcasted_iota(I32, (n_e, nbp), 0) < MOE_N_EXPERTS
        done = jnp.where(is_e & (jnp.tile(pends, (1, nbp // LANES)) <= blk_start), 1.0, 0.0)
        be = jnp.minimum(jnp.sum(done, axis=0, keepdims=True), float(MOE_N_EXPERTS - 1))
        be_ref[...] = jnp.broadcast_to(be, be_ref.shape).astype(I32)
        last = pends[MOE_N_EXPERTS - 1:MOE_N_EXPERTS, :] * (1.0 / blk)
        nu_ref[...] = jnp.broadcast_to(last, nu_ref.shape).astype(I32)
        zero_v[...] = jnp.zeros(zero_v.shape, I32)
        cp = pltpu.make_async_copy(zero_v, rowtok_ref, sem_s)
        cp.start()
        cp.wait()

    b0 = base[:, 0:1]
    c0 = _dot(oh[0].astype(BF16), upper[...])
    c1 = _dot(oh[1].astype(BF16), upper[...])
    d0 = jnp.sum(oh[0] * (b0 + c0), axis=0, keepdims=True)
    d1 = jnp.sum(oh[1] * (b0 + tot[0] + c1), axis=0, keepdims=True)
    base[...] += jnp.broadcast_to(tot[0] + tot[1], base.shape)
    row8 = lax.broadcasted_iota(I32, (8, tm), 0)
    dst = jnp.where(row8 == 0, d0, jnp.where(row8 == 1, d1, 0.0)).astype(I32)
    dest_ref[...] = dst
    dst_v[...] = dst
    cp = pltpu.make_async_copy(dst_v, dst_s, sem_s)
    cp.start()
    cp.wait()

    first = i * tm

    def body(r, carry):
        for k in range(MOE_TOP_K):
            rowtok_ref[dst_s[k, r]] = first + r
        return carry
    lax.fori_loop(0, tm, body, 0, unroll=8)


def _moe_plan(eid, counts, blk, name):
    t = eid.shape[0]
    assert blk & (blk - 1) == 0, "block size must be a power of two"
    tm = min(DSP_TM, t)
    nt = t // tm
    n_blocks = -(-(t * MOE_TOP_K) // blk) + MOE_N_EXPERTS
    nbp = -(-n_blocks // LANES) * LANES
    full = lambda i: (0, 0)
    dest, be, nu, rowtok = pl.pallas_call(
        functools.partial(_plan_kernel, blk=blk),
        grid=(nt,),
        in_specs=[pl.BlockSpec((tm, LANES), lambda i: (i, 0)),
                  pl.BlockSpec((8, LANES), full)],
        out_specs=[pl.BlockSpec((8, tm), lambda i: (0, i)),
                   pl.BlockSpec((8, nbp), full),
                   pl.BlockSpec((8, LANES), full),
                   pl.BlockSpec(memory_space=pltpu.SMEM)],
        out_shape=[jax.ShapeDtypeStruct((8, t), I32), jax.ShapeDtypeStruct((8, nbp), I32),
                   jax.ShapeDtypeStruct((8, LANES), I32),
                   jax.ShapeDtypeStruct((n_blocks * blk,), I32)],
        scratch_shapes=[pltpu.VMEM((LANES, LANES), F32),
                        pltpu.VMEM((tm, tm), BF16), pltpu.VMEM((n_blocks * blk,), I32),
                        pltpu.VMEM((8, tm), I32), pltpu.SMEM((8, tm), I32),
                        pltpu.SemaphoreType.DMA(())],
        compiler_params=_arb(1),
        name=name,
    )(eid, counts)
    return dest, rowtok, be[0, :n_blocks], nu[0, :1], n_blocks


def _ffn_kernel(be_ref, nu_ref, tok_ref, tokn_ref, h_hbm, wg_ref, wu_ref, wd_ref, o_ref, xbuf, sem):
    i = pl.program_id(0)
    nu = nu_ref[0]
    blk = xbuf.shape[1] // ROW_SLABS

    def start_gather(tref, slot):
        for r in range(blk):
            src = pl.multiple_of(tref[0, 0, r] * ROW_SLABS, ROW_SLABS)
            pltpu.make_async_copy(h_hbm.at[pl.ds(src, ROW_SLABS)],
                                  xbuf.at[slot, pl.ds(r * ROW_SLABS, ROW_SLABS)],
                                  sem.at[slot]).start(priority=r % 2)

    @pl.when(i == 0)
    def _():
        start_gather(tok_ref, 0)

    @pl.when(i + 1 < nu)
    def _():
        start_gather(tokn_ref, (i + 1) % 2)

    @pl.when(i < nu)
    def _():
        slot = i % 2
        pltpu.make_async_copy(h_hbm.at[pl.ds(0, blk * ROW_SLABS)], xbuf.at[slot], sem.at[slot]).wait()
        xb = _load_token_tiles(xbuf.at[slot], blk).astype(BF16)
        hid = _silu(_dot(xb, wg_ref[0])) * _dot(xb, wu_ref[0])
        _store_token_tiles(o_ref, _dot(hid.astype(BF16), wd_ref[0]))

    @pl.when(i >= nu)
    def _():
        o_ref[...] = jnp.zeros(o_ref.shape, o_ref.dtype)


def _moe_ffn(ht, rowtok, block_expert, n_used, n_blocks, weights, blk, name):
    w_gate, w_up, w_down = weights
    d, f = w_gate.shape[1], w_gate.shape[2]
    brows = blk * ROW_SLABS
    tok3 = rowtok.reshape(n_blocks, 1, blk)
    expert = lambda i, be, nu: (be[i], 0, 0)
    grid_spec = pltpu.PrefetchScalarGridSpec(
        num_scalar_prefetch=2,
        grid=(n_blocks,),
        in_specs=[pl.BlockSpec((1, 1, blk), lambda i, be, nu: (i, 0, 0), memory_space=pltpu.SMEM),
                  pl.BlockSpec((1, 1, blk), lambda i, be, nu: (jnp.minimum(i + 1, n_blocks - 1), 0, 0),
                               memory_space=pltpu.SMEM),
                  pl.BlockSpec(memory_space=pl.ANY),
                  pl.BlockSpec((1, d, f), expert), pl.BlockSpec((1, d, f), expert),
                  pl.BlockSpec((1, f, d), expert)],
        out_specs=pl.BlockSpec((brows, LANES), lambda i, be, nu: (i, 0)),
        scratch_shapes=[pltpu.VMEM((2, brows, LANES), F32), pltpu.SemaphoreType.DMA((2,))])
    return pl.pallas_call(
        _ffn_kernel,
        grid_spec=grid_spec,
        out_shape=jax.ShapeDtypeStruct((n_blocks * brows, LANES), F32),
        compiler_params=_arb(1),
        name=name,
    )(block_expert, n_used, tok3, tok3, ht, w_gate, w_up, w_down)


def _combine_ln_kernel(dst_ref, dstn_ref, h_ref, wt_ref, g_ref, b_ref, yb_hbm, o_ref, ybuf, sem):
    i = pl.program_id(0)
    n = pl.num_programs(0)
    tm = h_ref.shape[0]

    def start_gather(dref, slot):
        for r in range(tm):
            for k in range(MOE_TOP_K):
                src = pl.multiple_of(dref[k, r] * ROW_SLABS, ROW_SLABS)
                pltpu.make_async_copy(yb_hbm.at[pl.ds(src, ROW_SLABS)],
                                      ybuf.at[slot, k, pl.ds(r * ROW_SLABS, ROW_SLABS)],
                                      sem.at[slot]).start(priority=k)

    @pl.when(i == 0)
    def _():
        start_gather(dst_ref, 0)

    @pl.when(i + 1 < n)
    def _():
        start_gather(dstn_ref, (i + 1) % 2)

    slot = i % 2
    for k in range(MOE_TOP_K):
        pltpu.make_async_copy(yb_hbm.at[pl.ds(0, tm * ROW_SLABS)], ybuf.at[slot, k], sem.at[slot]).wait()
    wt = wt_ref[...]
    ffn = (wt[:, 0:1] * _load_token_tiles(ybuf.at[slot, 0], tm)
           + wt[:, 1:2] * _load_token_tiles(ybuf.at[slot, 1], tm))
    o_ref[...] = _layer_norm(DEEPNORM_ALPHA * h_ref[...] + ffn, g_ref[...], b_ref[...])


def _combine_ln(h, wt, dest, yb, g, b, name):
    t, d = h.shape
    tm = min(CMB_TM, t)
    nblk = t // tm
    row = lambda i: (i, 0)
    full = lambda i: (0, 0)
    return pl.pallas_call(
        _combine_ln_kernel,
        grid=(nblk,),
        in_specs=[pl.BlockSpec((8, tm), lambda i: (0, i), memory_space=pltpu.SMEM),
                  pl.BlockSpec((8, tm), lambda i: (0, jnp.minimum(i + 1, nblk - 1)),
                               memory_space=pltpu.SMEM),
                  pl.BlockSpec((tm, d), row), pl.BlockSpec((tm, LANES), row),
                  pl.BlockSpec((1, d), full), pl.BlockSpec((1, d), full),
                  pl.BlockSpec(memory_space=pl.ANY)],
        out_specs=pl.BlockSpec((tm, d), row),
        out_shape=jax.ShapeDtypeStruct((t, d), F32),
        scratch_shapes=[pltpu.VMEM((2, MOE_TOP_K, tm * ROW_SLABS, LANES), F32), pltpu.SemaphoreType.DMA((2,))],
        compiler_params=_arb(1),
        name=name,
    )(dest, dest, h, wt, g[None, :], b[None, :], yb)


def _hier_moe_ln(h, ht, eid, wt, counts, w_bf16, g, b, layer):
    dest, rowtok, block_expert, n_used, n_blocks = _moe_plan(eid, counts, FFN_BLK, f"moe_plan{layer}")
    yb = _moe_ffn(ht, rowtok, block_expert, n_used, n_blocks, w_bf16, FFN_BLK, f"moe_ffn{layer}")
    return _combine_ln(h, wt, dest, yb, g, b, f"moe_combine_ln{layer}")


def _qkv_rope_kernel(x_ref, w_ref, pos_ref, inv_ref, o_ref):
    xb = x_ref[...].astype(BF16)
    tm = xb.shape[0]
    n = D_MODEL
    ang = pos_ref[...].astype(F32) * inv_ref[...]
    lane = lax.broadcasted_iota(I32, (tm, LANES), 1)
    dd = lane & (ATTN_HEAD_DIM - 1)
    half = ROT_DIM // 2
    cosv = jnp.cos(ang)
    sinv = jnp.sin(ang)
    c_t = jnp.where(dd < ROT_DIM, cosv, 1.0)
    s_up = jnp.where(dd < half, -sinv, 0.0)
    s_dn = jnp.where((dd >= half) & (dd < ROT_DIM), sinv, 0.0)
    for j, sc in ((0, ATTN_HEAD_DIM ** -0.5 * math.log2(math.e)), (1, 1.0)):
        acc = _dot(xb, w_ref[:, j * n:(j + 1) * n])
        c_j, up_j, dn_j = c_t * sc, s_up * sc, s_dn * sc
        for blk in range(n // LANES):
            tt = acc[:, blk * LANES:(blk + 1) * LANES]
            out = tt * c_j + pltpu.roll(tt, LANES - half, 1) * up_j + pltpu.roll(tt, half, 1) * dn_j
            o_ref[:, j * n + blk * LANES:j * n + (blk + 1) * LANES] = out.astype(o_ref.dtype)
    o_ref[:, 2 * n:3 * n] = _dot(xb, w_ref[:, 2 * n:3 * n]).astype(o_ref.dtype)


def _rope_inv_table():
    inv = ROPE_THETA ** (-jnp.arange(0, ROT_DIM, 2, dtype=F32) / ROT_DIM)
    head = jnp.concatenate([inv, inv, jnp.zeros((ATTN_HEAD_DIM - ROT_DIM,), F32)])
    return jnp.tile(head, LANES // ATTN_HEAD_DIM)[None, :]


def _qkv_rope(h, w_qkv, positions):
    m, k = h.shape
    n = w_qkv.shape[1]
    tm = min(MM_TM, m)
    pos = positions.reshape(m, 1)
    return pl.pallas_call(
        _qkv_rope_kernel,
        grid=(m // tm,),
        in_specs=[pl.BlockSpec((tm, k), lambda i: (i, 0)),
                  pl.BlockSpec((k, n), lambda i: (0, 0)),
                  pl.BlockSpec((tm, 1), lambda i: (i, 0)),
                  pl.BlockSpec((1, LANES), lambda i: (0, 0))],
        out_specs=pl.BlockSpec((tm, n), lambda i: (i, 0)),
        out_shape=jax.ShapeDtypeStruct((m, n), BF16),
        compiler_params=_arb(1),
        name="mm_qkv_rope",
    )(h, w_qkv, pos, _rope_inv_table())


def _attn_kernel(q_ref, k_ref, v_ref, lq1_ref, lk1_ref, lq2_ref, lk2_ref, sw_ref, wg_ref, wu_ref, wd_ref,
                 o_ref, wgb_ref, wub_ref, wdb_ref,
                 vx_ref, m0_ref, m1_ref, acc0_ref, acc1_ref, *, lambda_init, cast):
    m_refs = (m0_ref, m1_ref)
    acc_refs = (acc0_ref, acc1_ref)
    jp = pl.program_id(2)
    tq = q_ref.shape[0] // 2
    cast.emit(pl.program_id(0) * pl.num_programs(1) + pl.program_id(1),
              (wg_ref, wu_ref, wd_ref), (wgb_ref, wub_ref, wdb_ref), extra=jp == 0)

    @pl.when(jp == 0)
    def _():
        vx_ref[:, 0:LANES] = v_ref[...]
        vx_ref[:, LANES:] = jnp.ones((vx_ref.shape[0], LANES), vx_ref.dtype)

    q = q_ref[...]
    lane = lax.broadcasted_iota(I32, q.shape, 1)
    zero = jnp.zeros((), q.dtype)
    qs = (jnp.where(lane < ATTN_HEAD_DIM, q, zero), jnp.where(lane >= ATTN_HEAD_DIM, q, zero))
    for c in range(2):
        m_refs[c][...] = jnp.full(m_refs[c].shape, -jnp.inf, F32)
        acc_refs[c][...] = jnp.zeros(acc_refs[c].shape, F32)

    def step(off, windows):
        off = pl.multiple_of(off, tq)
        widest = max(w for _, w, _ in windows)
        kb = k_ref[pl.ds(off, widest), :]
        vb = vx_ref[pl.ds(off, widest), :]
        chains = [(slice(r0, r0 + tq), w, d, c) for r0, w, d in windows for c in range(2)]
        ss = [lax.dot_general(qs[c][rows, :], kb[:w, :], (((1,), (1,)), ((), ())), preferred_element_type=F32)
              for rows, w, _, c in chains]
        for n, (_, w, d, _) in enumerate(chains):
            if d is not None:
                row_i = lax.broadcasted_iota(I32, (tq, w), 0)
                col_i = lax.broadcasted_iota(I32, (tq, w), 1)
                ss[n] = jnp.where(row_i + d >= col_i, ss[n], -jnp.inf)
        m_prev = [m_refs[c][rows, :] for rows, _, _, c in chains]
        mn = [jnp.maximum(mp, jnp.max(s, axis=-1, keepdims=True)) for mp, s in zip(m_prev, ss)]
        ps = [jnp.exp2(s - jnp.tile(m, (1, s.shape[1] // LANES))).astype(BF16) for s, m in zip(ss, mn)]
        for n, (rows, w, _, c) in enumerate(chains):
            alpha = jnp.exp2(m_prev[n] - mn[n])
            acc_refs[c][rows, :] = jnp.tile(alpha, (1, 2)) * acc_refs[c][rows, :] + _dot(ps[n], vb[:w, :])
            m_refs[c][rows, :] = mn[n]

    both = [(0, 2 * tq, None), (tq, 2 * tq, None)]
    lax.fori_loop(0, jp, lambda j, carry: (step(j * (2 * tq), both), carry)[1], 0)
    step(jp * (2 * tq), [(0, tq, 0), (tq, 2 * tq, tq)])

    lam = (jnp.exp(jnp.sum(lq1_ref[...] * lk1_ref[...], axis=-1, keepdims=True))
           - jnp.exp(jnp.sum(lq2_ref[...] * lk2_ref[...], axis=-1, keepdims=True)) + lambda_init)
    a1 = acc0_ref[...]
    a2 = acc1_ref[...]
    o = a1[:, :LANES] / a1[:, LANES:] - lam * (a2[:, :LANES] / a2[:, LANES:])
    o = o * lax.rsqrt(jnp.mean(o * o, axis=-1, keepdims=True) + NORM_EPS)
    o_ref[...] = (o * sw_ref[...] * (1.0 - lambda_init)).astype(o_ref.dtype)


def _diff_attention(qkv, lq1, lk1, lq2, lk2, subln_w, lambda_init, batch, seq, moe_weights, moe_layer):
    t = batch * seq
    tq = 2 * min(ATT_TQ, seq // 2)
    nq = seq // tq
    h_n = ATTN_N_HEADS
    vec = lambda b, h, i: (0, 0)
    cast = _CastPlan(*moe_weights, moe_layer, batch * h_n, lambda b, h, i: b * h_n + h)
    o, *w_bf16 = pl.pallas_call(
        functools.partial(_attn_kernel, lambda_init=lambda_init, cast=cast),
        grid=(batch, h_n, nq),
        in_specs=[pl.BlockSpec((tq, LANES), lambda b, h, i: (b * nq + i, h)),
                  pl.BlockSpec((seq, LANES), lambda b, h, i: (b, h_n + h)),
                  pl.BlockSpec((seq, LANES), lambda b, h, i: (b, 2 * h_n + h)),
                  pl.BlockSpec((1, ATTN_HEAD_DIM), vec), pl.BlockSpec((1, ATTN_HEAD_DIM), vec),
                  pl.BlockSpec((1, ATTN_HEAD_DIM), vec), pl.BlockSpec((1, ATTN_HEAD_DIM), vec),
                  pl.BlockSpec((1, ATTN_V_DIM), vec)] + cast.in_specs,
        out_specs=[pl.BlockSpec((tq, LANES), lambda b, h, i: (b * nq + i, h))] + cast.out_specs,
        out_shape=[jax.ShapeDtypeStruct((t, h_n * ATTN_V_DIM), BF16)] + cast.out_shape,
        scratch_shapes=[pltpu.VMEM((seq, 2 * LANES), BF16),
                        pltpu.VMEM((tq, LANES), F32), pltpu.VMEM((tq, LANES), F32),
                        pltpu.VMEM((tq, 2 * LANES), F32), pltpu.VMEM((tq, 2 * LANES), F32)],
        compiler_params=_arb(3),
        name="diff_attn",
    )(qkv, qkv, qkv, lq1[None, :], lk1[None, :], lq2[None, :], lk2[None, :], subln_w[None, :], *cast.weights)
    return o, tuple(w_bf16)


def kernel(x, positions, ln_mix_g, ln_mix_b, ln_ffn_g, ln_ffn_b, ssm_w_in, ssm_conv_w, ssm_conv_b, ssm_dt_bias, ssm_a_log, ssm_d, ssm_norm_w, ssm_w_out, attn_w_qkv, attn_lam_q1, attn_lam_k1, attn_lam_q2, attn_lam_k2, attn_subln_w, attn_w_o, moe_w_group, moe_w_expert, moe_w_gate, moe_w_up, moe_w_down):
    batch, seq, d = x.shape
    t = batch * seq
    h = x.reshape(t, d)

    w_in = ssm_w_in[0].astype(BF16)

    moe_weights = (moe_w_gate, moe_w_up, moe_w_down)
    zx = _matmul(h, w_in, SSM_ZX_DIM, F32)
    yg, moe_w0 = _ssd(zx, h, w_in[:, SSM_ZX_DIM:], ssm_conv_w[0], ssm_conv_b[0], ssm_dt_bias[0],
                      ssm_a_log[0], ssm_d[0], ssm_norm_w[0], batch, seq, moe_weights, 0)
    h, ht, eid, wt, cnt = _mm_ln_router(yg, ssm_w_out[0].astype(BF16), h, ln_mix_g[0], ln_mix_b[0],
                                        _router_weights(moe_w_group[0], moe_w_expert[0]), "mm_ssm_out_ln_router")
    h = _hier_moe_ln(h, ht, eid, wt, cnt, moe_w0, ln_ffn_g[0], ln_ffn_b[0], 0)

    lambda_init = 0.8 - 0.6 * math.exp(-0.3 * 1)
    qkv = _qkv_rope(h, attn_w_qkv[0].astype(BF16), positions)
    o, moe_w1 = _diff_attention(qkv, attn_lam_q1[0], attn_lam_k1[0], attn_lam_q2[0], attn_lam_k2[0],
                                attn_subln_w[0], lambda_init, batch, seq, moe_weights, 1)
    h, ht, eid, wt, cnt = _mm_ln_router(o, attn_w_o[0].astype(BF16), h, ln_mix_g[1], ln_mix_b[1],
                                        _router_weights(moe_w_group[1], moe_w_expert[1]), "mm_attn_out_ln_router")
    h = _hier_moe_ln(h, ht, eid, wt, cnt, moe_w1, ln_ffn_g[1], ln_ffn_b[1], 1)
    return h.reshape(batch, seq, d)
```

```python
import functools
import math

import jax
import jax.numpy as jnp
from jax import lax
from jax.experimental import pallas as pl
from jax.experimental.pallas import tpu as pltpu

F32 = jnp.float32
BF16 = jnp.bfloat16
I32 = jnp.int32

D_MODEL = 1024
DEPTH = 2
SSM_D_INNER = 2048
SSM_HEADDIM = 64
SSM_N_HEADS = 32
SSM_N_GROUPS = 8
SSM_HEADS_PER_GROUP = 4
SSM_D_STATE = 128
SSM_CONV_K = 4
SSM_CHUNK = 128
SSM_GROUP_WIDTH = SSM_HEADS_PER_GROUP * SSM_HEADDIM
SSM_ZX_DIM = 2 * SSM_D_INNER + 2 * SSM_N_GROUPS * SSM_D_STATE
ATTN_HEAD_DIM = 64
ATTN_N_HEADS = 8
ATTN_V_DIM = 128
ROT_DIM = 16
ROPE_THETA = 500000.0
MOE_GROUPS = 4
MOE_EXPERTS_PER_GROUP = 8
MOE_N_EXPERTS = 32
MOE_TOP_K = 2
MOE_D_FF = 512
DEEPNORM_ALPHA = (2 * DEPTH) ** 0.25
NORM_EPS = 1e-5

LANES = 128
ROW_SLABS = D_MODEL // LANES
CONV_TAIL = 8
CONV_ROWS = 128

MM_TM = 512
MM_TM_WIDE = 256
MM_TN = 1024
FFN_BLK = 512
DSP_TM = 512
SSD_GROUP_BATCH = 2
LN_SUBTILES = 4
CMB_TM = 256
ATT_TQ = 512


def _arb(n):
    return pltpu.CompilerParams(dimension_semantics=("arbitrary",) * n,
                                vmem_limit_bytes=56 * 1024 * 1024)


def _silu(x):
    hx = 0.5 * x
    return hx + hx * jnp.tanh(hx)


def _softplus(x):
    return jnp.maximum(x, 0.0) + jnp.log(1.0 + jnp.exp(-jnp.abs(x)))


def _layer_norm(y, g, b):
    mu = jnp.mean(y, axis=-1, keepdims=True)
    d = y - mu
    var = jnp.mean(d * d, axis=-1, keepdims=True)
    return d * lax.rsqrt(var + NORM_EPS) * g + b


def _split3(a):
    a1 = a.astype(BF16)
    r1 = a - a1.astype(F32)
    a2 = r1.astype(BF16)
    a3 = (r1 - a2.astype(F32)).astype(BF16)
    return a1, a2, a3


def _dot(a, b):
    return jnp.dot(a, b, preferred_element_type=F32)


def _dot_sel(a, sel):
    a1, a2, a3 = _split3(a)
    return _dot(a1, sel) + _dot(a2, sel) + _dot(a3, sel)


def _dot_f32(a, b):
    a1, a2, a3 = _split3(a)
    b1, b2, b3 = _split3(b)
    return (_dot(a1, b1) + _dot(a1, b2) + _dot(a2, b1)
            + _dot(a2, b2) + _dot(a1, b3) + _dot(a3, b1))


class _CastPlan:
    def __init__(self, w_gate, w_up, w_down, layer, steps, step_of):
        self.weights = (w_gate, w_up, w_down)
        self.n_e = w_gate.shape[1]
        self.eps = -(-self.n_e // steps)
        assert self.n_e % self.eps == 0
        self.n_cast = self.n_e // self.eps
        self.stride = steps // self.n_cast
        block = lambda *ids: jnp.minimum(step_of(*ids) // self.stride, self.n_cast - 1)
        self.in_specs = [pl.BlockSpec((1, self.eps) + w.shape[2:], lambda *ids: (layer, block(*ids), 0, 0))
                         for w in self.weights]
        self.out_specs = [pl.BlockSpec((self.eps,) + w.shape[2:], lambda *ids: (block(*ids), 0, 0))
                          for w in self.weights]
        self.out_shape = [jax.ShapeDtypeStruct(w.shape[1:], BF16) for w in self.weights]

    def emit(self, step, w_refs, o_refs, extra=True):
        @pl.when(extra & (step % self.stride == 0) & (step // self.stride < self.n_cast))
        def _():
            for w_ref, o_ref in zip(w_refs, o_refs):
                o_ref[...] = w_ref[0].astype(BF16)


def _mm_kernel(x_ref, w_ref, o_ref):
    xb = x_ref[...].astype(BF16)
    for j in range(o_ref.shape[1] // MM_TN):
        cols = slice(j * MM_TN, (j + 1) * MM_TN)
        o_ref[:, cols] = _dot(xb, w_ref[:, cols]).astype(o_ref.dtype)


def _matmul(x, w, n, out_dtype):
    m, k = x.shape
    tm = min(MM_TM_WIDE, m)
    return pl.pallas_call(
        _mm_kernel,
        grid=(m // tm,),
        in_specs=[pl.BlockSpec((tm, k), lambda i: (i, 0)), pl.BlockSpec((k, n), lambda i: (0, 0))],
        out_specs=pl.BlockSpec((tm, n), lambda i: (i, 0)),
        out_shape=jax.ShapeDtypeStruct((m, n), out_dtype),
        compiler_params=_arb(1),
        name="mm_inproj",
    )(x, w)


def _ssd_kernel(z_ref, xs_ref, bc_ref, x_ref, wdt_ref, cw_ref, cb_ref, dtb_ref, alog_ref, dskip_ref,
                nw_ref, e64_ref, e128_ref, wg_ref, wu_ref, wd_ref, o_ref, wgb_ref, wub_ref, wdb_ref,
                stage_x, stage_bc, state, acp_s, dtp_s, act_s, *, cast):
    G, W, N = SSM_N_GROUPS, SSM_GROUP_WIDTH, SSM_D_STATE
    L = z_ref.shape[0]
    c = pl.program_id(1)
    cast.emit(pl.program_id(0) * pl.num_programs(1) + c, (wg_ref, wu_ref, wd_ref), (wgb_ref, wub_ref, wdb_ref))

    @pl.when(c == 0)
    def _():
        state[...] = jnp.zeros(state.shape, F32)
        for stage in (stage_x, stage_bc):
            stage[0:CONV_TAIL, :] = jnp.zeros((CONV_TAIL, stage.shape[1]), F32)

    @pl.when(c != 0)
    def _():
        for stage in (stage_x, stage_bc):
            stage[0:CONV_TAIL, :] = stage[L:L + CONV_TAIL, :]

    stage_x[CONV_TAIL:CONV_TAIL + L, :] = xs_ref[...]
    stage_bc[CONV_TAIL:CONV_TAIL + L, :] = bc_ref[...]

    def conv(stage, c0, width, w0):
        taps = [cw_ref[k:k + 1, w0:w0 + width] for k in range(SSM_CONV_K)]
        bias = cb_ref[:, w0:w0 + width]
        outs = []
        for rb in range(L // CONV_ROWS):
            r0 = rb * CONV_ROWS
            u = stage[r0:r0 + CONV_ROWS + CONV_TAIL, c0:c0 + width]
            out = bias + taps[SSM_CONV_K - 1] * u[CONV_TAIL:, :]
            for k in range(SSM_CONV_K - 1):
                back = SSM_CONV_K - 1 - k
                out = out + taps[k] * pltpu.roll(u, back, 0)[CONV_TAIL:, :]
            outs.append(_silu(out))
        return jnp.concatenate(outs, axis=0)

    xc, bcv, ccv = {}, {}, {}

    def conv_groups(gs):
        for g in gs:
            xc[g] = conv(stage_x, g * W, W, g * W)
            bcv[g] = conv(stage_bc, g * N, N, SSM_D_INNER + g * N).astype(BF16)
            ccv[g] = conv(stage_bc, G * N + g * N, N, SSM_D_INNER + G * N + g * N).astype(BF16)

    x_of, b_of, c_of = xc.__getitem__, bcv.__getitem__, ccv.__getitem__

    dt = _softplus(_dot(x_ref[...].astype(BF16), wdt_ref[...].astype(BF16)) + dtb_ref[...])
    a = -jnp.exp(alog_ref[...])
    row_i = lax.broadcasted_iota(I32, (L, L), 0)
    col_i = lax.broadcasted_iota(I32, (L, L), 1)
    causal = row_i >= col_i
    a1, a2, a3 = _split3(dt * a)
    tri = causal.astype(BF16)
    ac = (_dot(tri, a1) + _dot(tri, a2) + _dot(tri, a3)) * math.log2(math.e)
    act_s[...] = ac.T
    for n, part in enumerate(_split3(ac)):
        acp_s[n] = part
    for n, part in enumerate(_split3(dt)[:2]):
        dtp_s[n] = part
    lane = lax.broadcasted_iota(I32, (L, LANES), 1)
    first_half = lane < SSM_HEADDIM

    def select(parts_ref, n_parts, sel):
        out = _dot(parts_ref[0], sel)
        for n in range(1, n_parts):
            out = out + _dot(parts_ref[n], sel)
        return out

    def run_groups(gs):
        conv_groups(gs)
        col4 = {g: select(acp_s, 3, e128_ref[g]) for g in gs}
        dt_e = {g: select(dtp_s, 2, e64_ref[g]) for g in gs}
        cb = {g: lax.dot_general(c_of(g), b_of(g), (((1,), (1,)), ((), ())), preferred_element_type=F32)
              for g in gs}
        y_cs = {g: _dot(c_of(g), state[g].astype(BF16)) for g in gs}
        a_e = {g: jnp.concatenate(
            [jnp.where(first_half, col4[g][:, 2 * p * LANES:(2 * p + 1) * LANES],
                       col4[g][:, (2 * p + 1) * LANES:(2 * p + 2) * LANES]) for p in range(2)], axis=1)
            for g in gs}
        xdt = {g: x_of(g) * dt_e[g] for g in gs}
        xdt_b = {g: xdt[g].astype(BF16) for g in gs}
        ys = {g: [None] * SSM_HEADS_PER_GROUP for g in gs}
        for r in range(SSM_HEADS_PER_GROUP):
            for g in gs:
                row = act_s[SSM_HEADS_PER_GROUP * g + r:SSM_HEADS_PER_GROUP * g + r + 1, :]
                seg = col4[g][:, LANES * r:LANES * (r + 1)] - row
                dec = jnp.where(causal, jnp.exp2(seg), 0.0)
                xp = xdt_b[g][:, LANES * (r // 2):LANES * (r // 2 + 1)]
                ys[g][r] = _dot((cb[g] * dec).astype(BF16), xp)
        for g in gs:
            y_diag = jnp.concatenate([jnp.where(first_half, ys[g][2 * p], ys[g][2 * p + 1])
                                      for p in range(2)], axis=1)
            al_e = a_e[g][L - 1:L, :]
            new = lax.dot_general(b_of(g), (xdt[g] * jnp.exp2(al_e - a_e[g])).astype(BF16),
                                  (((0,), (0,)), ((), ())), preferred_element_type=F32)
            y = y_diag + y_cs[g] * jnp.exp2(a_e[g]) + x_of(g) * dskip_ref[g]
            state[g] = state[g] * jnp.exp2(al_e) + new
            yg = y * _silu(z_ref[:, g * W:(g + 1) * W])
            ms = jnp.mean(yg * yg, axis=-1, keepdims=True)
            o_ref[:, g * W:(g + 1) * W] = (yg * lax.rsqrt(ms + NORM_EPS)
                                           * nw_ref[:, g * W:(g + 1) * W]).astype(o_ref.dtype)

    for g0 in range(0, G, SSD_GROUP_BATCH):
        run_groups(range(g0, g0 + SSD_GROUP_BATCH))


def _head_selectors():
    G, R = SSM_N_GROUPS, SSM_HEADS_PER_GROUP
    h = jnp.arange(LANES)[None, :, None]
    g = jnp.arange(G)[:, None, None]
    j64 = jnp.arange(SSM_GROUP_WIDTH)[None, None, :]
    j128 = jnp.arange(R * LANES)[None, None, :]
    e64 = (h == g * R + j64 // SSM_HEADDIM).astype(BF16)
    e128 = (h == g * R + j128 // LANES).astype(BF16)
    return e64, e128


def _ssd(zx, x2, w_dt, conv_w, conv_b, dt_bias, a_log, d_skip, norm_w, batch, seq, moe_weights, moe_layer):
    L, G, W, N = SSM_CHUNK, SSM_N_GROUPS, SSM_GROUP_WIDTH, SSM_D_STATE
    nc = seq // L
    cast = _CastPlan(*moe_weights, moe_layer, batch * nc, lambda b, c: b * nc + c)
    t = batch * seq
    pad = LANES - SSM_N_HEADS
    wdt = jnp.pad(w_dt, ((0, 0), (0, pad)))
    dtb = jnp.pad(dt_bias, (0, pad))[None, :]
    alog = jnp.pad(a_log, (0, pad))[None, :]
    dsk = jnp.repeat(d_skip, SSM_HEADDIM).reshape(G, 1, W)
    e64, e128 = _head_selectors()
    conv_dim = conv_w.shape[1]
    row = lambda b, c: (b * nc + c, 0)
    full2 = lambda b, c: (0, 0)
    full3 = lambda b, c: (0, 0, 0)
    yg, *w_bf16 = pl.pallas_call(
        functools.partial(_ssd_kernel, cast=cast),
        grid=(batch, nc),
        in_specs=[pl.BlockSpec((L, SSM_D_INNER), row),
                  pl.BlockSpec((L, SSM_D_INNER), lambda b, c: (b * nc + c, 1)),
                  pl.BlockSpec((L, 2 * G * N), lambda b, c: (b * nc + c, 2)),
                  pl.BlockSpec((L, D_MODEL), row),
                  pl.BlockSpec((D_MODEL, LANES), full2),
                  pl.BlockSpec((SSM_CONV_K, conv_dim), full2),
                  pl.BlockSpec((1, conv_dim), full2),
                  pl.BlockSpec((1, LANES), full2),
                  pl.BlockSpec((1, LANES), full2),
                  pl.BlockSpec((G, 1, W), full3),
                  pl.BlockSpec((1, SSM_D_INNER), full2),
                  pl.BlockSpec((G, LANES, W), full3),
                  pl.BlockSpec((G, LANES, SSM_HEADS_PER_GROUP * LANES), full3)] + cast.in_specs,
        out_specs=[pl.BlockSpec((L, SSM_D_INNER), row)] + cast.out_specs,
        out_shape=[jax.ShapeDtypeStruct((t, SSM_D_INNER), BF16)] + cast.out_shape,
        scratch_shapes=[pltpu.VMEM((CONV_TAIL + L, SSM_D_INNER), F32),
                        pltpu.VMEM((CONV_TAIL + L, 2 * G * N), F32),
                        pltpu.VMEM((G, N, W), F32),
                        pltpu.VMEM((3, L, LANES), BF16),
                        pltpu.VMEM((2, L, LANES), BF16),
                        pltpu.VMEM((LANES, L), F32)],
        compiler_params=_arb(2),
        name="ssd_scan",
    )(zx, zx, zx, x2, wdt, conv_w, conv_b[None, :], dtb, alog, dsk, norm_w[None, :], e64, e128, *cast.weights)
    return yg, tuple(w_bf16)


def _route(h, wr):
    tm = h.shape[0]
    logits = _dot(h.astype(BF16), wr)
    lane_i = lax.broadcasted_iota(I32, (tm, LANES), 1)
    lane = lane_i.astype(F32)
    neg = -jnp.inf
    big = float(LANES)

    def first_argmax(v, vmax):
        return jnp.min(jnp.where(v == vmax, lane, big), axis=-1, keepdims=True)

    gl = jnp.where((lane_i >= MOE_N_EXPERTS) & (lane_i < MOE_N_EXPERTS + MOE_GROUPS), logits, neg)
    gm = jnp.max(gl, axis=-1, keepdims=True)
    g_sel = first_argmax(gl, gm) - float(MOE_N_EXPERTS)
    g_gate = 1.0 / jnp.sum(jnp.exp(gl - gm), axis=-1, keepdims=True)
    lo = g_sel * float(MOE_EXPERTS_PER_GROUP)
    el = jnp.where((lane >= lo) & (lane < lo + float(MOE_EXPERTS_PER_GROUP)), logits, neg)
    m1 = jnp.max(el, axis=-1, keepdims=True)
    i1 = first_argmax(el, m1)
    el2 = jnp.where(lane == i1, neg, el)
    m2 = jnp.max(el2, axis=-1, keepdims=True)
    i2 = first_argmax(el2, m2)
    p2 = jnp.exp(m2 - m1)
    t1 = 1.0 / (1.0 + p2)
    t2 = p2 / (1.0 + p2)
    eid = jnp.where(lane_i == 0, i1, jnp.where(lane_i == 1, i2, 0.0)).astype(I32)
    wt = jnp.where(lane_i == 0, g_gate * t1, jnp.where(lane_i == 1, g_gate * t2, 0.0))
    count = jnp.sum(jnp.where(lane == i1, 1.0, 0.0) + jnp.where(lane == i2, 1.0, 0.0), axis=0, keepdims=True)
    return eid, wt, count


def _router_weights(w_group, w_expert):
    pad = LANES - MOE_N_EXPERTS - MOE_GROUPS
    return jnp.pad(jnp.concatenate([w_expert, w_group], axis=1), ((0, 0), (0, pad))).astype(BF16)


def _store_token_tiles(ref, v, first_token=0):
    rows = v.shape[0]
    for s in range(ROW_SLABS):
        ref[pl.ds(first_token * ROW_SLABS + s, rows, stride=ROW_SLABS), :] = v[:, s * LANES:(s + 1) * LANES]


def _load_token_tiles(ref, rows):
    return jnp.concatenate([ref[pl.ds(s, rows, stride=ROW_SLABS), :] for s in range(ROW_SLABS)], axis=1)


def _mm_ln_router_kernel(x_ref, w_ref, r_ref, g_ref, b_ref, wr_ref, h_ref, ht_ref, eid_ref, wt_ref, cnt_ref):
    @pl.when(pl.program_id(0) == 0)
    def _():
        cnt_ref[...] = jnp.zeros(cnt_ref.shape, F32)

    sub = x_ref.shape[0] // LN_SUBTILES
    rows = [slice(u * sub, (u + 1) * sub) for u in range(LN_SUBTILES)]
    ys = [DEEPNORM_ALPHA * r_ref[rs, :] + _dot(x_ref[rs, :].astype(BF16), w_ref[...]) for rs in rows]
    hs = [_layer_norm(y, g_ref[...], b_ref[...]) for y in ys]
    routes = [_route(h, wr_ref[...]) for h in hs]
    for u, rs in enumerate(rows):
        h_ref[rs, :] = hs[u]
        _store_token_tiles(ht_ref, hs[u], first_token=u * sub)
        eid_ref[rs, :] = routes[u][0]
        wt_ref[rs, :] = routes[u][1]
    cnt_ref[...] += jnp.broadcast_to(sum(r[2] for r in routes), cnt_ref.shape)


def _mm_ln_router(x, w, resid, g, b, wr, name):
    m, k = x.shape
    d = w.shape[1]
    tm = min(MM_TM, m)
    row = lambda i: (i, 0)
    full = lambda i: (0, 0)
    return pl.pallas_call(
        _mm_ln_router_kernel,
        grid=(m // tm,),
        in_specs=[pl.BlockSpec((tm, k), row), pl.BlockSpec((k, d), full), pl.BlockSpec((tm, d), row),
                  pl.BlockSpec((1, d), full), pl.BlockSpec((1, d), full), pl.BlockSpec((d, LANES), full)],
        out_specs=[pl.BlockSpec((tm, d), row), pl.BlockSpec((tm * ROW_SLABS, LANES), row),
                   pl.BlockSpec((tm, LANES), row), pl.BlockSpec((tm, LANES), row),
                   pl.BlockSpec((8, LANES), full)],
        out_shape=[jax.ShapeDtypeStruct((m, d), F32), jax.ShapeDtypeStruct((m * ROW_SLABS, LANES), F32),
                   jax.ShapeDtypeStruct((m, LANES), I32), jax.ShapeDtypeStruct((m, LANES), F32),
                   jax.ShapeDtypeStruct((8, LANES), F32)],
        compiler_params=_arb(1),
        name=name,
    )(x, w, resid, g[None, :], b[None, :], wr)


def _dispatch_kernel(eid_ref, cnt_ref, h_ref, dest_ref, be_ref, nu_ref, xs_hbm,
                     base, upper, zbuf, dst_v, dst_s, pe_v, pe_s, sem_z, sem_r, sem_s, *, blk):
    i = pl.program_id(0)
    tm = eid_ref.shape[0]
    n_e = LANES
    eid_t = eid_ref[...].astype(F32).T
    sub = lax.broadcasted_iota(I32, (n_e, tm), 0).astype(F32)
    oh = [(sub == eid_t[k:k + 1, :]).astype(F32) for k in range(MOE_TOP_K)]
    tot = [jnp.sum(o, axis=1, keepdims=True) for o in oh]

    @pl.when(i == 0)
    def _():
        r_i = lax.broadcasted_iota(I32, (tm, tm), 0)
        c_i = lax.broadcasted_iota(I32, (tm, tm), 1)
        upper[...] = (r_i < c_i).astype(BF16)
        counts = jnp.broadcast_to(cnt_ref[0:1, :], (n_e, n_e)).T
        padded = jnp.floor((counts + float(blk - 1)) * (1.0 / blk)) * float(blk)
        r_i = lax.broadcasted_iota(I32, (n_e, n_e), 0)
        c_i = lax.broadcasted_iota(I32, (n_e, n_e), 1)
        tril = (r_i >= c_i).astype(BF16)
        p1, p2, p3 = _split3(padded)
        pends = _dot(tril, p1) + _dot(tril, p2) + _dot(tril, p3)
        base[...] = pends - padded
        nbp = be_ref.shape[1]
        blk_start = lax.broadcasted_iota(I32, (n_e, nbp), 1).astype(F32) * float(blk)
        is_e = lax.broadcasted_iota(I32, (n_e, nbp), 0) < MOE_N_EXPERTS
        done = jnp.where(is_e & (jnp.tile(pends, (1, nbp // LANES)) <= blk_start), 1.0, 0.0)
        be = jnp.minimum(jnp.sum(done, axis=0, keepdims=True), float(MOE_N_EXPERTS - 1))
        be_ref[...] = jnp.broadcast_to(be, be_ref.shape).astype(I32)
        last = pends[MOE_N_EXPERTS - 1:MOE_N_EXPERTS, :] * (1.0 / blk)
        nu_ref[...] = jnp.broadcast_to(last, nu_ref.shape).astype(I32)
        zbuf[...] = jnp.zeros(zbuf.shape, F32)
        row8 = lax.broadcasted_iota(I32, (8, LANES), 0)
        pe_v[...] = jnp.where(row8 == 0, pends.T[0:8, :], counts.T[0:8, :]).astype(I32)
        cp = pltpu.make_async_copy(pe_v, pe_s, sem_s)
        cp.start()
        cp.wait()

        brows = blk * ROW_SLABS

        def zero_copy(e):
            start = pl.multiple_of((pe_s[0, e] - blk) * ROW_SLABS, brows)
            return pltpu.make_async_copy(zbuf, xs_hbm.at[pl.ds(start, brows)], sem_z)

        def tail_copy(b):
            return pltpu.make_async_copy(zbuf, xs_hbm.at[pl.ds(pl.multiple_of(b * brows, brows), brows)], sem_z)

        n_used = lax.shift_right_logical(pe_s[0, MOE_N_EXPERTS - 1], blk.bit_length() - 1)
        n_blocks = xs_hbm.shape[0] // brows
        for e in range(MOE_N_EXPERTS):
            @pl.when(pe_s[1, e] > 0)
            def _():
                zero_copy(e).start()
        lax.fori_loop(n_used, n_blocks, lambda b, c: (tail_copy(b).start(), c)[1], 0)
        for e in range(MOE_N_EXPERTS):
            @pl.when(pe_s[1, e] > 0)
            def _():
                zero_copy(e).wait()
        lax.fori_loop(n_used, n_blocks, lambda b, c: (tail_copy(b).wait(), c)[1], 0)

    b0 = base[:, 0:1]
    c0 = _dot(oh[0].astype(BF16), upper[...])
    c1 = _dot(oh[1].astype(BF16), upper[...])
    d0 = jnp.sum(oh[0] * (b0 + c0), axis=0, keepdims=True)
    d1 = jnp.sum(oh[1] * (b0 + tot[0] + c1), axis=0, keepdims=True)
    base[...] += jnp.broadcast_to(tot[0] + tot[1], base.shape)
    row8 = lax.broadcasted_iota(I32, (8, tm), 0)
    dst = jnp.where(row8 == 0, d0, jnp.where(row8 == 1, d1, 0.0)).astype(I32)
    dest_ref[...] = dst
    dst_v[...] = dst
    cp = pltpu.make_async_copy(dst_v, dst_s, sem_s)
    cp.start()
    cp.wait()

    for r in range(tm):
        for k in range(MOE_TOP_K):
            slot = pl.multiple_of(dst_s[k, r] * ROW_SLABS, ROW_SLABS)
            pltpu.make_async_copy(h_ref.at[pl.ds(r * ROW_SLABS, ROW_SLABS)],
                                  xs_hbm.at[pl.ds(slot, ROW_SLABS)], sem_r).start(priority=k)
    for k in range(MOE_TOP_K):
        pltpu.make_async_copy(h_ref, xs_hbm.at[pl.ds(0, tm * ROW_SLABS)], sem_r).wait()


def _moe_dispatch(ht, eid, counts, blk, name):
    t = ht.shape[0] // ROW_SLABS
    assert blk & (blk - 1) == 0, "block size must be a power of two"
    tm = min(DSP_TM, t)
    nt = t // tm
    n_blocks = -(-(t * MOE_TOP_K) // blk) + MOE_N_EXPERTS
    nbp = -(-n_blocks // LANES) * LANES
    full = lambda i: (0, 0)
    dest, be, nu, xs = pl.pallas_call(
        functools.partial(_dispatch_kernel, blk=blk),
        grid=(nt,),
        in_specs=[pl.BlockSpec((tm, LANES), lambda i: (i, 0)),
                  pl.BlockSpec((8, LANES), full),
                  pl.BlockSpec((tm * ROW_SLABS, LANES), lambda i: (i, 0))],
        out_specs=[pl.BlockSpec((8, tm), lambda i: (0, i)),
                   pl.BlockSpec((8, nbp), full),
                   pl.BlockSpec((8, LANES), full),
                   pl.BlockSpec(memory_space=pl.ANY)],
        out_shape=[jax.ShapeDtypeStruct((8, t), I32), jax.ShapeDtypeStruct((8, nbp), I32),
                   jax.ShapeDtypeStruct((8, LANES), I32),
                   jax.ShapeDtypeStruct((n_blocks * blk * ROW_SLABS, LANES), F32)],
        scratch_shapes=[pltpu.VMEM((LANES, LANES), F32),
                        pltpu.VMEM((tm, tm), BF16), pltpu.VMEM((blk * ROW_SLABS, LANES), F32),
                        pltpu.VMEM((8, tm), I32), pltpu.SMEM((8, tm), I32),
                        pltpu.VMEM((8, LANES), I32), pltpu.SMEM((8, LANES), I32),
                        pltpu.SemaphoreType.DMA(()), pltpu.SemaphoreType.DMA(()), pltpu.SemaphoreType.DMA(())],
        compiler_params=_arb(1),
        name=name,
    )(eid, counts, ht)
    return dest, be[0, :n_blocks], nu[0, :1], xs, n_blocks


def _ffn_kernel(be_ref, nu_ref, x_ref, wg_ref, wu_ref, wd_ref, o_ref):
    i = pl.program_id(0)

    @pl.when(i < nu_ref[0])
    def _():
        blk = x_ref.shape[0] // ROW_SLABS
        xb = _load_token_tiles(x_ref, blk).astype(BF16)
        hid = _silu(_dot(xb, wg_ref[0])) * _dot(xb, wu_ref[0])
        _store_token_tiles(o_ref, _dot(hid.astype(BF16), wd_ref[0]))

    @pl.when(i >= nu_ref[0])
    def _():
        o_ref[...] = jnp.zeros(o_ref.shape, o_ref.dtype)


def _moe_ffn(xs, block_expert, n_used, n_blocks, weights, blk, name):
    w_gate, w_up, w_down = weights
    d, f = w_gate.shape[1], w_gate.shape[2]
    brows = blk * ROW_SLABS
    used = lambda i, be, nu: (jnp.minimum(i, nu[0] - 1), 0)
    every = lambda i, be, nu: (i, 0)
    expert = lambda i, be, nu: (be[i], 0, 0)
    grid_spec = pltpu.PrefetchScalarGridSpec(
        num_scalar_prefetch=2,
        grid=(n_blocks,),
        in_specs=[pl.BlockSpec((brows, LANES), used),
                  pl.BlockSpec((1, d, f), expert), pl.BlockSpec((1, d, f), expert),
                  pl.BlockSpec((1, f, d), expert)],
        out_specs=pl.BlockSpec((brows, LANES), every))
    return pl.pallas_call(
        _ffn_kernel,
        grid_spec=grid_spec,
        out_shape=jax.ShapeDtypeStruct((n_blocks * brows, LANES), F32),
        compiler_params=_arb(1),
        name=name,
    )(block_expert, n_used, xs, w_gate, w_up, w_down)


def _combine_ln_kernel(dst_ref, dstn_ref, h_ref, wt_ref, g_ref, b_ref, yb_hbm, o_ref, ybuf, sem):
    i = pl.program_id(0)
    n = pl.num_programs(0)
    tm = h_ref.shape[0]

    def start_gather(dref, slot):
        for r in range(tm):
            for k in range(MOE_TOP_K):
                src = pl.multiple_of(dref[k, r] * ROW_SLABS, ROW_SLABS)
                pltpu.make_async_copy(yb_hbm.at[pl.ds(src, ROW_SLABS)],
                                      ybuf.at[slot, k, pl.ds(r * ROW_SLABS, ROW_SLABS)],
                                      sem.at[slot]).start(priority=k)

    @pl.when(i == 0)
    def _():
        start_gather(dst_ref, 0)

    @pl.when(i + 1 < n)
    def _():
        start_gather(dstn_ref, (i + 1) % 2)

    slot = i % 2
    for k in range(MOE_TOP_K):
        pltpu.make_async_copy(yb_hbm.at[pl.ds(0, tm * ROW_SLABS)], ybuf.at[slot, k], sem.at[slot]).wait()
    wt = wt_ref[...]
    ffn = (wt[:, 0:1] * _load_token_tiles(ybuf.at[slot, 0], tm)
           + wt[:, 1:2] * _load_token_tiles(ybuf.at[slot, 1], tm))
    o_ref[...] = _layer_norm(DEEPNORM_ALPHA * h_ref[...] + ffn, g_ref[...], b_ref[...])


def _combine_ln(h, wt, dest, yb, g, b, name):
    t, d = h.shape
    tm = min(CMB_TM, t)
    nblk = t // tm
    row = lambda i: (i, 0)
    full = lambda i: (0, 0)
    return pl.pallas_call(
        _combine_ln_kernel,
        grid=(nblk,),
        in_specs=[pl.BlockSpec((8, tm), lambda i: (0, i), memory_space=pltpu.SMEM),
                  pl.BlockSpec((8, tm), lambda i: (0, jnp.minimum(i + 1, nblk - 1)),
                               memory_space=pltpu.SMEM),
                  pl.BlockSpec((tm, d), row), pl.BlockSpec((tm, LANES), row),
                  pl.BlockSpec((1, d), full), pl.BlockSpec((1, d), full),
                  pl.BlockSpec(memory_space=pl.ANY)],
        out_specs=pl.BlockSpec((tm, d), row),
        out_shape=jax.ShapeDtypeStruct((t, d), F32),
        scratch_shapes=[pltpu.VMEM((2, MOE_TOP_K, tm * ROW_SLABS, LANES), F32), pltpu.SemaphoreType.DMA((2,))],
        compiler_params=_arb(1),
        name=name,
    )(dest, dest, h, wt, g[None, :], b[None, :], yb)


def _hier_moe_ln(h, ht, eid, wt, counts, w_bf16, g, b, layer):
    dest, block_expert, n_used, xs, n_blocks = _moe_dispatch(ht, eid, counts, FFN_BLK, f"moe_dispatch{layer}")
    yb = _moe_ffn(xs, block_expert, n_used, n_blocks, w_bf16, FFN_BLK, f"moe_ffn{layer}")
    return _combine_ln(h, wt, dest, yb, g, b, f"moe_combine_ln{layer}")


def _qkv_rope_kernel(x_ref, w_ref, pos_ref, inv_ref, o_ref):
    xb = x_ref[...].astype(BF16)
    tm = xb.shape[0]
    n = D_MODEL
    ang = pos_ref[...].astype(F32) * inv_ref[...]
    lane = lax.broadcasted_iota(I32, (tm, LANES), 1)
    dd = lane & (ATTN_HEAD_DIM - 1)
    half = ROT_DIM // 2
    cosv = jnp.cos(ang)
    sinv = jnp.sin(ang)
    c_t = jnp.where(dd < ROT_DIM, cosv, 1.0)
    s_up = jnp.where(dd < half, -sinv, 0.0)
    s_dn = jnp.where((dd >= half) & (dd < ROT_DIM), sinv, 0.0)
    for j, sc in ((0, ATTN_HEAD_DIM ** -0.5 * math.log2(math.e)), (1, 1.0)):
        acc = _dot(xb, w_ref[:, j * n:(j + 1) * n])
        c_j, up_j, dn_j = c_t * sc, s_up * sc, s_dn * sc
        for blk in range(n // LANES):
            tt = acc[:, blk * LANES:(blk + 1) * LANES]
            out = tt * c_j + pltpu.roll(tt, LANES - half, 1) * up_j + pltpu.roll(tt, half, 1) * dn_j
            o_ref[:, j * n + blk * LANES:j * n + (blk + 1) * LANES] = out.astype(o_ref.dtype)
    o_ref[:, 2 * n:3 * n] = _dot(xb, w_ref[:, 2 * n:3 * n]).astype(o_ref.dtype)


def _rope_inv_table():
    inv = ROPE_THETA ** (-jnp.arange(0, ROT_DIM, 2, dtype=F32) / ROT_DIM)
    head = jnp.concatenate([inv, inv, jnp.zeros((ATTN_HEAD_DIM - ROT_DIM,), F32)])
    return jnp.tile(head, LANES // ATTN_HEAD_DIM)[None, :]


def _qkv_rope(h, w_qkv, positions):
    m, k = h.shape
    n = w_qkv.shape[1]
    tm = min(MM_TM, m)
    pos = positions.reshape(m, 1)
    return pl.pallas_call(
        _qkv_rope_kernel,
        grid=(m // tm,),
        in_specs=[pl.BlockSpec((tm, k), lambda i: (i, 0)),
                  pl.BlockSpec((k, n), lambda i: (0, 0)),
                  pl.BlockSpec((tm, 1), lambda i: (i, 0)),
                  pl.BlockSpec((1, LANES), lambda i: (0, 0))],
        out_specs=pl.BlockSpec((tm, n), lambda i: (i, 0)),
        out_shape=jax.ShapeDtypeStruct((m, n), BF16),
        compiler_params=_arb(1),
        name="mm_qkv_rope",
    )(h, w_qkv, pos, _rope_inv_table())


def _attn_kernel(q_ref, k_ref, v_ref, lq1_ref, lk1_ref, lq2_ref, lk2_ref, sw_ref, wg_ref, wu_ref, wd_ref,
                 o_ref, wgb_ref, wub_ref, wdb_ref,
                 vx_ref, m0_ref, m1_ref, acc0_ref, acc1_ref, *, lambda_init, cast):
    m_refs = (m0_ref, m1_ref)
    acc_refs = (acc0_ref, acc1_ref)
    jp = pl.program_id(2)
    tq = q_ref.shape[0] // 2
    cast.emit(pl.program_id(0) * pl.num_programs(1) + pl.program_id(1),
              (wg_ref, wu_ref, wd_ref), (wgb_ref, wub_ref, wdb_ref), extra=jp == 0)

    @pl.when(jp == 0)
    def _():
        vx_ref[:, 0:LANES] = v_ref[...]
        vx_ref[:, LANES:] = jnp.ones((vx_ref.shape[0], LANES), vx_ref.dtype)

    q = q_ref[...]
    lane = lax.broadcasted_iota(I32, q.shape, 1)
    zero = jnp.zeros((), q.dtype)
    qs = (jnp.where(lane < ATTN_HEAD_DIM, q, zero), jnp.where(lane >= ATTN_HEAD_DIM, q, zero))
    for c in range(2):
        m_refs[c][...] = jnp.full(m_refs[c].shape, -jnp.inf, F32)
        acc_refs[c][...] = jnp.zeros(acc_refs[c].shape, F32)

    def step(off, windows):
        off = pl.multiple_of(off, tq)
        widest = max(w for _, w, _ in windows)
        kb = k_ref[pl.ds(off, widest), :]
        vb = vx_ref[pl.ds(off, widest), :]
        chains = [(slice(r0, r0 + tq), w, d, c) for r0, w, d in windows for c in range(2)]
        ss = [lax.dot_general(qs[c][rows, :], kb[:w, :], (((1,), (1,)), ((), ())), preferred_element_type=F32)
              for rows, w, _, c in chains]
        for n, (_, w, d, _) in enumerate(chains):
            if d is not None:
                row_i = lax.broadcasted_iota(I32, (tq, w), 0)
                col_i = lax.broadcasted_iota(I32, (tq, w), 1)
                ss[n] = jnp.where(row_i + d >= col_i, ss[n], -jnp.inf)
        m_prev = [m_refs[c][rows, :] for rows, _, _, c in chains]
        mn = [jnp.maximum(mp, jnp.max(s, axis=-1, keepdims=True)) for mp, s in zip(m_prev, ss)]
        ps = [jnp.exp2(s - jnp.tile(m, (1, s.shape[1] // LANES))).astype(BF16) for s, m in zip(ss, mn)]
        for n, (rows, w, _, c) in enumerate(chains):
            alpha = jnp.exp2(m_prev[n] - mn[n])
            acc_refs[c][rows, :] = jnp.tile(alpha, (1, 2)) * acc_refs[c][rows, :] + _dot(ps[n], vb[:w, :])
            m_refs[c][rows, :] = mn[n]

    both = [(0, 2 * tq, None), (tq, 2 * tq, None)]
    lax.fori_loop(0, jp, lambda j, carry: (step(j * (2 * tq), both), carry)[1], 0)
    step(jp * (2 * tq), [(0, tq, 0), (tq, 2 * tq, tq)])

    lam = (jnp.exp(jnp.sum(lq1_ref[...] * lk1_ref[...], axis=-1, keepdims=True))
           - jnp.exp(jnp.sum(lq2_ref[...] * lk2_ref[...], axis=-1, keepdims=True)) + lambda_init)
    a1 = acc0_ref[...]
    a2 = acc1_ref[...]
    o = a1[:, :LANES] / a1[:, LANES:] - lam * (a2[:, :LANES] / a2[:, LANES:])
    o = o * lax.rsqrt(jnp.mean(o * o, axis=-1, keepdims=True) + NORM_EPS)
    o_ref[...] = (o * sw_ref[...] * (1.0 - lambda_init)).astype(o_ref.dtype)


def _diff_attention(qkv, lq1, lk1, lq2, lk2, subln_w, lambda_init, batch, seq, moe_weights, moe_layer):
    t = batch * seq
    tq = 2 * min(ATT_TQ, seq // 2)
    nq = seq // tq
    h_n = ATTN_N_HEADS
    vec = lambda b, h, i: (0, 0)
    cast = _CastPlan(*moe_weights, moe_layer, batch * h_n, lambda b, h, i: b * h_n + h)
    o, *w_bf16 = pl.pallas_call(
        functools.partial(_attn_kernel, lambda_init=lambda_init, cast=cast),
        grid=(batch, h_n, nq),
        in_specs=[pl.BlockSpec((tq, LANES), lambda b, h, i: (b * nq + i, h)),
                  pl.BlockSpec((seq, LANES), lambda b, h, i: (b, h_n + h)),
                  pl.BlockSpec((seq, LANES), lambda b, h, i: (b, 2 * h_n + h)),
                  pl.BlockSpec((1, ATTN_HEAD_DIM), vec), pl.BlockSpec((1, ATTN_HEAD_DIM), vec),
                  pl.BlockSpec((1, ATTN_HEAD_DIM), vec), pl.BlockSpec((1, ATTN_HEAD_DIM), vec),
                  pl.BlockSpec((1, ATTN_V_DIM), vec)] + cast.in_specs,
        out_specs=[pl.BlockSpec((tq, LANES), lambda b, h, i: (b * nq + i, h))] + cast.out_specs,
        out_shape=[jax.ShapeDtypeStruct((t, h_n * ATTN_V_DIM), BF16)] + cast.out_shape,
        scratch_shapes=[pltpu.VMEM((seq, 2 * LANES), BF16),
                        pltpu.VMEM((tq, LANES), F32), pltpu.VMEM((tq, LANES), F32),
                        pltpu.VMEM((tq, 2 * LANES), F32), pltpu.VMEM((tq, 2 * LANES), F32)],
        compiler_params=_arb(3),
        name="diff_attn",
    )(qkv, qkv, qkv, lq1[None, :], lk1[None, :], lq2[None, :], lk2[None, :], subln_w[None, :], *cast.weights)
    return o, tuple(w_bf16)


def kernel(x, positions, ln_mix_g, ln_mix_b, ln_ffn_g, ln_ffn_b, ssm_w_in, ssm_conv_w, ssm_conv_b, ssm_dt_bias, ssm_a_log, ssm_d, ssm_norm_w, ssm_w_out, attn_w_qkv, attn_lam_q1, attn_lam_k1, attn_lam_q2, attn_lam_k2, attn_subln_w, attn_w_o, moe_w_group, moe_w_expert, moe_w_gate, moe_w_up, moe_w_down):
    batch, seq, d = x.shape
    t = batch * seq
    h = x.reshape(t, d)

    w_in = ssm_w_in[0].astype(BF16)

    moe_weights = (moe_w_gate, moe_w_up, moe_w_down)
    zx = _matmul(h, w_in, SSM_ZX_DIM, F32)
    yg, moe_w0 = _ssd(zx, h, w_in[:, SSM_ZX_DIM:], ssm_conv_w[0], ssm_conv_b[0], ssm_dt_bias[0],
                      ssm_a_log[0], ssm_d[0], ssm_norm_w[0], batch, seq, moe_weights, 0)
    h, ht, eid, wt, cnt = _mm_ln_router(yg, ssm_w_out[0].astype(BF16), h, ln_mix_g[0], ln_mix_b[0],
                                        _router_weights(moe_w_group[0], moe_w_expert[0]), "mm_ssm_out_ln_router")
    h = _hier_moe_ln(h, ht, eid, wt, cnt, moe_w0, ln_ffn_g[0], ln_ffn_b[0], 0)

    lambda_init = 0.8 - 0.6 * math.exp(-0.3 * 1)
    qkv = _qkv_rope(h, attn_w_qkv[0].astype(BF16), positions)
    o, moe_w1 = _diff_attention(qkv, attn_lam_q1[0], attn_lam_k1[0], attn_lam_q2[0], attn_lam_k2[0],
                                attn_subln_w[0], lambda_init, batch, seq, moe_weights, 1)
    h, ht, eid, wt, cnt = _mm_ln_router(o, attn_w_o[0].astype(BF16), h, ln_mix_g[1], ln_mix_b[1],
                                        _router_weights(moe_w_group[1], moe_w_expert[1]), "mm_attn_out_ln_router")
    h = _hier_moe_ln(h, ht, eid, wt, cnt, moe_w1, ln_ffn_g[1], ln_ffn_b[1], 1)
    return h.reshape(batch, seq, d)
```

```python
import functools
import math

import jax
import jax.numpy as jnp
from jax import lax
from jax.experimental import pallas as pl
from jax.experimental.pallas import tpu as pltpu

F32 = jnp.float32
BF16 = jnp.bfloat16
I32 = jnp.int32

D_MODEL = 1024
DEPTH = 2
SSM_D_INNER = 2048
SSM_HEADDIM = 64
SSM_N_HEADS = 32
SSM_N_GROUPS = 8
SSM_HEADS_PER_GROUP = 4
SSM_D_STATE = 128
SSM_CONV_K = 4
SSM_CHUNK = 128
SSM_GROUP_WIDTH = SSM_HEADS_PER_GROUP * SSM_HEADDIM
SSM_ZX_DIM = 2 * SSM_D_INNER + 2 * SSM_N_GROUPS * SSM_D_STATE
ATTN_HEAD_DIM = 64
ATTN_N_HEADS = 8
ATTN_V_DIM = 128
ROT_DIM = 16
ROPE_THETA = 500000.0
MOE_GROUPS = 4
MOE_EXPERTS_PER_GROUP = 8
MOE_N_EXPERTS = 32
MOE_TOP_K = 2
MOE_D_FF = 512
DEEPNORM_ALPHA = (2 * DEPTH) ** 0.25
NORM_EPS = 1e-5

LANES = 128
ROW_SLABS = D_MODEL // LANES
CONV_TAIL = 8
CONV_ROWS = 128

MM_TM = 512
MM_TM_WIDE = 256
MM_TN = 1024
FFN_BLK = 512
DSP_TM = 512
SSD_GROUP_BATCH = 2
LN_SUBTILES = 4
CMB_TM = 256
ATT_TQ = 512
ATT_TILES_PER_STEP = 2


def _arb(n):
    return pltpu.CompilerParams(dimension_semantics=("arbitrary",) * n,
                                vmem_limit_bytes=56 * 1024 * 1024)


def _silu(x):
    hx = 0.5 * x
    return hx + hx * jnp.tanh(hx)


def _softplus(x):
    return jnp.maximum(x, 0.0) + jnp.log(1.0 + jnp.exp(-jnp.abs(x)))


def _layer_norm(y, g, b):
    mu = jnp.mean(y, axis=-1, keepdims=True)
    d = y - mu
    var = jnp.mean(d * d, axis=-1, keepdims=True)
    return d * lax.rsqrt(var + NORM_EPS) * g + b


def _split3(a):
    a1 = a.astype(BF16)
    r1 = a - a1.astype(F32)
    a2 = r1.astype(BF16)
    a3 = (r1 - a2.astype(F32)).astype(BF16)
    return a1, a2, a3


def _dot(a, b):
    return jnp.dot(a, b, preferred_element_type=F32)


def _dot_sel(a, sel):
    a1, a2, a3 = _split3(a)
    return _dot(a1, sel) + _dot(a2, sel) + _dot(a3, sel)


def _dot_f32(a, b):
    a1, a2, a3 = _split3(a)
    b1, b2, b3 = _split3(b)
    return (_dot(a1, b1) + _dot(a1, b2) + _dot(a2, b1)
            + _dot(a2, b2) + _dot(a1, b3) + _dot(a3, b1))


class _CastPlan:
    def __init__(self, w_gate, w_up, w_down, layer, steps, step_of):
        self.weights = (w_gate, w_up, w_down)
        self.n_e = w_gate.shape[1]
        self.eps = -(-self.n_e // steps)
        assert self.n_e % self.eps == 0
        self.n_cast = self.n_e // self.eps
        self.stride = steps // self.n_cast
        block = lambda *ids: jnp.minimum(step_of(*ids) // self.stride, self.n_cast - 1)
        self.in_specs = [pl.BlockSpec((1, self.eps) + w.shape[2:], lambda *ids: (layer, block(*ids), 0, 0))
                         for w in self.weights]
        self.out_specs = [pl.BlockSpec((self.eps,) + w.shape[2:], lambda *ids: (block(*ids), 0, 0))
                          for w in self.weights]
        self.out_shape = [jax.ShapeDtypeStruct(w.shape[1:], BF16) for w in self.weights]

    def emit(self, step, w_refs, o_refs, extra=True):
        @pl.when(extra & (step % self.stride == 0) & (step // self.stride < self.n_cast))
        def _():
            for w_ref, o_ref in zip(w_refs, o_refs):
                o_ref[...] = w_ref[0].astype(BF16)


def _mm_kernel(x_ref, w_ref, o_ref):
    xb = x_ref[...].astype(BF16)
    for j in range(o_ref.shape[1] // MM_TN):
        cols = slice(j * MM_TN, (j + 1) * MM_TN)
        o_ref[:, cols] = _dot(xb, w_ref[:, cols]).astype(o_ref.dtype)


def _matmul(x, w, n, out_dtype):
    m, k = x.shape
    tm = min(MM_TM_WIDE, m)
    return pl.pallas_call(
        _mm_kernel,
        grid=(m // tm,),
        in_specs=[pl.BlockSpec((tm, k), lambda i: (i, 0)), pl.BlockSpec((k, n), lambda i: (0, 0))],
        out_specs=pl.BlockSpec((tm, n), lambda i: (i, 0)),
        out_shape=jax.ShapeDtypeStruct((m, n), out_dtype),
        compiler_params=_arb(1),
        name="mm_inproj",
    )(x, w)


def _ssd_kernel(z_ref, xs_ref, bc_ref, x_ref, wdt_ref, cw_ref, cb_ref, dtb_ref, alog_ref, dskip_ref,
                nw_ref, e64_ref, e128_ref, wg_ref, wu_ref, wd_ref, o_ref, wgb_ref, wub_ref, wdb_ref,
                stage_x, stage_bc, state, acp_s, dtp_s, act_s, *, cast):
    G, W, N = SSM_N_GROUPS, SSM_GROUP_WIDTH, SSM_D_STATE
    L = z_ref.shape[0]
    c = pl.program_id(1)
    cast.emit(pl.program_id(0) * pl.num_programs(1) + c, (wg_ref, wu_ref, wd_ref), (wgb_ref, wub_ref, wdb_ref))

    @pl.when(c == 0)
    def _():
        state[...] = jnp.zeros(state.shape, F32)
        for stage in (stage_x, stage_bc):
            stage[0:CONV_TAIL, :] = jnp.zeros((CONV_TAIL, stage.shape[1]), F32)

    @pl.when(c != 0)
    def _():
        for stage in (stage_x, stage_bc):
            stage[0:CONV_TAIL, :] = stage[L:L + CONV_TAIL, :]

    stage_x[CONV_TAIL:CONV_TAIL + L, :] = xs_ref[...]
    stage_bc[CONV_TAIL:CONV_TAIL + L, :] = bc_ref[...]

    def conv(stage, c0, width, w0):
        taps = [cw_ref[k:k + 1, w0:w0 + width] for k in range(SSM_CONV_K)]
        bias = cb_ref[:, w0:w0 + width]
        outs = []
        for rb in range(L // CONV_ROWS):
            r0 = rb * CONV_ROWS
            u = stage[r0:r0 + CONV_ROWS + CONV_TAIL, c0:c0 + width]
            out = bias + taps[SSM_CONV_K - 1] * u[CONV_TAIL:, :]
            for k in range(SSM_CONV_K - 1):
                back = SSM_CONV_K - 1 - k
                out = out + taps[k] * pltpu.roll(u, back, 0)[CONV_TAIL:, :]
            outs.append(_silu(out))
        return jnp.concatenate(outs, axis=0)

    xc, bcv, ccv = {}, {}, {}

    def conv_groups(gs):
        for g in gs:
            xc[g] = conv(stage_x, g * W, W, g * W)
            bcv[g] = conv(stage_bc, g * N, N, SSM_D_INNER + g * N).astype(BF16)
            ccv[g] = conv(stage_bc, G * N + g * N, N, SSM_D_INNER + G * N + g * N).astype(BF16)

    x_of, b_of, c_of = xc.__getitem__, bcv.__getitem__, ccv.__getitem__

    dt = _softplus(_dot(x_ref[...].astype(BF16), wdt_ref[...].astype(BF16)) + dtb_ref[...])
    a = -jnp.exp(alog_ref[...])
    row_i = lax.broadcasted_iota(I32, (L, L), 0)
    col_i = lax.broadcasted_iota(I32, (L, L), 1)
    causal = row_i >= col_i
    a1, a2, a3 = _split3(dt * a)
    tri = causal.astype(BF16)
    ac = (_dot(tri, a1) + _dot(tri, a2) + _dot(tri, a3)) * math.log2(math.e)
    act_s[...] = ac.T
    for n, part in enumerate(_split3(ac)):
        acp_s[n] = part
    for n, part in enumerate(_split3(dt)[:2]):
        dtp_s[n] = part
    lane = lax.broadcasted_iota(I32, (L, LANES), 1)
    first_half = lane < SSM_HEADDIM

    def select(parts_ref, n_parts, sel):
        out = _dot(parts_ref[0], sel)
        for n in range(1, n_parts):
            out = out + _dot(parts_ref[n], sel)
        return out

    def run_groups(gs):
        conv_groups(gs)
        col4 = {g: select(acp_s, 3, e128_ref[g]) for g in gs}
        dt_e = {g: select(dtp_s, 2, e64_ref[g]) for g in gs}
        cb = {g: lax.dot_general(c_of(g), b_of(g), (((1,), (1,)), ((), ())), preferred_element_type=F32)
              for g in gs}
        y_cs = {g: _dot(c_of(g), state[g].astype(BF16)) for g in gs}
        a_e = {g: jnp.concatenate(
            [jnp.where(first_half, col4[g][:, 2 * p * LANES:(2 * p + 1) * LANES],
                       col4[g][:, (2 * p + 1) * LANES:(2 * p + 2) * LANES]) for p in range(2)], axis=1)
            for g in gs}
        xdt = {g: x_of(g) * dt_e[g] for g in gs}
        xdt_b = {g: xdt[g].astype(BF16) for g in gs}
        ys = {g: [None] * SSM_HEADS_PER_GROUP for g in gs}
        for r in range(SSM_HEADS_PER_GROUP):
            for g in gs:
                row = act_s[SSM_HEADS_PER_GROUP * g + r:SSM_HEADS_PER_GROUP * g + r + 1, :]
                seg = col4[g][:, LANES * r:LANES * (r + 1)] - row
                dec = jnp.where(causal, jnp.exp2(seg), 0.0)
                xp = xdt_b[g][:, LANES * (r // 2):LANES * (r // 2 + 1)]
                ys[g][r] = _dot((cb[g] * dec).astype(BF16), xp)
        for g in gs:
            y_diag = jnp.concatenate([jnp.where(first_half, ys[g][2 * p], ys[g][2 * p + 1])
                                      for p in range(2)], axis=1)
            al_e = a_e[g][L - 1:L, :]
            new = lax.dot_general(b_of(g), (xdt[g] * jnp.exp2(al_e - a_e[g])).astype(BF16),
                                  (((0,), (0,)), ((), ())), preferred_element_type=F32)
            y = y_diag + y_cs[g] * jnp.exp2(a_e[g]) + x_of(g) * dskip_ref[g]
            state[g] = state[g] * jnp.exp2(al_e) + new
            yg = y * _silu(z_ref[:, g * W:(g + 1) * W])
            ms = jnp.mean(yg * yg, axis=-1, keepdims=True)
            o_ref[:, g * W:(g + 1) * W] = (yg * lax.rsqrt(ms + NORM_EPS)
                                           * nw_ref[:, g * W:(g + 1) * W]).astype(o_ref.dtype)

    for g0 in range(0, G, SSD_GROUP_BATCH):
        run_groups(range(g0, g0 + SSD_GROUP_BATCH))


def _head_selectors():
    G, R = SSM_N_GROUPS, SSM_HEADS_PER_GROUP
    h = jnp.arange(LANES)[None, :, None]
    g = jnp.arange(G)[:, None, None]
    j64 = jnp.arange(SSM_GROUP_WIDTH)[None, None, :]
    j128 = jnp.arange(R * LANES)[None, None, :]
    e64 = (h == g * R + j64 // SSM_HEADDIM).astype(BF16)
    e128 = (h == g * R + j128 // LANES).astype(BF16)
    return e64, e128


def _ssd(zx, x2, w_dt, conv_w, conv_b, dt_bias, a_log, d_skip, norm_w, batch, seq, moe_weights, moe_layer):
    L, G, W, N = SSM_CHUNK, SSM_N_GROUPS, SSM_GROUP_WIDTH, SSM_D_STATE
    nc = seq // L
    cast = _CastPlan(*moe_weights, moe_layer, batch * nc, lambda b, c: b * nc + c)
    t = batch * seq
    pad = LANES - SSM_N_HEADS
    wdt = jnp.pad(w_dt, ((0, 0), (0, pad)))
    dtb = jnp.pad(dt_bias, (0, pad))[None, :]
    alog = jnp.pad(a_log, (0, pad))[None, :]
    dsk = jnp.repeat(d_skip, SSM_HEADDIM).reshape(G, 1, W)
    e64, e128 = _head_selectors()
    conv_dim = conv_w.shape[1]
    row = lambda b, c: (b * nc + c, 0)
    full2 = lambda b, c: (0, 0)
    full3 = lambda b, c: (0, 0, 0)
    yg, *w_bf16 = pl.pallas_call(
        functools.partial(_ssd_kernel, cast=cast),
        grid=(batch, nc),
        in_specs=[pl.BlockSpec((L, SSM_D_INNER), row),
                  pl.BlockSpec((L, SSM_D_INNER), lambda b, c: (b * nc + c, 1)),
                  pl.BlockSpec((L, 2 * G * N), lambda b, c: (b * nc + c, 2)),
                  pl.BlockSpec((L, D_MODEL), row),
                  pl.BlockSpec((D_MODEL, LANES), full2),
                  pl.BlockSpec((SSM_CONV_K, conv_dim), full2),
                  pl.BlockSpec((1, conv_dim), full2),
                  pl.BlockSpec((1, LANES), full2),
                  pl.BlockSpec((1, LANES), full2),
                  pl.BlockSpec((G, 1, W), full3),
                  pl.BlockSpec((1, SSM_D_INNER), full2),
                  pl.BlockSpec((G, LANES, W), full3),
                  pl.BlockSpec((G, LANES, SSM_HEADS_PER_GROUP * LANES), full3)] + cast.in_specs,
        out_specs=[pl.BlockSpec((L, SSM_D_INNER), row)] + cast.out_specs,
        out_shape=[jax.ShapeDtypeStruct((t, SSM_D_INNER), BF16)] + cast.out_shape,
        scratch_shapes=[pltpu.VMEM((CONV_TAIL + L, SSM_D_INNER), F32),
                        pltpu.VMEM((CONV_TAIL + L, 2 * G * N), F32),
                        pltpu.VMEM((G, N, W), F32),
                        pltpu.VMEM((3, L, LANES), BF16),
                        pltpu.VMEM((2, L, LANES), BF16),
                        pltpu.VMEM((LANES, L), F32)],
        compiler_params=_arb(2),
        name="ssd_scan",
    )(zx, zx, zx, x2, wdt, conv_w, conv_b[None, :], dtb, alog, dsk, norm_w[None, :], e64, e128, *cast.weights)
    return yg, tuple(w_bf16)


def _route(h, wr):
    tm = h.shape[0]
    logits = _dot(h.astype(BF16), wr)
    lane_i = lax.broadcasted_iota(I32, (tm, LANES), 1)
    lane = lane_i.astype(F32)
    neg = -jnp.inf
    big = float(LANES)

    def first_argmax(v, vmax):
        return jnp.min(jnp.where(v == vmax, lane, big), axis=-1, keepdims=True)

    gl = jnp.where((lane_i >= MOE_N_EXPERTS) & (lane_i < MOE_N_EXPERTS + MOE_GROUPS), logits, neg)
    gm = jnp.max(gl, axis=-1, keepdims=True)
    g_sel = first_argmax(gl, gm) - float(MOE_N_EXPERTS)
    g_gate = 1.0 / jnp.sum(jnp.exp(gl - gm), axis=-1, keepdims=True)
    lo = g_sel * float(MOE_EXPERTS_PER_GROUP)
    el = jnp.where((lane >= lo) & (lane < lo + float(MOE_EXPERTS_PER_GROUP)), logits, neg)
    m1 = jnp.max(el, axis=-1, keepdims=True)
    i1 = first_argmax(el, m1)
    el2 = jnp.where(lane == i1, neg, el)
    m2 = jnp.max(el2, axis=-1, keepdims=True)
    i2 = first_argmax(el2, m2)
    p2 = jnp.exp(m2 - m1)
    t1 = 1.0 / (1.0 + p2)
    t2 = p2 / (1.0 + p2)
    eid = jnp.where(lane_i == 0, i1, jnp.where(lane_i == 1, i2, 0.0)).astype(I32)
    wt = jnp.where(lane_i == 0, g_gate * t1, jnp.where(lane_i == 1, g_gate * t2, 0.0))
    count = jnp.sum(jnp.where(lane == i1, 1.0, 0.0) + jnp.where(lane == i2, 1.0, 0.0), axis=0, keepdims=True)
    return eid, wt, count


def _router_weights(w_group, w_expert):
    pad = LANES - MOE_N_EXPERTS - MOE_GROUPS
    return jnp.pad(jnp.concatenate([w_expert, w_group], axis=1), ((0, 0), (0, pad))).astype(BF16)


def _store_token_tiles(ref, v, first_token=0):
    rows = v.shape[0]
    for s in range(ROW_SLABS):
        ref[pl.ds(first_token * ROW_SLABS + s, rows, stride=ROW_SLABS), :] = v[:, s * LANES:(s + 1) * LANES]


def _load_token_tiles(ref, rows):
    return jnp.concatenate([ref[pl.ds(s, rows, stride=ROW_SLABS), :] for s in range(ROW_SLABS)], axis=1)


def _mm_ln_router_kernel(x_ref, w_ref, r_ref, g_ref, b_ref, wr_ref, ht_ref, eid_ref, wt_ref, cnt_ref):
    @pl.when(pl.program_id(0) == 0)
    def _():
        cnt_ref[...] = jnp.zeros(cnt_ref.shape, F32)

    sub = x_ref.shape[0] // LN_SUBTILES
    rows = [slice(u * sub, (u + 1) * sub) for u in range(LN_SUBTILES)]
    ys = [DEEPNORM_ALPHA * r_ref[rs, :] + _dot(x_ref[rs, :].astype(BF16), w_ref[...]) for rs in rows]
    hs = [_layer_norm(y, g_ref[...], b_ref[...]) for y in ys]
    routes = [_route(h, wr_ref[...]) for h in hs]
    for u, rs in enumerate(rows):
        _store_token_tiles(ht_ref, hs[u], first_token=u * sub)
        eid_ref[rs, :] = routes[u][0]
        wt_ref[rs, :] = routes[u][1]
    cnt_ref[...] += jnp.broadcast_to(sum(r[2] for r in routes), cnt_ref.shape)


def _mm_ln_router(x, w, resid, g, b, wr, name):
    m, k = x.shape
    d = w.shape[1]
    tm = min(MM_TM, m)
    row = lambda i: (i, 0)
    full = lambda i: (0, 0)
    return pl.pallas_call(
        _mm_ln_router_kernel,
        grid=(m // tm,),
        in_specs=[pl.BlockSpec((tm, k), row), pl.BlockSpec((k, d), full), pl.BlockSpec((tm, d), row),
                  pl.BlockSpec((1, d), full), pl.BlockSpec((1, d), full), pl.BlockSpec((d, LANES), full)],
        out_specs=[pl.BlockSpec((tm * ROW_SLABS, LANES), row),
                   pl.BlockSpec((tm, LANES), row), pl.BlockSpec((tm, LANES), row),
                   pl.BlockSpec((8, LANES), full)],
        out_shape=[jax.ShapeDtypeStruct((m * ROW_SLABS, LANES), F32),
                   jax.ShapeDtypeStruct((m, LANES), I32), jax.ShapeDtypeStruct((m, LANES), F32),
                   jax.ShapeDtypeStruct((8, LANES), F32)],
        compiler_params=_arb(1),
        name=name,
    )(x, w, resid, g[None, :], b[None, :], wr)


def _dispatch_kernel(eid_ref, cnt_ref, h_ref, dest_ref, be_ref, nu_ref, xs_hbm,
                     base, upper, zbuf, dst_v, dst_s, pe_v, pe_s, sem_z, sem_r, sem_s, *, blk):
    i = pl.program_id(0)
    tm = eid_ref.shape[0]
    n_e = LANES
    eid_t = eid_ref[...].astype(F32).T
    sub = lax.broadcasted_iota(I32, (n_e, tm), 0).astype(F32)
    oh = [(sub == eid_t[k:k + 1, :]).astype(F32) for k in range(MOE_TOP_K)]
    tot = [jnp.sum(o, axis=1, keepdims=True) for o in oh]

    @pl.when(i == 0)
    def _():
        r_i = lax.broadcasted_iota(I32, (tm, tm), 0)
        c_i = lax.broadcasted_iota(I32, (tm, tm), 1)
        upper[...] = (r_i < c_i).astype(BF16)
        counts = jnp.broadcast_to(cnt_ref[0:1, :], (n_e, n_e)).T
        padded = jnp.floor((counts + float(blk - 1)) * (1.0 / blk)) * float(blk)
        r_i = lax.broadcasted_iota(I32, (n_e, n_e), 0)
        c_i = lax.broadcasted_iota(I32, (n_e, n_e), 1)
        tril = (r_i >= c_i).astype(BF16)
        p1, p2, p3 = _split3(padded)
        pends = _dot(tril, p1) + _dot(tril, p2) + _dot(tril, p3)
        base[...] = pends - padded
        nbp = be_ref.shape[1]
        blk_start = lax.broadcasted_iota(I32, (n_e, nbp), 1).astype(F32) * float(blk)
        is_e = lax.broadcasted_iota(I32, (n_e, nbp), 0) < MOE_N_EXPERTS
        done = jnp.where(is_e & (jnp.tile(pends, (1, nbp // LANES)) <= blk_start), 1.0, 0.0)
        be = jnp.minimum(jnp.sum(done, axis=0, keepdims=True), float(MOE_N_EXPERTS - 1))
        be_ref[...] = jnp.broadcast_to(be, be_ref.shape).astype(I32)
        last = pends[MOE_N_EXPERTS - 1:MOE_N_EXPERTS, :] * (1.0 / blk)
        nu_ref[...] = jnp.broadcast_to(last, nu_ref.shape).astype(I32)
        zbuf[...] = jnp.zeros(zbuf.shape, F32)
        row8 = lax.broadcasted_iota(I32, (8, LANES), 0)
        pe_v[...] = jnp.where(row8 == 0, pends.T[0:8, :], counts.T[0:8, :]).astype(I32)
        cp = pltpu.make_async_copy(pe_v, pe_s, sem_s)
        cp.start()
        cp.wait()

        brows = blk * ROW_SLABS

        def zero_copy(e):
            start = pl.multiple_of((pe_s[0, e] - blk) * ROW_SLABS, brows)
            return pltpu.make_async_copy(zbuf, xs_hbm.at[pl.ds(start, brows)], sem_z)

        def tail_copy(b):
            return pltpu.make_async_copy(zbuf, xs_hbm.at[pl.ds(pl.multiple_of(b * brows, brows), brows)], sem_z)

        n_used = lax.shift_right_logical(pe_s[0, MOE_N_EXPERTS - 1], blk.bit_length() - 1)
        n_blocks = xs_hbm.shape[0] // brows
        for e in range(MOE_N_EXPERTS):
            @pl.when(pe_s[1, e] > 0)
            def _():
                zero_copy(e).start()
        lax.fori_loop(n_used, n_blocks, lambda b, c: (tail_copy(b).start(), c)[1], 0)
        for e in range(MOE_N_EXPERTS):
            @pl.when(pe_s[1, e] > 0)
            def _():
                zero_copy(e).wait()
        lax.fori_loop(n_used, n_blocks, lambda b, c: (tail_copy(b).wait(), c)[1], 0)

    b0 = base[:, 0:1]
    c0 = _dot(oh[0].astype(BF16), upper[...])
    c1 = _dot(oh[1].astype(BF16), upper[...])
    d0 = jnp.sum(oh[0] * (b0 + c0), axis=0, keepdims=True)
    d1 = jnp.sum(oh[1] * (b0 + tot[0] + c1), axis=0, keepdims=True)
    base[...] += jnp.broadcast_to(tot[0] + tot[1], base.shape)
    row8 = lax.broadcasted_iota(I32, (8, tm), 0)
    dst = jnp.where(row8 == 0, d0, jnp.where(row8 == 1, d1, 0.0)).astype(I32)
    dest_ref[...] = dst
    dst_v[...] = dst
    cp = pltpu.make_async_copy(dst_v, dst_s, sem_s)
    cp.start()
    cp.wait()

    for r in range(tm):
        for k in range(MOE_TOP_K):
            slot = pl.multiple_of(dst_s[k, r] * ROW_SLABS, ROW_SLABS)
            pltpu.make_async_copy(h_ref.at[pl.ds(r * ROW_SLABS, ROW_SLABS)],
                                  xs_hbm.at[pl.ds(slot, ROW_SLABS)], sem_r).start(priority=k)
    for k in range(MOE_TOP_K):
        pltpu.make_async_copy(h_ref, xs_hbm.at[pl.ds(0, tm * ROW_SLABS)], sem_r).wait()


def _moe_dispatch(ht, eid, counts, blk, name):
    t = ht.shape[0] // ROW_SLABS
    assert blk & (blk - 1) == 0, "block size must be a power of two"
    tm = min(DSP_TM, t)
    nt = t // tm
    n_blocks = -(-(t * MOE_TOP_K) // blk) + MOE_N_EXPERTS
    nbp = -(-n_blocks // LANES) * LANES
    full = lambda i: (0, 0)
    dest, be, nu, xs = pl.pallas_call(
        functools.partial(_dispatch_kernel, blk=blk),
        grid=(nt,),
        in_specs=[pl.BlockSpec((tm, LANES), lambda i: (i, 0)),
                  pl.BlockSpec((8, LANES), full),
                  pl.BlockSpec((tm * ROW_SLABS, LANES), lambda i: (i, 0))],
        out_specs=[pl.BlockSpec((8, tm), lambda i: (0, i)),
                   pl.BlockSpec((8, nbp), full),
                   pl.BlockSpec((8, LANES), full),
                   pl.BlockSpec(memory_space=pl.ANY)],
        out_shape=[jax.ShapeDtypeStruct((8, t), I32), jax.ShapeDtypeStruct((8, nbp), I32),
                   jax.ShapeDtypeStruct((8, LANES), I32),
                   jax.ShapeDtypeStruct((n_blocks * blk * ROW_SLABS, LANES), F32)],
        scratch_shapes=[pltpu.VMEM((LANES, LANES), F32),
                        pltpu.VMEM((tm, tm), BF16), pltpu.VMEM((blk * ROW_SLABS, LANES), F32),
                        pltpu.VMEM((8, tm), I32), pltpu.SMEM((8, tm), I32),
                        pltpu.VMEM((8, LANES), I32), pltpu.SMEM((8, LANES), I32),
                        pltpu.SemaphoreType.DMA(()), pltpu.SemaphoreType.DMA(()), pltpu.SemaphoreType.DMA(())],
        compiler_params=_arb(1),
        name=name,
    )(eid, counts, ht)
    return dest, be[0, :n_blocks], nu[0, :1], xs, n_blocks


def _ffn_kernel(be_ref, nu_ref, x_ref, wg_ref, wu_ref, wd_ref, o_ref):
    i = pl.program_id(0)

    @pl.when(i < nu_ref[0])
    def _():
        blk = x_ref.shape[0] // ROW_SLABS
        xb = _load_token_tiles(x_ref, blk).astype(BF16)
        hid = _silu(_dot(xb, wg_ref[0])) * _dot(xb, wu_ref[0])
        _store_token_tiles(o_ref, _dot(hid.astype(BF16), wd_ref[0]))

    @pl.when(i >= nu_ref[0])
    def _():
        o_ref[...] = jnp.zeros(o_ref.shape, o_ref.dtype)


def _moe_ffn(xs, block_expert, n_used, n_blocks, weights, blk, name):
    w_gate, w_up, w_down = weights
    d, f = w_gate.shape[1], w_gate.shape[2]
    brows = blk * ROW_SLABS
    used = lambda i, be, nu: (jnp.minimum(i, nu[0] - 1), 0)
    every = lambda i, be, nu: (i, 0)
    expert = lambda i, be, nu: (be[i], 0, 0)
    grid_spec = pltpu.PrefetchScalarGridSpec(
        num_scalar_prefetch=2,
        grid=(n_blocks,),
        in_specs=[pl.BlockSpec((brows, LANES), used),
                  pl.BlockSpec((1, d, f), expert), pl.BlockSpec((1, d, f), expert),
                  pl.BlockSpec((1, f, d), expert)],
        out_specs=pl.BlockSpec((brows, LANES), every))
    return pl.pallas_call(
        _ffn_kernel,
        grid_spec=grid_spec,
        out_shape=jax.ShapeDtypeStruct((n_blocks * brows, LANES), F32),
        compiler_params=_arb(1),
        name=name,
    )(block_expert, n_used, xs, w_gate, w_up, w_down)


def _combine_ln_kernel(dst_ref, dstn_ref, ht_ref, wt_ref, g_ref, b_ref, yb_hbm, o_ref, ybuf, sem):
    i = pl.program_id(0)
    n = pl.num_programs(0)
    tm = o_ref.shape[0]

    def start_gather(dref, slot):
        for r in range(tm):
            for k in range(MOE_TOP_K):
                src = pl.multiple_of(dref[k, r] * ROW_SLABS, ROW_SLABS)
                pltpu.make_async_copy(yb_hbm.at[pl.ds(src, ROW_SLABS)],
                                      ybuf.at[slot, k, pl.ds(r * ROW_SLABS, ROW_SLABS)],
                                      sem.at[slot]).start(priority=k)

    @pl.when(i == 0)
    def _():
        start_gather(dst_ref, 0)

    @pl.when(i + 1 < n)
    def _():
        start_gather(dstn_ref, (i + 1) % 2)

    slot = i % 2
    for k in range(MOE_TOP_K):
        pltpu.make_async_copy(yb_hbm.at[pl.ds(0, tm * ROW_SLABS)], ybuf.at[slot, k], sem.at[slot]).wait()
    wt = wt_ref[...]
    ffn = (wt[:, 0:1] * _load_token_tiles(ybuf.at[slot, 0], tm)
           + wt[:, 1:2] * _load_token_tiles(ybuf.at[slot, 1], tm))
    h = _load_token_tiles(ht_ref, tm)
    o_ref[...] = _layer_norm(DEEPNORM_ALPHA * h + ffn, g_ref[...], b_ref[...])


def _combine_ln(ht, wt, dest, yb, g, b, name):
    t, d = ht.shape[0] // ROW_SLABS, D_MODEL
    tm = min(CMB_TM, t)
    nblk = t // tm
    row = lambda i: (i, 0)
    full = lambda i: (0, 0)
    return pl.pallas_call(
        _combine_ln_kernel,
        grid=(nblk,),
        in_specs=[pl.BlockSpec((8, tm), lambda i: (0, i), memory_space=pltpu.SMEM),
                  pl.BlockSpec((8, tm), lambda i: (0, jnp.minimum(i + 1, nblk - 1)),
                               memory_space=pltpu.SMEM),
                  pl.BlockSpec((tm * ROW_SLABS, LANES), row), pl.BlockSpec((tm, LANES), row),
                  pl.BlockSpec((1, d), full), pl.BlockSpec((1, d), full),
                  pl.BlockSpec(memory_space=pl.ANY)],
        out_specs=pl.BlockSpec((tm, d), row),
        out_shape=jax.ShapeDtypeStruct((t, d), F32),
        scratch_shapes=[pltpu.VMEM((2, MOE_TOP_K, tm * ROW_SLABS, LANES), F32), pltpu.SemaphoreType.DMA((2,))],
        compiler_params=_arb(1),
        name=name,
    )(dest, dest, ht, wt, g[None, :], b[None, :], yb)


def _hier_moe_ln(ht, eid, wt, counts, w_bf16, g, b, layer):
    dest, block_expert, n_used, xs, n_blocks = _moe_dispatch(ht, eid, counts, FFN_BLK, f"moe_dispatch{layer}")
    yb = _moe_ffn(xs, block_expert, n_used, n_blocks, w_bf16, FFN_BLK, f"moe_ffn{layer}")
    return _combine_ln(ht, wt, dest, yb, g, b, f"moe_combine_ln{layer}")


def _qkv_rope_kernel(x_ref, w_ref, pos_ref, inv_ref, o_ref):
    xb = x_ref[...].astype(BF16)
    tm = xb.shape[0]
    n = D_MODEL
    ang = pos_ref[...].astype(F32) * inv_ref[...]
    lane = lax.broadcasted_iota(I32, (tm, LANES), 1)
    dd = lane & (ATTN_HEAD_DIM - 1)
    half = ROT_DIM // 2
    cosv = jnp.cos(ang)
    sinv = jnp.sin(ang)
    c_t = jnp.where(dd < ROT_DIM, cosv, 1.0)
    s_up = jnp.where(dd < half, -sinv, 0.0)
    s_dn = jnp.where((dd >= half) & (dd < ROT_DIM), sinv, 0.0)
    for j, sc in ((0, ATTN_HEAD_DIM ** -0.5 * math.log2(math.e)), (1, 1.0)):
        acc = _dot(xb, w_ref[:, j * n:(j + 1) * n])
        c_j, up_j, dn_j = c_t * sc, s_up * sc, s_dn * sc
        for blk in range(n // LANES):
            tt = acc[:, blk * LANES:(blk + 1) * LANES]
            out = tt * c_j + pltpu.roll(tt, LANES - half, 1) * up_j + pltpu.roll(tt, half, 1) * dn_j
            o_ref[:, j * n + blk * LANES:j * n + (blk + 1) * LANES] = out.astype(o_ref.dtype)
    o_ref[:, 2 * n:3 * n] = _dot(xb, w_ref[:, 2 * n:3 * n]).astype(o_ref.dtype)


def _rope_inv_table():
    inv = ROPE_THETA ** (-jnp.arange(0, ROT_DIM, 2, dtype=F32) / ROT_DIM)
    head = jnp.concatenate([inv, inv, jnp.zeros((ATTN_HEAD_DIM - ROT_DIM,), F32)])
    return jnp.tile(head, LANES // ATTN_HEAD_DIM)[None, :]


def _qkv_rope(h, w_qkv, positions):
    m, k = h.shape
    n = w_qkv.shape[1]
    tm = min(MM_TM, m)
    pos = positions.reshape(m, 1)
    return pl.pallas_call(
        _qkv_rope_kernel,
        grid=(m // tm,),
        in_specs=[pl.BlockSpec((tm, k), lambda i: (i, 0)),
                  pl.BlockSpec((k, n), lambda i: (0, 0)),
                  pl.BlockSpec((tm, 1), lambda i: (i, 0)),
                  pl.BlockSpec((1, LANES), lambda i: (0, 0))],
        out_specs=pl.BlockSpec((tm, n), lambda i: (i, 0)),
        out_shape=jax.ShapeDtypeStruct((m, n), BF16),
        compiler_params=_arb(1),
        name="mm_qkv_rope",
    )(h, w_qkv, pos, _rope_inv_table())


def _attn_kernel(q_ref, k_ref, v_ref, lq1_ref, lk1_ref, lq2_ref, lk2_ref, sw_ref, wg_ref, wu_ref, wd_ref,
                 o_ref, wgb_ref, wub_ref, wdb_ref,
                 vx_ref, m0_ref, m1_ref, acc0_ref, acc1_ref, *, lambda_init, cast):
    m_refs = (m0_ref, m1_ref)
    acc_refs = (acc0_ref, acc1_ref)
    jp = pl.program_id(2)
    n_tiles = ATT_TILES_PER_STEP
    tq = q_ref.shape[0] // n_tiles
    cast.emit(pl.program_id(0) * pl.num_programs(1) + pl.program_id(1),
              (wg_ref, wu_ref, wd_ref), (wgb_ref, wub_ref, wdb_ref), extra=jp == 0)

    @pl.when(jp == 0)
    def _():
        vx_ref[:, 0:LANES] = v_ref[...]
        vx_ref[:, LANES:] = jnp.ones((vx_ref.shape[0], LANES), vx_ref.dtype)

    q = q_ref[...]
    lane = lax.broadcasted_iota(I32, q.shape, 1)
    zero = jnp.zeros((), q.dtype)
    qs = (jnp.where(lane < ATTN_HEAD_DIM, q, zero), jnp.where(lane >= ATTN_HEAD_DIM, q, zero))
    for c in range(2):
        m_refs[c][...] = jnp.full(m_refs[c].shape, -jnp.inf, F32)
        acc_refs[c][...] = jnp.zeros(acc_refs[c].shape, F32)

    def step(off, windows):
        off = pl.multiple_of(off, tq)
        widest = max(w for _, w, _ in windows)
        kb = k_ref[pl.ds(off, widest), :]
        vb = vx_ref[pl.ds(off, widest), :]
        chains = [(slice(r0, r0 + tq), w, d, c) for r0, w, d in windows for c in range(2)]
        ss = [lax.dot_general(qs[c][rows, :], kb[:w, :], (((1,), (1,)), ((), ())), preferred_element_type=F32)
              for rows, w, _, c in chains]
        for n, (_, w, d, _) in enumerate(chains):
            if d is not None:
                row_i = lax.broadcasted_iota(I32, (tq, w), 0)
                col_i = lax.broadcasted_iota(I32, (tq, w), 1)
                ss[n] = jnp.where(row_i + d >= col_i, ss[n], -jnp.inf)
        m_prev = [m_refs[c][rows, :] for rows, _, _, c in chains]
        mn = [jnp.maximum(mp, jnp.max(s, axis=-1, keepdims=True)) for mp, s in zip(m_prev, ss)]
        ps = [jnp.exp2(s - jnp.tile(m, (1, s.shape[1] // LANES))).astype(BF16) for s, m in zip(ss, mn)]
        for n, (rows, w, _, c) in enumerate(chains):
            alpha = jnp.exp2(m_prev[n] - mn[n])
            acc_refs[c][rows, :] = jnp.tile(alpha, (1, 2)) * acc_refs[c][rows, :] + _dot(ps[n], vb[:w, :])
            m_refs[c][rows, :] = mn[n]

    def shared(j, carry):
        for p in range(n_tiles // 2):
            step(j * (2 * tq), [((2 * p) * tq, 2 * tq, None), ((2 * p + 1) * tq, 2 * tq, None)])
        return carry
    lax.fori_loop(0, jp * (n_tiles // 2), shared, 0)
    base = jp * (n_tiles * tq)
    for p in range(n_tiles // 2):
        a, b = 2 * p * tq, (2 * p + 1) * tq
        step(base + a, [(a, tq, 0), (b, 2 * tq, tq)])
        later = [(r * tq, 2 * tq, None) for r in range(2 * p + 2, n_tiles)]
        if later:
            step(base + a, later)

    lam = (jnp.exp(jnp.sum(lq1_ref[...] * lk1_ref[...], axis=-1, keepdims=True))
           - jnp.exp(jnp.sum(lq2_ref[...] * lk2_ref[...], axis=-1, keepdims=True)) + lambda_init)
    a1 = acc0_ref[...]
    a2 = acc1_ref[...]
    o = a1[:, :LANES] / a1[:, LANES:] - lam * (a2[:, :LANES] / a2[:, LANES:])
    o = o * lax.rsqrt(jnp.mean(o * o, axis=-1, keepdims=True) + NORM_EPS)
    o_ref[...] = (o * sw_ref[...] * (1.0 - lambda_init)).astype(o_ref.dtype)


def _diff_attention(qkv, lq1, lk1, lq2, lk2, subln_w, lambda_init, batch, seq, moe_weights, moe_layer):
    t = batch * seq
    tq = ATT_TILES_PER_STEP * min(ATT_TQ, seq // ATT_TILES_PER_STEP)
    nq = seq // tq
    h_n = ATTN_N_HEADS
    vec = lambda b, h, i: (0, 0)
    cast = _CastPlan(*moe_weights, moe_layer, batch * h_n, lambda b, h, i: b * h_n + h)
    o, *w_bf16 = pl.pallas_call(
        functools.partial(_attn_kernel, lambda_init=lambda_init, cast=cast),
        grid=(batch, h_n, nq),
        in_specs=[pl.BlockSpec((tq, LANES), lambda b, h, i: (b * nq + i, h)),
                  pl.BlockSpec((seq, LANES), lambda b, h, i: (b, h_n + h)),
                  pl.BlockSpec((seq, LANES), lambda b, h, i: (b, 2 * h_n + h)),
                  pl.BlockSpec((1, ATTN_HEAD_DIM), vec), pl.BlockSpec((1, ATTN_HEAD_DIM), vec),
                  pl.BlockSpec((1, ATTN_HEAD_DIM), vec), pl.BlockSpec((1, ATTN_HEAD_DIM), vec),
                  pl.BlockSpec((1, ATTN_V_DIM), vec)] + cast.in_specs,
        out_specs=[pl.BlockSpec((tq, LANES), lambda b, h, i: (b * nq + i, h))] + cast.out_specs,
        out_shape=[jax.ShapeDtypeStruct((t, h_n * ATTN_V_DIM), BF16)] + cast.out_shape,
        scratch_shapes=[pltpu.VMEM((seq, 2 * LANES), BF16),
                        pltpu.VMEM((tq, LANES), F32), pltpu.VMEM((tq, LANES), F32),
                        pltpu.VMEM((tq, 2 * LANES), F32), pltpu.VMEM((tq, 2 * LANES), F32)],
        compiler_params=_arb(3),
        name="diff_attn",
    )(qkv, qkv, qkv, lq1[None, :], lk1[None, :], lq2[None, :], lk2[None, :], subln_w[None, :], *cast.weights)
    return o, tuple(w_bf16)


def kernel(x, positions, ln_mix_g, ln_mix_b, ln_ffn_g, ln_ffn_b, ssm_w_in, ssm_conv_w, ssm_conv_b, ssm_dt_bias, ssm_a_log, ssm_d, ssm_norm_w, ssm_w_out, attn_w_qkv, attn_lam_q1, attn_lam_k1, attn_lam_q2, attn_lam_k2, attn_subln_w, attn_w_o, moe_w_group, moe_w_expert, moe_w_gate, moe_w_up, moe_w_down):
    batch, seq, d = x.shape
    t = batch * seq
    h = x.reshape(t, d)

    w_in = ssm_w_in[0].astype(BF16)

    moe_weights = (moe_w_gate, moe_w_up, moe_w_down)
    zx = _matmul(h, w_in, SSM_ZX_DIM, F32)
    yg, moe_w0 = _ssd(zx, h, w_in[:, SSM_ZX_DIM:], ssm_conv_w[0], ssm_conv_b[0], ssm_dt_bias[0],
                      ssm_a_log[0], ssm_d[0], ssm_norm_w[0], batch, seq, moe_weights, 0)
    ht, eid, wt, cnt = _mm_ln_router(yg, ssm_w_out[0].astype(BF16), h, ln_mix_g[0], ln_mix_b[0],
                                     _router_weights(moe_w_group[0], moe_w_expert[0]), "mm_ssm_out_ln_router")
    h = _hier_moe_ln(ht, eid, wt, cnt, moe_w0, ln_ffn_g[0], ln_ffn_b[0], 0)

    lambda_init = 0.8 - 0.6 * math.exp(-0.3 * 1)
    qkv = _qkv_rope(h, attn_w_qkv[0].astype(BF16), positions)
    o, moe_w1 = _diff_attention(qkv, attn_lam_q1[0], attn_lam_k1[0], attn_lam_q2[0], attn_lam_k2[0],
                                attn_subln_w[0], lambda_init, batch, seq, moe_weights, 1)
    ht, eid, wt, cnt = _mm_ln_router(o, attn_w_o[0].astype(BF16), h, ln_mix_g[1], ln_mix_b[1],
                                     _router_weights(moe_w_group[1], moe_w_expert[1]), "mm_attn_out_ln_router")
    h = _hier_moe_ln(ht, eid, wt, cnt, moe_w1, ln_ffn_g[1], ln_ffn_b[1], 1)
    return h.reshape(batch, seq, d)
```

```python
import functools
import math

import jax
import jax.numpy as jnp
from jax import lax
from jax.experimental import pallas as pl
from jax.experimental.pallas import tpu as pltpu

F32 = jnp.float32
BF16 = jnp.bfloat16
I32 = jnp.int32

D_MODEL = 1024
DEPTH = 2
SSM_D_INNER = 2048
SSM_HEADDIM = 64
SSM_N_HEADS = 32
SSM_N_GROUPS = 8
SSM_HEADS_PER_GROUP = 4
SSM_D_STATE = 128
SSM_CONV_K = 4
SSM_CHUNK = 128
SSM_GROUP_WIDTH = SSM_HEADS_PER_GROUP * SSM_HEADDIM
SSM_ZX_DIM = 2 * SSM_D_INNER + 2 * SSM_N_GROUPS * SSM_D_STATE
ATTN_HEAD_DIM = 64
ATTN_N_HEADS = 8
ATTN_V_DIM = 128
ROT_DIM = 16
ROPE_THETA = 500000.0
MOE_GROUPS = 4
MOE_EXPERTS_PER_GROUP = 8
MOE_N_EXPERTS = 32
MOE_TOP_K = 2
MOE_D_FF = 512
DEEPNORM_ALPHA = (2 * DEPTH) ** 0.25
NORM_EPS = 1e-5

LANES = 128
ROW_SLABS = D_MODEL // LANES
CONV_TAIL = 8
CONV_ROWS = 128

MM_TM = 512
MM_TN = 1024
FFN_BLK = 512
DSP_TM = 512
SSD_GROUP_BATCH = 2
LN_SUBTILES = 4
CMB_TM = 256
ATT_TQ = 512
ATT_TILES_PER_STEP = 2


def _arb(n):
    return pltpu.CompilerParams(dimension_semantics=("arbitrary",) * n,
                                vmem_limit_bytes=56 * 1024 * 1024)


def _silu(x):
    hx = 0.5 * x
    return hx + hx * jnp.tanh(hx)


def _softplus(x):
    return jnp.maximum(x, 0.0) + jnp.log(1.0 + jnp.exp(-jnp.abs(x)))


def _layer_norm(y, g, b):
    mu = jnp.mean(y, axis=-1, keepdims=True)
    d = y - mu
    var = jnp.mean(d * d, axis=-1, keepdims=True)
    return d * lax.rsqrt(var + NORM_EPS) * g + b


def _split3(a):
    a1 = a.astype(BF16)
    r1 = a - a1.astype(F32)
    a2 = r1.astype(BF16)
    a3 = (r1 - a2.astype(F32)).astype(BF16)
    return a1, a2, a3


def _dot(a, b):
    return jnp.dot(a, b, preferred_element_type=F32)


def _dot_sel(a, sel):
    a1, a2, a3 = _split3(a)
    return _dot(a1, sel) + _dot(a2, sel) + _dot(a3, sel)


def _dot_f32(a, b):
    a1, a2, a3 = _split3(a)
    b1, b2, b3 = _split3(b)
    return (_dot(a1, b1) + _dot(a1, b2) + _dot(a2, b1)
            + _dot(a2, b2) + _dot(a1, b3) + _dot(a3, b1))


class _CastPlan:
    def __init__(self, w_gate, w_up, w_down, layer, steps, step_of):
        self.weights = (w_gate, w_up, w_down)
        self.n_e = w_gate.shape[1]
        self.eps = -(-self.n_e // steps)
        assert self.n_e % self.eps == 0
        self.n_cast = self.n_e // self.eps
        self.stride = steps // self.n_cast
        block = lambda *ids: jnp.minimum(step_of(*ids) // self.stride, self.n_cast - 1)
        self.in_specs = [pl.BlockSpec((1, self.eps) + w.shape[2:], lambda *ids: (layer, block(*ids), 0, 0))
                         for w in self.weights]
        self.out_specs = [pl.BlockSpec((self.eps,) + w.shape[2:], lambda *ids: (block(*ids), 0, 0))
                          for w in self.weights]
        self.out_shape = [jax.ShapeDtypeStruct(w.shape[1:], BF16) for w in self.weights]

    def emit(self, step, w_refs, o_refs, extra=True):
        @pl.when(extra & (step % self.stride == 0) & (step // self.stride < self.n_cast))
        def _():
            for w_ref, o_ref in zip(w_refs, o_refs):
                o_ref[...] = w_ref[0].astype(BF16)


def _mm_kernel(x_ref, w_ref, o_ref):
    xb = x_ref[...].astype(BF16)
    for j in range(o_ref.shape[1] // MM_TN):
        cols = slice(j * MM_TN, (j + 1) * MM_TN)
        o_ref[:, cols] = _dot(xb, w_ref[:, cols]).astype(o_ref.dtype)


def _matmul(x, w, n, out_dtype):
    m, k = x.shape
    tm = min(MM_TM, m)
    return pl.pallas_call(
        _mm_kernel,
        grid=(m // tm,),
        in_specs=[pl.BlockSpec((tm, k), lambda i: (i, 0)),
                  pl.BlockSpec((k, n), lambda i: (0, 0), pipeline_mode=pl.Buffered(1))],
        out_specs=pl.BlockSpec((tm, n), lambda i: (i, 0)),
        out_shape=jax.ShapeDtypeStruct((m, n), out_dtype),
        compiler_params=_arb(1),
        name="mm_inproj",
    )(x, w)


def _ssd_kernel(z_ref, xs_ref, bc_ref, x_ref, wdt_ref, cw_ref, cb_ref, dtb_ref, alog_ref, dskip_ref,
                nw_ref, e64_ref, e128_ref, wg_ref, wu_ref, wd_ref, o_ref, wgb_ref, wub_ref, wdb_ref,
                stage_x, stage_bc, state, acp_s, dtp_s, act_s, *, cast):
    G, W, N = SSM_N_GROUPS, SSM_GROUP_WIDTH, SSM_D_STATE
    L = z_ref.shape[0]
    c = pl.program_id(1)
    cast.emit(pl.program_id(0) * pl.num_programs(1) + c, (wg_ref, wu_ref, wd_ref), (wgb_ref, wub_ref, wdb_ref))

    @pl.when(c == 0)
    def _():
        state[...] = jnp.zeros(state.shape, F32)
        for stage in (stage_x, stage_bc):
            stage[0:CONV_TAIL, :] = jnp.zeros((CONV_TAIL, stage.shape[1]), F32)

    @pl.when(c != 0)
    def _():
        for stage in (stage_x, stage_bc):
            stage[0:CONV_TAIL, :] = stage[L:L + CONV_TAIL, :]

    stage_x[CONV_TAIL:CONV_TAIL + L, :] = xs_ref[...]
    stage_bc[CONV_TAIL:CONV_TAIL + L, :] = bc_ref[...]

    def conv(stage, c0, width, w0):
        taps = [cw_ref[k:k + 1, w0:w0 + width] for k in range(SSM_CONV_K)]
        bias = cb_ref[:, w0:w0 + width]
        outs = []
        for rb in range(L // CONV_ROWS):
            r0 = rb * CONV_ROWS
            u = stage[r0:r0 + CONV_ROWS + CONV_TAIL, c0:c0 + width]
            out = bias + taps[SSM_CONV_K - 1] * u[CONV_TAIL:, :]
            for k in range(SSM_CONV_K - 1):
                back = SSM_CONV_K - 1 - k
                out = out + taps[k] * pltpu.roll(u, back, 0)[CONV_TAIL:, :]
            outs.append(_silu(out))
        return jnp.concatenate(outs, axis=0)

    xc, bcv, ccv = {}, {}, {}

    def conv_groups(gs):
        for g in gs:
            xc[g] = conv(stage_x, g * W, W, g * W)
            bcv[g] = conv(stage_bc, g * N, N, SSM_D_INNER + g * N).astype(BF16)
            ccv[g] = conv(stage_bc, G * N + g * N, N, SSM_D_INNER + G * N + g * N).astype(BF16)

    x_of, b_of, c_of = xc.__getitem__, bcv.__getitem__, ccv.__getitem__

    dt = _softplus(_dot(x_ref[...].astype(BF16), wdt_ref[...].astype(BF16)) + dtb_ref[...])
    a = -jnp.exp(alog_ref[...])
    row_i = lax.broadcasted_iota(I32, (L, L), 0)
    col_i = lax.broadcasted_iota(I32, (L, L), 1)
    causal = row_i >= col_i
    a1, a2, a3 = _split3(dt * a)
    tri = causal.astype(BF16)
    ac = (_dot(tri, a1) + _dot(tri, a2) + _dot(tri, a3)) * math.log2(math.e)
    act_s[...] = ac.T
    for n, part in enumerate(_split3(ac)):
        acp_s[n] = part
    for n, part in enumerate(_split3(dt)[:2]):
        dtp_s[n] = part
    lane = lax.broadcasted_iota(I32, (L, LANES), 1)
    first_half = lane < SSM_HEADDIM

    def select(parts_ref, n_parts, sel):
        out = _dot(parts_ref[0], sel)
        for n in range(1, n_parts):
            out = out + _dot(parts_ref[n], sel)
        return out

    def run_groups(gs):
        conv_groups(gs)
        col4 = {g: select(acp_s, 3, e128_ref[g]) for g in gs}
        dt_e = {g: select(dtp_s, 2, e64_ref[g]) for g in gs}
        cb = {g: lax.dot_general(c_of(g), b_of(g), (((1,), (1,)), ((), ())), preferred_element_type=F32)
              for g in gs}
        y_cs = {g: _dot(c_of(g), state[g].astype(BF16)) for g in gs}
        a_e = {g: jnp.concatenate(
            [jnp.where(first_half, col4[g][:, 2 * p * LANES:(2 * p + 1) * LANES],
                       col4[g][:, (2 * p + 1) * LANES:(2 * p + 2) * LANES]) for p in range(2)], axis=1)
            for g in gs}
        xdt = {g: x_of(g) * dt_e[g] for g in gs}
        xdt_b = {g: xdt[g].astype(BF16) for g in gs}
        ys = {g: [None] * SSM_HEADS_PER_GROUP for g in gs}
        for r in range(SSM_HEADS_PER_GROUP):
            for g in gs:
                row = act_s[SSM_HEADS_PER_GROUP * g + r:SSM_HEADS_PER_GROUP * g + r + 1, :]
                seg = col4[g][:, LANES * r:LANES * (r + 1)] - row
                dec = jnp.where(causal, jnp.exp2(seg), 0.0)
                xp = xdt_b[g][:, LANES * (r // 2):LANES * (r // 2 + 1)]
                ys[g][r] = _dot((cb[g] * dec).astype(BF16), xp)
        for g in gs:
            y_diag = jnp.concatenate([jnp.where(first_half, ys[g][2 * p], ys[g][2 * p + 1])
                                      for p in range(2)], axis=1)
            al_e = a_e[g][L - 1:L, :]
            new = lax.dot_general(b_of(g), (xdt[g] * jnp.exp2(al_e - a_e[g])).astype(BF16),
                                  (((0,), (0,)), ((), ())), preferred_element_type=F32)
            y = y_diag + y_cs[g] * jnp.exp2(a_e[g]) + x_of(g) * dskip_ref[g]
            state[g] = state[g] * jnp.exp2(al_e) + new
            yg = y * _silu(z_ref[:, g * W:(g + 1) * W])
            ms = jnp.mean(yg * yg, axis=-1, keepdims=True)
            o_ref[:, g * W:(g + 1) * W] = (yg * lax.rsqrt(ms + NORM_EPS)
                                           * nw_ref[:, g * W:(g + 1) * W]).astype(o_ref.dtype)

    for g0 in range(0, G, SSD_GROUP_BATCH):
        run_groups(range(g0, g0 + SSD_GROUP_BATCH))


def _head_selectors():
    G, R = SSM_N_GROUPS, SSM_HEADS_PER_GROUP
    h = jnp.arange(LANES)[None, :, None]
    g = jnp.arange(G)[:, None, None]
    j64 = jnp.arange(SSM_GROUP_WIDTH)[None, None, :]
    j128 = jnp.arange(R * LANES)[None, None, :]
    e64 = (h == g * R + j64 // SSM_HEADDIM).astype(BF16)
    e128 = (h == g * R + j128 // LANES).astype(BF16)
    return e64, e128


def _ssd(zx, x2, w_dt, conv_w, conv_b, dt_bias, a_log, d_skip, norm_w, batch, seq, moe_weights, moe_layer):
    L, G, W, N = SSM_CHUNK, SSM_N_GROUPS, SSM_GROUP_WIDTH, SSM_D_STATE
    nc = seq // L
    cast = _CastPlan(*moe_weights, moe_layer, batch * nc, lambda b, c: b * nc + c)
    t = batch * seq
    pad = LANES - SSM_N_HEADS
    wdt = jnp.pad(w_dt, ((0, 0), (0, pad)))
    dtb = jnp.pad(dt_bias, (0, pad))[None, :]
    alog = jnp.pad(a_log, (0, pad))[None, :]
    dsk = jnp.repeat(d_skip, SSM_HEADDIM).reshape(G, 1, W)
    e64, e128 = _head_selectors()
    conv_dim = conv_w.shape[1]
    row = lambda b, c: (b * nc + c, 0)
    full2 = lambda b, c: (0, 0)
    full3 = lambda b, c: (0, 0, 0)
    yg, *w_bf16 = pl.pallas_call(
        functools.partial(_ssd_kernel, cast=cast),
        grid=(batch, nc),
        in_specs=[pl.BlockSpec((L, SSM_D_INNER), row),
                  pl.BlockSpec((L, SSM_D_INNER), lambda b, c: (b * nc + c, 1)),
                  pl.BlockSpec((L, 2 * G * N), lambda b, c: (b * nc + c, 2)),
                  pl.BlockSpec((L, D_MODEL), row),
                  pl.BlockSpec((D_MODEL, LANES), full2),
                  pl.BlockSpec((SSM_CONV_K, conv_dim), full2),
                  pl.BlockSpec((1, conv_dim), full2),
                  pl.BlockSpec((1, LANES), full2),
                  pl.BlockSpec((1, LANES), full2),
                  pl.BlockSpec((G, 1, W), full3),
                  pl.BlockSpec((1, SSM_D_INNER), full2),
                  pl.BlockSpec((G, LANES, W), full3),
                  pl.BlockSpec((G, LANES, SSM_HEADS_PER_GROUP * LANES), full3)] + cast.in_specs,
        out_specs=[pl.BlockSpec((L, SSM_D_INNER), row)] + cast.out_specs,
        out_shape=[jax.ShapeDtypeStruct((t, SSM_D_INNER), BF16)] + cast.out_shape,
        scratch_shapes=[pltpu.VMEM((CONV_TAIL + L, SSM_D_INNER), F32),
                        pltpu.VMEM((CONV_TAIL + L, 2 * G * N), F32),
                        pltpu.VMEM((G, N, W), F32),
                        pltpu.VMEM((3, L, LANES), BF16),
                        pltpu.VMEM((2, L, LANES), BF16),
                        pltpu.VMEM((LANES, L), F32)],
        compiler_params=_arb(2),
        name="ssd_scan",
    )(zx, zx, zx, x2, wdt, conv_w, conv_b[None, :], dtb, alog, dsk, norm_w[None, :], e64, e128, *cast.weights)
    return yg, tuple(w_bf16)


def _route(h, wr):
    tm = h.shape[0]
    logits = _dot(h.astype(BF16), wr)
    lane_i = lax.broadcasted_iota(I32, (tm, LANES), 1)
    lane = lane_i.astype(F32)
    neg = -jnp.inf
    big = float(LANES)

    def first_argmax(v, vmax):
        return jnp.min(jnp.where(v == vmax, lane, big), axis=-1, keepdims=True)

    gl = jnp.where((lane_i >= MOE_N_EXPERTS) & (lane_i < MOE_N_EXPERTS + MOE_GROUPS), logits, neg)
    gm = jnp.max(gl, axis=-1, keepdims=True)
    g_sel = first_argmax(gl, gm) - float(MOE_N_EXPERTS)
    g_gate = 1.0 / jnp.sum(jnp.exp(gl - gm), axis=-1, keepdims=True)
    lo = g_sel * float(MOE_EXPERTS_PER_GROUP)
    el = jnp.where((lane >= lo) & (lane < lo + float(MOE_EXPERTS_PER_GROUP)), logits, neg)
    m1 = jnp.max(el, axis=-1, keepdims=True)
    i1 = first_argmax(el, m1)
    el2 = jnp.where(lane == i1, neg, el)
    m2 = jnp.max(el2, axis=-1, keepdims=True)
    i2 = first_argmax(el2, m2)
    p2 = jnp.exp(m2 - m1)
    t1 = 1.0 / (1.0 + p2)
    t2 = p2 / (1.0 + p2)
    eid = jnp.where(lane_i == 0, i1, jnp.where(lane_i == 1, i2, 0.0)).astype(I32)
    wt = jnp.where(lane_i == 0, g_gate * t1, jnp.where(lane_i == 1, g_gate * t2, 0.0))
    count = jnp.sum(jnp.where(lane == i1, 1.0, 0.0) + jnp.where(lane == i2, 1.0, 0.0), axis=0, keepdims=True)
    return eid, wt, count


def _router_weights(w_group, w_expert):
    pad = LANES - MOE_N_EXPERTS - MOE_GROUPS
    return jnp.pad(jnp.concatenate([w_expert, w_group], axis=1), ((0, 0), (0, pad))).astype(BF16)


def _store_token_tiles(ref, v, first_token=0):
    rows = v.shape[0]
    for s in range(ROW_SLABS):
        ref[pl.ds(first_token * ROW_SLABS + s, rows, stride=ROW_SLABS), :] = v[:, s * LANES:(s + 1) * LANES]


def _load_token_tiles(ref, rows):
    return jnp.concatenate([ref[pl.ds(s, rows, stride=ROW_SLABS), :] for s in range(ROW_SLABS)], axis=1)


def _mm_ln_router_kernel(x_ref, w_ref, r_ref, g_ref, b_ref, wr_ref, ht_ref, eid_ref, wt_ref, cnt_ref, wb_ref):
    @pl.when(pl.program_id(0) == 0)
    def _():
        cnt_ref[...] = jnp.zeros(cnt_ref.shape, F32)
        wb_ref[...] = w_ref[...].astype(BF16)

    sub = x_ref.shape[0] // LN_SUBTILES
    rows = [slice(u * sub, (u + 1) * sub) for u in range(LN_SUBTILES)]
    ys = [DEEPNORM_ALPHA * r_ref[rs, :] + _dot(x_ref[rs, :].astype(BF16), wb_ref[...]) for rs in rows]
    hs = [_layer_norm(y, g_ref[...], b_ref[...]) for y in ys]
    routes = [_route(h, wr_ref[...]) for h in hs]
    for u, rs in enumerate(rows):
        _store_token_tiles(ht_ref, hs[u], first_token=u * sub)
        eid_ref[rs, :] = routes[u][0]
        wt_ref[rs, :] = routes[u][1]
    cnt_ref[...] += jnp.broadcast_to(sum(r[2] for r in routes), cnt_ref.shape)


def _mm_ln_router(x, w, resid, g, b, wr, name):
    m, k = x.shape
    d = w.shape[1]
    tm = min(MM_TM, m)
    row = lambda i: (i, 0)
    full = lambda i: (0, 0)
    return pl.pallas_call(
        _mm_ln_router_kernel,
        grid=(m // tm,),
        in_specs=[pl.BlockSpec((tm, k), row),
                  pl.BlockSpec((k, d), full, pipeline_mode=pl.Buffered(1)),
                  pl.BlockSpec((tm, d), row),
                  pl.BlockSpec((1, d), full), pl.BlockSpec((1, d), full), pl.BlockSpec((d, LANES), full)],
        out_specs=[pl.BlockSpec((tm * ROW_SLABS, LANES), row),
                   pl.BlockSpec((tm, LANES), row), pl.BlockSpec((tm, LANES), row),
                   pl.BlockSpec((8, LANES), full)],
        out_shape=[jax.ShapeDtypeStruct((m * ROW_SLABS, LANES), F32),
                   jax.ShapeDtypeStruct((m, LANES), I32), jax.ShapeDtypeStruct((m, LANES), F32),
                   jax.ShapeDtypeStruct((8, LANES), F32)],
        scratch_shapes=[pltpu.VMEM((k, d), BF16)],
        compiler_params=_arb(1),
        name=name,
    )(x, w, resid, g[None, :], b[None, :], wr)


def _dispatch_kernel(eid_ref, cnt_ref, h_ref, dest_ref, be_ref, nu_ref, xs_hbm,
                     base, upper, zbuf, dst_v, dst_s, pe_v, pe_s, sem_z, sem_r, sem_s, *, blk):
    i = pl.program_id(0)
    tm = eid_ref.shape[0]
    n_e = LANES
    eid_t = eid_ref[...].astype(F32).T
    sub = lax.broadcasted_iota(I32, (n_e, tm), 0).astype(F32)
    oh = [(sub == eid_t[k:k + 1, :]).astype(F32) for k in range(MOE_TOP_K)]
    tot = [jnp.sum(o, axis=1, keepdims=True) for o in oh]

    @pl.when(i == 0)
    def _():
        r_i = lax.broadcasted_iota(I32, (tm, tm), 0)
        c_i = lax.broadcasted_iota(I32, (tm, tm), 1)
        upper[...] = (r_i < c_i).astype(BF16)
        counts = jnp.broadcast_to(cnt_ref[0:1, :], (n_e, n_e)).T
        padded = jnp.floor((counts + float(blk - 1)) * (1.0 / blk)) * float(blk)
        r_i = lax.broadcasted_iota(I32, (n_e, n_e), 0)
        c_i = lax.broadcasted_iota(I32, (n_e, n_e), 1)
        tril = (r_i >= c_i).astype(BF16)
        p1, p2, p3 = _split3(padded)
        pends = _dot(tril, p1) + _dot(tril, p2) + _dot(tril, p3)
        base[...] = pends - padded
        nbp = be_ref.shape[1]
        blk_start = lax.broadcasted_iota(I32, (n_e, nbp), 1).astype(F32) * float(blk)
        is_e = lax.broadcasted_iota(I32, (n_e, nbp), 0) < MOE_N_EXPERTS
        done = jnp.where(is_e & (jnp.tile(pends, (1, nbp // LANES)) <= blk_start), 1.0, 0.0)
        be = jnp.minimum(jnp.sum(done, axis=0, keepdims=True), float(MOE_N_EXPERTS - 1))
        be_ref[...] = jnp.broadcast_to(be, be_ref.shape).astype(I32)
        last = pends[MOE_N_EXPERTS - 1:MOE_N_EXPERTS, :] * (1.0 / blk)
        nu_ref[...] = jnp.broadcast_to(last, nu_ref.shape).astype(I32)
        zbuf[...] = jnp.zeros(zbuf.shape, F32)
        row8 = lax.broadcasted_iota(I32, (8, LANES), 0)
        pe_v[...] = jnp.where(row8 == 0, pends.T[0:8, :], counts.T[0:8, :]).astype(I32)
        cp = pltpu.make_async_copy(pe_v, pe_s, sem_s)
        cp.start()
        cp.wait()

        brows = blk * ROW_SLABS

        def zero_copy(e):
            start = pl.multiple_of((pe_s[0, e] - blk) * ROW_SLABS, brows)
            return pltpu.make_async_copy(zbuf, xs_hbm.at[pl.ds(start, brows)], sem_z)

        def tail_copy(b):
            return pltpu.make_async_copy(zbuf, xs_hbm.at[pl.ds(pl.multiple_of(b * brows, brows), brows)], sem_z)

        n_used = lax.shift_right_logical(pe_s[0, MOE_N_EXPERTS - 1], blk.bit_length() - 1)
        n_blocks = xs_hbm.shape[0] // brows
        for e in range(MOE_N_EXPERTS):
            @pl.when(pe_s[1, e] > 0)
            def _():
                zero_copy(e).start()
        lax.fori_loop(n_used, n_blocks, lambda b, c: (tail_copy(b).start(), c)[1], 0)
        for e in range(MOE_N_EXPERTS):
            @pl.when(pe_s[1, e] > 0)
            def _():
                zero_copy(e).wait()
        lax.fori_loop(n_used, n_blocks, lambda b, c: (tail_copy(b).wait(), c)[1], 0)

    b0 = base[:, 0:1]
    c0 = _dot(oh[0].astype(BF16), upper[...])
    c1 = _dot(oh[1].astype(BF16), upper[...])
    d0 = jnp.sum(oh[0] * (b0 + c0), axis=0, keepdims=True)
    d1 = jnp.sum(oh[1] * (b0 + tot[0] + c1), axis=0, keepdims=True)
    base[...] += jnp.broadcast_to(tot[0] + tot[1], base.shape)
    row8 = lax.broadcasted_iota(I32, (8, tm), 0)
    dst = jnp.where(row8 == 0, d0, jnp.where(row8 == 1, d1, 0.0)).astype(I32)
    dest_ref[...] = dst
    dst_v[...] = dst
    cp = pltpu.make_async_copy(dst_v, dst_s, sem_s)
    cp.start()
    cp.wait()

    for r in range(tm):
        for k in range(MOE_TOP_K):
            slot = pl.multiple_of(dst_s[k, r] * ROW_SLABS, ROW_SLABS)
            pltpu.make_async_copy(h_ref.at[pl.ds(r * ROW_SLABS, ROW_SLABS)],
                                  xs_hbm.at[pl.ds(slot, ROW_SLABS)], sem_r).start(priority=k)
    for k in range(MOE_TOP_K):
        pltpu.make_async_copy(h_ref, xs_hbm.at[pl.ds(0, tm * ROW_SLABS)], sem_r).wait()


def _moe_dispatch(ht, eid, counts, blk, name):
    t = ht.shape[0] // ROW_SLABS
    assert blk & (blk - 1) == 0, "block size must be a power of two"
    tm = min(DSP_TM, t)
    nt = t // tm
    n_blocks = -(-(t * MOE_TOP_K) // blk) + MOE_N_EXPERTS
    nbp = -(-n_blocks // LANES) * LANES
    full = lambda i: (0, 0)
    dest, be, nu, xs = pl.pallas_call(
        functools.partial(_dispatch_kernel, blk=blk),
        grid=(nt,),
        in_specs=[pl.BlockSpec((tm, LANES), lambda i: (i, 0)),
                  pl.BlockSpec((8, LANES), full),
                  pl.BlockSpec((tm * ROW_SLABS, LANES), lambda i: (i, 0))],
        out_specs=[pl.BlockSpec((8, tm), lambda i: (0, i)),
                   pl.BlockSpec((8, nbp), full),
                   pl.BlockSpec((8, LANES), full),
                   pl.BlockSpec(memory_space=pl.ANY)],
        out_shape=[jax.ShapeDtypeStruct((8, t), I32), jax.ShapeDtypeStruct((8, nbp), I32),
                   jax.ShapeDtypeStruct((8, LANES), I32),
                   jax.ShapeDtypeStruct((n_blocks * blk * ROW_SLABS, LANES), F32)],
        scratch_shapes=[pltpu.VMEM((LANES, LANES), F32),
                        pltpu.VMEM((tm, tm), BF16), pltpu.VMEM((blk * ROW_SLABS, LANES), F32),
                        pltpu.VMEM((8, tm), I32), pltpu.SMEM((8, tm), I32),
                        pltpu.VMEM((8, LANES), I32), pltpu.SMEM((8, LANES), I32),
                        pltpu.SemaphoreType.DMA(()), pltpu.SemaphoreType.DMA(()), pltpu.SemaphoreType.DMA(())],
        compiler_params=_arb(1),
        name=name,
    )(eid, counts, ht)
    return dest, be[0, :n_blocks], nu[0, :1], xs, n_blocks


def _ffn_kernel(be_ref, nu_ref, x_ref, wg_ref, wu_ref, wd_ref, o_ref):
    i = pl.program_id(0)

    @pl.when(i < nu_ref[0])
    def _():
        blk = x_ref.shape[0] // ROW_SLABS
        xb = _load_token_tiles(x_ref, blk).astype(BF16)
        hid = _silu(_dot(xb, wg_ref[0])) * _dot(xb, wu_ref[0])
        _store_token_tiles(o_ref, _dot(hid.astype(BF16), wd_ref[0]))

    @pl.when(i >= nu_ref[0])
    def _():
        o_ref[...] = jnp.zeros(o_ref.shape, o_ref.dtype)


def _moe_ffn(xs, block_expert, n_used, n_blocks, weights, blk, name):
    w_gate, w_up, w_down = weights
    d, f = w_gate.shape[1], w_gate.shape[2]
    brows = blk * ROW_SLABS
    used = lambda i, be, nu: (jnp.minimum(i, nu[0] - 1), 0)
    every = lambda i, be, nu: (i, 0)
    expert = lambda i, be, nu: (be[i], 0, 0)
    grid_spec = pltpu.PrefetchScalarGridSpec(
        num_scalar_prefetch=2,
        grid=(n_blocks,),
        in_specs=[pl.BlockSpec((brows, LANES), used),
                  pl.BlockSpec((1, d, f), expert), pl.BlockSpec((1, d, f), expert),
                  pl.BlockSpec((1, f, d), expert)],
        out_specs=pl.BlockSpec((brows, LANES), every))
    return pl.pallas_call(
        _ffn_kernel,
        grid_spec=grid_spec,
        out_shape=jax.ShapeDtypeStruct((n_blocks * brows, LANES), F32),
        compiler_params=_arb(1),
        name=name,
    )(block_expert, n_used, xs, w_gate, w_up, w_down)


def _combine_ln_kernel(dst_ref, dstn_ref, ht_ref, wt_ref, g_ref, b_ref, yb_hbm, o_ref, ybuf, sem):
    i = pl.program_id(0)
    n = pl.num_programs(0)
    tm = o_ref.shape[0]

    def start_gather(dref, slot):
        for r in range(tm):
            for k in range(MOE_TOP_K):
                src = pl.multiple_of(dref[k, r] * ROW_SLABS, ROW_SLABS)
                pltpu.make_async_copy(yb_hbm.at[pl.ds(src, ROW_SLABS)],
                                      ybuf.at[slot, k, pl.ds(r * ROW_SLABS, ROW_SLABS)],
                                      sem.at[slot]).start(priority=k)

    @pl.when(i == 0)
    def _():
        start_gather(dst_ref, 0)

    @pl.when(i + 1 < n)
    def _():
        start_gather(dstn_ref, (i + 1) % 2)

    slot = i % 2
    for k in range(MOE_TOP_K):
        pltpu.make_async_copy(yb_hbm.at[pl.ds(0, tm * ROW_SLABS)], ybuf.at[slot, k], sem.at[slot]).wait()
    wt = wt_ref[...]
    ffn = (wt[:, 0:1] * _load_token_tiles(ybuf.at[slot, 0], tm)
           + wt[:, 1:2] * _load_token_tiles(ybuf.at[slot, 1], tm))
    h = _load_token_tiles(ht_ref, tm)
    o_ref[...] = _layer_norm(DEEPNORM_ALPHA * h + ffn, g_ref[...], b_ref[...])


def _combine_ln(ht, wt, dest, yb, g, b, name):
    t, d = ht.shape[0] // ROW_SLABS, D_MODEL
    tm = min(CMB_TM, t)
    nblk = t // tm
    row = lambda i: (i, 0)
    full = lambda i: (0, 0)
    return pl.pallas_call(
        _combine_ln_kernel,
        grid=(nblk,),
        in_specs=[pl.BlockSpec((8, tm), lambda i: (0, i), memory_space=pltpu.SMEM),
                  pl.BlockSpec((8, tm), lambda i: (0, jnp.minimum(i + 1, nblk - 1)),
                               memory_space=pltpu.SMEM),
                  pl.BlockSpec((tm * ROW_SLABS, LANES), row), pl.BlockSpec((tm, LANES), row),
                  pl.BlockSpec((1, d), full), pl.BlockSpec((1, d), full),
                  pl.BlockSpec(memory_space=pl.ANY)],
        out_specs=pl.BlockSpec((tm, d), row),
        out_shape=jax.ShapeDtypeStruct((t, d), F32),
        scratch_shapes=[pltpu.VMEM((2, MOE_TOP_K, tm * ROW_SLABS, LANES), F32), pltpu.SemaphoreType.DMA((2,))],
        compiler_params=_arb(1),
        name=name,
    )(dest, dest, ht, wt, g[None, :], b[None, :], yb)


def _hier_moe_ln(ht, eid, wt, counts, w_bf16, g, b, layer):
    dest, block_expert, n_used, xs, n_blocks = _moe_dispatch(ht, eid, counts, FFN_BLK, f"moe_dispatch{layer}")
    yb = _moe_ffn(xs, block_expert, n_used, n_blocks, w_bf16, FFN_BLK, f"moe_ffn{layer}")
    return _combine_ln(ht, wt, dest, yb, g, b, f"moe_combine_ln{layer}")


def _qkv_rope_kernel(x_ref, wf_ref, pos_ref, inv_ref, o_ref, w_ref):
    @pl.when(pl.program_id(0) == 0)
    def _():
        w_ref[...] = wf_ref[...].astype(BF16)

    xb = x_ref[...].astype(BF16)
    tm = xb.shape[0]
    n = D_MODEL
    ang = pos_ref[...].astype(F32) * inv_ref[...]
    lane = lax.broadcasted_iota(I32, (tm, LANES), 1)
    dd = lane & (ATTN_HEAD_DIM - 1)
    half = ROT_DIM // 2
    cosv = jnp.cos(ang)
    sinv = jnp.sin(ang)
    c_t = jnp.where(dd < ROT_DIM, cosv, 1.0)
    s_up = jnp.where(dd < half, -sinv, 0.0)
    s_dn = jnp.where((dd >= half) & (dd < ROT_DIM), sinv, 0.0)
    for j, sc in ((0, ATTN_HEAD_DIM ** -0.5 * math.log2(math.e)), (1, 1.0)):
        acc = _dot(xb, w_ref[:, j * n:(j + 1) * n])
        c_j, up_j, dn_j = c_t * sc, s_up * sc, s_dn * sc
        for blk in range(n // LANES):
            tt = acc[:, blk * LANES:(blk + 1) * LANES]
            out = tt * c_j + pltpu.roll(tt, LANES - half, 1) * up_j + pltpu.roll(tt, half, 1) * dn_j
            o_ref[:, j * n + blk * LANES:j * n + (blk + 1) * LANES] = out.astype(o_ref.dtype)
    o_ref[:, 2 * n:3 * n] = _dot(xb, w_ref[:, 2 * n:3 * n]).astype(o_ref.dtype)


def _rope_inv_table():
    inv = ROPE_THETA ** (-jnp.arange(0, ROT_DIM, 2, dtype=F32) / ROT_DIM)
    head = jnp.concatenate([inv, inv, jnp.zeros((ATTN_HEAD_DIM - ROT_DIM,), F32)])
    return jnp.tile(head, LANES // ATTN_HEAD_DIM)[None, :]


def _qkv_rope(h, w_qkv, positions):
    m, k = h.shape
    n = w_qkv.shape[1]
    tm = min(MM_TM, m)
    pos = positions.reshape(m, 1)
    return pl.pallas_call(
        _qkv_rope_kernel,
        grid=(m // tm,),
        in_specs=[pl.BlockSpec((tm, k), lambda i: (i, 0)),
                  pl.BlockSpec((k, n), lambda i: (0, 0), pipeline_mode=pl.Buffered(1)),
                  pl.BlockSpec((tm, 1), lambda i: (i, 0)),
                  pl.BlockSpec((1, LANES), lambda i: (0, 0))],
        out_specs=pl.BlockSpec((tm, n), lambda i: (i, 0)),
        out_shape=jax.ShapeDtypeStruct((m, n), BF16),
        scratch_shapes=[pltpu.VMEM((k, n), BF16)],
        compiler_params=_arb(1),
        name="mm_qkv_rope",
    )(h, w_qkv, pos, _rope_inv_table())


def _attn_kernel(q_ref, k_ref, v_ref, lq1_ref, lk1_ref, lq2_ref, lk2_ref, sw_ref, wg_ref, wu_ref, wd_ref,
                 o_ref, wgb_ref, wub_ref, wdb_ref,
                 vx_ref, m0_ref, m1_ref, acc0_ref, acc1_ref, *, lambda_init, cast):
    m_refs = (m0_ref, m1_ref)
    acc_refs = (acc0_ref, acc1_ref)
    jp = pl.program_id(2)
    n_tiles = ATT_TILES_PER_STEP
    tq = q_ref.shape[0] // n_tiles
    cast.emit(pl.program_id(0) * pl.num_programs(1) + pl.program_id(1),
              (wg_ref, wu_ref, wd_ref), (wgb_ref, wub_ref, wdb_ref), extra=jp == 0)

    @pl.when(jp == 0)
    def _():
        vx_ref[:, 0:LANES] = v_ref[...]
        vx_ref[:, LANES:] = jnp.ones((vx_ref.shape[0], LANES), vx_ref.dtype)

    q = q_ref[...]
    lane = lax.broadcasted_iota(I32, q.shape, 1)
    zero = jnp.zeros((), q.dtype)
    qs = (jnp.where(lane < ATTN_HEAD_DIM, q, zero), jnp.where(lane >= ATTN_HEAD_DIM, q, zero))
    for c in range(2):
        m_refs[c][...] = jnp.full(m_refs[c].shape, -jnp.inf, F32)
        acc_refs[c][...] = jnp.zeros(acc_refs[c].shape, F32)

    def step(off, windows):
        off = pl.multiple_of(off, tq)
        widest = max(w for _, w, _ in windows)
        kb = k_ref[pl.ds(off, widest), :]
        vb = vx_ref[pl.ds(off, widest), :]
        chains = [(slice(r0, r0 + tq), w, d, c) for r0, w, d in windows for c in range(2)]
        ss = [lax.dot_general(qs[c][rows, :], kb[:w, :], (((1,), (1,)), ((), ())), preferred_element_type=F32)
              for rows, w, _, c in chains]
        for n, (_, w, d, _) in enumerate(chains):
            if d is not None:
                row_i = lax.broadcasted_iota(I32, (tq, w), 0)
                col_i = lax.broadcasted_iota(I32, (tq, w), 1)
                ss[n] = jnp.where(row_i + d >= col_i, ss[n], -jnp.inf)
        m_prev = [m_refs[c][rows, :] for rows, _, _, c in chains]
        mn = [jnp.maximum(mp, jnp.max(s, axis=-1, keepdims=True)) for mp, s in zip(m_prev, ss)]
        ps = [jnp.exp2(s - jnp.tile(m, (1, s.shape[1] // LANES))).astype(BF16) for s, m in zip(ss, mn)]
        for n, (rows, w, _, c) in enumerate(chains):
            alpha = jnp.exp2(m_prev[n] - mn[n])
            acc_refs[c][rows, :] = jnp.tile(alpha, (1, 2)) * acc_refs[c][rows, :] + _dot(ps[n], vb[:w, :])
            m_refs[c][rows, :] = mn[n]

    def shared(j, carry):
        for p in range(n_tiles // 2):
            step(j * (2 * tq), [((2 * p) * tq, 2 * tq, None), ((2 * p + 1) * tq, 2 * tq, None)])
        return carry
    lax.fori_loop(0, jp * (n_tiles // 2), shared, 0)
    base = jp * (n_tiles * tq)
    for p in range(n_tiles // 2):
        a, b = 2 * p * tq, (2 * p + 1) * tq
        step(base + a, [(a, tq, 0), (b, 2 * tq, tq)])
        later = [(r * tq, 2 * tq, None) for r in range(2 * p + 2, n_tiles)]
        if later:
            step(base + a, later)

    lam = (jnp.exp(jnp.sum(lq1_ref[...] * lk1_ref[...], axis=-1, keepdims=True))
           - jnp.exp(jnp.sum(lq2_ref[...] * lk2_ref[...], axis=-1, keepdims=True)) + lambda_init)
    a1 = acc0_ref[...]
    a2 = acc1_ref[...]
    o = a1[:, :LANES] / a1[:, LANES:] - lam * (a2[:, :LANES] / a2[:, LANES:])
    o = o * lax.rsqrt(jnp.mean(o * o, axis=-1, keepdims=True) + NORM_EPS)
    o_ref[...] = (o * sw_ref[...] * (1.0 - lambda_init)).astype(o_ref.dtype)


def _diff_attention(qkv, lq1, lk1, lq2, lk2, subln_w, lambda_init, batch, seq, moe_weights, moe_layer):
    t = batch * seq
    tq = ATT_TILES_PER_STEP * min(ATT_TQ, seq // ATT_TILES_PER_STEP)
    nq = seq // tq
    h_n = ATTN_N_HEADS
    vec = lambda b, h, i: (0, 0)
    cast = _CastPlan(*moe_weights, moe_layer, batch * h_n, lambda b, h, i: b * h_n + h)
    o, *w_bf16 = pl.pallas_call(
        functools.partial(_attn_kernel, lambda_init=lambda_init, cast=cast),
        grid=(batch, h_n, nq),
        in_specs=[pl.BlockSpec((tq, LANES), lambda b, h, i: (b * nq + i, h)),
                  pl.BlockSpec((seq, LANES), lambda b, h, i: (b, h_n + h)),
                  pl.BlockSpec((seq, LANES), lambda b, h, i: (b, 2 * h_n + h)),
                  pl.BlockSpec((1, ATTN_HEAD_DIM), vec), pl.BlockSpec((1, ATTN_HEAD_DIM), vec),
                  pl.BlockSpec((1, ATTN_HEAD_DIM), vec), pl.BlockSpec((1, ATTN_HEAD_DIM), vec),
                  pl.BlockSpec((1, ATTN_V_DIM), vec)] + cast.in_specs,
        out_specs=[pl.BlockSpec((tq, LANES), lambda b, h, i: (b * nq + i, h))] + cast.out_specs,
        out_shape=[jax.ShapeDtypeStruct((t, h_n * ATTN_V_DIM), BF16)] + cast.out_shape,
        scratch_shapes=[pltpu.VMEM((seq, 2 * LANES), BF16),
                        pltpu.VMEM((tq, LANES), F32), pltpu.VMEM((tq, LANES), F32),
                        pltpu.VMEM((tq, 2 * LANES), F32), pltpu.VMEM((tq, 2 * LANES), F32)],
        compiler_params=_arb(3),
        name="diff_attn",
    )(qkv, qkv, qkv, lq1[None, :], lk1[None, :], lq2[None, :], lk2[None, :], subln_w[None, :], *cast.weights)
    return o, tuple(w_bf16)


def kernel(x, positions, ln_mix_g, ln_mix_b, ln_ffn_g, ln_ffn_b, ssm_w_in, ssm_conv_w, ssm_conv_b, ssm_dt_bias, ssm_a_log, ssm_d, ssm_norm_w, ssm_w_out, attn_w_qkv, attn_lam_q1, attn_lam_k1, attn_lam_q2, attn_lam_k2, attn_subln_w, attn_w_o, moe_w_group, moe_w_expert, moe_w_gate, moe_w_up, moe_w_down):
    batch, seq, d = x.shape
    t = batch * seq
    h = x.reshape(t, d)

    w_in = ssm_w_in[0].astype(BF16)

    moe_weights = (moe_w_gate, moe_w_up, moe_w_down)
    zx = _matmul(h, w_in, SSM_ZX_DIM, F32)
    yg, moe_w0 = _ssd(zx, h, w_in[:, SSM_ZX_DIM:], ssm_conv_w[0], ssm_conv_b[0], ssm_dt_bias[0],
                      ssm_a_log[0], ssm_d[0], ssm_norm_w[0], batch, seq, moe_weights, 0)
    ht, eid, wt, cnt = _mm_ln_router(yg, ssm_w_out[0], h, ln_mix_g[0], ln_mix_b[0],
                                     _router_weights(moe_w_group[0], moe_w_expert[0]), "mm_ssm_out_ln_router")
    h = _hier_moe_ln(ht, eid, wt, cnt, moe_w0, ln_ffn_g[0], ln_ffn_b[0], 0)

    lambda_init = 0.8 - 0.6 * math.exp(-0.3 * 1)
    qkv = _qkv_rope(h, attn_w_qkv[0], positions)
    o, moe_w1 = _diff_attention(qkv, attn_lam_q1[0], attn_lam_k1[0], attn_lam_q2[0], attn_lam_k2[0],
                                attn_subln_w[0], lambda_init, batch, seq, moe_weights, 1)
    ht, eid, wt, cnt = _mm_ln_router(o, attn_w_o[0], h, ln_mix_g[1], ln_mix_b[1],
                                     _router_weights(moe_w_group[1], moe_w_expert[1]), "mm_attn_out_ln_router")
    h = _hier_moe_ln(ht, eid, wt, cnt, moe_w1, ln_ffn_g[1], ln_ffn_b[1], 1)
    return h.reshape(batch, seq, d)
```

```python
import functools
import math

import jax
import jax.numpy as jnp
from jax import lax
from jax.experimental import pallas as pl
from jax.experimental.pallas import tpu as pltpu

F32 = jnp.float32
BF16 = jnp.bfloat16
I32 = jnp.int32

D_MODEL = 1024
DEPTH = 2
SSM_D_INNER = 2048
SSM_HEADDIM = 64
SSM_N_HEADS = 32
SSM_N_GROUPS = 8
SSM_HEADS_PER_GROUP = 4
SSM_D_STATE = 128
SSM_CONV_K = 4
SSM_CHUNK = 128
SSM_GROUP_WIDTH = SSM_HEADS_PER_GROUP * SSM_HEADDIM
SSM_ZX_DIM = 2 * SSM_D_INNER + 2 * SSM_N_GROUPS * SSM_D_STATE
ATTN_HEAD_DIM = 64
ATTN_N_HEADS = 8
ATTN_V_DIM = 128
ROT_DIM = 16
ROPE_THETA = 500000.0
MOE_GROUPS = 4
MOE_EXPERTS_PER_GROUP = 8
MOE_N_EXPERTS = 32
MOE_TOP_K = 2
MOE_D_FF = 512
DEEPNORM_ALPHA = (2 * DEPTH) ** 0.25
NORM_EPS = 1e-5

LANES = 128
ROW_SLABS = D_MODEL // LANES
CONV_TAIL = 8
CONV_ROWS = 128

MM_TM = 512
MM_TN = 1024
FFN_BLK = 512
DSP_TM = 512
SSD_GROUP_BATCH = 2
LN_SUBTILES = 4
CMB_TM = 256
ATT_TQ = 512
ATT_TILES_PER_STEP = 2


def _arb(n):
    return pltpu.CompilerParams(dimension_semantics=("arbitrary",) * n,
                                vmem_limit_bytes=56 * 1024 * 1024)


def _silu(x):
    hx = 0.5 * x
    return hx + hx * jnp.tanh(hx)


def _softplus(x):
    return jnp.maximum(x, 0.0) + jnp.log(1.0 + jnp.exp(-jnp.abs(x)))


def _layer_norm(y, g, b):
    mu = jnp.mean(y, axis=-1, keepdims=True)
    d = y - mu
    var = jnp.mean(d * d, axis=-1, keepdims=True)
    return d * lax.rsqrt(var + NORM_EPS) * g + b


def _split3(a):
    a1 = a.astype(BF16)
    r1 = a - a1.astype(F32)
    a2 = r1.astype(BF16)
    a3 = (r1 - a2.astype(F32)).astype(BF16)
    return a1, a2, a3


def _dot(a, b):
    return jnp.dot(a, b, preferred_element_type=F32)


def _dot_sel(a, sel):
    a1, a2, a3 = _split3(a)
    return _dot(a1, sel) + _dot(a2, sel) + _dot(a3, sel)


def _dot_f32(a, b):
    a1, a2, a3 = _split3(a)
    b1, b2, b3 = _split3(b)
    return (_dot(a1, b1) + _dot(a1, b2) + _dot(a2, b1)
            + _dot(a2, b2) + _dot(a1, b3) + _dot(a3, b1))


class _CastPlan:
    def __init__(self, w_gate, w_up, w_down, layer, steps, step_of):
        self.weights = (w_gate, w_up, w_down)
        self.n_e = w_gate.shape[1]
        self.eps = -(-self.n_e // steps)
        assert self.n_e % self.eps == 0
        self.n_cast = self.n_e // self.eps
        self.stride = steps // self.n_cast
        block = lambda *ids: jnp.minimum(step_of(*ids) // self.stride, self.n_cast - 1)
        self.in_specs = [pl.BlockSpec((1, self.eps) + w.shape[2:], lambda *ids: (layer, block(*ids), 0, 0))
                         for w in self.weights]
        self.out_specs = [pl.BlockSpec((self.eps,) + w.shape[2:], lambda *ids: (block(*ids), 0, 0))
                          for w in self.weights]
        self.out_shape = [jax.ShapeDtypeStruct(w.shape[1:], BF16) for w in self.weights]

    def emit(self, step, w_refs, o_refs, extra=True):
        @pl.when(extra & (step % self.stride == 0) & (step // self.stride < self.n_cast))
        def _():
            for w_ref, o_ref in zip(w_refs, o_refs):
                o_ref[...] = w_ref[0].astype(BF16)


def _mm_kernel(x_ref, w_hbm, o_ref, w_ref, stage, sem):
    n_chunks = o_ref.shape[1] // MM_TN

    @pl.when(pl.program_id(0) == 0)
    def _():
        def chunk_copy(j):
            return pltpu.make_async_copy(w_hbm.at[:, pl.ds(j * MM_TN, MM_TN)], stage.at[j % 2], sem.at[j % 2])

        chunk_copy(0).start()
        for j in range(n_chunks):
            if j + 1 < n_chunks:
                chunk_copy(j + 1).start()
            chunk_copy(j).wait()
            w_ref[:, j * MM_TN:(j + 1) * MM_TN] = stage[j % 2].astype(BF16)

    xb = x_ref[...].astype(BF16)
    for j in range(n_chunks):
        cols = slice(j * MM_TN, (j + 1) * MM_TN)
        o_ref[:, cols] = _dot(xb, w_ref[:, cols]).astype(o_ref.dtype)


def _matmul(x, w, n, out_dtype):
    m, k = x.shape
    tm = min(MM_TM, m)
    return pl.pallas_call(
        _mm_kernel,
        grid=(m // tm,),
        in_specs=[pl.BlockSpec((tm, k), lambda i: (i, 0)), pl.BlockSpec(memory_space=pl.ANY)],
        out_specs=pl.BlockSpec((tm, n), lambda i: (i, 0)),
        out_shape=jax.ShapeDtypeStruct((m, n), out_dtype),
        scratch_shapes=[pltpu.VMEM((k, n), BF16), pltpu.VMEM((2, k, MM_TN), F32),
                        pltpu.SemaphoreType.DMA((2,))],
        compiler_params=_arb(1),
        name="mm_inproj",
    )(x, w)


def _ssd_kernel(z_ref, xs_ref, bc_ref, x_ref, wdt_ref, cw_ref, cb_ref, dtb_ref, alog_ref, dskip_ref,
                nw_ref, e64_ref, e128_ref, wg_ref, wu_ref, wd_ref, o_ref, wgb_ref, wub_ref, wdb_ref,
                stage_x, stage_bc, state, acp_s, dtp_s, act_s, *, cast):
    G, W, N = SSM_N_GROUPS, SSM_GROUP_WIDTH, SSM_D_STATE
    L = z_ref.shape[0]
    c = pl.program_id(1)
    cast.emit(pl.program_id(0) * pl.num_programs(1) + c, (wg_ref, wu_ref, wd_ref), (wgb_ref, wub_ref, wdb_ref))

    @pl.when(c == 0)
    def _():
        state[...] = jnp.zeros(state.shape, F32)
        for stage in (stage_x, stage_bc):
            stage[0:CONV_TAIL, :] = jnp.zeros((CONV_TAIL, stage.shape[1]), F32)

    @pl.when(c != 0)
    def _():
        for stage in (stage_x, stage_bc):
            stage[0:CONV_TAIL, :] = stage[L:L + CONV_TAIL, :]

    stage_x[CONV_TAIL:CONV_TAIL + L, :] = xs_ref[...]
    stage_bc[CONV_TAIL:CONV_TAIL + L, :] = bc_ref[...]

    def conv(stage, c0, width, w0):
        taps = [cw_ref[k:k + 1, w0:w0 + width] for k in range(SSM_CONV_K)]
        bias = cb_ref[:, w0:w0 + width]
        outs = []
        for rb in range(L // CONV_ROWS):
            r0 = rb * CONV_ROWS
            u = stage[r0:r0 + CONV_ROWS + CONV_TAIL, c0:c0 + width]
            out = bias + taps[SSM_CONV_K - 1] * u[CONV_TAIL:, :]
            for k in range(SSM_CONV_K - 1):
                back = SSM_CONV_K - 1 - k
                out = out + taps[k] * pltpu.roll(u, back, 0)[CONV_TAIL:, :]
            outs.append(_silu(out))
        return jnp.concatenate(outs, axis=0)

    xc, bcv, ccv = {}, {}, {}

    def conv_groups(gs):
        for g in gs:
            xc[g] = conv(stage_x, g * W, W, g * W)
            bcv[g] = conv(stage_bc, g * N, N, SSM_D_INNER + g * N).astype(BF16)
            ccv[g] = conv(stage_bc, G * N + g * N, N, SSM_D_INNER + G * N + g * N).astype(BF16)

    x_of, b_of, c_of = xc.__getitem__, bcv.__getitem__, ccv.__getitem__

    dt = _softplus(_dot(x_ref[...].astype(BF16), wdt_ref[...].astype(BF16)) + dtb_ref[...])
    a = -jnp.exp(alog_ref[...])
    row_i = lax.broadcasted_iota(I32, (L, L), 0)
    col_i = lax.broadcasted_iota(I32, (L, L), 1)
    causal = row_i >= col_i
    a1, a2, a3 = _split3(dt * a)
    tri = causal.astype(BF16)
    ac = (_dot(tri, a1) + _dot(tri, a2) + _dot(tri, a3)) * math.log2(math.e)
    act_s[...] = ac.T
    for n, part in enumerate(_split3(ac)):
        acp_s[n] = part
    for n, part in enumerate(_split3(dt)[:2]):
        dtp_s[n] = part
    lane = lax.broadcasted_iota(I32, (L, LANES), 1)
    first_half = lane < SSM_HEADDIM

    def select(parts_ref, n_parts, sel):
        out = _dot(parts_ref[0], sel)
        for n in range(1, n_parts):
            out = out + _dot(parts_ref[n], sel)
        return out

    def run_groups(gs):
        conv_groups(gs)
        col4 = {g: select(acp_s, 3, e128_ref[g]) for g in gs}
        dt_e = {g: select(dtp_s, 2, e64_ref[g]) for g in gs}
        cb = {g: lax.dot_general(c_of(g), b_of(g), (((1,), (1,)), ((), ())), preferred_element_type=F32)
              for g in gs}
        y_cs = {g: _dot(c_of(g), state[g].astype(BF16)) for g in gs}
        a_e = {g: jnp.concatenate(
            [jnp.where(first_half, col4[g][:, 2 * p * LANES:(2 * p + 1) * LANES],
                       col4[g][:, (2 * p + 1) * LANES:(2 * p + 2) * LANES]) for p in range(2)], axis=1)
            for g in gs}
        xdt = {g: x_of(g) * dt_e[g] for g in gs}
        xdt_b = {g: xdt[g].astype(BF16) for g in gs}
        ys = {g: [None] * SSM_HEADS_PER_GROUP for g in gs}
        for r in range(SSM_HEADS_PER_GROUP):
            for g in gs:
                row = act_s[SSM_HEADS_PER_GROUP * g + r:SSM_HEADS_PER_GROUP * g + r + 1, :]
                seg = col4[g][:, LANES * r:LANES * (r + 1)] - row
                dec = jnp.where(causal, jnp.exp2(seg), 0.0)
                xp = xdt_b[g][:, LANES * (r // 2):LANES * (r // 2 + 1)]
                ys[g][r] = _dot((cb[g] * dec).astype(BF16), xp)
        for g in gs:
            y_diag = jnp.concatenate([jnp.where(first_half, ys[g][2 * p], ys[g][2 * p + 1])
                                      for p in range(2)], axis=1)
            al_e = a_e[g][L - 1:L, :]
            new = lax.dot_general(b_of(g), (xdt[g] * jnp.exp2(al_e - a_e[g])).astype(BF16),
                                  (((0,), (0,)), ((), ())), preferred_element_type=F32)
            y = y_diag + y_cs[g] * jnp.exp2(a_e[g]) + x_of(g) * dskip_ref[g]
            state[g] = state[g] * jnp.exp2(al_e) + new
            yg = y * _silu(z_ref[:, g * W:(g + 1) * W])
            ms = jnp.mean(yg * yg, axis=-1, keepdims=True)
            o_ref[:, g * W:(g + 1) * W] = (yg * lax.rsqrt(ms + NORM_EPS)
                                           * nw_ref[:, g * W:(g + 1) * W]).astype(o_ref.dtype)

    for g0 in range(0, G, SSD_GROUP_BATCH):
        run_groups(range(g0, g0 + SSD_GROUP_BATCH))


def _head_selectors():
    G, R = SSM_N_GROUPS, SSM_HEADS_PER_GROUP
    h = jnp.arange(LANES)[None, :, None]
    g = jnp.arange(G)[:, None, None]
    j64 = jnp.arange(SSM_GROUP_WIDTH)[None, None, :]
    j128 = jnp.arange(R * LANES)[None, None, :]
    e64 = (h == g * R + j64 // SSM_HEADDIM).astype(BF16)
    e128 = (h == g * R + j128 // LANES).astype(BF16)
    return e64, e128


def _ssd(zx, x2, w_dt, conv_w, conv_b, dt_bias, a_log, d_skip, norm_w, batch, seq, moe_weights, moe_layer):
    L, G, W, N = SSM_CHUNK, SSM_N_GROUPS, SSM_GROUP_WIDTH, SSM_D_STATE
    nc = seq // L
    cast = _CastPlan(*moe_weights, moe_layer, batch * nc, lambda b, c: b * nc + c)
    t = batch * seq
    pad = LANES - SSM_N_HEADS
    wdt = jnp.pad(w_dt, ((0, 0), (0, pad)))
    dtb = jnp.pad(dt_bias, (0, pad))[None, :]
    alog = jnp.pad(a_log, (0, pad))[None, :]
    dsk = jnp.repeat(d_skip, SSM_HEADDIM).reshape(G, 1, W)
    e64, e128 = _head_selectors()
    conv_dim = conv_w.shape[1]
    row = lambda b, c: (b * nc + c, 0)
    full2 = lambda b, c: (0, 0)
    full3 = lambda b, c: (0, 0, 0)
    yg, *w_bf16 = pl.pallas_call(
        functools.partial(_ssd_kernel, cast=cast),
        grid=(batch, nc),
        in_specs=[pl.BlockSpec((L, SSM_D_INNER), row),
                  pl.BlockSpec((L, SSM_D_INNER), lambda b, c: (b * nc + c, 1)),
                  pl.BlockSpec((L, 2 * G * N), lambda b, c: (b * nc + c, 2)),
                  pl.BlockSpec((L, D_MODEL), row),
                  pl.BlockSpec((D_MODEL, LANES), full2),
                  pl.BlockSpec((SSM_CONV_K, conv_dim), full2),
                  pl.BlockSpec((1, conv_dim), full2),
                  pl.BlockSpec((1, LANES), full2),
                  pl.BlockSpec((1, LANES), full2),
                  pl.BlockSpec((G, 1, W), full3),
                  pl.BlockSpec((1, SSM_D_INNER), full2),
                  pl.BlockSpec((G, LANES, W), full3),
                  pl.BlockSpec((G, LANES, SSM_HEADS_PER_GROUP * LANES), full3)] + cast.in_specs,
        out_specs=[pl.BlockSpec((L, SSM_D_INNER), row)] + cast.out_specs,
        out_shape=[jax.ShapeDtypeStruct((t, SSM_D_INNER), BF16)] + cast.out_shape,
        scratch_shapes=[pltpu.VMEM((CONV_TAIL + L, SSM_D_INNER), F32),
                        pltpu.VMEM((CONV_TAIL + L, 2 * G * N), F32),
                        pltpu.VMEM((G, N, W), F32),
                        pltpu.VMEM((3, L, LANES), BF16),
                        pltpu.VMEM((2, L, LANES), BF16),
                        pltpu.VMEM((LANES, L), F32)],
        compiler_params=_arb(2),
        name="ssd_scan",
    )(zx, zx, zx, x2, wdt, conv_w, conv_b[None, :], dtb, alog, dsk, norm_w[None, :], e64, e128, *cast.weights)
    return yg, tuple(w_bf16)


def _route(h, wr):
    tm = h.shape[0]
    logits = _dot(h.astype(BF16), wr)
    lane_i = lax.broadcasted_iota(I32, (tm, LANES), 1)
    lane = lane_i.astype(F32)
    neg = -jnp.inf
    big = float(LANES)

    def first_argmax(v, vmax):
        return jnp.min(jnp.where(v == vmax, lane, big), axis=-1, keepdims=True)

    gl = jnp.where((lane_i >= MOE_N_EXPERTS) & (lane_i < MOE_N_EXPERTS + MOE_GROUPS), logits, neg)
    gm = jnp.max(gl, axis=-1, keepdims=True)
    g_sel = first_argmax(gl, gm) - float(MOE_N_EXPERTS)
    g_gate = 1.0 / jnp.sum(jnp.exp(gl - gm), axis=-1, keepdims=True)
    lo = g_sel * float(MOE_EXPERTS_PER_GROUP)
    el = jnp.where((lane >= lo) & (lane < lo + float(MOE_EXPERTS_PER_GROUP)), logits, neg)
    m1 = jnp.max(el, axis=-1, keepdims=True)
    i1 = first_argmax(el, m1)
    el2 = jnp.where(lane == i1, neg, el)
    m2 = jnp.max(el2, axis=-1, keepdims=True)
    i2 = first_argmax(el2, m2)
    p2 = jnp.exp(m2 - m1)
    t1 = 1.0 / (1.0 + p2)
    t2 = p2 / (1.0 + p2)
    eid = jnp.where(lane_i == 0, i1, jnp.where(lane_i == 1, i2, 0.0)).astype(I32)
    wt = jnp.where(lane_i == 0, g_gate * t1, jnp.where(lane_i == 1, g_gate * t2, 0.0))
    count = jnp.sum(jnp.where(lane == i1, 1.0, 0.0) + jnp.where(lane == i2, 1.0, 0.0), axis=0, keepdims=True)
    return eid, wt, count


def _router_weights(w_group, w_expert):
    pad = LANES - MOE_N_EXPERTS - MOE_GROUPS
    return jnp.pad(jnp.concatenate([w_expert, w_group], axis=1), ((0, 0), (0, pad))).astype(BF16)


def _store_token_tiles(ref, v, first_token=0):
    rows = v.shape[0]
    for s in range(ROW_SLABS):
        ref[pl.ds(first_token * ROW_SLABS + s, rows, stride=ROW_SLABS), :] = v[:, s * LANES:(s + 1) * LANES]


def _load_token_tiles(ref, rows):
    return jnp.concatenate([ref[pl.ds(s, rows, stride=ROW_SLABS), :] for s in range(ROW_SLABS)], axis=1)


def _mm_ln_router_kernel(x_ref, w_ref, r_ref, g_ref, b_ref, wr_ref, ht_ref, eid_ref, wt_ref, cnt_ref, wb_ref):
    @pl.when(pl.program_id(0) == 0)
    def _():
        cnt_ref[...] = jnp.zeros(cnt_ref.shape, F32)
        wb_ref[...] = w_ref[...].astype(BF16)

    sub = x_ref.shape[0] // LN_SUBTILES
    rows = [slice(u * sub, (u + 1) * sub) for u in range(LN_SUBTILES)]
    ys = [DEEPNORM_ALPHA * r_ref[rs, :] + _dot(x_ref[rs, :].astype(BF16), wb_ref[...]) for rs in rows]
    hs = [_layer_norm(y, g_ref[...], b_ref[...]) for y in ys]
    routes = [_route(h, wr_ref[...]) for h in hs]
    for u, rs in enumerate(rows):
        _store_token_tiles(ht_ref, hs[u], first_token=u * sub)
        eid_ref[rs, :] = routes[u][0]
        wt_ref[rs, :] = routes[u][1]
    cnt_ref[...] += jnp.broadcast_to(sum(r[2] for r in routes), cnt_ref.shape)


def _mm_ln_router(x, w, resid, g, b, wr, name):
    m, k = x.shape
    d = w.shape[1]
    tm = min(MM_TM, m)
    row = lambda i: (i, 0)
    full = lambda i: (0, 0)
    return pl.pallas_call(
        _mm_ln_router_kernel,
        grid=(m // tm,),
        in_specs=[pl.BlockSpec((tm, k), row),
                  pl.BlockSpec((k, d), full, pipeline_mode=pl.Buffered(1)),
                  pl.BlockSpec((tm, d), row),
                  pl.BlockSpec((1, d), full), pl.BlockSpec((1, d), full), pl.BlockSpec((d, LANES), full)],
        out_specs=[pl.BlockSpec((tm * ROW_SLABS, LANES), row),
                   pl.BlockSpec((tm, LANES), row), pl.BlockSpec((tm, LANES), row),
                   pl.BlockSpec((8, LANES), full)],
        out_shape=[jax.ShapeDtypeStruct((m * ROW_SLABS, LANES), F32),
                   jax.ShapeDtypeStruct((m, LANES), I32), jax.ShapeDtypeStruct((m, LANES), F32),
                   jax.ShapeDtypeStruct((8, LANES), F32)],
        scratch_shapes=[pltpu.VMEM((k, d), BF16)],
        compiler_params=_arb(1),
        name=name,
    )(x, w, resid, g[None, :], b[None, :], wr)


def _dispatch_kernel(eid_ref, cnt_ref, h_ref, dest_ref, be_ref, nu_ref, xs_hbm,
                     base, upper, zbuf, dst_v, dst_s, pe_v, pe_s, sem_z, sem_r, sem_s, *, blk):
    i = pl.program_id(0)
    tm = eid_ref.shape[0]
    n_e = LANES
    eid_t = eid_ref[...].astype(F32).T
    sub = lax.broadcasted_iota(I32, (n_e, tm), 0).astype(F32)
    oh = [(sub == eid_t[k:k + 1, :]).astype(F32) for k in range(MOE_TOP_K)]
    tot = [jnp.sum(o, axis=1, keepdims=True) for o in oh]

    @pl.when(i == 0)
    def _():
        r_i = lax.broadcasted_iota(I32, (tm, tm), 0)
        c_i = lax.broadcasted_iota(I32, (tm, tm), 1)
        upper[...] = (r_i < c_i).astype(BF16)
        counts = jnp.broadcast_to(cnt_ref[0:1, :], (n_e, n_e)).T
        padded = jnp.floor((counts + float(blk - 1)) * (1.0 / blk)) * float(blk)
        r_i = lax.broadcasted_iota(I32, (n_e, n_e), 0)
        c_i = lax.broadcasted_iota(I32, (n_e, n_e), 1)
        tril = (r_i >= c_i).astype(BF16)
        p1, p2, p3 = _split3(padded)
        pends = _dot(tril, p1) + _dot(tril, p2) + _dot(tril, p3)
        base[...] = pends - padded
        nbp = be_ref.shape[1]
        blk_start = lax.broadcasted_iota(I32, (n_e, nbp), 1).astype(F32) * float(blk)
        is_e = lax.broadcasted_iota(I32, (n_e, nbp), 0) < MOE_N_EXPERTS
        done = jnp.where(is_e & (jnp.tile(pends, (1, nbp // LANES)) <= blk_start), 1.0, 0.0)
        be = jnp.minimum(jnp.sum(done, axis=0, keepdims=True), float(MOE_N_EXPERTS - 1))
        be_ref[...] = jnp.broadcast_to(be, be_ref.shape).astype(I32)
        last = pends[MOE_N_EXPERTS - 1:MOE_N_EXPERTS, :] * (1.0 / blk)
        nu_ref[...] = jnp.broadcast_to(last, nu_ref.shape).astype(I32)
        zbuf[...] = jnp.zeros(zbuf.shape, F32)
        row8 = lax.broadcasted_iota(I32, (8, LANES), 0)
        pe_v[...] = jnp.where(row8 == 0, pends.T[0:8, :], counts.T[0:8, :]).astype(I32)
        cp = pltpu.make_async_copy(pe_v, pe_s, sem_s)
        cp.start()
        cp.wait()

        brows = blk * ROW_SLABS

        def zero_copy(e):
            start = pl.multiple_of((pe_s[0, e] - blk) * ROW_SLABS, brows)
            return pltpu.make_async_copy(zbuf, xs_hbm.at[pl.ds(start, brows)], sem_z)

        def tail_copy(b):
            return pltpu.make_async_copy(zbuf, xs_hbm.at[pl.ds(pl.multiple_of(b * brows, brows), brows)], sem_z)

        n_used = lax.shift_right_logical(pe_s[0, MOE_N_EXPERTS - 1], blk.bit_length() - 1)
        n_blocks = xs_hbm.shape[0] // brows
        for e in range(MOE_N_EXPERTS):
            @pl.when(pe_s[1, e] > 0)
            def _():
                zero_copy(e).start()
        lax.fori_loop(n_used, n_blocks, lambda b, c: (tail_copy(b).start(), c)[1], 0)
        for e in range(MOE_N_EXPERTS):
            @pl.when(pe_s[1, e] > 0)
            def _():
                zero_copy(e).wait()
        lax.fori_loop(n_used, n_blocks, lambda b, c: (tail_copy(b).wait(), c)[1], 0)

    b0 = base[:, 0:1]
    c0 = _dot(oh[0].astype(BF16), upper[...])
    c1 = _dot(oh[1].astype(BF16), upper[...])
    d0 = jnp.sum(oh[0] * (b0 + c0), axis=0, keepdims=True)
    d1 = jnp.sum(oh[1] * (b0 + tot[0] + c1), axis=0, keepdims=True)
    base[...] += jnp.broadcast_to(tot[0] + tot[1], base.shape)
    row8 = lax.broadcasted_iota(I32, (8, tm), 0)
    dst = jnp.where(row8 == 0, d0, jnp.where(row8 == 1, d1, 0.0)).astype(I32)
    dest_ref[...] = dst
    dst_v[...] = dst
    cp = pltpu.make_async_copy(dst_v, dst_s, sem_s)
    cp.start()
    cp.wait()

    for r in range(tm):
        for k in range(MOE_TOP_K):
            slot = pl.multiple_of(dst_s[k, r] * ROW_SLABS, ROW_SLABS)
            pltpu.make_async_copy(h_ref.at[pl.ds(r * ROW_SLABS, ROW_SLABS)],
                                  xs_hbm.at[pl.ds(slot, ROW_SLABS)], sem_r).start(priority=k)
    for k in range(MOE_TOP_K):
        pltpu.make_async_copy(h_ref, xs_hbm.at[pl.ds(0, tm * ROW_SLABS)], sem_r).wait()


def _moe_dispatch(ht, eid, counts, blk, name):
    t = ht.shape[0] // ROW_SLABS
    assert blk & (blk - 1) == 0, "block size must be a power of two"
    tm = min(DSP_TM, t)
    nt = t // tm
    n_blocks = -(-(t * MOE_TOP_K) // blk) + MOE_N_EXPERTS
    nbp = -(-n_blocks // LANES) * LANES
    full = lambda i: (0, 0)
    dest, be, nu, xs = pl.pallas_call(
        functools.partial(_dispatch_kernel, blk=blk),
        grid=(nt,),
        in_specs=[pl.BlockSpec((tm, LANES), lambda i: (i, 0)),
                  pl.BlockSpec((8, LANES), full),
                  pl.BlockSpec((tm * ROW_SLABS, LANES), lambda i: (i, 0))],
        out_specs=[pl.BlockSpec((8, tm), lambda i: (0, i)),
                   pl.BlockSpec((8, nbp), full),
                   pl.BlockSpec((8, LANES), full),
                   pl.BlockSpec(memory_space=pl.ANY)],
        out_shape=[jax.ShapeDtypeStruct((8, t), I32), jax.ShapeDtypeStruct((8, nbp), I32),
                   jax.ShapeDtypeStruct((8, LANES), I32),
                   jax.ShapeDtypeStruct((n_blocks * blk * ROW_SLABS, LANES), F32)],
        scratch_shapes=[pltpu.VMEM((LANES, LANES), F32),
                        pltpu.VMEM((tm, tm), BF16), pltpu.VMEM((blk * ROW_SLABS, LANES), F32),
                        pltpu.VMEM((8, tm), I32), pltpu.SMEM((8, tm), I32),
                        pltpu.VMEM((8, LANES), I32), pltpu.SMEM((8, LANES), I32),
                        pltpu.SemaphoreType.DMA(()), pltpu.SemaphoreType.DMA(()), pltpu.SemaphoreType.DMA(())],
        compiler_params=_arb(1),
        name=name,
    )(eid, counts, ht)
    return dest, be[0, :n_blocks], nu[0, :1], xs, n_blocks


def _ffn_kernel(be_ref, nu_ref, x_ref, wg_ref, wu_ref, wd_ref, o_ref):
    i = pl.program_id(0)

    @pl.when(i < nu_ref[0])
    def _():
        blk = x_ref.shape[0] // ROW_SLABS
        xb = _load_token_tiles(x_ref, blk).astype(BF16)
        hid = _silu(_dot(xb, wg_ref[0])) * _dot(xb, wu_ref[0])
        _store_token_tiles(o_ref, _dot(hid.astype(BF16), wd_ref[0]))

    @pl.when(i >= nu_ref[0])
    def _():
        o_ref[...] = jnp.zeros(o_ref.shape, o_ref.dtype)


def _moe_ffn(xs, block_expert, n_used, n_blocks, weights, blk, name):
    w_gate, w_up, w_down = weights
    d, f = w_gate.shape[1], w_gate.shape[2]
    brows = blk * ROW_SLABS
    used = lambda i, be, nu: (jnp.minimum(i, nu[0] - 1), 0)
    every = lambda i, be, nu: (i, 0)
    expert = lambda i, be, nu: (be[i], 0, 0)
    grid_spec = pltpu.PrefetchScalarGridSpec(
        num_scalar_prefetch=2,
        grid=(n_blocks,),
        in_specs=[pl.BlockSpec((brows, LANES), used),
                  pl.BlockSpec((1, d, f), expert), pl.BlockSpec((1, d, f), expert),
                  pl.BlockSpec((1, f, d), expert)],
        out_specs=pl.BlockSpec((brows, LANES), every))
    return pl.pallas_call(
        _ffn_kernel,
        grid_spec=grid_spec,
        out_shape=jax.ShapeDtypeStruct((n_blocks * brows, LANES), F32),
        compiler_params=_arb(1),
        name=name,
    )(block_expert, n_used, xs, w_gate, w_up, w_down)


def _combine_ln_kernel(dst_ref, dstn_ref, ht_ref, wt_ref, g_ref, b_ref, yb_hbm, o_ref, ybuf, sem):
    i = pl.program_id(0)
    n = pl.num_programs(0)
    tm = o_ref.shape[0]

    def start_gather(dref, slot):
        for r in range(tm):
            for k in range(MOE_TOP_K):
                src = pl.multiple_of(dref[k, r] * ROW_SLABS, ROW_SLABS)
                pltpu.make_async_copy(yb_hbm.at[pl.ds(src, ROW_SLABS)],
                                      ybuf.at[slot, k, pl.ds(r * ROW_SLABS, ROW_SLABS)],
                                      sem.at[slot]).start(priority=k)

    @pl.when(i == 0)
    def _():
        start_gather(dst_ref, 0)

    @pl.when(i + 1 < n)
    def _():
        start_gather(dstn_ref, (i + 1) % 2)

    slot = i % 2
    for k in range(MOE_TOP_K):
        pltpu.make_async_copy(yb_hbm.at[pl.ds(0, tm * ROW_SLABS)], ybuf.at[slot, k], sem.at[slot]).wait()
    wt = wt_ref[...]
    ffn = (wt[:, 0:1] * _load_token_tiles(ybuf.at[slot, 0], tm)
           + wt[:, 1:2] * _load_token_tiles(ybuf.at[slot, 1], tm))
    h = _load_token_tiles(ht_ref, tm)
    o_ref[...] = _layer_norm(DEEPNORM_ALPHA * h + ffn, g_ref[...], b_ref[...])


def _combine_ln(ht, wt, dest, yb, g, b, name):
    t, d = ht.shape[0] // ROW_SLABS, D_MODEL
    tm = min(CMB_TM, t)
    nblk = t // tm
    row = lambda i: (i, 0)
    full = lambda i: (0, 0)
    return pl.pallas_call(
        _combine_ln_kernel,
        grid=(nblk,),
        in_specs=[pl.BlockSpec((8, tm), lambda i: (0, i), memory_space=pltpu.SMEM),
                  pl.BlockSpec((8, tm), lambda i: (0, jnp.minimum(i + 1, nblk - 1)),
                               memory_space=pltpu.SMEM),
                  pl.BlockSpec((tm * ROW_SLABS, LANES), row), pl.BlockSpec((tm, LANES), row),
                  pl.BlockSpec((1, d), full), pl.BlockSpec((1, d), full),
                  pl.BlockSpec(memory_space=pl.ANY)],
        out_specs=pl.BlockSpec((tm, d), row),
        out_shape=jax.ShapeDtypeStruct((t, d), F32),
        scratch_shapes=[pltpu.VMEM((2, MOE_TOP_K, tm * ROW_SLABS, LANES), F32), pltpu.SemaphoreType.DMA((2,))],
        compiler_params=_arb(1),
        name=name,
    )(dest, dest, ht, wt, g[None, :], b[None, :], yb)


def _hier_moe_ln(ht, eid, wt, counts, w_bf16, g, b, layer):
    dest, block_expert, n_used, xs, n_blocks = _moe_dispatch(ht, eid, counts, FFN_BLK, f"moe_dispatch{layer}")
    yb = _moe_ffn(xs, block_expert, n_used, n_blocks, w_bf16, FFN_BLK, f"moe_ffn{layer}")
    return _combine_ln(ht, wt, dest, yb, g, b, f"moe_combine_ln{layer}")


def _qkv_rope_kernel(x_ref, wf_ref, pos_ref, inv_ref, o_ref, w_ref):
    @pl.when(pl.program_id(0) == 0)
    def _():
        w_ref[...] = wf_ref[...].astype(BF16)

    xb = x_ref[...].astype(BF16)
    tm = xb.shape[0]
    n = D_MODEL
    ang = pos_ref[...].astype(F32) * inv_ref[...]
    lane = lax.broadcasted_iota(I32, (tm, LANES), 1)
    dd = lane & (ATTN_HEAD_DIM - 1)
    half = ROT_DIM // 2
    cosv = jnp.cos(ang)
    sinv = jnp.sin(ang)
    c_t = jnp.where(dd < ROT_DIM, cosv, 1.0)
    s_up = jnp.where(dd < half, -sinv, 0.0)
    s_dn = jnp.where((dd >= half) & (dd < ROT_DIM), sinv, 0.0)
    for j, sc in ((0, ATTN_HEAD_DIM ** -0.5 * math.log2(math.e)), (1, 1.0)):
        acc = _dot(xb, w_ref[:, j * n:(j + 1) * n])
        c_j, up_j, dn_j = c_t * sc, s_up * sc, s_dn * sc
        for blk in range(n // LANES):
            tt = acc[:, blk * LANES:(blk + 1) * LANES]
            out = tt * c_j + pltpu.roll(tt, LANES - half, 1) * up_j + pltpu.roll(tt, half, 1) * dn_j
            o_ref[:, j * n + blk * LANES:j * n + (blk + 1) * LANES] = out.astype(o_ref.dtype)
    o_ref[:, 2 * n:3 * n] = _dot(xb, w_ref[:, 2 * n:3 * n]).astype(o_ref.dtype)


def _rope_inv_table():
    inv = ROPE_THETA ** (-jnp.arange(0, ROT_DIM, 2, dtype=F32) / ROT_DIM)
    head = jnp.concatenate([inv, inv, jnp.zeros((ATTN_HEAD_DIM - ROT_DIM,), F32)])
    return jnp.tile(head, LANES // ATTN_HEAD_DIM)[None, :]


def _qkv_rope(h, w_qkv, positions):
    m, k = h.shape
    n = w_qkv.shape[1]
    tm = min(MM_TM, m)
    pos = positions.reshape(m, 1)
    return pl.pallas_call(
        _qkv_rope_kernel,
        grid=(m // tm,),
        in_specs=[pl.BlockSpec((tm, k), lambda i: (i, 0)),
                  pl.BlockSpec((k, n), lambda i: (0, 0), pipeline_mode=pl.Buffered(1)),
                  pl.BlockSpec((tm, 1), lambda i: (i, 0)),
                  pl.BlockSpec((1, LANES), lambda i: (0, 0))],
        out_specs=pl.BlockSpec((tm, n), lambda i: (i, 0)),
        out_shape=jax.ShapeDtypeStruct((m, n), BF16),
        scratch_shapes=[pltpu.VMEM((k, n), BF16)],
        compiler_params=_arb(1),
        name="mm_qkv_rope",
    )(h, w_qkv, pos, _rope_inv_table())


def _attn_kernel(q_ref, k_ref, v_ref, lq1_ref, lk1_ref, lq2_ref, lk2_ref, sw_ref, wg_ref, wu_ref, wd_ref,
                 o_ref, wgb_ref, wub_ref, wdb_ref,
                 vx_ref, m0_ref, m1_ref, acc0_ref, acc1_ref, *, lambda_init, cast):
    m_refs = (m0_ref, m1_ref)
    acc_refs = (acc0_ref, acc1_ref)
    jp = pl.program_id(2)
    n_tiles = ATT_TILES_PER_STEP
    tq = q_ref.shape[0] // n_tiles
    cast.emit(pl.program_id(0) * pl.num_programs(1) + pl.program_id(1),
              (wg_ref, wu_ref, wd_ref), (wgb_ref, wub_ref, wdb_ref), extra=jp == 0)

    @pl.when(jp == 0)
    def _():
        vx_ref[:, 0:LANES] = v_ref[...]
        vx_ref[:, LANES:] = jnp.ones((vx_ref.shape[0], LANES), vx_ref.dtype)

    q = q_ref[...]
    lane = lax.broadcasted_iota(I32, q.shape, 1)
    zero = jnp.zeros((), q.dtype)
    qs = (jnp.where(lane < ATTN_HEAD_DIM, q, zero), jnp.where(lane >= ATTN_HEAD_DIM, q, zero))
    for c in range(2):
        m_refs[c][...] = jnp.full(m_refs[c].shape, -jnp.inf, F32)
        acc_refs[c][...] = jnp.zeros(acc_refs[c].shape, F32)

    def step(off, windows):
        off = pl.multiple_of(off, tq)
        widest = max(w for _, w, _ in windows)
        kb = k_ref[pl.ds(off, widest), :]
        vb = vx_ref[pl.ds(off, widest), :]
        chains = [(slice(r0, r0 + tq), w, d, c) for r0, w, d in windows for c in range(2)]
        ss = [lax.dot_general(qs[c][rows, :], kb[:w, :], (((1,), (1,)), ((), ())), preferred_element_type=F32)
              for rows, w, _, c in chains]
        for n, (_, w, d, _) in enumerate(chains):
            if d is not None:
                row_i = lax.broadcasted_iota(I32, (tq, w), 0)
                col_i = lax.broadcasted_iota(I32, (tq, w), 1)
                ss[n] = jnp.where(row_i + d >= col_i, ss[n], -jnp.inf)
        m_prev = [m_refs[c][rows, :] for rows, _, _, c in chains]
        mn = [jnp.maximum(mp, jnp.max(s, axis=-1, keepdims=True)) for mp, s in zip(m_prev, ss)]
        ps = [jnp.exp2(s - jnp.tile(m, (1, s.shape[1] // LANES))).astype(BF16) for s, m in zip(ss, mn)]
        for n, (rows, w, _, c) in enumerate(chains):
            alpha = jnp.exp2(m_prev[n] - mn[n])
            acc_refs[c][rows, :] = jnp.tile(alpha, (1, 2)) * acc_refs[c][rows, :] + _dot(ps[n], vb[:w, :])
            m_refs[c][rows, :] = mn[n]

    def shared(j, carry):
        for p in range(n_tiles // 2):
            step(j * (2 * tq), [((2 * p) * tq, 2 * tq, None), ((2 * p + 1) * tq, 2 * tq, None)])
        return carry
    lax.fori_loop(0, jp * (n_tiles // 2), shared, 0)
    base = jp * (n_tiles * tq)
    for p in range(n_tiles // 2):
        a, b = 2 * p * tq, (2 * p + 1) * tq
        step(base + a, [(a, tq, 0), (b, 2 * tq, tq)])
        later = [(r * tq, 2 * tq, None) for r in range(2 * p + 2, n_tiles)]
        if later:
            step(base + a, later)

    lam = (jnp.exp(jnp.sum(lq1_ref[...] * lk1_ref[...], axis=-1, keepdims=True))
           - jnp.exp(jnp.sum(lq2_ref[...] * lk2_ref[...], axis=-1, keepdims=True)) + lambda_init)
    a1 = acc0_ref[...]
    a2 = acc1_ref[...]
    o = a1[:, :LANES] / a1[:, LANES:] - lam * (a2[:, :LANES] / a2[:, LANES:])
    o = o * lax.rsqrt(jnp.mean(o * o, axis=-1, keepdims=True) + NORM_EPS)
    o_ref[...] = (o * sw_ref[...] * (1.0 - lambda_init)).astype(o_ref.dtype)


def _diff_attention(qkv, lq1, lk1, lq2, lk2, subln_w, lambda_init, batch, seq, moe_weights, moe_layer):
    t = batch * seq
    tq = ATT_TILES_PER_STEP * min(ATT_TQ, seq // ATT_TILES_PER_STEP)
    nq = seq // tq
    h_n = ATTN_N_HEADS
    vec = lambda b, h, i: (0, 0)
    cast = _CastPlan(*moe_weights, moe_layer, batch * h_n, lambda b, h, i: b * h_n + h)
    o, *w_bf16 = pl.pallas_call(
        functools.partial(_attn_kernel, lambda_init=lambda_init, cast=cast),
        grid=(batch, h_n, nq),
        in_specs=[pl.BlockSpec((tq, LANES), lambda b, h, i: (b * nq + i, h)),
                  pl.BlockSpec((seq, LANES), lambda b, h, i: (b, h_n + h)),
                  pl.BlockSpec((seq, LANES), lambda b, h, i: (b, 2 * h_n + h)),
                  pl.BlockSpec((1, ATTN_HEAD_DIM), vec), pl.BlockSpec((1, ATTN_HEAD_DIM), vec),
                  pl.BlockSpec((1, ATTN_HEAD_DIM), vec), pl.BlockSpec((1, ATTN_HEAD_DIM), vec),
                  pl.BlockSpec((1, ATTN_V_DIM), vec)] + cast.in_specs,
        out_specs=[pl.BlockSpec((tq, LANES), lambda b, h, i: (b * nq + i, h))] + cast.out_specs,
        out_shape=[jax.ShapeDtypeStruct((t, h_n * ATTN_V_DIM), BF16)] + cast.out_shape,
        scratch_shapes=[pltpu.VMEM((seq, 2 * LANES), BF16),
                        pltpu.VMEM((tq, LANES), F32), pltpu.VMEM((tq, LANES), F32),
                        pltpu.VMEM((tq, 2 * LANES), F32), pltpu.VMEM((tq, 2 * LANES), F32)],
        compiler_params=_arb(3),
        name="diff_attn",
    )(qkv, qkv, qkv, lq1[None, :], lk1[None, :], lq2[None, :], lk2[None, :], subln_w[None, :], *cast.weights)
    return o, tuple(w_bf16)


def kernel(x, positions, ln_mix_g, ln_mix_b, ln_ffn_g, ln_ffn_b, ssm_w_in, ssm_conv_w, ssm_conv_b, ssm_dt_bias, ssm_a_log, ssm_d, ssm_norm_w, ssm_w_out, attn_w_qkv, attn_lam_q1, attn_lam_k1, attn_lam_q2, attn_lam_k2, attn_subln_w, attn_w_o, moe_w_group, moe_w_expert, moe_w_gate, moe_w_up, moe_w_down):
    batch, seq, d = x.shape
    t = batch * seq
    h = x.reshape(t, d)

    moe_weights = (moe_w_gate, moe_w_up, moe_w_down)
    w_in = ssm_w_in[0]
    zx = _matmul(h, w_in, SSM_ZX_DIM, F32)
    yg, moe_w0 = _ssd(zx, h, w_in[:, SSM_ZX_DIM:], ssm_conv_w[0], ssm_conv_b[0], ssm_dt_bias[0],
                      ssm_a_log[0], ssm_d[0], ssm_norm_w[0], batch, seq, moe_weights, 0)
    ht, eid, wt, cnt = _mm_ln_router(yg, ssm_w_out[0], h, ln_mix_g[0], ln_mix_b[0],
                                     _router_weights(moe_w_group[0], moe_w_expert[0]), "mm_ssm_out_ln_router")
    h = _hier_moe_ln(ht, eid, wt, cnt, moe_w0, ln_ffn_g[0], ln_ffn_b[0], 0)

    lambda_init = 0.8 - 0.6 * math.exp(-0.3 * 1)
    qkv = _qkv_rope(h, attn_w_qkv[0], positions)
    o, moe_w1 = _diff_attention(qkv, attn_lam_q1[0], attn_lam_k1[0], attn_lam_q2[0], attn_lam_k2[0],
                                attn_subln_w[0], lambda_init, batch, seq, moe_weights, 1)
    ht, eid, wt, cnt = _mm_ln_router(o, attn_w_o[0], h, ln_mix_g[1], ln_mix_b[1],
                                     _router_weights(moe_w_group[1], moe_w_expert[1]), "mm_attn_out_ln_router")
    h = _hier_moe_ln(ht, eid, wt, cnt, moe_w1, ln_ffn_g[1], ln_ffn_b[1], 1)
    return h.reshape(batch, seq, d)
```

```python
import functools
import math

import jax
import jax.numpy as jnp
from jax import lax
from jax.experimental import pallas as pl
from jax.experimental.pallas import tpu as pltpu

F32 = jnp.float32
BF16 = jnp.bfloat16
I32 = jnp.int32

D_MODEL = 1024
DEPTH = 2
SSM_D_INNER = 2048
SSM_HEADDIM = 64
SSM_N_HEADS = 32
SSM_N_GROUPS = 8
SSM_HEADS_PER_GROUP = 4
SSM_D_STATE = 128
SSM_CONV_K = 4
SSM_CHUNK = 128
SSM_GROUP_WIDTH = SSM_HEADS_PER_GROUP * SSM_HEADDIM
SSM_ZX_DIM = 2 * SSM_D_INNER + 2 * SSM_N_GROUPS * SSM_D_STATE
ATTN_HEAD_DIM = 64
ATTN_N_HEADS = 8
ATTN_V_DIM = 128
ROT_DIM = 16
ROPE_THETA = 500000.0
MOE_GROUPS = 4
MOE_EXPERTS_PER_GROUP = 8
MOE_N_EXPERTS = 32
MOE_TOP_K = 2
MOE_D_FF = 512
DEEPNORM_ALPHA = (2 * DEPTH) ** 0.25
NORM_EPS = 1e-5

LANES = 128
ROW_SLABS = D_MODEL // LANES
CONV_TAIL = 8
CONV_ROWS = 128

MM_TM = 512
MM_TN = 1024
FFN_BLK = 512
DSP_TM = 512
ZERO_PIECE = 64
SSD_GROUP_BATCH = 2
LN_SUBTILES = 4
CMB_TM = 256
ATT_TQ = 512
ATT_TILES_PER_STEP = 2


def _arb(n):
    return pltpu.CompilerParams(dimension_semantics=("arbitrary",) * n,
                                vmem_limit_bytes=56 * 1024 * 1024)


def _silu(x):
    hx = 0.5 * x
    return hx + hx * jnp.tanh(hx)


def _softplus(x):
    return jnp.maximum(x, 0.0) + jnp.log(1.0 + jnp.exp(-jnp.abs(x)))


def _layer_norm(y, g, b):
    mu = jnp.mean(y, axis=-1, keepdims=True)
    d = y - mu
    var = jnp.mean(d * d, axis=-1, keepdims=True)
    return d * lax.rsqrt(var + NORM_EPS) * g + b


def _split3(a):
    a1 = a.astype(BF16)
    r1 = a - a1.astype(F32)
    a2 = r1.astype(BF16)
    a3 = (r1 - a2.astype(F32)).astype(BF16)
    return a1, a2, a3


def _dot(a, b):
    return jnp.dot(a, b, preferred_element_type=F32)


def _dot_sel(a, sel):
    a1, a2, a3 = _split3(a)
    return _dot(a1, sel) + _dot(a2, sel) + _dot(a3, sel)


def _dot_f32(a, b):
    a1, a2, a3 = _split3(a)
    b1, b2, b3 = _split3(b)
    return (_dot(a1, b1) + _dot(a1, b2) + _dot(a2, b1)
            + _dot(a2, b2) + _dot(a1, b3) + _dot(a3, b1))


class _CastPlan:
    def __init__(self, w_gate, w_up, w_down, layer, steps, step_of):
        self.weights = (w_gate, w_up, w_down)
        self.n_e = w_gate.shape[1]
        self.eps = -(-self.n_e // steps)
        assert self.n_e % self.eps == 0
        self.n_cast = self.n_e // self.eps
        self.stride = steps // self.n_cast
        block = lambda *ids: jnp.minimum(step_of(*ids) // self.stride, self.n_cast - 1)
        self.in_specs = [pl.BlockSpec((1, self.eps) + w.shape[2:], lambda *ids: (layer, block(*ids), 0, 0))
                         for w in self.weights]
        self.out_specs = [pl.BlockSpec((self.eps,) + w.shape[2:], lambda *ids: (block(*ids), 0, 0))
                          for w in self.weights]
        self.out_shape = [jax.ShapeDtypeStruct(w.shape[1:], BF16) for w in self.weights]

    def emit(self, step, w_refs, o_refs, extra=True):
        @pl.when(extra & (step % self.stride == 0) & (step // self.stride < self.n_cast))
        def _():
            for w_ref, o_ref in zip(w_refs, o_refs):
                o_ref[...] = w_ref[0].astype(BF16)


def _mm_kernel(x_ref, w_ref, o_ref):
    xb = x_ref[...].astype(BF16)
    for j in range(o_ref.shape[1] // MM_TN):
        cols = slice(j * MM_TN, (j + 1) * MM_TN)
        o_ref[:, cols] = _dot(xb, w_ref[:, cols]).astype(o_ref.dtype)


def _matmul(x, w, n, out_dtype):
    m, k = x.shape
    tm = min(MM_TM, m)
    return pl.pallas_call(
        _mm_kernel,
        grid=(m // tm,),
        in_specs=[pl.BlockSpec((tm, k), lambda i: (i, 0)),
                  pl.BlockSpec((k, n), lambda i: (0, 0), pipeline_mode=pl.Buffered(1))],
        out_specs=pl.BlockSpec((tm, n), lambda i: (i, 0)),
        out_shape=jax.ShapeDtypeStruct((m, n), out_dtype),
        compiler_params=_arb(1),
        name="mm_inproj",
    )(x, w)


def _ssd_kernel(z_ref, xs_ref, bc_ref, x_ref, wdt_ref, cw_ref, cb_ref, dtb_ref, alog_ref, dskip_ref,
                nw_ref, e64_ref, e128_ref, wg_ref, wu_ref, wd_ref, o_ref, wgb_ref, wub_ref, wdb_ref,
                stage_x, stage_bc, state, acp_s, dtp_s, act_s, *, cast):
    G, W, N = SSM_N_GROUPS, SSM_GROUP_WIDTH, SSM_D_STATE
    L = z_ref.shape[0]
    c = pl.program_id(1)
    cast.emit(pl.program_id(0) * pl.num_programs(1) + c, (wg_ref, wu_ref, wd_ref), (wgb_ref, wub_ref, wdb_ref))

    @pl.when(c == 0)
    def _():
        state[...] = jnp.zeros(state.shape, F32)
        for stage in (stage_x, stage_bc):
            stage[0:CONV_TAIL, :] = jnp.zeros((CONV_TAIL, stage.shape[1]), F32)

    @pl.when(c != 0)
    def _():
        for stage in (stage_x, stage_bc):
            stage[0:CONV_TAIL, :] = stage[L:L + CONV_TAIL, :]

    stage_x[CONV_TAIL:CONV_TAIL + L, :] = xs_ref[...]
    stage_bc[CONV_TAIL:CONV_TAIL + L, :] = bc_ref[...]

    def conv(stage, c0, width, w0):
        taps = [cw_ref[k:k + 1, w0:w0 + width] for k in range(SSM_CONV_K)]
        bias = cb_ref[:, w0:w0 + width]
        outs = []
        for rb in range(L // CONV_ROWS):
            r0 = rb * CONV_ROWS
            u = stage[r0:r0 + CONV_ROWS + CONV_TAIL, c0:c0 + width]
            out = bias + taps[SSM_CONV_K - 1] * u[CONV_TAIL:, :]
            for k in range(SSM_CONV_K - 1):
                back = SSM_CONV_K - 1 - k
                out = out + taps[k] * pltpu.roll(u, back, 0)[CONV_TAIL:, :]
            outs.append(_silu(out))
        return jnp.concatenate(outs, axis=0)

    xc, bcv, ccv = {}, {}, {}

    def conv_groups(gs):
        for g in gs:
            xc[g] = conv(stage_x, g * W, W, g * W)
            bcv[g] = conv(stage_bc, g * N, N, SSM_D_INNER + g * N).astype(BF16)
            ccv[g] = conv(stage_bc, G * N + g * N, N, SSM_D_INNER + G * N + g * N).astype(BF16)

    x_of, b_of, c_of = xc.__getitem__, bcv.__getitem__, ccv.__getitem__

    dt = _softplus(_dot(x_ref[...].astype(BF16), wdt_ref[...].astype(BF16)) + dtb_ref[...])
    a = -jnp.exp(alog_ref[...])
    row_i = lax.broadcasted_iota(I32, (L, L), 0)
    col_i = lax.broadcasted_iota(I32, (L, L), 1)
    causal = row_i >= col_i
    a1, a2, a3 = _split3(dt * a)
    tri = causal.astype(BF16)
    ac = (_dot(tri, a1) + _dot(tri, a2) + _dot(tri, a3)) * math.log2(math.e)
    act_s[...] = ac.T
    for n, part in enumerate(_split3(ac)):
        acp_s[n] = part
    for n, part in enumerate(_split3(dt)[:2]):
        dtp_s[n] = part
    lane = lax.broadcasted_iota(I32, (L, LANES), 1)
    first_half = lane < SSM_HEADDIM

    def select(parts_ref, n_parts, sel):
        out = _dot(parts_ref[0], sel)
        for n in range(1, n_parts):
            out = out + _dot(parts_ref[n], sel)
        return out

    def run_groups(gs):
        conv_groups(gs)
        col4 = {g: select(acp_s, 3, e128_ref[g]) for g in gs}
        dt_e = {g: select(dtp_s, 2, e64_ref[g]) for g in gs}
        cb = {g: lax.dot_general(c_of(g), b_of(g), (((1,), (1,)), ((), ())), preferred_element_type=F32)
              for g in gs}
        y_cs = {g: _dot(c_of(g), state[g].astype(BF16)) for g in gs}
        a_e = {g: jnp.concatenate(
            [jnp.where(first_half, col4[g][:, 2 * p * LANES:(2 * p + 1) * LANES],
                       col4[g][:, (2 * p + 1) * LANES:(2 * p + 2) * LANES]) for p in range(2)], axis=1)
            for g in gs}
        xdt = {g: x_of(g) * dt_e[g] for g in gs}
        xdt_b = {g: xdt[g].astype(BF16) for g in gs}
        ys = {g: [None] * SSM_HEADS_PER_GROUP for g in gs}
        for r in range(SSM_HEADS_PER_GROUP):
            for g in gs:
                row = act_s[SSM_HEADS_PER_GROUP * g + r:SSM_HEADS_PER_GROUP * g + r + 1, :]
                seg = col4[g][:, LANES * r:LANES * (r + 1)] - row
                dec = jnp.where(causal, jnp.exp2(seg), 0.0)
                xp = xdt_b[g][:, LANES * (r // 2):LANES * (r // 2 + 1)]
                ys[g][r] = _dot((cb[g] * dec).astype(BF16), xp)
        for g in gs:
            y_diag = jnp.concatenate([jnp.where(first_half, ys[g][2 * p], ys[g][2 * p + 1])
                                      for p in range(2)], axis=1)
            al_e = a_e[g][L - 1:L, :]
            new = lax.dot_general(b_of(g), (xdt[g] * jnp.exp2(al_e - a_e[g])).astype(BF16),
                                  (((0,), (0,)), ((), ())), preferred_element_type=F32)
            y = y_diag + y_cs[g] * jnp.exp2(a_e[g]) + x_of(g) * dskip_ref[g]
            state[g] = state[g] * jnp.exp2(al_e) + new
            yg = y * _silu(z_ref[:, g * W:(g + 1) * W])
            ms = jnp.mean(yg * yg, axis=-1, keepdims=True)
            o_ref[:, g * W:(g + 1) * W] = (yg * lax.rsqrt(ms + NORM_EPS)
                                           * nw_ref[:, g * W:(g + 1) * W]).astype(o_ref.dtype)

    for g0 in range(0, G, SSD_GROUP_BATCH):
        run_groups(range(g0, g0 + SSD_GROUP_BATCH))


def _head_selectors():
    G, R = SSM_N_GROUPS, SSM_HEADS_PER_GROUP
    h = jnp.arange(LANES)[None, :, None]
    g = jnp.arange(G)[:, None, None]
    j64 = jnp.arange(SSM_GROUP_WIDTH)[None, None, :]
    j128 = jnp.arange(R * LANES)[None, None, :]
    e64 = (h == g * R + j64 // SSM_HEADDIM).astype(BF16)
    e128 = (h == g * R + j128 // LANES).astype(BF16)
    return e64, e128


def _ssd(zx, x2, w_dt, conv_w, conv_b, dt_bias, a_log, d_skip, norm_w, batch, seq, moe_weights, moe_layer):
    L, G, W, N = SSM_CHUNK, SSM_N_GROUPS, SSM_GROUP_WIDTH, SSM_D_STATE
    nc = seq // L
    cast = _CastPlan(*moe_weights, moe_layer, batch * nc, lambda b, c: b * nc + c)
    t = batch * seq
    pad = LANES - SSM_N_HEADS
    wdt = jnp.pad(w_dt, ((0, 0), (0, pad)))
    dtb = jnp.pad(dt_bias, (0, pad))[None, :]
    alog = jnp.pad(a_log, (0, pad))[None, :]
    dsk = jnp.repeat(d_skip, SSM_HEADDIM).reshape(G, 1, W)
    e64, e128 = _head_selectors()
    conv_dim = conv_w.shape[1]
    row = lambda b, c: (b * nc + c, 0)
    full2 = lambda b, c: (0, 0)
    full3 = lambda b, c: (0, 0, 0)
    yg, *w_bf16 = pl.pallas_call(
        functools.partial(_ssd_kernel, cast=cast),
        grid=(batch, nc),
        in_specs=[pl.BlockSpec((L, SSM_D_INNER), row),
                  pl.BlockSpec((L, SSM_D_INNER), lambda b, c: (b * nc + c, 1)),
                  pl.BlockSpec((L, 2 * G * N), lambda b, c: (b * nc + c, 2)),
                  pl.BlockSpec((L, D_MODEL), row),
                  pl.BlockSpec((D_MODEL, LANES), full2),
                  pl.BlockSpec((SSM_CONV_K, conv_dim), full2),
                  pl.BlockSpec((1, conv_dim), full2),
                  pl.BlockSpec((1, LANES), full2),
                  pl.BlockSpec((1, LANES), full2),
                  pl.BlockSpec((G, 1, W), full3),
                  pl.BlockSpec((1, SSM_D_INNER), full2),
                  pl.BlockSpec((G, LANES, W), full3),
                  pl.BlockSpec((G, LANES, SSM_HEADS_PER_GROUP * LANES), full3)] + cast.in_specs,
        out_specs=[pl.BlockSpec((L, SSM_D_INNER), row)] + cast.out_specs,
        out_shape=[jax.ShapeDtypeStruct((t, SSM_D_INNER), BF16)] + cast.out_shape,
        scratch_shapes=[pltpu.VMEM((CONV_TAIL + L, SSM_D_INNER), F32),
                        pltpu.VMEM((CONV_TAIL + L, 2 * G * N), F32),
                        pltpu.VMEM((G, N, W), F32),
                        pltpu.VMEM((3, L, LANES), BF16),
                        pltpu.VMEM((2, L, LANES), BF16),
                        pltpu.VMEM((LANES, L), F32)],
        compiler_params=_arb(2),
        name="ssd_scan",
    )(zx, zx, zx, x2, wdt, conv_w, conv_b[None, :], dtb, alog, dsk, norm_w[None, :], e64, e128, *cast.weights)
    return yg, tuple(w_bf16)


def _route(h, wr):
    tm = h.shape[0]
    logits = _dot(h.astype(BF16), wr)
    lane_i = lax.broadcasted_iota(I32, (tm, LANES), 1)
    lane = lane_i.astype(F32)
    neg = -jnp.inf
    big = float(LANES)

    def first_argmax(v, vmax):
        return jnp.min(jnp.where(v == vmax, lane, big), axis=-1, keepdims=True)

    gl = jnp.where((lane_i >= MOE_N_EXPERTS) & (lane_i < MOE_N_EXPERTS + MOE_GROUPS), logits, neg)
    gm = jnp.max(gl, axis=-1, keepdims=True)
    g_sel = first_argmax(gl, gm) - float(MOE_N_EXPERTS)
    g_gate = 1.0 / jnp.sum(jnp.exp(gl - gm), axis=-1, keepdims=True)
    lo = g_sel * float(MOE_EXPERTS_PER_GROUP)
    el = jnp.where((lane >= lo) & (lane < lo + float(MOE_EXPERTS_PER_GROUP)), logits, neg)
    m1 = jnp.max(el, axis=-1, keepdims=True)
    i1 = first_argmax(el, m1)
    el2 = jnp.where(lane == i1, neg, el)
    m2 = jnp.max(el2, axis=-1, keepdims=True)
    i2 = first_argmax(el2, m2)
    p2 = jnp.exp(m2 - m1)
    t1 = 1.0 / (1.0 + p2)
    t2 = p2 / (1.0 + p2)
    eid = jnp.where(lane_i == 0, i1, jnp.where(lane_i == 1, i2, 0.0)).astype(I32)
    wt = jnp.where(lane_i == 0, g_gate * t1, jnp.where(lane_i == 1, g_gate * t2, 0.0))
    count = jnp.sum(jnp.where(lane == i1, 1.0, 0.0) + jnp.where(lane == i2, 1.0, 0.0), axis=0, keepdims=True)
    return eid, wt, count


def _router_weights(w_group, w_expert):
    pad = LANES - MOE_N_EXPERTS - MOE_GROUPS
    return jnp.pad(jnp.concatenate([w_expert, w_group], axis=1), ((0, 0), (0, pad))).astype(BF16)


def _store_token_tiles(ref, v, first_token=0):
    rows = v.shape[0]
    for s in range(ROW_SLABS):
        ref[pl.ds(first_token * ROW_SLABS + s, rows, stride=ROW_SLABS), :] = v[:, s * LANES:(s + 1) * LANES]


def _load_token_tiles(ref, rows):
    return jnp.concatenate([ref[pl.ds(s, rows, stride=ROW_SLABS), :] for s in range(ROW_SLABS)], axis=1)


def _mm_ln_router_kernel(x_ref, w_ref, r_ref, g_ref, b_ref, wr_ref, ht_ref, eid_ref, wt_ref, cnt_ref, wb_ref):
    @pl.when(pl.program_id(0) == 0)
    def _():
        cnt_ref[...] = jnp.zeros(cnt_ref.shape, F32)
        wb_ref[...] = w_ref[...].astype(BF16)

    sub = x_ref.shape[0] // LN_SUBTILES
    rows = [slice(u * sub, (u + 1) * sub) for u in range(LN_SUBTILES)]
    ys = [DEEPNORM_ALPHA * r_ref[rs, :] + _dot(x_ref[rs, :].astype(BF16), wb_ref[...]) for rs in rows]
    hs = [_layer_norm(y, g_ref[...], b_ref[...]) for y in ys]
    routes = [_route(h, wr_ref[...]) for h in hs]
    for u, rs in enumerate(rows):
        _store_token_tiles(ht_ref, hs[u], first_token=u * sub)
        eid_ref[rs, :] = routes[u][0]
        wt_ref[rs, :] = routes[u][1]
    cnt_ref[...] += jnp.broadcast_to(sum(r[2] for r in routes), cnt_ref.shape)


def _mm_ln_router(x, w, resid, g, b, wr, name):
    m, k = x.shape
    d = w.shape[1]
    tm = min(MM_TM, m)
    row = lambda i: (i, 0)
    full = lambda i: (0, 0)
    return pl.pallas_call(
        _mm_ln_router_kernel,
        grid=(m // tm,),
        in_specs=[pl.BlockSpec((tm, k), row),
                  pl.BlockSpec((k, d), full, pipeline_mode=pl.Buffered(1)),
                  pl.BlockSpec((tm, d), row),
                  pl.BlockSpec((1, d), full), pl.BlockSpec((1, d), full), pl.BlockSpec((d, LANES), full)],
        out_specs=[pl.BlockSpec((tm * ROW_SLABS, LANES), row),
                   pl.BlockSpec((tm, LANES), row), pl.BlockSpec((tm, LANES), row),
                   pl.BlockSpec((8, LANES), full)],
        out_shape=[jax.ShapeDtypeStruct((m * ROW_SLABS, LANES), F32),
                   jax.ShapeDtypeStruct((m, LANES), I32), jax.ShapeDtypeStruct((m, LANES), F32),
                   jax.ShapeDtypeStruct((8, LANES), F32)],
        scratch_shapes=[pltpu.VMEM((k, d), BF16)],
        compiler_params=_arb(1),
        name=name,
    )(x, w, resid, g[None, :], b[None, :], wr)


def _dispatch_kernel(eid_ref, cnt_ref, h_ref, dest_ref, be_ref, nu_ref, xs_hbm,
                     base, upper, zbuf, dst_v, dst_s, pe_v, pe_s, sem_z, sem_r, sem_s, *, blk):
    i = pl.program_id(0)
    tm = eid_ref.shape[0]
    n_e = LANES
    eid_t = eid_ref[...].astype(F32).T
    sub = lax.broadcasted_iota(I32, (n_e, tm), 0).astype(F32)
    oh = [(sub == eid_t[k:k + 1, :]).astype(F32) for k in range(MOE_TOP_K)]
    tot = [jnp.sum(o, axis=1, keepdims=True) for o in oh]

    @pl.when(i == 0)
    def _():
        r_i = lax.broadcasted_iota(I32, (tm, tm), 0)
        c_i = lax.broadcasted_iota(I32, (tm, tm), 1)
        upper[...] = (r_i < c_i).astype(BF16)
        counts = jnp.broadcast_to(cnt_ref[0:1, :], (n_e, n_e)).T
        padded = jnp.floor((counts + float(blk - 1)) * (1.0 / blk)) * float(blk)
        r_i = lax.broadcasted_iota(I32, (n_e, n_e), 0)
        c_i = lax.broadcasted_iota(I32, (n_e, n_e), 1)
        tril = (r_i >= c_i).astype(BF16)
        p1, p2, p3 = _split3(padded)
        pends = _dot(tril, p1) + _dot(tril, p2) + _dot(tril, p3)
        base[...] = pends - padded
        nbp = be_ref.shape[1]
        blk_start = lax.broadcasted_iota(I32, (n_e, nbp), 1).astype(F32) * float(blk)
        is_e = lax.broadcasted_iota(I32, (n_e, nbp), 0) < MOE_N_EXPERTS
        done = jnp.where(is_e & (jnp.tile(pends, (1, nbp // LANES)) <= blk_start), 1.0, 0.0)
        be = jnp.minimum(jnp.sum(done, axis=0, keepdims=True), float(MOE_N_EXPERTS - 1))
        be_ref[...] = jnp.broadcast_to(be, be_ref.shape).astype(I32)
        last = pends[MOE_N_EXPERTS - 1:MOE_N_EXPERTS, :] * (1.0 / blk)
        nu_ref[...] = jnp.broadcast_to(last, nu_ref.shape).astype(I32)
        zbuf[...] = jnp.zeros(zbuf.shape, F32)
        row8 = lax.broadcasted_iota(I32, (8, LANES), 0)
        pe_v[...] = jnp.where(row8 == 0, pends.T[0:8, :], counts.T[0:8, :]).astype(I32)
        cp = pltpu.make_async_copy(pe_v, pe_s, sem_s)
        cp.start()
        cp.wait()

        brows = blk * ROW_SLABS
        prows = ZERO_PIECE * ROW_SLABS
        shift = blk.bit_length() - 1

        def piece_copy(e, p):
            start = pl.multiple_of((pe_s[0, e] - (p + 1) * ZERO_PIECE) * ROW_SLABS, prows)
            return pltpu.make_async_copy(zbuf.at[pl.ds(0, prows)], xs_hbm.at[pl.ds(start, prows)], sem_z)

        def n_pieces(e):
            cnt = pe_s[1, e]
            pad = lax.shift_left(lax.shift_right_logical(cnt + (blk - 1), shift), shift) - cnt
            return lax.shift_right_logical(pad + (ZERO_PIECE - 1), ZERO_PIECE.bit_length() - 1)

        def tail_copy(b):
            return pltpu.make_async_copy(zbuf, xs_hbm.at[pl.ds(pl.multiple_of(b * brows, brows), brows)], sem_z)

        n_used = lax.shift_right_logical(pe_s[0, MOE_N_EXPERTS - 1], shift)
        n_blocks = xs_hbm.shape[0] // brows
        for e in range(MOE_N_EXPERTS):
            lax.fori_loop(0, n_pieces(e), lambda p, c, e=e: (piece_copy(e, p).start(), c)[1], 0)
        lax.fori_loop(n_used, n_blocks, lambda b, c: (tail_copy(b).start(), c)[1], 0)
        for e in range(MOE_N_EXPERTS):
            lax.fori_loop(0, n_pieces(e), lambda p, c, e=e: (piece_copy(e, p).wait(), c)[1], 0)
        lax.fori_loop(n_used, n_blocks, lambda b, c: (tail_copy(b).wait(), c)[1], 0)

    b0 = base[:, 0:1]
    c0 = _dot(oh[0].astype(BF16), upper[...])
    c1 = _dot(oh[1].astype(BF16), upper[...])
    d0 = jnp.sum(oh[0] * (b0 + c0), axis=0, keepdims=True)
    d1 = jnp.sum(oh[1] * (b0 + tot[0] + c1), axis=0, keepdims=True)
    base[...] += jnp.broadcast_to(tot[0] + tot[1], base.shape)
    row8 = lax.broadcasted_iota(I32, (8, tm), 0)
    dst = jnp.where(row8 == 0, d0, jnp.where(row8 == 1, d1, 0.0)).astype(I32)
    dest_ref[...] = dst
    dst_v[...] = dst
    cp = pltpu.make_async_copy(dst_v, dst_s, sem_s)
    cp.start()
    cp.wait()

    for r in range(tm):
        for k in range(MOE_TOP_K):
            slot = pl.multiple_of(dst_s[k, r] * ROW_SLABS, ROW_SLABS)
            pltpu.make_async_copy(h_ref.at[pl.ds(r * ROW_SLABS, ROW_SLABS)],
                                  xs_hbm.at[pl.ds(slot, ROW_SLABS)], sem_r).start(priority=k)
    for k in range(MOE_TOP_K):
        pltpu.make_async_copy(h_ref, xs_hbm.at[pl.ds(0, tm * ROW_SLABS)], sem_r).wait()


def _moe_dispatch(ht, eid, counts, blk, name):
    t = ht.shape[0] // ROW_SLABS
    assert blk & (blk - 1) == 0, "block size must be a power of two"
    tm = min(DSP_TM, t)
    nt = t // tm
    n_blocks = -(-(t * MOE_TOP_K) // blk) + MOE_N_EXPERTS
    nbp = -(-n_blocks // LANES) * LANES
    full = lambda i: (0, 0)
    dest, be, nu, xs = pl.pallas_call(
        functools.partial(_dispatch_kernel, blk=blk),
        grid=(nt,),
        in_specs=[pl.BlockSpec((tm, LANES), lambda i: (i, 0)),
                  pl.BlockSpec((8, LANES), full),
                  pl.BlockSpec((tm * ROW_SLABS, LANES), lambda i: (i, 0))],
        out_specs=[pl.BlockSpec((8, tm), lambda i: (0, i)),
                   pl.BlockSpec((8, nbp), full),
                   pl.BlockSpec((8, LANES), full),
                   pl.BlockSpec(memory_space=pl.ANY)],
        out_shape=[jax.ShapeDtypeStruct((8, t), I32), jax.ShapeDtypeStruct((8, nbp), I32),
                   jax.ShapeDtypeStruct((8, LANES), I32),
                   jax.ShapeDtypeStruct((n_blocks * blk * ROW_SLABS, LANES), F32)],
        scratch_shapes=[pltpu.VMEM((LANES, LANES), F32),
                        pltpu.VMEM((tm, tm), BF16), pltpu.VMEM((blk * ROW_SLABS, LANES), F32),
                        pltpu.VMEM((8, tm), I32), pltpu.SMEM((8, tm), I32),
                        pltpu.VMEM((8, LANES), I32), pltpu.SMEM((8, LANES), I32),
                        pltpu.SemaphoreType.DMA(()), pltpu.SemaphoreType.DMA(()), pltpu.SemaphoreType.DMA(())],
        compiler_params=_arb(1),
        name=name,
    )(eid, counts, ht)
    return dest, be[0, :n_blocks], nu[0, :1], xs, n_blocks


def _ffn_kernel(be_ref, nu_ref, x_ref, wg_ref, wu_ref, wd_ref, o_ref):
    i = pl.program_id(0)

    @pl.when(i < nu_ref[0])
    def _():
        blk = x_ref.shape[0] // ROW_SLABS
        xb = _load_token_tiles(x_ref, blk).astype(BF16)
        hid = _silu(_dot(xb, wg_ref[0])) * _dot(xb, wu_ref[0])
        _store_token_tiles(o_ref, _dot(hid.astype(BF16), wd_ref[0]))

    @pl.when(i >= nu_ref[0])
    def _():
        o_ref[...] = jnp.zeros(o_ref.shape, o_ref.dtype)


def _moe_ffn(xs, block_expert, n_used, n_blocks, weights, blk, name):
    w_gate, w_up, w_down = weights
    d, f = w_gate.shape[1], w_gate.shape[2]
    brows = blk * ROW_SLABS
    used = lambda i, be, nu: (jnp.minimum(i, nu[0] - 1), 0)
    every = lambda i, be, nu: (i, 0)
    expert = lambda i, be, nu: (be[i], 0, 0)
    grid_spec = pltpu.PrefetchScalarGridSpec(
        num_scalar_prefetch=2,
        grid=(n_blocks,),
        in_specs=[pl.BlockSpec((brows, LANES), used),
                  pl.BlockSpec((1, d, f), expert), pl.BlockSpec((1, d, f), expert),
                  pl.BlockSpec((1, f, d), expert)],
        out_specs=pl.BlockSpec((brows, LANES), every))
    return pl.pallas_call(
        _ffn_kernel,
        grid_spec=grid_spec,
        out_shape=jax.ShapeDtypeStruct((n_blocks * brows, LANES), F32),
        compiler_params=_arb(1),
        name=name,
    )(block_expert, n_used, xs, w_gate, w_up, w_down)


def _combine_ln_kernel(dst_ref, dstn_ref, ht_ref, wt_ref, g_ref, b_ref, yb_hbm, o_ref, ybuf, sem):
    i = pl.program_id(0)
    n = pl.num_programs(0)
    tm = o_ref.shape[0]

    def start_gather(dref, slot):
        for r in range(tm):
            for k in range(MOE_TOP_K):
                src = pl.multiple_of(dref[k, r] * ROW_SLABS, ROW_SLABS)
                pltpu.make_async_copy(yb_hbm.at[pl.ds(src, ROW_SLABS)],
                                      ybuf.at[slot, k, pl.ds(r * ROW_SLABS, ROW_SLABS)],
                                      sem.at[slot]).start(priority=k)

    @pl.when(i == 0)
    def _():
        start_gather(dst_ref, 0)

    @pl.when(i + 1 < n)
    def _():
        start_gather(dstn_ref, (i + 1) % 2)

    slot = i % 2
    for k in range(MOE_TOP_K):
        pltpu.make_async_copy(yb_hbm.at[pl.ds(0, tm * ROW_SLABS)], ybuf.at[slot, k], sem.at[slot]).wait()
    wt = wt_ref[...]
    ffn = (wt[:, 0:1] * _load_token_tiles(ybuf.at[slot, 0], tm)
           + wt[:, 1:2] * _load_token_tiles(ybuf.at[slot, 1], tm))
    h = _load_token_tiles(ht_ref, tm)
    o_ref[...] = _layer_norm(DEEPNORM_ALPHA * h + ffn, g_ref[...], b_ref[...])


def _combine_ln(ht, wt, dest, yb, g, b, name):
    t, d = ht.shape[0] // ROW_SLABS, D_MODEL
    tm = min(CMB_TM, t)
    nblk = t // tm
    row = lambda i: (i, 0)
    full = lambda i: (0, 0)
    return pl.pallas_call(
        _combine_ln_kernel,
        grid=(nblk,),
        in_specs=[pl.BlockSpec((8, tm), lambda i: (0, i), memory_space=pltpu.SMEM),
                  pl.BlockSpec((8, tm), lambda i: (0, jnp.minimum(i + 1, nblk - 1)),
                               memory_space=pltpu.SMEM),
                  pl.BlockSpec((tm * ROW_SLABS, LANES), row), pl.BlockSpec((tm, LANES), row),
                  pl.BlockSpec((1, d), full), pl.BlockSpec((1, d), full),
                  pl.BlockSpec(memory_space=pl.ANY)],
        out_specs=pl.BlockSpec((tm, d), row),
        out_shape=jax.ShapeDtypeStruct((t, d), F32),
        scratch_shapes=[pltpu.VMEM((2, MOE_TOP_K, tm * ROW_SLABS, LANES), F32), pltpu.SemaphoreType.DMA((2,))],
        compiler_params=_arb(1),
        name=name,
    )(dest, dest, ht, wt, g[None, :], b[None, :], yb)


def _hier_moe_ln(ht, eid, wt, counts, w_bf16, g, b, layer):
    dest, block_expert, n_used, xs, n_blocks = _moe_dispatch(ht, eid, counts, FFN_BLK, f"moe_dispatch{layer}")
    yb = _moe_ffn(xs, block_expert, n_used, n_blocks, w_bf16, FFN_BLK, f"moe_ffn{layer}")
    return _combine_ln(ht, wt, dest, yb, g, b, f"moe_combine_ln{layer}")


def _qkv_rope_kernel(x_ref, wf_ref, pos_ref, inv_ref, o_ref, w_ref):
    @pl.when(pl.program_id(0) == 0)
    def _():
        w_ref[...] = wf_ref[...].astype(BF16)

    xb = x_ref[...].astype(BF16)
    tm = xb.shape[0]
    n = D_MODEL
    ang = pos_ref[...].astype(F32) * inv_ref[...]
    lane = lax.broadcasted_iota(I32, (tm, LANES), 1)
    dd = lane & (ATTN_HEAD_DIM - 1)
    half = ROT_DIM // 2
    cosv = jnp.cos(ang)
    sinv = jnp.sin(ang)
    c_t = jnp.where(dd < ROT_DIM, cosv, 1.0)
    s_up = jnp.where(dd < half, -sinv, 0.0)
    s_dn = jnp.where((dd >= half) & (dd < ROT_DIM), sinv, 0.0)
    for j, sc in ((0, ATTN_HEAD_DIM ** -0.5 * math.log2(math.e)), (1, 1.0)):
        acc = _dot(xb, w_ref[:, j * n:(j + 1) * n])
        c_j, up_j, dn_j = c_t * sc, s_up * sc, s_dn * sc
        for blk in range(n // LANES):
            tt = acc[:, blk * LANES:(blk + 1) * LANES]
            out = tt * c_j + pltpu.roll(tt, LANES - half, 1) * up_j + pltpu.roll(tt, half, 1) * dn_j
            o_ref[:, j * n + blk * LANES:j * n + (blk + 1) * LANES] = out.astype(o_ref.dtype)
    o_ref[:, 2 * n:3 * n] = _dot(xb, w_ref[:, 2 * n:3 * n]).astype(o_ref.dtype)


def _rope_inv_table():
    inv = ROPE_THETA ** (-jnp.arange(0, ROT_DIM, 2, dtype=F32) / ROT_DIM)
    head = jnp.concatenate([inv, inv, jnp.zeros((ATTN_HEAD_DIM - ROT_DIM,), F32)])
    return jnp.tile(head, LANES // ATTN_HEAD_DIM)[None, :]


def _qkv_rope(h, w_qkv, positions):
    m, k = h.shape
    n = w_qkv.shape[1]
    tm = min(MM_TM, m)
    pos = positions.reshape(m, 1)
    return pl.pallas_call(
        _qkv_rope_kernel,
        grid=(m // tm,),
        in_specs=[pl.BlockSpec((tm, k), lambda i: (i, 0)),
                  pl.BlockSpec((k, n), lambda i: (0, 0), pipeline_mode=pl.Buffered(1)),
                  pl.BlockSpec((tm, 1), lambda i: (i, 0)),
                  pl.BlockSpec((1, LANES), lambda i: (0, 0))],
        out_specs=pl.BlockSpec((tm, n), lambda i: (i, 0)),
        out_shape=jax.ShapeDtypeStruct((m, n), BF16),
        scratch_shapes=[pltpu.VMEM((k, n), BF16)],
        compiler_params=_arb(1),
        name="mm_qkv_rope",
    )(h, w_qkv, pos, _rope_inv_table())


def _attn_kernel(q_ref, k_ref, v_ref, lq1_ref, lk1_ref, lq2_ref, lk2_ref, sw_ref, wg_ref, wu_ref, wd_ref,
                 o_ref, wgb_ref, wub_ref, wdb_ref,
                 vx_ref, m0_ref, m1_ref, acc0_ref, acc1_ref, *, lambda_init, cast):
    m_refs = (m0_ref, m1_ref)
    acc_refs = (acc0_ref, acc1_ref)
    jp = pl.program_id(2)
    n_tiles = ATT_TILES_PER_STEP
    tq = q_ref.shape[0] // n_tiles
    cast.emit(pl.program_id(0) * pl.num_programs(1) + pl.program_id(1),
              (wg_ref, wu_ref, wd_ref), (wgb_ref, wub_ref, wdb_ref), extra=jp == 0)

    @pl.when(jp == 0)
    def _():
        vx_ref[:, 0:LANES] = v_ref[...]
        vx_ref[:, LANES:] = jnp.ones((vx_ref.shape[0], LANES), vx_ref.dtype)

    q = q_ref[...]
    lane = lax.broadcasted_iota(I32, q.shape, 1)
    zero = jnp.zeros((), q.dtype)
    qs = (jnp.where(lane < ATTN_HEAD_DIM, q, zero), jnp.where(lane >= ATTN_HEAD_DIM, q, zero))
    for c in range(2):
        m_refs[c][...] = jnp.full(m_refs[c].shape, -jnp.inf, F32)
        acc_refs[c][...] = jnp.zeros(acc_refs[c].shape, F32)

    def step(off, windows):
        off = pl.multiple_of(off, tq)
        widest = max(w for _, w, _ in windows)
        kb = k_ref[pl.ds(off, widest), :]
        vb = vx_ref[pl.ds(off, widest), :]
        chains = [(slice(r0, r0 + tq), w, d, c) for r0, w, d in windows for c in range(2)]
        ss = [lax.dot_general(qs[c][rows, :], kb[:w, :], (((1,), (1,)), ((), ())), preferred_element_type=F32)
              for rows, w, _, c in chains]
        for n, (_, w, d, _) in enumerate(chains):
            if d is not None:
                row_i = lax.broadcasted_iota(I32, (tq, w), 0)
                col_i = lax.broadcasted_iota(I32, (tq, w), 1)
                ss[n] = jnp.where(row_i + d >= col_i, ss[n], -jnp.inf)
        m_prev = [m_refs[c][rows, :] for rows, _, _, c in chains]
        mn = [jnp.maximum(mp, jnp.max(s, axis=-1, keepdims=True)) for mp, s in zip(m_prev, ss)]
        ps = [jnp.exp2(s - jnp.tile(m, (1, s.shape[1] // LANES))).astype(BF16) for s, m in zip(ss, mn)]
        for n, (rows, w, _, c) in enumerate(chains):
            alpha = jnp.exp2(m_prev[n] - mn[n])
            acc_refs[c][rows, :] = jnp.tile(alpha, (1, 2)) * acc_refs[c][rows, :] + _dot(ps[n], vb[:w, :])
            m_refs[c][rows, :] = mn[n]

    def shared(j, carry):
        for p in range(n_tiles // 2):
            step(j * (2 * tq), [((2 * p) * tq, 2 * tq, None), ((2 * p + 1) * tq, 2 * tq, None)])
        return carry
    lax.fori_loop(0, jp * (n_tiles // 2), shared, 0)
    base = jp * (n_tiles * tq)
    for p in range(n_tiles // 2):
        a, b = 2 * p * tq, (2 * p + 1) * tq
        step(base + a, [(a, tq, 0), (b, 2 * tq, tq)])
        later = [(r * tq, 2 * tq, None) for r in range(2 * p + 2, n_tiles)]
        if later:
            step(base + a, later)

    lam = (jnp.exp(jnp.sum(lq1_ref[...] * lk1_ref[...], axis=-1, keepdims=True))
           - jnp.exp(jnp.sum(lq2_ref[...] * lk2_ref[...], axis=-1, keepdims=True)) + lambda_init)
    a1 = acc0_ref[...]
    a2 = acc1_ref[...]
    o = a1[:, :LANES] / a1[:, LANES:] - lam * (a2[:, :LANES] / a2[:, LANES:])
    o = o * lax.rsqrt(jnp.mean(o * o, axis=-1, keepdims=True) + NORM_EPS)
    o_ref[...] = (o * sw_ref[...] * (1.0 - lambda_init)).astype(o_ref.dtype)


def _diff_attention(qkv, lq1, lk1, lq2, lk2, subln_w, lambda_init, batch, seq, moe_weights, moe_layer):
    t = batch * seq
    tq = ATT_TILES_PER_STEP * min(ATT_TQ, seq // ATT_TILES_PER_STEP)
    nq = seq // tq
    h_n = ATTN_N_HEADS
    vec = lambda b, h, i: (0, 0)
    cast = _CastPlan(*moe_weights, moe_layer, batch * h_n, lambda b, h, i: b * h_n + h)
    o, *w_bf16 = pl.pallas_call(
        functools.partial(_attn_kernel, lambda_init=lambda_init, cast=cast),
        grid=(batch, h_n, nq),
        in_specs=[pl.BlockSpec((tq, LANES), lambda b, h, i: (b * nq + i, h)),
                  pl.BlockSpec((seq, LANES), lambda b, h, i: (b, h_n + h)),
                  pl.BlockSpec((seq, LANES), lambda b, h, i: (b, 2 * h_n + h)),
                  pl.BlockSpec((1, ATTN_HEAD_DIM), vec), pl.BlockSpec((1, ATTN_HEAD_DIM), vec),
                  pl.BlockSpec((1, ATTN_HEAD_DIM), vec), pl.BlockSpec((1, ATTN_HEAD_DIM), vec),
                  pl.BlockSpec((1, ATTN_V_DIM), vec)] + cast.in_specs,
        out_specs=[pl.BlockSpec((tq, LANES), lambda b, h, i: (b * nq + i, h))] + cast.out_specs,
        out_shape=[jax.ShapeDtypeStruct((t, h_n * ATTN_V_DIM), BF16)] + cast.out_shape,
        scratch_shapes=[pltpu.VMEM((seq, 2 * LANES), BF16),
                        pltpu.VMEM((tq, LANES), F32), pltpu.VMEM((tq, LANES), F32),
                        pltpu.VMEM((tq, 2 * LANES), F32), pltpu.VMEM((tq, 2 * LANES), F32)],
        compiler_params=_arb(3),
        name="diff_attn",
    )(qkv, qkv, qkv, lq1[None, :], lk1[None, :], lq2[None, :], lk2[None, :], subln_w[None, :], *cast.weights)
    return o, tuple(w_bf16)


def kernel(x, positions, ln_mix_g, ln_mix_b, ln_ffn_g, ln_ffn_b, ssm_w_in, ssm_conv_w, ssm_conv_b, ssm_dt_bias, ssm_a_log, ssm_d, ssm_norm_w, ssm_w_out, attn_w_qkv, attn_lam_q1, attn_lam_k1, attn_lam_q2, attn_lam_k2, attn_subln_w, attn_w_o, moe_w_group, moe_w_expert, moe_w_gate, moe_w_up, moe_w_down):
    batch, seq, d = x.shape
    t = batch * seq
    h = x.reshape(t, d)

    w_in = ssm_w_in[0].astype(BF16)

    moe_weights = (moe_w_gate, moe_w_up, moe_w_down)
    zx = _matmul(h, w_in, SSM_ZX_DIM, F32)
    yg, moe_w0 = _ssd(zx, h, w_in[:, SSM_ZX_DIM:], ssm_conv_w[0], ssm_conv_b[0], ssm_dt_bias[0],
                      ssm_a_log[0], ssm_d[0], ssm_norm_w[0], batch, seq, moe_weights, 0)
    ht, eid, wt, cnt = _mm_ln_router(yg, ssm_w_out[0], h, ln_mix_g[0], ln_mix_b[0],
                                     _router_weights(moe_w_group[0], moe_w_expert[0]), "mm_ssm_out_ln_router")
    h = _hier_moe_ln(ht, eid, wt, cnt, moe_w0, ln_ffn_g[0], ln_ffn_b[0], 0)

    lambda_init = 0.8 - 0.6 * math.exp(-0.3 * 1)
    qkv = _qkv_rope(h, attn_w_qkv[0], positions)
    o, moe_w1 = _diff_attention(qkv, attn_lam_q1[0], attn_lam_k1[0], attn_lam_q2[0], attn_lam_k2[0],
                                attn_subln_w[0], lambda_init, batch, seq, moe_weights, 1)
    ht, eid, wt, cnt = _mm_ln_router(o, attn_w_o[0], h, ln_mix_g[1], ln_mix_b[1],
                                     _router_weights(moe_w_group[1], moe_w_expert[1]), "mm_attn_out_ln_router")
    h = _hier_moe_ln(ht, eid, wt, cnt, moe_w1, ln_ffn_g[1], ln_ffn_b[1], 1)
    return h.reshape(batch, seq, d)
```

```python
import functools
import math

import jax
import jax.numpy as jnp
from jax import lax
from jax.experimental import pallas as pl
from jax.experimental.pallas import tpu as pltpu

F32 = jnp.float32
BF16 = jnp.bfloat16
I32 = jnp.int32

D_MODEL = 1024
DEPTH = 2
SSM_D_INNER = 2048
SSM_HEADDIM = 64
SSM_N_HEADS = 32
SSM_N_GROUPS = 8
SSM_HEADS_PER_GROUP = 4
SSM_D_STATE = 128
SSM_CONV_K = 4
SSM_CHUNK = 128
SSM_GROUP_WIDTH = SSM_HEADS_PER_GROUP * SSM_HEADDIM
SSM_ZX_DIM = 2 * SSM_D_INNER + 2 * SSM_N_GROUPS * SSM_D_STATE
ATTN_HEAD_DIM = 64
ATTN_N_HEADS = 8
ATTN_V_DIM = 128
ROT_DIM = 16
ROPE_THETA = 500000.0
MOE_GROUPS = 4
MOE_EXPERTS_PER_GROUP = 8
MOE_N_EXPERTS = 32
MOE_TOP_K = 2
MOE_D_FF = 512
DEEPNORM_ALPHA = (2 * DEPTH) ** 0.25
NORM_EPS = 1e-5

LANES = 128
ROW_SLABS = D_MODEL // LANES
CONV_TAIL = 8
CONV_ROWS = 128

MM_TM = 512
MM_TN = 1024
FFN_BLK = 512
DSP_TM = 512
ZERO_PIECE = 64
SSD_GROUP_BATCH = 2
LN_SUBTILES = 4
CMB_TM = 256
ATT_TQ = 512
ATT_TILES_PER_STEP = 4


def _arb(n):
    return pltpu.CompilerParams(dimension_semantics=("arbitrary",) * n,
                                vmem_limit_bytes=56 * 1024 * 1024)


def _silu(x):
    hx = 0.5 * x
    return hx + hx * jnp.tanh(hx)


def _softplus(x):
    return jnp.maximum(x, 0.0) + jnp.log(1.0 + jnp.exp(-jnp.abs(x)))


def _layer_norm(y, g, b):
    mu = jnp.mean(y, axis=-1, keepdims=True)
    d = y - mu
    var = jnp.mean(d * d, axis=-1, keepdims=True)
    return d * lax.rsqrt(var + NORM_EPS) * g + b


def _split3(a):
    a1 = a.astype(BF16)
    r1 = a - a1.astype(F32)
    a2 = r1.astype(BF16)
    a3 = (r1 - a2.astype(F32)).astype(BF16)
    return a1, a2, a3


def _dot(a, b):
    return jnp.dot(a, b, preferred_element_type=F32)


def _dot_sel(a, sel):
    a1, a2, a3 = _split3(a)
    return _dot(a1, sel) + _dot(a2, sel) + _dot(a3, sel)


def _dot_f32(a, b):
    a1, a2, a3 = _split3(a)
    b1, b2, b3 = _split3(b)
    return (_dot(a1, b1) + _dot(a1, b2) + _dot(a2, b1)
            + _dot(a2, b2) + _dot(a1, b3) + _dot(a3, b1))


class _CastPlan:
    def __init__(self, w_gate, w_up, w_down, layer, steps, step_of):
        self.weights = (w_gate, w_up, w_down)
        self.n_e = w_gate.shape[1]
        self.eps = -(-self.n_e // steps)
        assert self.n_e % self.eps == 0
        self.n_cast = self.n_e // self.eps
        self.stride = steps // self.n_cast
        block = lambda *ids: jnp.minimum(step_of(*ids) // self.stride, self.n_cast - 1)
        self.in_specs = [pl.BlockSpec((1, self.eps) + w.shape[2:], lambda *ids: (layer, block(*ids), 0, 0))
                         for w in self.weights]
        self.out_specs = [pl.BlockSpec((self.eps,) + w.shape[2:], lambda *ids: (block(*ids), 0, 0))
                          for w in self.weights]
        self.out_shape = [jax.ShapeDtypeStruct(w.shape[1:], BF16) for w in self.weights]

    def emit(self, step, w_refs, o_refs, extra=True):
        @pl.when(extra & (step % self.stride == 0) & (step // self.stride < self.n_cast))
        def _():
            for w_ref, o_ref in zip(w_refs, o_refs):
                o_ref[...] = w_ref[0].astype(BF16)


def _mm_kernel(x_ref, w_ref, o_ref):
    xb = x_ref[...].astype(BF16)
    for j in range(o_ref.shape[1] // MM_TN):
        cols = slice(j * MM_TN, (j + 1) * MM_TN)
        o_ref[:, cols] = _dot(xb, w_ref[:, cols]).astype(o_ref.dtype)


def _matmul(x, w, n, out_dtype):
    m, k = x.shape
    tm = min(MM_TM, m)
    return pl.pallas_call(
        _mm_kernel,
        grid=(m // tm,),
        in_specs=[pl.BlockSpec((tm, k), lambda i: (i, 0)),
                  pl.BlockSpec((k, n), lambda i: (0, 0), pipeline_mode=pl.Buffered(1))],
        out_specs=pl.BlockSpec((tm, n), lambda i: (i, 0)),
        out_shape=jax.ShapeDtypeStruct((m, n), out_dtype),
        compiler_params=_arb(1),
        name="mm_inproj",
    )(x, w)


def _ssd_kernel(z_ref, xs_ref, bc_ref, x_ref, wdt_ref, cw_ref, cb_ref, dtb_ref, alog_ref, dskip_ref,
                nw_ref, e64_ref, e128_ref, wg_ref, wu_ref, wd_ref, o_ref, wgb_ref, wub_ref, wdb_ref,
                stage_x, stage_bc, state, acp_s, dtp_s, act_s, *, cast):
    G, W, N = SSM_N_GROUPS, SSM_GROUP_WIDTH, SSM_D_STATE
    L = z_ref.shape[0]
    c = pl.program_id(1)
    cast.emit(pl.program_id(0) * pl.num_programs(1) + c, (wg_ref, wu_ref, wd_ref), (wgb_ref, wub_ref, wdb_ref))

    @pl.when(c == 0)
    def _():
        state[...] = jnp.zeros(state.shape, F32)
        for stage in (stage_x, stage_bc):
            stage[0:CONV_TAIL, :] = jnp.zeros((CONV_TAIL, stage.shape[1]), F32)

    @pl.when(c != 0)
    def _():
        for stage in (stage_x, stage_bc):
            stage[0:CONV_TAIL, :] = stage[L:L + CONV_TAIL, :]

    stage_x[CONV_TAIL:CONV_TAIL + L, :] = xs_ref[...]
    stage_bc[CONV_TAIL:CONV_TAIL + L, :] = bc_ref[...]

    def conv(stage, c0, width, w0):
        taps = [cw_ref[k:k + 1, w0:w0 + width] for k in range(SSM_CONV_K)]
        bias = cb_ref[:, w0:w0 + width]
        outs = []
        for rb in range(L // CONV_ROWS):
            r0 = rb * CONV_ROWS
            u = stage[r0:r0 + CONV_ROWS + CONV_TAIL, c0:c0 + width]
            out = bias + taps[SSM_CONV_K - 1] * u[CONV_TAIL:, :]
            for k in range(SSM_CONV_K - 1):
                back = SSM_CONV_K - 1 - k
                out = out + taps[k] * pltpu.roll(u, back, 0)[CONV_TAIL:, :]
            outs.append(_silu(out))
        return jnp.concatenate(outs, axis=0)

    xc, bcv, ccv = {}, {}, {}

    def conv_groups(gs):
        for g in gs:
            xc[g] = conv(stage_x, g * W, W, g * W)
            bcv[g] = conv(stage_bc, g * N, N, SSM_D_INNER + g * N).astype(BF16)
            ccv[g] = conv(stage_bc, G * N + g * N, N, SSM_D_INNER + G * N + g * N).astype(BF16)

    x_of, b_of, c_of = xc.__getitem__, bcv.__getitem__, ccv.__getitem__

    dt = _softplus(_dot(x_ref[...].astype(BF16), wdt_ref[...].astype(BF16)) + dtb_ref[...])
    a = -jnp.exp(alog_ref[...])
    row_i = lax.broadcasted_iota(I32, (L, L), 0)
    col_i = lax.broadcasted_iota(I32, (L, L), 1)
    causal = row_i >= col_i
    a1, a2, a3 = _split3(dt * a)
    tri = causal.astype(BF16)
    ac = (_dot(tri, a1) + _dot(tri, a2) + _dot(tri, a3)) * math.log2(math.e)
    act_s[...] = ac.T
    for n, part in enumerate(_split3(ac)):
        acp_s[n] = part
    for n, part in enumerate(_split3(dt)[:2]):
        dtp_s[n] = part
    lane = lax.broadcasted_iota(I32, (L, LANES), 1)
    first_half = lane < SSM_HEADDIM

    def select(parts_ref, n_parts, sel):
        out = _dot(parts_ref[0], sel)
        for n in range(1, n_parts):
            out = out + _dot(parts_ref[n], sel)
        return out

    def run_groups(gs):
        conv_groups(gs)
        col4 = {g: select(acp_s, 3, e128_ref[g]) for g in gs}
        dt_e = {g: select(dtp_s, 2, e64_ref[g]) for g in gs}
        cb = {g: lax.dot_general(c_of(g), b_of(g), (((1,), (1,)), ((), ())), preferred_element_type=F32)
              for g in gs}
        y_cs = {g: _dot(c_of(g), state[g].astype(BF16)) for g in gs}
        a_e = {g: jnp.concatenate(
            [jnp.where(first_half, col4[g][:, 2 * p * LANES:(2 * p + 1) * LANES],
                       col4[g][:, (2 * p + 1) * LANES:(2 * p + 2) * LANES]) for p in range(2)], axis=1)
            for g in gs}
        xdt = {g: x_of(g) * dt_e[g] for g in gs}
        xdt_b = {g: xdt[g].astype(BF16) for g in gs}
        ys = {g: [None] * SSM_HEADS_PER_GROUP for g in gs}
        for r in range(SSM_HEADS_PER_GROUP):
            for g in gs:
                row = act_s[SSM_HEADS_PER_GROUP * g + r:SSM_HEADS_PER_GROUP * g + r + 1, :]
                seg = col4[g][:, LANES * r:LANES * (r + 1)] - row
                dec = jnp.where(causal, jnp.exp2(seg), 0.0)
                xp = xdt_b[g][:, LANES * (r // 2):LANES * (r // 2 + 1)]
                ys[g][r] = _dot((cb[g] * dec).astype(BF16), xp)
        for g in gs:
            y_diag = jnp.concatenate([jnp.where(first_half, ys[g][2 * p], ys[g][2 * p + 1])
                                      for p in range(2)], axis=1)
            al_e = a_e[g][L - 1:L, :]
            new = lax.dot_general(b_of(g), (xdt[g] * jnp.exp2(al_e - a_e[g])).astype(BF16),
                                  (((0,), (0,)), ((), ())), preferred_element_type=F32)
            y = y_diag + y_cs[g] * jnp.exp2(a_e[g]) + x_of(g) * dskip_ref[g]
            state[g] = state[g] * jnp.exp2(al_e) + new
            yg = y * _silu(z_ref[:, g * W:(g + 1) * W])
            ms = jnp.mean(yg * yg, axis=-1, keepdims=True)
            o_ref[:, g * W:(g + 1) * W] = (yg * lax.rsqrt(ms + NORM_EPS)
                                           * nw_ref[:, g * W:(g + 1) * W]).astype(o_ref.dtype)

    for g0 in range(0, G, SSD_GROUP_BATCH):
        run_groups(range(g0, g0 + SSD_GROUP_BATCH))


def _head_selectors():
    G, R = SSM_N_GROUPS, SSM_HEADS_PER_GROUP
    h = jnp.arange(LANES)[None, :, None]
    g = jnp.arange(G)[:, None, None]
    j64 = jnp.arange(SSM_GROUP_WIDTH)[None, None, :]
    j128 = jnp.arange(R * LANES)[None, None, :]
    e64 = (h == g * R + j64 // SSM_HEADDIM).astype(BF16)
    e128 = (h == g * R + j128 // LANES).astype(BF16)
    return e64, e128


def _ssd(zx, x2, w_dt, conv_w, conv_b, dt_bias, a_log, d_skip, norm_w, batch, seq, moe_weights, moe_layer):
    L, G, W, N = SSM_CHUNK, SSM_N_GROUPS, SSM_GROUP_WIDTH, SSM_D_STATE
    nc = seq // L
    cast = _CastPlan(*moe_weights, moe_layer, batch * nc, lambda b, c: b * nc + c)
    t = batch * seq
    pad = LANES - SSM_N_HEADS
    wdt = jnp.pad(w_dt, ((0, 0), (0, pad)))
    dtb = jnp.pad(dt_bias, (0, pad))[None, :]
    alog = jnp.pad(a_log, (0, pad))[None, :]
    dsk = jnp.repeat(d_skip, SSM_HEADDIM).reshape(G, 1, W)
    e64, e128 = _head_selectors()
    conv_dim = conv_w.shape[1]
    row = lambda b, c: (b * nc + c, 0)
    full2 = lambda b, c: (0, 0)
    full3 = lambda b, c: (0, 0, 0)
    yg, *w_bf16 = pl.pallas_call(
        functools.partial(_ssd_kernel, cast=cast),
        grid=(batch, nc),
        in_specs=[pl.BlockSpec((L, SSM_D_INNER), row),
                  pl.BlockSpec((L, SSM_D_INNER), lambda b, c: (b * nc + c, 1)),
                  pl.BlockSpec((L, 2 * G * N), lambda b, c: (b * nc + c, 2)),
                  pl.BlockSpec((L, D_MODEL), row),
                  pl.BlockSpec((D_MODEL, LANES), full2),
                  pl.BlockSpec((SSM_CONV_K, conv_dim), full2),
                  pl.BlockSpec((1, conv_dim), full2),
                  pl.BlockSpec((1, LANES), full2),
                  pl.BlockSpec((1, LANES), full2),
                  pl.BlockSpec((G, 1, W), full3),
                  pl.BlockSpec((1, SSM_D_INNER), full2),
                  pl.BlockSpec((G, LANES, W), full3),
                  pl.BlockSpec((G, LANES, SSM_HEADS_PER_GROUP * LANES), full3)] + cast.in_specs,
        out_specs=[pl.BlockSpec((L, SSM_D_INNER), row)] + cast.out_specs,
        out_shape=[jax.ShapeDtypeStruct((t, SSM_D_INNER), BF16)] + cast.out_shape,
        scratch_shapes=[pltpu.VMEM((CONV_TAIL + L, SSM_D_INNER), F32),
                        pltpu.VMEM((CONV_TAIL + L, 2 * G * N), F32),
                        pltpu.VMEM((G, N, W), F32),
                        pltpu.VMEM((3, L, LANES), BF16),
                        pltpu.VMEM((2, L, LANES), BF16),
                        pltpu.VMEM((LANES, L), F32)],
        compiler_params=_arb(2),
        name="ssd_scan",
    )(zx, zx, zx, x2, wdt, conv_w, conv_b[None, :], dtb, alog, dsk, norm_w[None, :], e64, e128, *cast.weights)
    return yg, tuple(w_bf16)


def _route(h, wr):
    tm = h.shape[0]
    logits = _dot(h.astype(BF16), wr)
    lane_i = lax.broadcasted_iota(I32, (tm, LANES), 1)
    lane = lane_i.astype(F32)
    neg = -jnp.inf
    big = float(LANES)

    def first_argmax(v, vmax):
        return jnp.min(jnp.where(v == vmax, lane, big), axis=-1, keepdims=True)

    gl = jnp.where((lane_i >= MOE_N_EXPERTS) & (lane_i < MOE_N_EXPERTS + MOE_GROUPS), logits, neg)
    gm = jnp.max(gl, axis=-1, keepdims=True)
    g_sel = first_argmax(gl, gm) - float(MOE_N_EXPERTS)
    g_gate = 1.0 / jnp.sum(jnp.exp(gl - gm), axis=-1, keepdims=True)
    lo = g_sel * float(MOE_EXPERTS_PER_GROUP)
    el = jnp.where((lane >= lo) & (lane < lo + float(MOE_EXPERTS_PER_GROUP)), logits, neg)
    m1 = jnp.max(el, axis=-1, keepdims=True)
    i1 = first_argmax(el, m1)
    el2 = jnp.where(lane == i1, neg, el)
    m2 = jnp.max(el2, axis=-1, keepdims=True)
    i2 = first_argmax(el2, m2)
    p2 = jnp.exp(m2 - m1)
    t1 = 1.0 / (1.0 + p2)
    t2 = p2 / (1.0 + p2)
    eid = jnp.where(lane_i == 0, i1, jnp.where(lane_i == 1, i2, 0.0)).astype(I32)
    wt = jnp.where(lane_i == 0, g_gate * t1, jnp.where(lane_i == 1, g_gate * t2, 0.0))
    count = jnp.sum(jnp.where(lane == i1, 1.0, 0.0) + jnp.where(lane == i2, 1.0, 0.0), axis=0, keepdims=True)
    return eid, wt, count


def _router_weights(w_group, w_expert):
    pad = LANES - MOE_N_EXPERTS - MOE_GROUPS
    return jnp.pad(jnp.concatenate([w_expert, w_group], axis=1), ((0, 0), (0, pad))).astype(BF16)


def _store_token_tiles(ref, v, first_token=0):
    rows = v.shape[0]
    for s in range(ROW_SLABS):
        ref[pl.ds(first_token * ROW_SLABS + s, rows, stride=ROW_SLABS), :] = v[:, s * LANES:(s + 1) * LANES]


def _load_token_tiles(ref, rows):
    return jnp.concatenate([ref[pl.ds(s, rows, stride=ROW_SLABS), :] for s in range(ROW_SLABS)], axis=1)


def _mm_ln_router_kernel(x_ref, w_ref, r_ref, g_ref, b_ref, wr_ref, ht_ref, eid_ref, wt_ref, cnt_ref, wb_ref):
    @pl.when(pl.program_id(0) == 0)
    def _():
        cnt_ref[...] = jnp.zeros(cnt_ref.shape, F32)
        wb_ref[...] = w_ref[...].astype(BF16)

    sub = x_ref.shape[0] // LN_SUBTILES
    rows = [slice(u * sub, (u + 1) * sub) for u in range(LN_SUBTILES)]
    ys = [DEEPNORM_ALPHA * r_ref[rs, :] + _dot(x_ref[rs, :].astype(BF16), wb_ref[...]) for rs in rows]
    hs = [_layer_norm(y, g_ref[...], b_ref[...]) for y in ys]
    routes = [_route(h, wr_ref[...]) for h in hs]
    for u, rs in enumerate(rows):
        _store_token_tiles(ht_ref, hs[u], first_token=u * sub)
        eid_ref[rs, :] = routes[u][0]
        wt_ref[rs, :] = routes[u][1]
    cnt_ref[...] += jnp.broadcast_to(sum(r[2] for r in routes), cnt_ref.shape)


def _mm_ln_router(x, w, resid, g, b, wr, name):
    m, k = x.shape
    d = w.shape[1]
    tm = min(MM_TM, m)
    row = lambda i: (i, 0)
    full = lambda i: (0, 0)
    return pl.pallas_call(
        _mm_ln_router_kernel,
        grid=(m // tm,),
        in_specs=[pl.BlockSpec((tm, k), row),
                  pl.BlockSpec((k, d), full, pipeline_mode=pl.Buffered(1)),
                  pl.BlockSpec((tm, d), row),
                  pl.BlockSpec((1, d), full), pl.BlockSpec((1, d), full), pl.BlockSpec((d, LANES), full)],
        out_specs=[pl.BlockSpec((tm * ROW_SLABS, LANES), row),
                   pl.BlockSpec((tm, LANES), row), pl.BlockSpec((tm, LANES), row),
                   pl.BlockSpec((8, LANES), full)],
        out_shape=[jax.ShapeDtypeStruct((m * ROW_SLABS, LANES), F32),
                   jax.ShapeDtypeStruct((m, LANES), I32), jax.ShapeDtypeStruct((m, LANES), F32),
                   jax.ShapeDtypeStruct((8, LANES), F32)],
        scratch_shapes=[pltpu.VMEM((k, d), BF16)],
        compiler_params=_arb(1),
        name=name,
    )(x, w, resid, g[None, :], b[None, :], wr)


def _dispatch_kernel(eid_ref, cnt_ref, h_ref, dest_ref, be_ref, nu_ref, xs_hbm,
                     base, upper, zbuf, dst_v, dst_s, pe_v, pe_s, sem_z, sem_r, sem_s, *, blk):
    i = pl.program_id(0)
    tm = eid_ref.shape[0]
    n_e = LANES
    eid_t = eid_ref[...].astype(F32).T
    sub = lax.broadcasted_iota(I32, (n_e, tm), 0).astype(F32)
    oh = [(sub == eid_t[k:k + 1, :]).astype(F32) for k in range(MOE_TOP_K)]
    tot = [jnp.sum(o, axis=1, keepdims=True) for o in oh]

    @pl.when(i == 0)
    def _():
        r_i = lax.broadcasted_iota(I32, (tm, tm), 0)
        c_i = lax.broadcasted_iota(I32, (tm, tm), 1)
        upper[...] = (r_i < c_i).astype(BF16)
        counts = jnp.broadcast_to(cnt_ref[0:1, :], (n_e, n_e)).T
        padded = jnp.floor((counts + float(blk - 1)) * (1.0 / blk)) * float(blk)
        r_i = lax.broadcasted_iota(I32, (n_e, n_e), 0)
        c_i = lax.broadcasted_iota(I32, (n_e, n_e), 1)
        tril = (r_i >= c_i).astype(BF16)
        p1, p2, p3 = _split3(padded)
        pends = _dot(tril, p1) + _dot(tril, p2) + _dot(tril, p3)
        base[...] = pends - padded
        nbp = be_ref.shape[1]
        blk_start = lax.broadcasted_iota(I32, (n_e, nbp), 1).astype(F32) * float(blk)
        is_e = lax.broadcasted_iota(I32, (n_e, nbp), 0) < MOE_N_EXPERTS
        done = jnp.where(is_e & (jnp.tile(pends, (1, nbp // LANES)) <= blk_start), 1.0, 0.0)
        be = jnp.minimum(jnp.sum(done, axis=0, keepdims=True), float(MOE_N_EXPERTS - 1))
        be_ref[...] = jnp.broadcast_to(be, be_ref.shape).astype(I32)
        last = pends[MOE_N_EXPERTS - 1:MOE_N_EXPERTS, :] * (1.0 / blk)
        nu_ref[...] = jnp.broadcast_to(last, nu_ref.shape).astype(I32)
        zbuf[...] = jnp.zeros(zbuf.shape, F32)
        row8 = lax.broadcasted_iota(I32, (8, LANES), 0)
        pe_v[...] = jnp.where(row8 == 0, pends.T[0:8, :], counts.T[0:8, :]).astype(I32)
        cp = pltpu.make_async_copy(pe_v, pe_s, sem_s)
        cp.start()
        cp.wait()

        brows = blk * ROW_SLABS
        prows = ZERO_PIECE * ROW_SLABS
        shift = blk.bit_length() - 1

        def piece_copy(e, p):
            start = pl.multiple_of((pe_s[0, e] - (p + 1) * ZERO_PIECE) * ROW_SLABS, prows)
            return pltpu.make_async_copy(zbuf.at[pl.ds(0, prows)], xs_hbm.at[pl.ds(start, prows)], sem_z)

        def n_pieces(e):
            cnt = pe_s[1, e]
            pad = lax.shift_left(lax.shift_right_logical(cnt + (blk - 1), shift), shift) - cnt
            return lax.shift_right_logical(pad + (ZERO_PIECE - 1), ZERO_PIECE.bit_length() - 1)

        def tail_copy(b):
            return pltpu.make_async_copy(zbuf, xs_hbm.at[pl.ds(pl.multiple_of(b * brows, brows), brows)], sem_z)

        n_used = lax.shift_right_logical(pe_s[0, MOE_N_EXPERTS - 1], shift)
        n_blocks = xs_hbm.shape[0] // brows
        for e in range(MOE_N_EXPERTS):
            lax.fori_loop(0, n_pieces(e), lambda p, c, e=e: (piece_copy(e, p).start(), c)[1], 0)
        lax.fori_loop(n_used, n_blocks, lambda b, c: (tail_copy(b).start(), c)[1], 0)
        for e in range(MOE_N_EXPERTS):
            lax.fori_loop(0, n_pieces(e), lambda p, c, e=e: (piece_copy(e, p).wait(), c)[1], 0)
        lax.fori_loop(n_used, n_blocks, lambda b, c: (tail_copy(b).wait(), c)[1], 0)

    b0 = base[:, 0:1]
    c0 = _dot(oh[0].astype(BF16), upper[...])
    c1 = _dot(oh[1].astype(BF16), upper[...])
    d0 = jnp.sum(oh[0] * (b0 + c0), axis=0, keepdims=True)
    d1 = jnp.sum(oh[1] * (b0 + tot[0] + c1), axis=0, keepdims=True)
    base[...] += jnp.broadcast_to(tot[0] + tot[1], base.shape)
    row8 = lax.broadcasted_iota(I32, (8, tm), 0)
    dst = jnp.where(row8 == 0, d0, jnp.where(row8 == 1, d1, 0.0)).astype(I32)
    dest_ref[...] = dst
    dst_v[...] = dst
    cp = pltpu.make_async_copy(dst_v, dst_s, sem_s)
    cp.start()
    cp.wait()

    for r in range(tm):
        for k in range(MOE_TOP_K):
            slot = pl.multiple_of(dst_s[k, r] * ROW_SLABS, ROW_SLABS)
            pltpu.make_async_copy(h_ref.at[pl.ds(r * ROW_SLABS, ROW_SLABS)],
                                  xs_hbm.at[pl.ds(slot, ROW_SLABS)], sem_r).start(priority=k)
    for k in range(MOE_TOP_K):
        pltpu.make_async_copy(h_ref, xs_hbm.at[pl.ds(0, tm * ROW_SLABS)], sem_r).wait()


def _moe_dispatch(ht, eid, counts, blk, name):
    t = ht.shape[0] // ROW_SLABS
    assert blk & (blk - 1) == 0, "block size must be a power of two"
    tm = min(DSP_TM, t)
    nt = t // tm
    n_blocks = -(-(t * MOE_TOP_K) // blk) + MOE_N_EXPERTS
    nbp = -(-n_blocks // LANES) * LANES
    full = lambda i: (0, 0)
    dest, be, nu, xs = pl.pallas_call(
        functools.partial(_dispatch_kernel, blk=blk),
        grid=(nt,),
        in_specs=[pl.BlockSpec((tm, LANES), lambda i: (i, 0)),
                  pl.BlockSpec((8, LANES), full),
                  pl.BlockSpec((tm * ROW_SLABS, LANES), lambda i: (i, 0))],
        out_specs=[pl.BlockSpec((8, tm), lambda i: (0, i)),
                   pl.BlockSpec((8, nbp), full),
                   pl.BlockSpec((8, LANES), full),
                   pl.BlockSpec(memory_space=pl.ANY)],
        out_shape=[jax.ShapeDtypeStruct((8, t), I32), jax.ShapeDtypeStruct((8, nbp), I32),
                   jax.ShapeDtypeStruct((8, LANES), I32),
                   jax.ShapeDtypeStruct((n_blocks * blk * ROW_SLABS, LANES), F32)],
        scratch_shapes=[pltpu.VMEM((LANES, LANES), F32),
                        pltpu.VMEM((tm, tm), BF16), pltpu.VMEM((blk * ROW_SLABS, LANES), F32),
                        pltpu.VMEM((8, tm), I32), pltpu.SMEM((8, tm), I32),
                        pltpu.VMEM((8, LANES), I32), pltpu.SMEM((8, LANES), I32),
                        pltpu.SemaphoreType.DMA(()), pltpu.SemaphoreType.DMA(()), pltpu.SemaphoreType.DMA(())],
        compiler_params=_arb(1),
        name=name,
    )(eid, counts, ht)
    return dest, be[0, :n_blocks], nu[0, :1], xs, n_blocks


def _ffn_kernel(be_ref, nu_ref, x_ref, wg_ref, wu_ref, wd_ref, o_ref):
    i = pl.program_id(0)

    @pl.when(i < nu_ref[0])
    def _():
        blk = x_ref.shape[0] // ROW_SLABS
        xb = _load_token_tiles(x_ref, blk).astype(BF16)
        hid = _silu(_dot(xb, wg_ref[0])) * _dot(xb, wu_ref[0])
        _store_token_tiles(o_ref, _dot(hid.astype(BF16), wd_ref[0]))

    @pl.when(i >= nu_ref[0])
    def _():
        o_ref[...] = jnp.zeros(o_ref.shape, o_ref.dtype)


def _moe_ffn(xs, block_expert, n_used, n_blocks, weights, blk, name):
    w_gate, w_up, w_down = weights
    d, f = w_gate.shape[1], w_gate.shape[2]
    brows = blk * ROW_SLABS
    used = lambda i, be, nu: (jnp.minimum(i, nu[0] - 1), 0)
    every = lambda i, be, nu: (i, 0)
    expert = lambda i, be, nu: (be[i], 0, 0)
    grid_spec = pltpu.PrefetchScalarGridSpec(
        num_scalar_prefetch=2,
        grid=(n_blocks,),
        in_specs=[pl.BlockSpec((brows, LANES), used),
                  pl.BlockSpec((1, d, f), expert), pl.BlockSpec((1, d, f), expert),
                  pl.BlockSpec((1, f, d), expert)],
        out_specs=pl.BlockSpec((brows, LANES), every))
    return pl.pallas_call(
        _ffn_kernel,
        grid_spec=grid_spec,
        out_shape=jax.ShapeDtypeStruct((n_blocks * brows, LANES), F32),
        compiler_params=_arb(1),
        name=name,
    )(block_expert, n_used, xs, w_gate, w_up, w_down)


def _combine_ln_kernel(dst_ref, dstn_ref, ht_ref, wt_ref, g_ref, b_ref, yb_hbm, o_ref, ybuf, sem):
    i = pl.program_id(0)
    n = pl.num_programs(0)
    tm = o_ref.shape[0]

    def start_gather(dref, slot):
        for r in range(tm):
            for k in range(MOE_TOP_K):
                src = pl.multiple_of(dref[k, r] * ROW_SLABS, ROW_SLABS)
                pltpu.make_async_copy(yb_hbm.at[pl.ds(src, ROW_SLABS)],
                                      ybuf.at[slot, k, pl.ds(r * ROW_SLABS, ROW_SLABS)],
                                      sem.at[slot]).start(priority=k)

    @pl.when(i == 0)
    def _():
        start_gather(dst_ref, 0)

    @pl.when(i + 1 < n)
    def _():
        start_gather(dstn_ref, (i + 1) % 2)

    slot = i % 2
    for k in range(MOE_TOP_K):
        pltpu.make_async_copy(yb_hbm.at[pl.ds(0, tm * ROW_SLABS)], ybuf.at[slot, k], sem.at[slot]).wait()
    wt = wt_ref[...]
    ffn = (wt[:, 0:1] * _load_token_tiles(ybuf.at[slot, 0], tm)
           + wt[:, 1:2] * _load_token_tiles(ybuf.at[slot, 1], tm))
    h = _load_token_tiles(ht_ref, tm)
    o_ref[...] = _layer_norm(DEEPNORM_ALPHA * h + ffn, g_ref[...], b_ref[...])


def _combine_ln(ht, wt, dest, yb, g, b, name):
    t, d = ht.shape[0] // ROW_SLABS, D_MODEL
    tm = min(CMB_TM, t)
    nblk = t // tm
    row = lambda i: (i, 0)
    full = lambda i: (0, 0)
    return pl.pallas_call(
        _combine_ln_kernel,
        grid=(nblk,),
        in_specs=[pl.BlockSpec((8, tm), lambda i: (0, i), memory_space=pltpu.SMEM),
                  pl.BlockSpec((8, tm), lambda i: (0, jnp.minimum(i + 1, nblk - 1)),
                               memory_space=pltpu.SMEM),
                  pl.BlockSpec((tm * ROW_SLABS, LANES), row), pl.BlockSpec((tm, LANES), row),
                  pl.BlockSpec((1, d), full), pl.BlockSpec((1, d), full),
                  pl.BlockSpec(memory_space=pl.ANY)],
        out_specs=pl.BlockSpec((tm, d), row),
        out_shape=jax.ShapeDtypeStruct((t, d), F32),
        scratch_shapes=[pltpu.VMEM((2, MOE_TOP_K, tm * ROW_SLABS, LANES), F32), pltpu.SemaphoreType.DMA((2,))],
        compiler_params=_arb(1),
        name=name,
    )(dest, dest, ht, wt, g[None, :], b[None, :], yb)


def _hier_moe_ln(ht, eid, wt, counts, w_bf16, g, b, layer):
    dest, block_expert, n_used, xs, n_blocks = _moe_dispatch(ht, eid, counts, FFN_BLK, f"moe_dispatch{layer}")
    yb = _moe_ffn(xs, block_expert, n_used, n_blocks, w_bf16, FFN_BLK, f"moe_ffn{layer}")
    return _combine_ln(ht, wt, dest, yb, g, b, f"moe_combine_ln{layer}")


def _qkv_rope_kernel(x_ref, wf_ref, pos_ref, inv_ref, o_ref, w_ref):
    @pl.when(pl.program_id(0) == 0)
    def _():
        w_ref[...] = wf_ref[...].astype(BF16)

    xb = x_ref[...].astype(BF16)
    tm = xb.shape[0]
    n = D_MODEL
    ang = pos_ref[...].astype(F32) * inv_ref[...]
    lane = lax.broadcasted_iota(I32, (tm, LANES), 1)
    dd = lane & (ATTN_HEAD_DIM - 1)
    half = ROT_DIM // 2
    cosv = jnp.cos(ang)
    sinv = jnp.sin(ang)
    c_t = jnp.where(dd < ROT_DIM, cosv, 1.0)
    s_up = jnp.where(dd < half, -sinv, 0.0)
    s_dn = jnp.where((dd >= half) & (dd < ROT_DIM), sinv, 0.0)
    for j, sc in ((0, ATTN_HEAD_DIM ** -0.5 * math.log2(math.e)), (1, 1.0)):
        acc = _dot(xb, w_ref[:, j * n:(j + 1) * n])
        c_j, up_j, dn_j = c_t * sc, s_up * sc, s_dn * sc
        for blk in range(n // LANES):
            tt = acc[:, blk * LANES:(blk + 1) * LANES]
            out = tt * c_j + pltpu.roll(tt, LANES - half, 1) * up_j + pltpu.roll(tt, half, 1) * dn_j
            o_ref[:, j * n + blk * LANES:j * n + (blk + 1) * LANES] = out.astype(o_ref.dtype)
    o_ref[:, 2 * n:3 * n] = _dot(xb, w_ref[:, 2 * n:3 * n]).astype(o_ref.dtype)


def _rope_inv_table():
    inv = ROPE_THETA ** (-jnp.arange(0, ROT_DIM, 2, dtype=F32) / ROT_DIM)
    head = jnp.concatenate([inv, inv, jnp.zeros((ATTN_HEAD_DIM - ROT_DIM,), F32)])
    return jnp.tile(head, LANES // ATTN_HEAD_DIM)[None, :]


def _qkv_rope(h, w_qkv, positions):
    m, k = h.shape
    n = w_qkv.shape[1]
    tm = min(MM_TM, m)
    pos = positions.reshape(m, 1)
    return pl.pallas_call(
        _qkv_rope_kernel,
        grid=(m // tm,),
        in_specs=[pl.BlockSpec((tm, k), lambda i: (i, 0)),
                  pl.BlockSpec((k, n), lambda i: (0, 0), pipeline_mode=pl.Buffered(1)),
                  pl.BlockSpec((tm, 1), lambda i: (i, 0)),
                  pl.BlockSpec((1, LANES), lambda i: (0, 0))],
        out_specs=pl.BlockSpec((tm, n), lambda i: (i, 0)),
        out_shape=jax.ShapeDtypeStruct((m, n), BF16),
        scratch_shapes=[pltpu.VMEM((k, n), BF16)],
        compiler_params=_arb(1),
        name="mm_qkv_rope",
    )(h, w_qkv, pos, _rope_inv_table())


def _attn_kernel(q_ref, k_ref, v_ref, lq1_ref, lk1_ref, lq2_ref, lk2_ref, sw_ref, wg_ref, wu_ref, wd_ref,
                 o_ref, wgb_ref, wub_ref, wdb_ref,
                 vx_ref, m0_ref, m1_ref, acc0_ref, acc1_ref, *, lambda_init, cast):
    m_refs = (m0_ref, m1_ref)
    acc_refs = (acc0_ref, acc1_ref)
    jp = pl.program_id(2)
    n_tiles = ATT_TILES_PER_STEP
    tq = q_ref.shape[0] // n_tiles
    cast.emit(pl.program_id(0) * pl.num_programs(1) + pl.program_id(1),
              (wg_ref, wu_ref, wd_ref), (wgb_ref, wub_ref, wdb_ref), extra=jp == 0)

    @pl.when(jp == 0)
    def _():
        vx_ref[:, 0:LANES] = v_ref[...]
        vx_ref[:, LANES:] = jnp.ones((vx_ref.shape[0], LANES), vx_ref.dtype)

    q = q_ref[...]
    lane = lax.broadcasted_iota(I32, q.shape, 1)
    zero = jnp.zeros((), q.dtype)
    qs = (jnp.where(lane < ATTN_HEAD_DIM, q, zero), jnp.where(lane >= ATTN_HEAD_DIM, q, zero))
    for c in range(2):
        m_refs[c][...] = jnp.full(m_refs[c].shape, -jnp.inf, F32)
        acc_refs[c][...] = jnp.zeros(acc_refs[c].shape, F32)

    def step(off, windows):
        off = pl.multiple_of(off, tq)
        widest = max(w for _, w, _ in windows)
        kb = k_ref[pl.ds(off, widest), :]
        vb = vx_ref[pl.ds(off, widest), :]
        chains = [(slice(r0, r0 + tq), w, d, c) for r0, w, d in windows for c in range(2)]
        ss = [lax.dot_general(qs[c][rows, :], kb[:w, :], (((1,), (1,)), ((), ())), preferred_element_type=F32)
              for rows, w, _, c in chains]
        for n, (_, w, d, _) in enumerate(chains):
            if d is not None:
                row_i = lax.broadcasted_iota(I32, (tq, w), 0)
                col_i = lax.broadcasted_iota(I32, (tq, w), 1)
                ss[n] = jnp.where(row_i + d >= col_i, ss[n], -jnp.inf)
        m_prev = [m_refs[c][rows, :] for rows, _, _, c in chains]
        mn = [jnp.maximum(mp, jnp.max(s, axis=-1, keepdims=True)) for mp, s in zip(m_prev, ss)]
        ps = [jnp.exp2(s - jnp.tile(m, (1, s.shape[1] // LANES))).astype(BF16) for s, m in zip(ss, mn)]
        for n, (rows, w, _, c) in enumerate(chains):
            alpha = jnp.exp2(m_prev[n] - mn[n])
            acc_refs[c][rows, :] = jnp.tile(alpha, (1, 2)) * acc_refs[c][rows, :] + _dot(ps[n], vb[:w, :])
            m_refs[c][rows, :] = mn[n]

    def shared(j, carry):
        for p in range(n_tiles // 2):
            step(j * (2 * tq), [((2 * p) * tq, 2 * tq, None), ((2 * p + 1) * tq, 2 * tq, None)])
        return carry
    lax.fori_loop(0, jp * (n_tiles // 2), shared, 0)
    base = jp * (n_tiles * tq)
    for p in range(n_tiles // 2):
        a, b = 2 * p * tq, (2 * p + 1) * tq
        step(base + a, [(a, tq, 0), (b, 2 * tq, tq)])
        later = [(r * tq, 2 * tq, None) for r in range(2 * p + 2, n_tiles)]
        if later:
            step(base + a, later)

    lam = (jnp.exp(jnp.sum(lq1_ref[...] * lk1_ref[...], axis=-1, keepdims=True))
           - jnp.exp(jnp.sum(lq2_ref[...] * lk2_ref[...], axis=-1, keepdims=True)) + lambda_init)
    a1 = acc0_ref[...]
    a2 = acc1_ref[...]
    o = a1[:, :LANES] / a1[:, LANES:] - lam * (a2[:, :LANES] / a2[:, LANES:])
    o = o * lax.rsqrt(jnp.mean(o * o, axis=-1, keepdims=True) + NORM_EPS)
    o_ref[...] = (o * sw_ref[...] * (1.0 - lambda_init)).astype(o_ref.dtype)


def _diff_attention(qkv, lq1, lk1, lq2, lk2, subln_w, lambda_init, batch, seq, moe_weights, moe_layer):
    t = batch * seq
    tq = ATT_TILES_PER_STEP * min(ATT_TQ, seq // ATT_TILES_PER_STEP)
    nq = seq // tq
    h_n = ATTN_N_HEADS
    vec = lambda b, h, i: (0, 0)
    cast = _CastPlan(*moe_weights, moe_layer, batch * h_n, lambda b, h, i: b * h_n + h)
    o, *w_bf16 = pl.pallas_call(
        functools.partial(_attn_kernel, lambda_init=lambda_init, cast=cast),
        grid=(batch, h_n, nq),
        in_specs=[pl.BlockSpec((tq, LANES), lambda b, h, i: (b * nq + i, h)),
                  pl.BlockSpec((seq, LANES), lambda b, h, i: (b, h_n + h)),
                  pl.BlockSpec((seq, LANES), lambda b, h, i: (b, 2 * h_n + h)),
                  pl.BlockSpec((1, ATTN_HEAD_DIM), vec), pl.BlockSpec((1, ATTN_HEAD_DIM), vec),
                  pl.BlockSpec((1, ATTN_HEAD_DIM), vec), pl.BlockSpec((1, ATTN_HEAD_DIM), vec),
                  pl.BlockSpec((1, ATTN_V_DIM), vec)] + cast.in_specs,
        out_specs=[pl.BlockSpec((tq, LANES), lambda b, h, i: (b * nq + i, h))] + cast.out_specs,
        out_shape=[jax.ShapeDtypeStruct((t, h_n * ATTN_V_DIM), BF16)] + cast.out_shape,
        scratch_shapes=[pltpu.VMEM((seq, 2 * LANES), BF16),
                        pltpu.VMEM((tq, LANES), F32), pltpu.VMEM((tq, LANES), F32),
                        pltpu.VMEM((tq, 2 * LANES), F32), pltpu.VMEM((tq, 2 * LANES), F32)],
        compiler_params=_arb(3),
        name="diff_attn",
    )(qkv, qkv, qkv, lq1[None, :], lk1[None, :], lq2[None, :], lk2[None, :], subln_w[None, :], *cast.weights)
    return o, tuple(w_bf16)


def kernel(x, positions, ln_mix_g, ln_mix_b, ln_ffn_g, ln_ffn_b, ssm_w_in, ssm_conv_w, ssm_conv_b, ssm_dt_bias, ssm_a_log, ssm_d, ssm_norm_w, ssm_w_out, attn_w_qkv, attn_lam_q1, attn_lam_k1, attn_lam_q2, attn_lam_k2, attn_subln_w, attn_w_o, moe_w_group, moe_w_expert, moe_w_gate, moe_w_up, moe_w_down):
    batch, seq, d = x.shape
    t = batch * seq
    h = x.reshape(t, d)

    w_in = ssm_w_in[0].astype(BF16)

    moe_weights = (moe_w_gate, moe_w_up, moe_w_down)
    zx = _matmul(h, w_in, SSM_ZX_DIM, F32)
    yg, moe_w0 = _ssd(zx, h, w_in[:, SSM_ZX_DIM:], ssm_conv_w[0], ssm_conv_b[0], ssm_dt_bias[0],
                      ssm_a_log[0], ssm_d[0], ssm_norm_w[0], batch, seq, moe_weights, 0)
    ht, eid, wt, cnt = _mm_ln_router(yg, ssm_w_out[0], h, ln_mix_g[0], ln_mix_b[0],
                                     _router_weights(moe_w_group[0], moe_w_expert[0]), "mm_ssm_out_ln_router")
    h = _hier_moe_ln(ht, eid, wt, cnt, moe_w0, ln_ffn_g[0], ln_ffn_b[0], 0)

    lambda_init = 0.8 - 0.6 * math.exp(-0.3 * 1)
    qkv = _qkv_rope(h, attn_w_qkv[0], positions)
    o, moe_w1 = _diff_attention(qkv, attn_lam_q1[0], attn_lam_k1[0], attn_lam_q2[0], attn_lam_k2[0],
                                attn_subln_w[0], lambda_init, batch, seq, moe_weights, 1)
    ht, eid, wt, cnt = _mm_ln_router(o, attn_w_o[0], h, ln_mix_g[1], ln_mix_b[1],
                                     _router_weights(moe_w_group[1], moe_w_expert[1]), "mm_attn_out_ln_router")
    h = _hier_moe_ln(ht, eid, wt, cnt, moe_w1, ln_ffn_g[1], ln_ffn_b[1], 1)
    return h.reshape(batch, seq, d)
```

```python
import functools
import math

import jax
import jax.numpy as jnp
from jax import lax
from jax.experimental import pallas as pl
from jax.experimental.pallas import tpu as pltpu

F32 = jnp.float32
BF16 = jnp.bfloat16
I32 = jnp.int32

D_MODEL = 1024
DEPTH = 2
SSM_D_INNER = 2048
SSM_HEADDIM = 64
SSM_N_HEADS = 32
SSM_N_GROUPS = 8
SSM_HEADS_PER_GROUP = 4
SSM_D_STATE = 128
SSM_CONV_K = 4
SSM_CHUNK = 128
SSM_GROUP_WIDTH = SSM_HEADS_PER_GROUP * SSM_HEADDIM
SSM_ZX_DIM = 2 * SSM_D_INNER + 2 * SSM_N_GROUPS * SSM_D_STATE
ATTN_HEAD_DIM = 64
ATTN_N_HEADS = 8
ATTN_V_DIM = 128
ROT_DIM = 16
ROPE_THETA = 500000.0
MOE_GROUPS = 4
MOE_EXPERTS_PER_GROUP = 8
MOE_N_EXPERTS = 32
MOE_TOP_K = 2
MOE_D_FF = 512
DEEPNORM_ALPHA = (2 * DEPTH) ** 0.25
NORM_EPS = 1e-5

LANES = 128
ROW_SLABS = D_MODEL // LANES
CONV_TAIL = 8
CONV_ROWS = 128

MM_TM = 512
MM_TN = 1024
FFN_BLK = 512
DSP_TM = 512
ZERO_PIECE = 64
SSD_GROUP_BATCH = 2
LN_TM = 1024
LN_SUBTILES = 4
CMB_TM = 256
ATT_TQ = 512
ATT_TILES_PER_STEP = 4


def _arb(n):
    return pltpu.CompilerParams(dimension_semantics=("arbitrary",) * n,
                                vmem_limit_bytes=56 * 1024 * 1024)


def _silu(x):
    hx = 0.5 * x
    return hx + hx * jnp.tanh(hx)


def _softplus(x):
    return jnp.maximum(x, 0.0) + jnp.log(1.0 + jnp.exp(-jnp.abs(x)))


def _layer_norm(y, g, b):
    mu = jnp.mean(y, axis=-1, keepdims=True)
    d = y - mu
    var = jnp.mean(d * d, axis=-1, keepdims=True)
    return d * lax.rsqrt(var + NORM_EPS) * g + b


def _split3(a):
    a1 = a.astype(BF16)
    r1 = a - a1.astype(F32)
    a2 = r1.astype(BF16)
    a3 = (r1 - a2.astype(F32)).astype(BF16)
    return a1, a2, a3


def _dot(a, b):
    return jnp.dot(a, b, preferred_element_type=F32)


def _dot_sel(a, sel):
    a1, a2, a3 = _split3(a)
    return _dot(a1, sel) + _dot(a2, sel) + _dot(a3, sel)


def _dot_f32(a, b):
    a1, a2, a3 = _split3(a)
    b1, b2, b3 = _split3(b)
    return (_dot(a1, b1) + _dot(a1, b2) + _dot(a2, b1)
            + _dot(a2, b2) + _dot(a1, b3) + _dot(a3, b1))


class _CastPlan:
    def __init__(self, w_gate, w_up, w_down, layer, steps, step_of):
        self.weights = (w_gate, w_up, w_down)
        self.n_e = w_gate.shape[1]
        self.eps = -(-self.n_e // steps)
        assert self.n_e % self.eps == 0
        self.n_cast = self.n_e // self.eps
        self.stride = steps // self.n_cast
        block = lambda *ids: jnp.minimum(step_of(*ids) // self.stride, self.n_cast - 1)
        self.in_specs = [pl.BlockSpec((1, self.eps) + w.shape[2:], lambda *ids: (layer, block(*ids), 0, 0))
                         for w in self.weights]
        self.out_specs = [pl.BlockSpec((self.eps,) + w.shape[2:], lambda *ids: (block(*ids), 0, 0))
                          for w in self.weights]
        self.out_shape = [jax.ShapeDtypeStruct(w.shape[1:], BF16) for w in self.weights]

    def emit(self, step, w_refs, o_refs, extra=True):
        @pl.when(extra & (step % self.stride == 0) & (step // self.stride < self.n_cast))
        def _():
            for w_ref, o_ref in zip(w_refs, o_refs):
                o_ref[...] = w_ref[0].astype(BF16)


def _mm_kernel(x_ref, w_ref, o_ref):
    xb = x_ref[...].astype(BF16)
    for j in range(o_ref.shape[1] // MM_TN):
        cols = slice(j * MM_TN, (j + 1) * MM_TN)
        o_ref[:, cols] = _dot(xb, w_ref[:, cols]).astype(o_ref.dtype)


def _matmul(x, w, n, out_dtype):
    m, k = x.shape
    tm = min(MM_TM, m)
    return pl.pallas_call(
        _mm_kernel,
        grid=(m // tm,),
        in_specs=[pl.BlockSpec((tm, k), lambda i: (i, 0)),
                  pl.BlockSpec((k, n), lambda i: (0, 0), pipeline_mode=pl.Buffered(1))],
        out_specs=pl.BlockSpec((tm, n), lambda i: (i, 0)),
        out_shape=jax.ShapeDtypeStruct((m, n), out_dtype),
        compiler_params=_arb(1),
        name="mm_inproj",
    )(x, w)


def _ssd_kernel(z_ref, xs_ref, bc_ref, x_ref, wdt_ref, cw_ref, cb_ref, dtb_ref, alog_ref, dskip_ref,
                nw_ref, e64_ref, e128_ref, wg_ref, wu_ref, wd_ref, o_ref, wgb_ref, wub_ref, wdb_ref,
                stage_x, stage_bc, state, acp_s, dtp_s, act_s, *, cast):
    G, W, N = SSM_N_GROUPS, SSM_GROUP_WIDTH, SSM_D_STATE
    L = z_ref.shape[0]
    c = pl.program_id(1)
    cast.emit(pl.program_id(0) * pl.num_programs(1) + c, (wg_ref, wu_ref, wd_ref), (wgb_ref, wub_ref, wdb_ref))

    @pl.when(c == 0)
    def _():
        state[...] = jnp.zeros(state.shape, F32)
        for stage in (stage_x, stage_bc):
            stage[0:CONV_TAIL, :] = jnp.zeros((CONV_TAIL, stage.shape[1]), F32)

    @pl.when(c != 0)
    def _():
        for stage in (stage_x, stage_bc):
            stage[0:CONV_TAIL, :] = stage[L:L + CONV_TAIL, :]

    stage_x[CONV_TAIL:CONV_TAIL + L, :] = xs_ref[...]
    stage_bc[CONV_TAIL:CONV_TAIL + L, :] = bc_ref[...]

    def conv(stage, c0, width, w0):
        taps = [cw_ref[k:k + 1, w0:w0 + width] for k in range(SSM_CONV_K)]
        bias = cb_ref[:, w0:w0 + width]
        outs = []
        for rb in range(L // CONV_ROWS):
            r0 = rb * CONV_ROWS
            u = stage[r0:r0 + CONV_ROWS + CONV_TAIL, c0:c0 + width]
            out = bias + taps[SSM_CONV_K - 1] * u[CONV_TAIL:, :]
            for k in range(SSM_CONV_K - 1):
                back = SSM_CONV_K - 1 - k
                out = out + taps[k] * pltpu.roll(u, back, 0)[CONV_TAIL:, :]
            outs.append(_silu(out))
        return jnp.concatenate(outs, axis=0)

    xc, bcv, ccv = {}, {}, {}

    def conv_groups(gs):
        for g in gs:
            xc[g] = conv(stage_x, g * W, W, g * W)
            bcv[g] = conv(stage_bc, g * N, N, SSM_D_INNER + g * N).astype(BF16)
            ccv[g] = conv(stage_bc, G * N + g * N, N, SSM_D_INNER + G * N + g * N).astype(BF16)

    x_of, b_of, c_of = xc.__getitem__, bcv.__getitem__, ccv.__getitem__

    dt = _softplus(_dot(x_ref[...].astype(BF16), wdt_ref[...].astype(BF16)) + dtb_ref[...])
    a = -jnp.exp(alog_ref[...])
    row_i = lax.broadcasted_iota(I32, (L, L), 0)
    col_i = lax.broadcasted_iota(I32, (L, L), 1)
    causal = row_i >= col_i
    a1, a2, a3 = _split3(dt * a)
    tri = causal.astype(BF16)
    ac = (_dot(tri, a1) + _dot(tri, a2) + _dot(tri, a3)) * math.log2(math.e)
    act_s[...] = ac.T
    for n, part in enumerate(_split3(ac)):
        acp_s[n] = part
    for n, part in enumerate(_split3(dt)[:2]):
        dtp_s[n] = part
    lane = lax.broadcasted_iota(I32, (L, LANES), 1)
    first_half = lane < SSM_HEADDIM

    def select(parts_ref, n_parts, sel):
        out = _dot(parts_ref[0], sel)
        for n in range(1, n_parts):
            out = out + _dot(parts_ref[n], sel)
        return out

    def run_groups(gs):
        conv_groups(gs)
        col4 = {g: select(acp_s, 3, e128_ref[g]) for g in gs}
        dt_e = {g: select(dtp_s, 2, e64_ref[g]) for g in gs}
        cb = {g: lax.dot_general(c_of(g), b_of(g), (((1,), (1,)), ((), ())), preferred_element_type=F32)
              for g in gs}
        y_cs = {g: _dot(c_of(g), state[g].astype(BF16)) for g in gs}
        a_e = {g: jnp.concatenate(
            [jnp.where(first_half, col4[g][:, 2 * p * LANES:(2 * p + 1) * LANES],
                       col4[g][:, (2 * p + 1) * LANES:(2 * p + 2) * LANES]) for p in range(2)], axis=1)
            for g in gs}
        xdt = {g: x_of(g) * dt_e[g] for g in gs}
        xdt_b = {g: xdt[g].astype(BF16) for g in gs}
        ys = {g: [None] * SSM_HEADS_PER_GROUP for g in gs}
        for r in range(SSM_HEADS_PER_GROUP):
            for g in gs:
                row = act_s[SSM_HEADS_PER_GROUP * g + r:SSM_HEADS_PER_GROUP * g + r + 1, :]
                seg = col4[g][:, LANES * r:LANES * (r + 1)] - row
                dec = jnp.where(causal, jnp.exp2(seg), 0.0)
                xp = xdt_b[g][:, LANES * (r // 2):LANES * (r // 2 + 1)]
                ys[g][r] = _dot((cb[g] * dec).astype(BF16), xp)
        for g in gs:
            y_diag = jnp.concatenate([jnp.where(first_half, ys[g][2 * p], ys[g][2 * p + 1])
                                      for p in range(2)], axis=1)
            al_e = a_e[g][L - 1:L, :]
            new = lax.dot_general(b_of(g), (xdt[g] * jnp.exp2(al_e - a_e[g])).astype(BF16),
                                  (((0,), (0,)), ((), ())), preferred_element_type=F32)
            y = y_diag + y_cs[g] * jnp.exp2(a_e[g]) + x_of(g) * dskip_ref[g]
            state[g] = state[g] * jnp.exp2(al_e) + new
            yg = y * _silu(z_ref[:, g * W:(g + 1) * W])
            ms = jnp.mean(yg * yg, axis=-1, keepdims=True)
            o_ref[:, g * W:(g + 1) * W] = (yg * lax.rsqrt(ms + NORM_EPS)
                                           * nw_ref[:, g * W:(g + 1) * W]).astype(o_ref.dtype)

    for g0 in range(0, G, SSD_GROUP_BATCH):
        run_groups(range(g0, g0 + SSD_GROUP_BATCH))


def _head_selectors():
    G, R = SSM_N_GROUPS, SSM_HEADS_PER_GROUP
    h = jnp.arange(LANES)[None, :, None]
    g = jnp.arange(G)[:, None, None]
    j64 = jnp.arange(SSM_GROUP_WIDTH)[None, None, :]
    j128 = jnp.arange(R * LANES)[None, None, :]
    e64 = (h == g * R + j64 // SSM_HEADDIM).astype(BF16)
    e128 = (h == g * R + j128 // LANES).astype(BF16)
    return e64, e128


def _ssd(zx, x2, w_dt, conv_w, conv_b, dt_bias, a_log, d_skip, norm_w, batch, seq, moe_weights, moe_layer):
    L, G, W, N = SSM_CHUNK, SSM_N_GROUPS, SSM_GROUP_WIDTH, SSM_D_STATE
    nc = seq // L
    cast = _CastPlan(*moe_weights, moe_layer, batch * nc, lambda b, c: b * nc + c)
    t = batch * seq
    pad = LANES - SSM_N_HEADS
    wdt = jnp.pad(w_dt, ((0, 0), (0, pad)))
    dtb = jnp.pad(dt_bias, (0, pad))[None, :]
    alog = jnp.pad(a_log, (0, pad))[None, :]
    dsk = jnp.repeat(d_skip, SSM_HEADDIM).reshape(G, 1, W)
    e64, e128 = _head_selectors()
    conv_dim = conv_w.shape[1]
    row = lambda b, c: (b * nc + c, 0)
    full2 = lambda b, c: (0, 0)
    full3 = lambda b, c: (0, 0, 0)
    yg, *w_bf16 = pl.pallas_call(
        functools.partial(_ssd_kernel, cast=cast),
        grid=(batch, nc),
        in_specs=[pl.BlockSpec((L, SSM_D_INNER), row),
                  pl.BlockSpec((L, SSM_D_INNER), lambda b, c: (b * nc + c, 1)),
                  pl.BlockSpec((L, 2 * G * N), lambda b, c: (b * nc + c, 2)),
                  pl.BlockSpec((L, D_MODEL), row),
                  pl.BlockSpec((D_MODEL, LANES), full2),
                  pl.BlockSpec((SSM_CONV_K, conv_dim), full2),
                  pl.BlockSpec((1, conv_dim), full2),
                  pl.BlockSpec((1, LANES), full2),
                  pl.BlockSpec((1, LANES), full2),
                  pl.BlockSpec((G, 1, W), full3),
                  pl.BlockSpec((1, SSM_D_INNER), full2),
                  pl.BlockSpec((G, LANES, W), full3),
                  pl.BlockSpec((G, LANES, SSM_HEADS_PER_GROUP * LANES), full3)] + cast.in_specs,
        out_specs=[pl.BlockSpec((L, SSM_D_INNER), row)] + cast.out_specs,
        out_shape=[jax.ShapeDtypeStruct((t, SSM_D_INNER), BF16)] + cast.out_shape,
        scratch_shapes=[pltpu.VMEM((CONV_TAIL + L, SSM_D_INNER), F32),
                        pltpu.VMEM((CONV_TAIL + L, 2 * G * N), F32),
                        pltpu.VMEM((G, N, W), F32),
                        pltpu.VMEM((3, L, LANES), BF16),
                        pltpu.VMEM((2, L, LANES), BF16),
                        pltpu.VMEM((LANES, L), F32)],
        compiler_params=_arb(2),
        name="ssd_scan",
    )(zx, zx, zx, x2, wdt, conv_w, conv_b[None, :], dtb, alog, dsk, norm_w[None, :], e64, e128, *cast.weights)
    return yg, tuple(w_bf16)


def _route(h, wr):
    tm = h.shape[0]
    logits = _dot(h.astype(BF16), wr)
    lane_i = lax.broadcasted_iota(I32, (tm, LANES), 1)
    lane = lane_i.astype(F32)
    neg = -jnp.inf
    big = float(LANES)

    def first_argmax(v, vmax):
        return jnp.min(jnp.where(v == vmax, lane, big), axis=-1, keepdims=True)

    gl = jnp.where((lane_i >= MOE_N_EXPERTS) & (lane_i < MOE_N_EXPERTS + MOE_GROUPS), logits, neg)
    gm = jnp.max(gl, axis=-1, keepdims=True)
    g_sel = first_argmax(gl, gm) - float(MOE_N_EXPERTS)
    g_gate = 1.0 / jnp.sum(jnp.exp(gl - gm), axis=-1, keepdims=True)
    lo = g_sel * float(MOE_EXPERTS_PER_GROUP)
    el = jnp.where((lane >= lo) & (lane < lo + float(MOE_EXPERTS_PER_GROUP)), logits, neg)
    m1 = jnp.max(el, axis=-1, keepdims=True)
    i1 = first_argmax(el, m1)
    el2 = jnp.where(lane == i1, neg, el)
    m2 = jnp.max(el2, axis=-1, keepdims=True)
    i2 = first_argmax(el2, m2)
    p2 = jnp.exp(m2 - m1)
    t1 = 1.0 / (1.0 + p2)
    t2 = p2 / (1.0 + p2)
    eid = jnp.where(lane_i == 0, i1, jnp.where(lane_i == 1, i2, 0.0)).astype(I32)
    wt = jnp.where(lane_i == 0, g_gate * t1, jnp.where(lane_i == 1, g_gate * t2, 0.0))
    count = jnp.sum(jnp.where(lane == i1, 1.0, 0.0) + jnp.where(lane == i2, 1.0, 0.0), axis=0, keepdims=True)
    return eid, wt, count


def _router_weights(w_group, w_expert):
    pad = LANES - MOE_N_EXPERTS - MOE_GROUPS
    return jnp.pad(jnp.concatenate([w_expert, w_group], axis=1), ((0, 0), (0, pad))).astype(BF16)


def _store_token_tiles(ref, v, first_token=0):
    rows = v.shape[0]
    for s in range(ROW_SLABS):
        ref[pl.ds(first_token * ROW_SLABS + s, rows, stride=ROW_SLABS), :] = v[:, s * LANES:(s + 1) * LANES]


def _load_token_tiles(ref, rows):
    return jnp.concatenate([ref[pl.ds(s, rows, stride=ROW_SLABS), :] for s in range(ROW_SLABS)], axis=1)


def _mm_ln_router_kernel(x_ref, w_ref, r_ref, g_ref, b_ref, wr_ref, ht_ref, eid_ref, wt_ref, cnt_ref, wb_ref):
    @pl.when(pl.program_id(0) == 0)
    def _():
        cnt_ref[...] = jnp.zeros(cnt_ref.shape, F32)
        wb_ref[...] = w_ref[...].astype(BF16)

    sub = x_ref.shape[0] // LN_SUBTILES
    rows = [slice(u * sub, (u + 1) * sub) for u in range(LN_SUBTILES)]
    ys = [DEEPNORM_ALPHA * r_ref[rs, :] + _dot(x_ref[rs, :].astype(BF16), wb_ref[...]) for rs in rows]
    hs = [_layer_norm(y, g_ref[...], b_ref[...]) for y in ys]
    routes = [_route(h, wr_ref[...]) for h in hs]
    for u, rs in enumerate(rows):
        _store_token_tiles(ht_ref, hs[u], first_token=u * sub)
        eid_ref[rs, :] = routes[u][0]
        wt_ref[rs, :] = routes[u][1]
    cnt_ref[...] += jnp.broadcast_to(sum(r[2] for r in routes), cnt_ref.shape)


def _mm_ln_router(x, w, resid, g, b, wr, name):
    m, k = x.shape
    d = w.shape[1]
    tm = min(LN_TM, m)
    row = lambda i: (i, 0)
    full = lambda i: (0, 0)
    return pl.pallas_call(
        _mm_ln_router_kernel,
        grid=(m // tm,),
        in_specs=[pl.BlockSpec((tm, k), row),
                  pl.BlockSpec((k, d), full, pipeline_mode=pl.Buffered(1)),
                  pl.BlockSpec((tm, d), row),
                  pl.BlockSpec((1, d), full), pl.BlockSpec((1, d), full), pl.BlockSpec((d, LANES), full)],
        out_specs=[pl.BlockSpec((tm * ROW_SLABS, LANES), row),
                   pl.BlockSpec((tm, LANES), row), pl.BlockSpec((tm, LANES), row),
                   pl.BlockSpec((8, LANES), full)],
        out_shape=[jax.ShapeDtypeStruct((m * ROW_SLABS, LANES), F32),
                   jax.ShapeDtypeStruct((m, LANES), I32), jax.ShapeDtypeStruct((m, LANES), F32),
                   jax.ShapeDtypeStruct((8, LANES), F32)],
        scratch_shapes=[pltpu.VMEM((k, d), BF16)],
        compiler_params=_arb(1),
        name=name,
    )(x, w, resid, g[None, :], b[None, :], wr)


def _dispatch_kernel(eid_ref, cnt_ref, h_ref, dest_ref, be_ref, nu_ref, xs_hbm,
                     base, upper, zbuf, dst_v, dst_s, pe_v, pe_s, sem_z, sem_r, sem_s, *, blk):
    i = pl.program_id(0)
    tm = eid_ref.shape[0]
    n_e = LANES
    eid_t = eid_ref[...].astype(F32).T
    sub = lax.broadcasted_iota(I32, (n_e, tm), 0).astype(F32)
    oh = [(sub == eid_t[k:k + 1, :]).astype(F32) for k in range(MOE_TOP_K)]
    tot = [jnp.sum(o, axis=1, keepdims=True) for o in oh]

    @pl.when(i == 0)
    def _():
        r_i = lax.broadcasted_iota(I32, (tm, tm), 0)
        c_i = lax.broadcasted_iota(I32, (tm, tm), 1)
        upper[...] = (r_i < c_i).astype(BF16)
        counts = jnp.broadcast_to(cnt_ref[0:1, :], (n_e, n_e)).T
        padded = jnp.floor((counts + float(blk - 1)) * (1.0 / blk)) * float(blk)
        r_i = lax.broadcasted_iota(I32, (n_e, n_e), 0)
        c_i = lax.broadcasted_iota(I32, (n_e, n_e), 1)
        tril = (r_i >= c_i).astype(BF16)
        p1, p2, p3 = _split3(padded)
        pends = _dot(tril, p1) + _dot(tril, p2) + _dot(tril, p3)
        base[...] = pends - padded
        nbp = be_ref.shape[1]
        blk_start = lax.broadcasted_iota(I32, (n_e, nbp), 1).astype(F32) * float(blk)
        is_e = lax.broadcasted_iota(I32, (n_e, nbp), 0) < MOE_N_EXPERTS
        done = jnp.where(is_e & (jnp.tile(pends, (1, nbp // LANES)) <= blk_start), 1.0, 0.0)
        be = jnp.minimum(jnp.sum(done, axis=0, keepdims=True), float(MOE_N_EXPERTS - 1))
        be_ref[...] = jnp.broadcast_to(be, be_ref.shape).astype(I32)
        last = pends[MOE_N_EXPERTS - 1:MOE_N_EXPERTS, :] * (1.0 / blk)
        nu_ref[...] = jnp.broadcast_to(last, nu_ref.shape).astype(I32)
        zbuf[...] = jnp.zeros(zbuf.shape, F32)
        row8 = lax.broadcasted_iota(I32, (8, LANES), 0)
        pe_v[...] = jnp.where(row8 == 0, pends.T[0:8, :], counts.T[0:8, :]).astype(I32)
        cp = pltpu.make_async_copy(pe_v, pe_s, sem_s)
        cp.start()
        cp.wait()

        brows = blk * ROW_SLABS
        prows = ZERO_PIECE * ROW_SLABS
        shift = blk.bit_length() - 1

        def piece_copy(e, p):
            start = pl.multiple_of((pe_s[0, e] - (p + 1) * ZERO_PIECE) * ROW_SLABS, prows)
            return pltpu.make_async_copy(zbuf.at[pl.ds(0, prows)], xs_hbm.at[pl.ds(start, prows)], sem_z)

        def n_pieces(e):
            cnt = pe_s[1, e]
            pad = lax.shift_left(lax.shift_right_logical(cnt + (blk - 1), shift), shift) - cnt
            return lax.shift_right_logical(pad + (ZERO_PIECE - 1), ZERO_PIECE.bit_length() - 1)

        def tail_copy(b):
            return pltpu.make_async_copy(zbuf, xs_hbm.at[pl.ds(pl.multiple_of(b * brows, brows), brows)], sem_z)

        n_used = lax.shift_right_logical(pe_s[0, MOE_N_EXPERTS - 1], shift)
        n_blocks = xs_hbm.shape[0] // brows
        for e in range(MOE_N_EXPERTS):
            lax.fori_loop(0, n_pieces(e), lambda p, c, e=e: (piece_copy(e, p).start(), c)[1], 0)
        lax.fori_loop(n_used, n_blocks, lambda b, c: (tail_copy(b).start(), c)[1], 0)
        for e in range(MOE_N_EXPERTS):
            lax.fori_loop(0, n_pieces(e), lambda p, c, e=e: (piece_copy(e, p).wait(), c)[1], 0)
        lax.fori_loop(n_used, n_blocks, lambda b, c: (tail_copy(b).wait(), c)[1], 0)

    b0 = base[:, 0:1]
    c0 = _dot(oh[0].astype(BF16), upper[...])
    c1 = _dot(oh[1].astype(BF16), upper[...])
    d0 = jnp.sum(oh[0] * (b0 + c0), axis=0, keepdims=True)
    d1 = jnp.sum(oh[1] * (b0 + tot[0] + c1), axis=0, keepdims=True)
    base[...] += jnp.broadcast_to(tot[0] + tot[1], base.shape)
    row8 = lax.broadcasted_iota(I32, (8, tm), 0)
    dst = jnp.where(row8 == 0, d0, jnp.where(row8 == 1, d1, 0.0)).astype(I32)
    dest_ref[...] = dst
    dst_v[...] = dst
    cp = pltpu.make_async_copy(dst_v, dst_s, sem_s)
    cp.start()
    cp.wait()

    for r in range(tm):
        for k in range(MOE_TOP_K):
            slot = pl.multiple_of(dst_s[k, r] * ROW_SLABS, ROW_SLABS)
            pltpu.make_async_copy(h_ref.at[pl.ds(r * ROW_SLABS, ROW_SLABS)],
                                  xs_hbm.at[pl.ds(slot, ROW_SLABS)], sem_r).start(priority=k)
    for k in range(MOE_TOP_K):
        pltpu.make_async_copy(h_ref, xs_hbm.at[pl.ds(0, tm * ROW_SLABS)], sem_r).wait()


def _moe_dispatch(ht, eid, counts, blk, name):
    t = ht.shape[0] // ROW_SLABS
    assert blk & (blk - 1) == 0, "block size must be a power of two"
    tm = min(DSP_TM, t)
    nt = t // tm
    n_blocks = -(-(t * MOE_TOP_K) // blk) + MOE_N_EXPERTS
    nbp = -(-n_blocks // LANES) * LANES
    full = lambda i: (0, 0)
    dest, be, nu, xs = pl.pallas_call(
        functools.partial(_dispatch_kernel, blk=blk),
        grid=(nt,),
        in_specs=[pl.BlockSpec((tm, LANES), lambda i: (i, 0)),
                  pl.BlockSpec((8, LANES), full),
                  pl.BlockSpec((tm * ROW_SLABS, LANES), lambda i: (i, 0))],
        out_specs=[pl.BlockSpec((8, tm), lambda i: (0, i)),
                   pl.BlockSpec((8, nbp), full),
                   pl.BlockSpec((8, LANES), full),
                   pl.BlockSpec(memory_space=pl.ANY)],
        out_shape=[jax.ShapeDtypeStruct((8, t), I32), jax.ShapeDtypeStruct((8, nbp), I32),
                   jax.ShapeDtypeStruct((8, LANES), I32),
                   jax.ShapeDtypeStruct((n_blocks * blk * ROW_SLABS, LANES), F32)],
        scratch_shapes=[pltpu.VMEM((LANES, LANES), F32),
                        pltpu.VMEM((tm, tm), BF16), pltpu.VMEM((blk * ROW_SLABS, LANES), F32),
                        pltpu.VMEM((8, tm), I32), pltpu.SMEM((8, tm), I32),
                        pltpu.VMEM((8, LANES), I32), pltpu.SMEM((8, LANES), I32),
                        pltpu.SemaphoreType.DMA(()), pltpu.SemaphoreType.DMA(()), pltpu.SemaphoreType.DMA(())],
        compiler_params=_arb(1),
        name=name,
    )(eid, counts, ht)
    return dest, be[0, :n_blocks], nu[0, :1], xs, n_blocks


def _ffn_kernel(be_ref, nu_ref, x_ref, wg_ref, wu_ref, wd_ref, o_ref):
    i = pl.program_id(0)

    @pl.when(i < nu_ref[0])
    def _():
        blk = x_ref.shape[0] // ROW_SLABS
        xb = _load_token_tiles(x_ref, blk).astype(BF16)
        hid = _silu(_dot(xb, wg_ref[0])) * _dot(xb, wu_ref[0])
        _store_token_tiles(o_ref, _dot(hid.astype(BF16), wd_ref[0]))

    @pl.when(i >= nu_ref[0])
    def _():
        o_ref[...] = jnp.zeros(o_ref.shape, o_ref.dtype)


def _moe_ffn(xs, block_expert, n_used, n_blocks, weights, blk, name):
    w_gate, w_up, w_down = weights
    d, f = w_gate.shape[1], w_gate.shape[2]
    brows = blk * ROW_SLABS
    used = lambda i, be, nu: (jnp.minimum(i, nu[0] - 1), 0)
    every = lambda i, be, nu: (i, 0)
    expert = lambda i, be, nu: (be[i], 0, 0)
    grid_spec = pltpu.PrefetchScalarGridSpec(
        num_scalar_prefetch=2,
        grid=(n_blocks,),
        in_specs=[pl.BlockSpec((brows, LANES), used),
                  pl.BlockSpec((1, d, f), expert), pl.BlockSpec((1, d, f), expert),
                  pl.BlockSpec((1, f, d), expert)],
        out_specs=pl.BlockSpec((brows, LANES), every))
    return pl.pallas_call(
        _ffn_kernel,
        grid_spec=grid_spec,
        out_shape=jax.ShapeDtypeStruct((n_blocks * brows, LANES), F32),
        compiler_params=_arb(1),
        name=name,
    )(block_expert, n_used, xs, w_gate, w_up, w_down)


def _combine_ln_kernel(dst_ref, dstn_ref, ht_ref, wt_ref, g_ref, b_ref, yb_hbm, o_ref, ybuf, sem):
    i = pl.program_id(0)
    n = pl.num_programs(0)
    tm = o_ref.shape[0]

    def start_gather(dref, slot):
        for r in range(tm):
            for k in range(MOE_TOP_K):
                src = pl.multiple_of(dref[k, r] * ROW_SLABS, ROW_SLABS)
                pltpu.make_async_copy(yb_hbm.at[pl.ds(src, ROW_SLABS)],
                                      ybuf.at[slot, k, pl.ds(r * ROW_SLABS, ROW_SLABS)],
                                      sem.at[slot]).start(priority=k)

    @pl.when(i == 0)
    def _():
        start_gather(dst_ref, 0)

    @pl.when(i + 1 < n)
    def _():
        start_gather(dstn_ref, (i + 1) % 2)

    slot = i % 2
    for k in range(MOE_TOP_K):
        pltpu.make_async_copy(yb_hbm.at[pl.ds(0, tm * ROW_SLABS)], ybuf.at[slot, k], sem.at[slot]).wait()
    wt = wt_ref[...]
    ffn = (wt[:, 0:1] * _load_token_tiles(ybuf.at[slot, 0], tm)
           + wt[:, 1:2] * _load_token_tiles(ybuf.at[slot, 1], tm))
    h = _load_token_tiles(ht_ref, tm)
    o_ref[...] = _layer_norm(DEEPNORM_ALPHA * h + ffn, g_ref[...], b_ref[...])


def _combine_ln(ht, wt, dest, yb, g, b, name):
    t, d = ht.shape[0] // ROW_SLABS, D_MODEL
    tm = min(CMB_TM, t)
    nblk = t // tm
    row = lambda i: (i, 0)
    full = lambda i: (0, 0)
    return pl.pallas_call(
        _combine_ln_kernel,
        grid=(nblk,),
        in_specs=[pl.BlockSpec((8, tm), lambda i: (0, i), memory_space=pltpu.SMEM),
                  pl.BlockSpec((8, tm), lambda i: (0, jnp.minimum(i + 1, nblk - 1)),
                               memory_space=pltpu.SMEM),
                  pl.BlockSpec((tm * ROW_SLABS, LANES), row), pl.BlockSpec((tm, LANES), row),
                  pl.BlockSpec((1, d), full), pl.BlockSpec((1, d), full),
                  pl.BlockSpec(memory_space=pl.ANY)],
        out_specs=pl.BlockSpec((tm, d), row),
        out_shape=jax.ShapeDtypeStruct((t, d), F32),
        scratch_shapes=[pltpu.VMEM((2, MOE_TOP_K, tm * ROW_SLABS, LANES), F32), pltpu.SemaphoreType.DMA((2,))],
        compiler_params=_arb(1),
        name=name,
    )(dest, dest, ht, wt, g[None, :], b[None, :], yb)


def _hier_moe_ln(ht, eid, wt, counts, w_bf16, g, b, layer):
    dest, block_expert, n_used, xs, n_blocks = _moe_dispatch(ht, eid, counts, FFN_BLK, f"moe_dispatch{layer}")
    yb = _moe_ffn(xs, block_expert, n_used, n_blocks, w_bf16, FFN_BLK, f"moe_ffn{layer}")
    return _combine_ln(ht, wt, dest, yb, g, b, f"moe_combine_ln{layer}")


def _qkv_rope_kernel(x_ref, wf_ref, pos_ref, inv_ref, o_ref, w_ref):
    @pl.when(pl.program_id(0) == 0)
    def _():
        w_ref[...] = wf_ref[...].astype(BF16)

    xb = x_ref[...].astype(BF16)
    tm = xb.shape[0]
    n = D_MODEL
    ang = pos_ref[...].astype(F32) * inv_ref[...]
    lane = lax.broadcasted_iota(I32, (tm, LANES), 1)
    dd = lane & (ATTN_HEAD_DIM - 1)
    half = ROT_DIM // 2
    cosv = jnp.cos(ang)
    sinv = jnp.sin(ang)
    c_t = jnp.where(dd < ROT_DIM, cosv, 1.0)
    s_up = jnp.where(dd < half, -sinv, 0.0)
    s_dn = jnp.where((dd >= half) & (dd < ROT_DIM), sinv, 0.0)
    for j, sc in ((0, ATTN_HEAD_DIM ** -0.5 * math.log2(math.e)), (1, 1.0)):
        acc = _dot(xb, w_ref[:, j * n:(j + 1) * n])
        c_j, up_j, dn_j = c_t * sc, s_up * sc, s_dn * sc
        for blk in range(n // LANES):
            tt = acc[:, blk * LANES:(blk + 1) * LANES]
            out = tt * c_j + pltpu.roll(tt, LANES - half, 1) * up_j + pltpu.roll(tt, half, 1) * dn_j
            o_ref[:, j * n + blk * LANES:j * n + (blk + 1) * LANES] = out.astype(o_ref.dtype)
    o_ref[:, 2 * n:3 * n] = _dot(xb, w_ref[:, 2 * n:3 * n]).astype(o_ref.dtype)


def _rope_inv_table():
    inv = ROPE_THETA ** (-jnp.arange(0, ROT_DIM, 2, dtype=F32) / ROT_DIM)
    head = jnp.concatenate([inv, inv, jnp.zeros((ATTN_HEAD_DIM - ROT_DIM,), F32)])
    return jnp.tile(head, LANES // ATTN_HEAD_DIM)[None, :]


def _qkv_rope(h, w_qkv, positions):
    m, k = h.shape
    n = w_qkv.shape[1]
    tm = min(MM_TM, m)
    pos = positions.reshape(m, 1)
    return pl.pallas_call(
        _qkv_rope_kernel,
        grid=(m // tm,),
        in_specs=[pl.BlockSpec((tm, k), lambda i: (i, 0)),
                  pl.BlockSpec((k, n), lambda i: (0, 0), pipeline_mode=pl.Buffered(1)),
                  pl.BlockSpec((tm, 1), lambda i: (i, 0)),
                  pl.BlockSpec((1, LANES), lambda i: (0, 0))],
        out_specs=pl.BlockSpec((tm, n), lambda i: (i, 0)),
        out_shape=jax.ShapeDtypeStruct((m, n), BF16),
        scratch_shapes=[pltpu.VMEM((k, n), BF16)],
        compiler_params=_arb(1),
        name="mm_qkv_rope",
    )(h, w_qkv, pos, _rope_inv_table())


def _attn_kernel(q_ref, k_ref, v_ref, lq1_ref, lk1_ref, lq2_ref, lk2_ref, sw_ref, wg_ref, wu_ref, wd_ref,
                 o_ref, wgb_ref, wub_ref, wdb_ref,
                 vx_ref, m0_ref, m1_ref, acc0_ref, acc1_ref, *, lambda_init, cast):
    m_refs = (m0_ref, m1_ref)
    acc_refs = (acc0_ref, acc1_ref)
    jp = pl.program_id(2)
    n_tiles = ATT_TILES_PER_STEP
    tq = q_ref.shape[0] // n_tiles
    cast.emit(pl.program_id(0) * pl.num_programs(1) + pl.program_id(1),
              (wg_ref, wu_ref, wd_ref), (wgb_ref, wub_ref, wdb_ref), extra=jp == 0)

    @pl.when(jp == 0)
    def _():
        vx_ref[:, 0:LANES] = v_ref[...]
        vx_ref[:, LANES:] = jnp.ones((vx_ref.shape[0], LANES), vx_ref.dtype)

    q = q_ref[...]
    lane = lax.broadcasted_iota(I32, q.shape, 1)
    zero = jnp.zeros((), q.dtype)
    qs = (jnp.where(lane < ATTN_HEAD_DIM, q, zero), jnp.where(lane >= ATTN_HEAD_DIM, q, zero))
    for c in range(2):
        m_refs[c][...] = jnp.full(m_refs[c].shape, -jnp.inf, F32)
        acc_refs[c][...] = jnp.zeros(acc_refs[c].shape, F32)

    def step(off, windows):
        off = pl.multiple_of(off, tq)
        widest = max(w for _, w, _ in windows)
        kb = k_ref[pl.ds(off, widest), :]
        vb = vx_ref[pl.ds(off, widest), :]
        chains = [(slice(r0, r0 + tq), w, d, c) for r0, w, d in windows for c in range(2)]
        ss = [lax.dot_general(qs[c][rows, :], kb[:w, :], (((1,), (1,)), ((), ())), preferred_element_type=F32)
              for rows, w, _, c in chains]
        for n, (_, w, d, _) in enumerate(chains):
            if d is not None:
                row_i = lax.broadcasted_iota(I32, (tq, w), 0)
                col_i = lax.broadcasted_iota(I32, (tq, w), 1)
                ss[n] = jnp.where(row_i + d >= col_i, ss[n], -jnp.inf)
        m_prev = [m_refs[c][rows, :] for rows, _, _, c in chains]
        mn = [jnp.maximum(mp, jnp.max(s, axis=-1, keepdims=True)) for mp, s in zip(m_prev, ss)]
        ps = [jnp.exp2(s - jnp.tile(m, (1, s.shape[1] // LANES))).astype(BF16) for s, m in zip(ss, mn)]
        for n, (rows, w, _, c) in enumerate(chains):
            alpha = jnp.exp2(m_prev[n] - mn[n])
            acc_refs[c][rows, :] = jnp.tile(alpha, (1, 2)) * acc_refs[c][rows, :] + _dot(ps[n], vb[:w, :])
            m_refs[c][rows, :] = mn[n]

    def shared(j, carry):
        for p in range(n_tiles // 2):
            step(j * (2 * tq), [((2 * p) * tq, 2 * tq, None), ((2 * p + 1) * tq, 2 * tq, None)])
        return carry
    lax.fori_loop(0, jp * (n_tiles // 2), shared, 0)
    base = jp * (n_tiles * tq)
    for p in range(n_tiles // 2):
        a, b = 2 * p * tq, (2 * p + 1) * tq
        step(base + a, [(a, tq, 0), (b, 2 * tq, tq)])
        later = [(r * tq, 2 * tq, None) for r in range(2 * p + 2, n_tiles)]
        if later:
            step(base + a, later)

    lam = (jnp.exp(jnp.sum(lq1_ref[...] * lk1_ref[...], axis=-1, keepdims=True))
           - jnp.exp(jnp.sum(lq2_ref[...] * lk2_ref[...], axis=-1, keepdims=True)) + lambda_init)
    a1 = acc0_ref[...]
    a2 = acc1_ref[...]
    o = a1[:, :LANES] / a1[:, LANES:] - lam * (a2[:, :LANES] / a2[:, LANES:])
    o = o * lax.rsqrt(jnp.mean(o * o, axis=-1, keepdims=True) + NORM_EPS)
    o_ref[...] = (o * sw_ref[...] * (1.0 - lambda_init)).astype(o_ref.dtype)


def _diff_attention(qkv, lq1, lk1, lq2, lk2, subln_w, lambda_init, batch, seq, moe_weights, moe_layer):
    t = batch * seq
    tq = ATT_TILES_PER_STEP * min(ATT_TQ, seq // ATT_TILES_PER_STEP)
    nq = seq // tq
    h_n = ATTN_N_HEADS
    vec = lambda b, h, i: (0, 0)
    cast = _CastPlan(*moe_weights, moe_layer, batch * h_n, lambda b, h, i: b * h_n + h)
    o, *w_bf16 = pl.pallas_call(
        functools.partial(_attn_kernel, lambda_init=lambda_init, cast=cast),
        grid=(batch, h_n, nq),
        in_specs=[pl.BlockSpec((tq, LANES), lambda b, h, i: (b * nq + i, h)),
                  pl.BlockSpec((seq, LANES), lambda b, h, i: (b, h_n + h)),
                  pl.BlockSpec((seq, LANES), lambda b, h, i: (b, 2 * h_n + h)),
                  pl.BlockSpec((1, ATTN_HEAD_DIM), vec), pl.BlockSpec((1, ATTN_HEAD_DIM), vec),
                  pl.BlockSpec((1, ATTN_HEAD_DIM), vec), pl.BlockSpec((1, ATTN_HEAD_DIM), vec),
                  pl.BlockSpec((1, ATTN_V_DIM), vec)] + cast.in_specs,
        out_specs=[pl.BlockSpec((tq, LANES), lambda b, h, i: (b * nq + i, h))] + cast.out_specs,
        out_shape=[jax.ShapeDtypeStruct((t, h_n * ATTN_V_DIM), BF16)] + cast.out_shape,
        scratch_shapes=[pltpu.VMEM((seq, 2 * LANES), BF16),
                        pltpu.VMEM((tq, LANES), F32), pltpu.VMEM((tq, LANES), F32),
                        pltpu.VMEM((tq, 2 * LANES), F32), pltpu.VMEM((tq, 2 * LANES), F32)],
        compiler_params=_arb(3),
        name="diff_attn",
    )(qkv, qkv, qkv, lq1[None, :], lk1[None, :], lq2[None, :], lk2[None, :], subln_w[None, :], *cast.weights)
    return o, tuple(w_bf16)


def kernel(x, positions, ln_mix_g, ln_mix_b, ln_ffn_g, ln_ffn_b, ssm_w_in, ssm_conv_w, ssm_conv_b, ssm_dt_bias, ssm_a_log, ssm_d, ssm_norm_w, ssm_w_out, attn_w_qkv, attn_lam_q1, attn_lam_k1, attn_lam_q2, attn_lam_k2, attn_subln_w, attn_w_o, moe_w_group, moe_w_expert, moe_w_gate, moe_w_up, moe_w_down):
    batch, seq, d = x.shape
    t = batch * seq
    h = x.reshape(t, d)

    w_in = ssm_w_in[0].astype(BF16)

    moe_weights = (moe_w_gate, moe_w_up, moe_w_down)
    zx = _matmul(h, w_in, SSM_ZX_DIM, F32)
    yg, moe_w0 = _ssd(zx, h, w_in[:, SSM_ZX_DIM:], ssm_conv_w[0], ssm_conv_b[0], ssm_dt_bias[0],
                      ssm_a_log[0], ssm_d[0], ssm_norm_w[0], batch, seq, moe_weights, 0)
    ht, eid, wt, cnt = _mm_ln_router(yg, ssm_w_out[0], h, ln_mix_g[0], ln_mix_b[0],
                                     _router_weights(moe_w_group[0], moe_w_expert[0]), "mm_ssm_out_ln_router")
    h = _hier_moe_ln(ht, eid, wt, cnt, moe_w0, ln_ffn_g[0], ln_ffn_b[0], 0)

    lambda_init = 0.8 - 0.6 * math.exp(-0.3 * 1)
    qkv = _qkv_rope(h, attn_w_qkv[0], positions)
    o, moe_w1 = _diff_attention(qkv, attn_lam_q1[0], attn_lam_k1[0], attn_lam_q2[0], attn_lam_k2[0],
                                attn_subln_w[0], lambda_init, batch, seq, moe_weights, 1)
    ht, eid, wt, cnt = _mm_ln_router(o, attn_w_o[0], h, ln_mix_g[1], ln_mix_b[1],
                                     _router_weights(moe_w_group[1], moe_w_expert[1]), "mm_attn_out_ln_router")
    h = _hier_moe_ln(ht, eid, wt, cnt, moe_w1, ln_ffn_g[1], ln_ffn_b[1], 1)
    return h.reshape(batch, seq, d)
```

```python
import functools
import math

import jax
import jax.numpy as jnp
from jax import lax
from jax.experimental import pallas as pl
from jax.experimental.pallas import tpu as pltpu

F32 = jnp.float32
BF16 = jnp.bfloat16
I32 = jnp.int32

D_MODEL = 1024
DEPTH = 2
SSM_D_INNER = 2048
SSM_HEADDIM = 64
SSM_N_HEADS = 32
SSM_N_GROUPS = 8
SSM_HEADS_PER_GROUP = 4
SSM_D_STATE = 128
SSM_CONV_K = 4
SSM_CHUNK = 128
SSM_GROUP_WIDTH = SSM_HEADS_PER_GROUP * SSM_HEADDIM
SSM_ZX_DIM = 2 * SSM_D_INNER + 2 * SSM_N_GROUPS * SSM_D_STATE
ATTN_HEAD_DIM = 64
ATTN_N_HEADS = 8
ATTN_V_DIM = 128
ROT_DIM = 16
ROPE_THETA = 500000.0
MOE_GROUPS = 4
MOE_EXPERTS_PER_GROUP = 8
MOE_N_EXPERTS = 32
MOE_TOP_K = 2
MOE_D_FF = 512
DEEPNORM_ALPHA = (2 * DEPTH) ** 0.25
NORM_EPS = 1e-5

LANES = 128
ROW_SLABS = D_MODEL // LANES
CONV_TAIL = 8
CONV_ROWS = 128

MM_TM = 512
MM_TN = 1024
FFN_BLK = 512
DSP_TM = 512
ZERO_PIECE = 64
SSD_GROUP_BATCH = 2
LN_TM = 1024
LN_SUBTILES = 4
CMB_TM = 256
ATT_TQ = 512
ATT_TILES_PER_STEP = 4


def _arb(n):
    return pltpu.CompilerParams(dimension_semantics=("arbitrary",) * n,
                                vmem_limit_bytes=56 * 1024 * 1024)


def _silu(x):
    hx = 0.5 * x
    return hx + hx * jnp.tanh(hx)


def _softplus(x):
    return jnp.maximum(x, 0.0) + jnp.log(1.0 + jnp.exp(-jnp.abs(x)))


def _layer_norm(y, g, b):
    mu = jnp.mean(y, axis=-1, keepdims=True)
    d = y - mu
    var = jnp.mean(d * d, axis=-1, keepdims=True)
    return d * lax.rsqrt(var + NORM_EPS) * g + b


def _split3(a):
    a1 = a.astype(BF16)
    r1 = a - a1.astype(F32)
    a2 = r1.astype(BF16)
    a3 = (r1 - a2.astype(F32)).astype(BF16)
    return a1, a2, a3


def _dot(a, b):
    return jnp.dot(a, b, preferred_element_type=F32)


def _dot_sel(a, sel):
    a1, a2, a3 = _split3(a)
    return _dot(a1, sel) + _dot(a2, sel) + _dot(a3, sel)


def _dot_f32(a, b):
    a1, a2, a3 = _split3(a)
    b1, b2, b3 = _split3(b)
    return (_dot(a1, b1) + _dot(a1, b2) + _dot(a2, b1)
            + _dot(a2, b2) + _dot(a1, b3) + _dot(a3, b1))


class _CastPlan:
    def __init__(self, w_gate, w_up, w_down, layer, steps, step_of):
        self.weights = (w_gate, w_up, w_down)
        self.n_e = w_gate.shape[1]
        self.eps = -(-self.n_e // steps)
        assert self.n_e % self.eps == 0
        self.n_cast = self.n_e // self.eps
        self.stride = steps // self.n_cast
        block = lambda *ids: jnp.minimum(step_of(*ids) // self.stride, self.n_cast - 1)
        self.in_specs = [pl.BlockSpec((1, self.eps) + w.shape[2:], lambda *ids: (layer, block(*ids), 0, 0))
                         for w in self.weights]
        self.out_specs = [pl.BlockSpec((self.eps,) + w.shape[2:], lambda *ids: (block(*ids), 0, 0))
                          for w in self.weights]
        self.out_shape = [jax.ShapeDtypeStruct(w.shape[1:], BF16) for w in self.weights]

    def emit(self, step, w_refs, o_refs, extra=True):
        @pl.when(extra & (step % self.stride == 0) & (step // self.stride < self.n_cast))
        def _():
            for w_ref, o_ref in zip(w_refs, o_refs):
                o_ref[...] = w_ref[0].astype(BF16)


def _mm_kernel(x_ref, w_ref, o_ref):
    xb = x_ref[...].astype(BF16)
    for j in range(o_ref.shape[1] // MM_TN):
        cols = slice(j * MM_TN, (j + 1) * MM_TN)
        o_ref[:, cols] = _dot(xb, w_ref[:, cols]).astype(o_ref.dtype)


def _matmul(x, w, n, out_dtype):
    m, k = x.shape
    tm = min(MM_TM, m)
    return pl.pallas_call(
        _mm_kernel,
        grid=(m // tm,),
        in_specs=[pl.BlockSpec((tm, k), lambda i: (i, 0)),
                  pl.BlockSpec((k, n), lambda i: (0, 0), pipeline_mode=pl.Buffered(1))],
        out_specs=pl.BlockSpec((tm, n), lambda i: (i, 0)),
        out_shape=jax.ShapeDtypeStruct((m, n), out_dtype),
        compiler_params=_arb(1),
        name="mm_inproj",
    )(x, w)


def _ssd_kernel(z_ref, xs_ref, bc_ref, x_ref, wdt_ref, cw_ref, cb_ref, dtb_ref, alog_ref, dskip_ref,
                nw_ref, e64_ref, e128_ref, wg_ref, wu_ref, wd_ref, o_ref, wgb_ref, wub_ref, wdb_ref,
                stage_x, stage_bc, state, acp_s, dtp_s, act_s, *, cast):
    G, W, N = SSM_N_GROUPS, SSM_GROUP_WIDTH, SSM_D_STATE
    L = z_ref.shape[0]
    c = pl.program_id(1)
    cast.emit(pl.program_id(0) * pl.num_programs(1) + c, (wg_ref, wu_ref, wd_ref), (wgb_ref, wub_ref, wdb_ref))

    @pl.when(c == 0)
    def _():
        state[...] = jnp.zeros(state.shape, F32)
        for stage in (stage_x, stage_bc):
            stage[0:CONV_TAIL, :] = jnp.zeros((CONV_TAIL, stage.shape[1]), F32)

    @pl.when(c != 0)
    def _():
        for stage in (stage_x, stage_bc):
            stage[0:CONV_TAIL, :] = stage[L:L + CONV_TAIL, :]

    stage_x[CONV_TAIL:CONV_TAIL + L, :] = xs_ref[...]
    stage_bc[CONV_TAIL:CONV_TAIL + L, :] = bc_ref[...]

    def conv(stage, c0, width, w0):
        taps = [cw_ref[k:k + 1, w0:w0 + width] for k in range(SSM_CONV_K)]
        bias = cb_ref[:, w0:w0 + width]
        outs = []
        for rb in range(L // CONV_ROWS):
            r0 = rb * CONV_ROWS
            u = stage[r0:r0 + CONV_ROWS + CONV_TAIL, c0:c0 + width]
            out = bias + taps[SSM_CONV_K - 1] * u[CONV_TAIL:, :]
            for k in range(SSM_CONV_K - 1):
                back = SSM_CONV_K - 1 - k
                out = out + taps[k] * pltpu.roll(u, back, 0)[CONV_TAIL:, :]
            outs.append(_silu(out))
        return jnp.concatenate(outs, axis=0)

    xc, bcv, ccv = {}, {}, {}

    def conv_groups(gs):
        for g in gs:
            xc[g] = conv(stage_x, g * W, W, g * W)
            bcv[g] = conv(stage_bc, g * N, N, SSM_D_INNER + g * N).astype(BF16)
            ccv[g] = conv(stage_bc, G * N + g * N, N, SSM_D_INNER + G * N + g * N).astype(BF16)

    x_of, b_of, c_of = xc.__getitem__, bcv.__getitem__, ccv.__getitem__

    dt = _softplus(_dot(x_ref[...].astype(BF16), wdt_ref[...].astype(BF16)) + dtb_ref[...])
    a = -jnp.exp(alog_ref[...])
    row_i = lax.broadcasted_iota(I32, (L, L), 0)
    col_i = lax.broadcasted_iota(I32, (L, L), 1)
    causal = row_i >= col_i
    a1, a2, a3 = _split3(dt * a)
    tri = causal.astype(BF16)
    ac = (_dot(tri, a1) + _dot(tri, a2) + _dot(tri, a3)) * math.log2(math.e)
    act_s[...] = ac.T
    for n, part in enumerate(_split3(ac)):
        acp_s[n] = part
    for n, part in enumerate(_split3(dt)[:2]):
        dtp_s[n] = part
    lane = lax.broadcasted_iota(I32, (L, LANES), 1)
    first_half = lane < SSM_HEADDIM

    def select(parts_ref, n_parts, sel):
        out = _dot(parts_ref[0], sel)
        for n in range(1, n_parts):
            out = out + _dot(parts_ref[n], sel)
        return out

    def run_groups(gs):
        conv_groups(gs)
        col4 = {g: select(acp_s, 3, e128_ref[g]) for g in gs}
        dt_e = {g: select(dtp_s, 2, e64_ref[g]) for g in gs}
        cb = {g: lax.dot_general(c_of(g), b_of(g), (((1,), (1,)), ((), ())), preferred_element_type=F32)
              for g in gs}
        y_cs = {g: _dot(c_of(g), state[g].astype(BF16)) for g in gs}
        a_e = {g: jnp.concatenate(
            [jnp.where(first_half, col4[g][:, 2 * p * LANES:(2 * p + 1) * LANES],
                       col4[g][:, (2 * p + 1) * LANES:(2 * p + 2) * LANES]) for p in range(2)], axis=1)
            for g in gs}
        xdt = {g: x_of(g) * dt_e[g] for g in gs}
        xdt_b = {g: xdt[g].astype(BF16) for g in gs}
        ys = {g: [None] * SSM_HEADS_PER_GROUP for g in gs}
        for r in range(SSM_HEADS_PER_GROUP):
            for g in gs:
                row = act_s[SSM_HEADS_PER_GROUP * g + r:SSM_HEADS_PER_GROUP * g + r + 1, :]
                seg = col4[g][:, LANES * r:LANES * (r + 1)] - row
                dec = jnp.where(causal, jnp.exp2(seg), 0.0)
                xp = xdt_b[g][:, LANES * (r // 2):LANES * (r // 2 + 1)]
                ys[g][r] = _dot((cb[g] * dec).astype(BF16), xp)
        for g in gs:
            y_diag = jnp.concatenate([jnp.where(first_half, ys[g][2 * p], ys[g][2 * p + 1])
                                      for p in range(2)], axis=1)
            al_e = a_e[g][L - 1:L, :]
            new = lax.dot_general(b_of(g), (xdt[g] * jnp.exp2(al_e - a_e[g])).astype(BF16),
                                  (((0,), (0,)), ((), ())), preferred_element_type=F32)
            y = y_diag + y_cs[g] * jnp.exp2(a_e[g]) + x_of(g) * dskip_ref[g]
            state[g] = state[g] * jnp.exp2(al_e) + new
            yg = y * _silu(z_ref[:, g * W:(g + 1) * W])
            ms = jnp.mean(yg * yg, axis=-1, keepdims=True)
            o_ref[:, g * W:(g + 1) * W] = (yg * lax.rsqrt(ms + NORM_EPS)
                                           * nw_ref[:, g * W:(g + 1) * W]).astype(o_ref.dtype)

    for g0 in range(0, G, SSD_GROUP_BATCH):
        run_groups(range(g0, g0 + SSD_GROUP_BATCH))


def _head_selectors():
    G, R = SSM_N_GROUPS, SSM_HEADS_PER_GROUP
    h = jnp.arange(LANES)[None, :, None]
    g = jnp.arange(G)[:, None, None]
    j64 = jnp.arange(SSM_GROUP_WIDTH)[None, None, :]
    j128 = jnp.arange(R * LANES)[None, None, :]
    e64 = (h == g * R + j64 // SSM_HEADDIM).astype(BF16)
    e128 = (h == g * R + j128 // LANES).astype(BF16)
    return e64, e128


def _ssd(zx, x2, w_dt, conv_w, conv_b, dt_bias, a_log, d_skip, norm_w, batch, seq, moe_weights, moe_layer):
    L, G, W, N = SSM_CHUNK, SSM_N_GROUPS, SSM_GROUP_WIDTH, SSM_D_STATE
    nc = seq // L
    cast = _CastPlan(*moe_weights, moe_layer, batch * nc, lambda b, c: b * nc + c)
    t = batch * seq
    pad = LANES - SSM_N_HEADS
    wdt = jnp.pad(w_dt, ((0, 0), (0, pad)))
    dtb = jnp.pad(dt_bias, (0, pad))[None, :]
    alog = jnp.pad(a_log, (0, pad))[None, :]
    dsk = jnp.repeat(d_skip, SSM_HEADDIM).reshape(G, 1, W)
    e64, e128 = _head_selectors()
    conv_dim = conv_w.shape[1]
    row = lambda b, c: (b * nc + c, 0)
    full2 = lambda b, c: (0, 0)
    full3 = lambda b, c: (0, 0, 0)
    yg, *w_bf16 = pl.pallas_call(
        functools.partial(_ssd_kernel, cast=cast),
        grid=(batch, nc),
        in_specs=[pl.BlockSpec((L, SSM_D_INNER), row),
                  pl.BlockSpec((L, SSM_D_INNER), lambda b, c: (b * nc + c, 1)),
                  pl.BlockSpec((L, 2 * G * N), lambda b, c: (b * nc + c, 2)),
                  pl.BlockSpec((L, D_MODEL), row),
                  pl.BlockSpec((D_MODEL, LANES), full2),
                  pl.BlockSpec((SSM_CONV_K, conv_dim), full2),
                  pl.BlockSpec((1, conv_dim), full2),
                  pl.BlockSpec((1, LANES), full2),
                  pl.BlockSpec((1, LANES), full2),
                  pl.BlockSpec((G, 1, W), full3),
                  pl.BlockSpec((1, SSM_D_INNER), full2),
                  pl.BlockSpec((G, LANES, W), full3),
                  pl.BlockSpec((G, LANES, SSM_HEADS_PER_GROUP * LANES), full3)] + cast.in_specs,
        out_specs=[pl.BlockSpec((L, SSM_D_INNER), row)] + cast.out_specs,
        out_shape=[jax.ShapeDtypeStruct((t, SSM_D_INNER), BF16)] + cast.out_shape,
        scratch_shapes=[pltpu.VMEM((CONV_TAIL + L, SSM_D_INNER), F32),
                        pltpu.VMEM((CONV_TAIL + L, 2 * G * N), F32),
                        pltpu.VMEM((G, N, W), F32),
                        pltpu.VMEM((3, L, LANES), BF16),
                        pltpu.VMEM((2, L, LANES), BF16),
                        pltpu.VMEM((LANES, L), F32)],
        compiler_params=_arb(2),
        name="ssd_scan",
    )(zx, zx, zx, x2, wdt, conv_w, conv_b[None, :], dtb, alog, dsk, norm_w[None, :], e64, e128, *cast.weights)
    return yg, tuple(w_bf16)


def _route(h, wr):
    tm = h.shape[0]
    logits = _dot(h.astype(BF16), wr)
    lane_i = lax.broadcasted_iota(I32, (tm, LANES), 1)
    lane = lane_i.astype(F32)
    neg = -jnp.inf
    big = float(LANES)

    def first_argmax(v, vmax):
        return jnp.min(jnp.where(v == vmax, lane, big), axis=-1, keepdims=True)

    gl = jnp.where((lane_i >= MOE_N_EXPERTS) & (lane_i < MOE_N_EXPERTS + MOE_GROUPS), logits, neg)
    gm = jnp.max(gl, axis=-1, keepdims=True)
    g_sel = first_argmax(gl, gm) - float(MOE_N_EXPERTS)
    g_gate = 1.0 / jnp.sum(jnp.exp(gl - gm), axis=-1, keepdims=True)
    lo = g_sel * float(MOE_EXPERTS_PER_GROUP)
    el = jnp.where((lane >= lo) & (lane < lo + float(MOE_EXPERTS_PER_GROUP)), logits, neg)
    m1 = jnp.max(el, axis=-1, keepdims=True)
    i1 = first_argmax(el, m1)
    el2 = jnp.where(lane == i1, neg, el)
    m2 = jnp.max(el2, axis=-1, keepdims=True)
    i2 = first_argmax(el2, m2)
    p2 = jnp.exp(m2 - m1)
    t1 = 1.0 / (1.0 + p2)
    t2 = p2 / (1.0 + p2)
    eid = jnp.where(lane_i == 0, i1, jnp.where(lane_i == 1, i2, 0.0)).astype(I32)
    wt = jnp.where(lane_i == 0, g_gate * t1, jnp.where(lane_i == 1, g_gate * t2, 0.0))
    count = jnp.sum(jnp.where(lane == i1, 1.0, 0.0) + jnp.where(lane == i2, 1.0, 0.0), axis=0, keepdims=True)
    return eid, wt, count


def _router_weights(w_group, w_expert):
    pad = LANES - MOE_N_EXPERTS - MOE_GROUPS
    return jnp.pad(jnp.concatenate([w_expert, w_group], axis=1), ((0, 0), (0, pad))).astype(BF16)


def _store_token_tiles(ref, v, first_token=0):
    rows = v.shape[0]
    for s in range(ROW_SLABS):
        ref[pl.ds(first_token * ROW_SLABS + s, rows, stride=ROW_SLABS), :] = v[:, s * LANES:(s + 1) * LANES]


def _load_token_tiles(ref, rows):
    return jnp.concatenate([ref[pl.ds(s, rows, stride=ROW_SLABS), :] for s in range(ROW_SLABS)], axis=1)


def _mm_ln_router_kernel(x_ref, w_ref, r_ref, g_ref, b_ref, wr_ref, ht_ref, eid_ref, wt_ref, cnt_ref, wb_ref):
    @pl.when(pl.program_id(0) == 0)
    def _():
        cnt_ref[...] = jnp.zeros(cnt_ref.shape, F32)
        wb_ref[...] = w_ref[...].astype(BF16)

    sub = x_ref.shape[0] // LN_SUBTILES
    rows = [slice(u * sub, (u + 1) * sub) for u in range(LN_SUBTILES)]
    ys = [DEEPNORM_ALPHA * r_ref[rs, :] + _dot(x_ref[rs, :].astype(BF16), wb_ref[...]) for rs in rows]
    hs = [_layer_norm(y, g_ref[...], b_ref[...]) for y in ys]
    routes = [_route(h, wr_ref[...]) for h in hs]
    for u, rs in enumerate(rows):
        _store_token_tiles(ht_ref, hs[u], first_token=u * sub)
        eid_ref[rs, :] = routes[u][0]
        wt_ref[rs, :] = routes[u][1]
    cnt_ref[...] += jnp.broadcast_to(sum(r[2] for r in routes), cnt_ref.shape)


def _mm_ln_router(x, w, resid, g, b, wr, name):
    m, k = x.shape
    d = w.shape[1]
    tm = min(LN_TM, m)
    row = lambda i: (i, 0)
    full = lambda i: (0, 0)
    return pl.pallas_call(
        _mm_ln_router_kernel,
        grid=(m // tm,),
        in_specs=[pl.BlockSpec((tm, k), row),
                  pl.BlockSpec((k, d), full, pipeline_mode=pl.Buffered(1)),
                  pl.BlockSpec((tm, d), row),
                  pl.BlockSpec((1, d), full), pl.BlockSpec((1, d), full), pl.BlockSpec((d, LANES), full)],
        out_specs=[pl.BlockSpec((tm * ROW_SLABS, LANES), row),
                   pl.BlockSpec((tm, LANES), row), pl.BlockSpec((tm, LANES), row),
                   pl.BlockSpec((8, LANES), full)],
        out_shape=[jax.ShapeDtypeStruct((m * ROW_SLABS, LANES), F32),
                   jax.ShapeDtypeStruct((m, LANES), I32), jax.ShapeDtypeStruct((m, LANES), F32),
                   jax.ShapeDtypeStruct((8, LANES), F32)],
        scratch_shapes=[pltpu.VMEM((k, d), BF16)],
        compiler_params=_arb(1),
        name=name,
    )(x, w, resid, g[None, :], b[None, :], wr)


def _dispatch_kernel(eid_ref, cnt_ref, h_ref, dest_ref, be_ref, nu_ref, xs_hbm,
                     base, upper, zbuf, dst_v, dst_s, pe_v, pe_s, sem_z, sem_r, sem_s, *, blk):
    i = pl.program_id(0)
    tm = eid_ref.shape[0]
    n_e = LANES
    eid_t = eid_ref[...].astype(F32).T
    sub = lax.broadcasted_iota(I32, (n_e, tm), 0).astype(F32)
    oh = [(sub == eid_t[k:k + 1, :]).astype(F32) for k in range(MOE_TOP_K)]
    tot = [jnp.sum(o, axis=1, keepdims=True) for o in oh]

    @pl.when(i == 0)
    def _():
        r_i = lax.broadcasted_iota(I32, (tm, tm), 0)
        c_i = lax.broadcasted_iota(I32, (tm, tm), 1)
        upper[...] = (r_i < c_i).astype(BF16)
        counts = jnp.broadcast_to(cnt_ref[0:1, :], (n_e, n_e)).T
        padded = jnp.floor((counts + float(blk - 1)) * (1.0 / blk)) * float(blk)
        r_i = lax.broadcasted_iota(I32, (n_e, n_e), 0)
        c_i = lax.broadcasted_iota(I32, (n_e, n_e), 1)
        tril = (r_i >= c_i).astype(BF16)
        p1, p2, p3 = _split3(padded)
        pends = _dot(tril, p1) + _dot(tril, p2) + _dot(tril, p3)
        base[...] = pends - padded
        nbp = be_ref.shape[1]
        blk_start = lax.broadcasted_iota(I32, (n_e, nbp), 1).astype(F32) * float(blk)
        is_e = lax.broadcasted_iota(I32, (n_e, nbp), 0) < MOE_N_EXPERTS
        done = jnp.where(is_e & (jnp.tile(pends, (1, nbp // LANES)) <= blk_start), 1.0, 0.0)
        be = jnp.minimum(jnp.sum(done, axis=0, keepdims=True), float(MOE_N_EXPERTS - 1))
        be_ref[...] = jnp.broadcast_to(be, be_ref.shape).astype(I32)
        last = pends[MOE_N_EXPERTS - 1:MOE_N_EXPERTS, :] * (1.0 / blk)
        nu_ref[...] = jnp.broadcast_to(last, nu_ref.shape).astype(I32)
        zbuf[...] = jnp.zeros(zbuf.shape, F32)
        row8 = lax.broadcasted_iota(I32, (8, LANES), 0)
        pe_v[...] = jnp.where(row8 == 0, pends.T[0:8, :], counts.T[0:8, :]).astype(I32)
        cp = pltpu.make_async_copy(pe_v, pe_s, sem_s)
        cp.start()
        cp.wait()

        brows = blk * ROW_SLABS
        prows = ZERO_PIECE * ROW_SLABS
        shift = blk.bit_length() - 1

        def piece_copy(e, p):
            start = pl.multiple_of((pe_s[0, e] - (p + 1) * ZERO_PIECE) * ROW_SLABS, prows)
            return pltpu.make_async_copy(zbuf.at[pl.ds(0, prows)], xs_hbm.at[pl.ds(start, prows)], sem_z)

        def n_pieces(e):
            cnt = pe_s[1, e]
            pad = lax.shift_left(lax.shift_right_logical(cnt + (blk - 1), shift), shift) - cnt
            return lax.shift_right_logical(pad + (ZERO_PIECE - 1), ZERO_PIECE.bit_length() - 1)

        def tail_copy(b):
            return pltpu.make_async_copy(zbuf, xs_hbm.at[pl.ds(pl.multiple_of(b * brows, brows), brows)], sem_z)

        n_used = lax.shift_right_logical(pe_s[0, MOE_N_EXPERTS - 1], shift)
        n_blocks = xs_hbm.shape[0] // brows
        for e in range(MOE_N_EXPERTS):
            lax.fori_loop(0, n_pieces(e), lambda p, c, e=e: (piece_copy(e, p).start(), c)[1], 0)
        lax.fori_loop(n_used, n_blocks, lambda b, c: (tail_copy(b).start(), c)[1], 0)
        for e in range(MOE_N_EXPERTS):
            lax.fori_loop(0, n_pieces(e), lambda p, c, e=e: (piece_copy(e, p).wait(), c)[1], 0)
        lax.fori_loop(n_used, n_blocks, lambda b, c: (tail_copy(b).wait(), c)[1], 0)

    b0 = base[:, 0:1]
    c0 = _dot(oh[0].astype(BF16), upper[...])
    c1 = _dot(oh[1].astype(BF16), upper[...])
    d0 = jnp.sum(oh[0] * (b0 + c0), axis=0, keepdims=True)
    d1 = jnp.sum(oh[1] * (b0 + tot[0] + c1), axis=0, keepdims=True)
    base[...] += jnp.broadcast_to(tot[0] + tot[1], base.shape)
    row8 = lax.broadcasted_iota(I32, (8, tm), 0)
    dst = jnp.where(row8 == 0, d0, jnp.where(row8 == 1, d1, 0.0)).astype(I32)
    dest_ref[...] = dst
    dst_v[...] = dst
    cp = pltpu.make_async_copy(dst_v, dst_s, sem_s)
    cp.start()
    cp.wait()

    for r in range(tm):
        for k in range(MOE_TOP_K):
            slot = pl.multiple_of(dst_s[k, r] * ROW_SLABS, ROW_SLABS)
            pltpu.make_async_copy(h_ref.at[pl.ds(r * ROW_SLABS, ROW_SLABS)],
                                  xs_hbm.at[pl.ds(slot, ROW_SLABS)], sem_r).start(priority=k)
    for k in range(MOE_TOP_K):
        pltpu.make_async_copy(h_ref, xs_hbm.at[pl.ds(0, tm * ROW_SLABS)], sem_r).wait()


def _moe_dispatch(ht, eid, counts, blk, name):
    t = ht.shape[0] // ROW_SLABS
    assert blk & (blk - 1) == 0, "block size must be a power of two"
    tm = min(DSP_TM, t)
    nt = t // tm
    n_blocks = -(-(t * MOE_TOP_K) // blk) + MOE_N_EXPERTS
    nbp = -(-n_blocks // LANES) * LANES
    full = lambda i: (0, 0)
    dest, be, nu, xs = pl.pallas_call(
        functools.partial(_dispatch_kernel, blk=blk),
        grid=(nt,),
        in_specs=[pl.BlockSpec((tm, LANES), lambda i: (i, 0)),
                  pl.BlockSpec((8, LANES), full),
                  pl.BlockSpec((tm * ROW_SLABS, LANES), lambda i: (i, 0))],
        out_specs=[pl.BlockSpec((8, tm), lambda i: (0, i)),
                   pl.BlockSpec((8, nbp), full),
                   pl.BlockSpec((8, LANES), full),
                   pl.BlockSpec(memory_space=pl.ANY)],
        out_shape=[jax.ShapeDtypeStruct((8, t), I32), jax.ShapeDtypeStruct((8, nbp), I32),
                   jax.ShapeDtypeStruct((8, LANES), I32),
                   jax.ShapeDtypeStruct((n_blocks * blk * ROW_SLABS, LANES), F32)],
        scratch_shapes=[pltpu.VMEM((LANES, LANES), F32),
                        pltpu.VMEM((tm, tm), BF16), pltpu.VMEM((blk * ROW_SLABS, LANES), F32),
                        pltpu.VMEM((8, tm), I32), pltpu.SMEM((8, tm), I32),
                        pltpu.VMEM((8, LANES), I32), pltpu.SMEM((8, LANES), I32),
                        pltpu.SemaphoreType.DMA(()), pltpu.SemaphoreType.DMA(()), pltpu.SemaphoreType.DMA(())],
        compiler_params=_arb(1),
        name=name,
    )(eid, counts, ht)
    return dest, be[0, :n_blocks], nu[0, :1], xs, n_blocks


def _ffn_kernel(be_ref, nu_ref, x_ref, wg_ref, wu_ref, wd_ref, o_ref):
    i = pl.program_id(0)

    @pl.when(i < nu_ref[0])
    def _():
        blk = x_ref.shape[0] // ROW_SLABS
        xb = _load_token_tiles(x_ref, blk).astype(BF16)
        hid = _silu(_dot(xb, wg_ref[0])) * _dot(xb, wu_ref[0])
        _store_token_tiles(o_ref, _dot(hid.astype(BF16), wd_ref[0]))

    @pl.when(i >= nu_ref[0])
    def _():
        o_ref[...] = jnp.zeros(o_ref.shape, o_ref.dtype)


def _moe_ffn(xs, block_expert, n_used, n_blocks, weights, blk, name):
    w_gate, w_up, w_down = weights
    d, f = w_gate.shape[1], w_gate.shape[2]
    brows = blk * ROW_SLABS
    used = lambda i, be, nu: (jnp.minimum(i, nu[0] - 1), 0)
    every = lambda i, be, nu: (i, 0)
    expert = lambda i, be, nu: (be[i], 0, 0)
    grid_spec = pltpu.PrefetchScalarGridSpec(
        num_scalar_prefetch=2,
        grid=(n_blocks,),
        in_specs=[pl.BlockSpec((brows, LANES), used),
                  pl.BlockSpec((1, d, f), expert), pl.BlockSpec((1, d, f), expert),
                  pl.BlockSpec((1, f, d), expert)],
        out_specs=pl.BlockSpec((brows, LANES), every))
    return pl.pallas_call(
        _ffn_kernel,
        grid_spec=grid_spec,
        out_shape=jax.ShapeDtypeStruct((n_blocks * brows, LANES), F32),
        compiler_params=_arb(1),
        name=name,
    )(block_expert, n_used, xs, w_gate, w_up, w_down)


def _combine_ln_kernel(dst_ref, dstn_ref, ht_ref, wt_ref, g_ref, b_ref, yb_hbm, o_ref, ybuf, sem):
    i = pl.program_id(0)
    n = pl.num_programs(0)
    tm = o_ref.shape[0]

    def start_gather(dref, slot):
        for r in range(tm):
            for k in range(MOE_TOP_K):
                src = pl.multiple_of(dref[k, r] * ROW_SLABS, ROW_SLABS)
                pltpu.make_async_copy(yb_hbm.at[pl.ds(src, ROW_SLABS)],
                                      ybuf.at[slot, k, pl.ds(r * ROW_SLABS, ROW_SLABS)],
                                      sem.at[slot]).start(priority=k)

    @pl.when(i == 0)
    def _():
        start_gather(dst_ref, 0)

    @pl.when(i + 1 < n)
    def _():
        start_gather(dstn_ref, (i + 1) % 2)

    slot = i % 2
    for k in range(MOE_TOP_K):
        pltpu.make_async_copy(yb_hbm.at[pl.ds(0, tm * ROW_SLABS)], ybuf.at[slot, k], sem.at[slot]).wait()
    wt = wt_ref[...]
    ffn = (wt[:, 0:1] * _load_token_tiles(ybuf.at[slot, 0], tm)
           + wt[:, 1:2] * _load_token_tiles(ybuf.at[slot, 1], tm))
    h = _load_token_tiles(ht_ref, tm)
    o_ref[...] = _layer_norm(DEEPNORM_ALPHA * h + ffn, g_ref[...], b_ref[...])


def _combine_ln(ht, wt, dest, yb, g, b, name):
    t, d = ht.shape[0] // ROW_SLABS, D_MODEL
    tm = min(CMB_TM, t)
    nblk = t // tm
    row = lambda i: (i, 0)
    full = lambda i: (0, 0)
    return pl.pallas_call(
        _combine_ln_kernel,
        grid=(nblk,),
        in_specs=[pl.BlockSpec((8, tm), lambda i: (0, i), memory_space=pltpu.SMEM),
                  pl.BlockSpec((8, tm), lambda i: (0, jnp.minimum(i + 1, nblk - 1)),
                               memory_space=pltpu.SMEM),
                  pl.BlockSpec((tm * ROW_SLABS, LANES), row), pl.BlockSpec((tm, LANES), row),
                  pl.BlockSpec((1, d), full), pl.BlockSpec((1, d), full),
                  pl.BlockSpec(memory_space=pl.ANY)],
        out_specs=pl.BlockSpec((tm, d), row),
        out_shape=jax.ShapeDtypeStruct((t, d), F32),
        scratch_shapes=[pltpu.VMEM((2, MOE_TOP_K, tm * ROW_SLABS, LANES), F32), pltpu.SemaphoreType.DMA((2,))],
        compiler_params=_arb(1),
        name=name,
    )(dest, dest, ht, wt, g[None, :], b[None, :], yb)


def _combine_ln_qkv_kernel(dst_ref, dstn_ref, ht_ref, wt_ref, g_ref, b_ref, wf_ref, pos_ref, inv_ref, yb_hbm,
                           o_ref, qkv_ref, ybuf, sem, w_ref):
    @pl.when(pl.program_id(0) == 0)
    def _():
        w_ref[...] = wf_ref[...].astype(BF16)

    _combine_ln_kernel(dst_ref, dstn_ref, ht_ref, wt_ref, g_ref, b_ref, yb_hbm, o_ref, ybuf, sem)
    _project_qkv(o_ref[...], pos_ref, inv_ref, w_ref, qkv_ref)


def _combine_ln_qkv(ht, wt, dest, yb, g, b, w_qkv, positions, name):
    t, d = ht.shape[0] // ROW_SLABS, D_MODEL
    n = w_qkv.shape[1]
    tm = min(CMB_TM, t)
    nblk = t // tm
    row = lambda i: (i, 0)
    full = lambda i: (0, 0)
    return pl.pallas_call(
        _combine_ln_qkv_kernel,
        grid=(nblk,),
        in_specs=[pl.BlockSpec((8, tm), lambda i: (0, i), memory_space=pltpu.SMEM),
                  pl.BlockSpec((8, tm), lambda i: (0, jnp.minimum(i + 1, nblk - 1)),
                               memory_space=pltpu.SMEM),
                  pl.BlockSpec((tm * ROW_SLABS, LANES), row), pl.BlockSpec((tm, LANES), row),
                  pl.BlockSpec((1, d), full), pl.BlockSpec((1, d), full),
                  pl.BlockSpec((d, n), full, pipeline_mode=pl.Buffered(1)),
                  pl.BlockSpec((tm, 1), row), pl.BlockSpec((1, LANES), full),
                  pl.BlockSpec(memory_space=pl.ANY)],
        out_specs=[pl.BlockSpec((tm, d), row), pl.BlockSpec((tm, n), row)],
        out_shape=[jax.ShapeDtypeStruct((t, d), F32), jax.ShapeDtypeStruct((t, n), BF16)],
        scratch_shapes=[pltpu.VMEM((2, MOE_TOP_K, tm * ROW_SLABS, LANES), F32), pltpu.SemaphoreType.DMA((2,)),
                        pltpu.VMEM((d, n), BF16)],
        compiler_params=_arb(1),
        name=name,
    )(dest, dest, ht, wt, g[None, :], b[None, :], w_qkv, positions.reshape(t, 1), _rope_inv_table(), yb)


def _hier_moe_ln(ht, eid, wt, counts, w_bf16, g, b, layer, qkv=None):
    dest, block_expert, n_used, xs, n_blocks = _moe_dispatch(ht, eid, counts, FFN_BLK, f"moe_dispatch{layer}")
    yb = _moe_ffn(xs, block_expert, n_used, n_blocks, w_bf16, FFN_BLK, f"moe_ffn{layer}")
    if qkv is not None:
        return _combine_ln_qkv(ht, wt, dest, yb, g, b, *qkv, f"moe_combine_ln_qkv{layer}")
    return _combine_ln(ht, wt, dest, yb, g, b, f"moe_combine_ln{layer}")


def _qkv_rope_kernel(x_ref, wf_ref, pos_ref, inv_ref, o_ref, w_ref):
    @pl.when(pl.program_id(0) == 0)
    def _():
        w_ref[...] = wf_ref[...].astype(BF16)

    _project_qkv(x_ref[...], pos_ref, inv_ref, w_ref, o_ref)


def _project_qkv(x, pos_ref, inv_ref, w_ref, o_ref):
    xb = x.astype(BF16)
    tm = xb.shape[0]
    n = D_MODEL
    ang = pos_ref[...].astype(F32) * inv_ref[...]
    lane = lax.broadcasted_iota(I32, (tm, LANES), 1)
    dd = lane & (ATTN_HEAD_DIM - 1)
    half = ROT_DIM // 2
    cosv = jnp.cos(ang)
    sinv = jnp.sin(ang)
    c_t = jnp.where(dd < ROT_DIM, cosv, 1.0)
    s_up = jnp.where(dd < half, -sinv, 0.0)
    s_dn = jnp.where((dd >= half) & (dd < ROT_DIM), sinv, 0.0)
    for j, sc in ((0, ATTN_HEAD_DIM ** -0.5 * math.log2(math.e)), (1, 1.0)):
        acc = _dot(xb, w_ref[:, j * n:(j + 1) * n])
        c_j, up_j, dn_j = c_t * sc, s_up * sc, s_dn * sc
        for blk in range(n // LANES):
            tt = acc[:, blk * LANES:(blk + 1) * LANES]
            out = tt * c_j + pltpu.roll(tt, LANES - half, 1) * up_j + pltpu.roll(tt, half, 1) * dn_j
            o_ref[:, j * n + blk * LANES:j * n + (blk + 1) * LANES] = out.astype(o_ref.dtype)
    o_ref[:, 2 * n:3 * n] = _dot(xb, w_ref[:, 2 * n:3 * n]).astype(o_ref.dtype)


def _rope_inv_table():
    inv = ROPE_THETA ** (-jnp.arange(0, ROT_DIM, 2, dtype=F32) / ROT_DIM)
    head = jnp.concatenate([inv, inv, jnp.zeros((ATTN_HEAD_DIM - ROT_DIM,), F32)])
    return jnp.tile(head, LANES // ATTN_HEAD_DIM)[None, :]


def _qkv_rope(h, w_qkv, positions):
    m, k = h.shape
    n = w_qkv.shape[1]
    tm = min(MM_TM, m)
    pos = positions.reshape(m, 1)
    return pl.pallas_call(
        _qkv_rope_kernel,
        grid=(m // tm,),
        in_specs=[pl.BlockSpec((tm, k), lambda i: (i, 0)),
                  pl.BlockSpec((k, n), lambda i: (0, 0), pipeline_mode=pl.Buffered(1)),
                  pl.BlockSpec((tm, 1), lambda i: (i, 0)),
                  pl.BlockSpec((1, LANES), lambda i: (0, 0))],
        out_specs=pl.BlockSpec((tm, n), lambda i: (i, 0)),
        out_shape=jax.ShapeDtypeStruct((m, n), BF16),
        scratch_shapes=[pltpu.VMEM((k, n), BF16)],
        compiler_params=_arb(1),
        name="mm_qkv_rope",
    )(h, w_qkv, pos, _rope_inv_table())


def _attn_kernel(q_ref, k_ref, v_ref, lq1_ref, lk1_ref, lq2_ref, lk2_ref, sw_ref, wg_ref, wu_ref, wd_ref,
                 o_ref, wgb_ref, wub_ref, wdb_ref,
                 vx_ref, m0_ref, m1_ref, acc0_ref, acc1_ref, *, lambda_init, cast):
    m_refs = (m0_ref, m1_ref)
    acc_refs = (acc0_ref, acc1_ref)
    jp = pl.program_id(2)
    n_tiles = ATT_TILES_PER_STEP
    tq = q_ref.shape[0] // n_tiles
    cast.emit(pl.program_id(0) * pl.num_programs(1) + pl.program_id(1),
              (wg_ref, wu_ref, wd_ref), (wgb_ref, wub_ref, wdb_ref), extra=jp == 0)

    @pl.when(jp == 0)
    def _():
        vx_ref[:, 0:LANES] = v_ref[...]
        vx_ref[:, LANES:] = jnp.ones((vx_ref.shape[0], LANES), vx_ref.dtype)

    q = q_ref[...]
    lane = lax.broadcasted_iota(I32, q.shape, 1)
    zero = jnp.zeros((), q.dtype)
    qs = (jnp.where(lane < ATTN_HEAD_DIM, q, zero), jnp.where(lane >= ATTN_HEAD_DIM, q, zero))
    for c in range(2):
        m_refs[c][...] = jnp.full(m_refs[c].shape, -jnp.inf, F32)
        acc_refs[c][...] = jnp.zeros(acc_refs[c].shape, F32)

    def step(off, windows):
        off = pl.multiple_of(off, tq)
        widest = max(w for _, w, _ in windows)
        kb = k_ref[pl.ds(off, widest), :]
        vb = vx_ref[pl.ds(off, widest), :]
        chains = [(slice(r0, r0 + tq), w, d, c) for r0, w, d in windows for c in range(2)]
        ss = [lax.dot_general(qs[c][rows, :], kb[:w, :], (((1,), (1,)), ((), ())), preferred_element_type=F32)
              for rows, w, _, c in chains]
        for n, (_, w, d, _) in enumerate(chains):
            if d is not None:
                row_i = lax.broadcasted_iota(I32, (tq, w), 0)
                col_i = lax.broadcasted_iota(I32, (tq, w), 1)
                ss[n] = jnp.where(row_i + d >= col_i, ss[n], -jnp.inf)
        m_prev = [m_refs[c][rows, :] for rows, _, _, c in chains]
        mn = [jnp.maximum(mp, jnp.max(s, axis=-1, keepdims=True)) for mp, s in zip(m_prev, ss)]
        ps = [jnp.exp2(s - jnp.tile(m, (1, s.shape[1] // LANES))).astype(BF16) for s, m in zip(ss, mn)]
        for n, (rows, w, _, c) in enumerate(chains):
            alpha = jnp.exp2(m_prev[n] - mn[n])
            acc_refs[c][rows, :] = jnp.tile(alpha, (1, 2)) * acc_refs[c][rows, :] + _dot(ps[n], vb[:w, :])
            m_refs[c][rows, :] = mn[n]

    def shared(j, carry):
        for p in range(n_tiles // 2):
            step(j * (2 * tq), [((2 * p) * tq, 2 * tq, None), ((2 * p + 1) * tq, 2 * tq, None)])
        return carry
    lax.fori_loop(0, jp * (n_tiles // 2), shared, 0)
    base = jp * (n_tiles * tq)
    for p in range(n_tiles // 2):
        a, b = 2 * p * tq, (2 * p + 1) * tq
        step(base + a, [(a, tq, 0), (b, 2 * tq, tq)])
        later = [(r * tq, 2 * tq, None) for r in range(2 * p + 2, n_tiles)]
        if later:
            step(base + a, later)

    lam = (jnp.exp(jnp.sum(lq1_ref[...] * lk1_ref[...], axis=-1, keepdims=True))
           - jnp.exp(jnp.sum(lq2_ref[...] * lk2_ref[...], axis=-1, keepdims=True)) + lambda_init)
    a1 = acc0_ref[...]
    a2 = acc1_ref[...]
    o = a1[:, :LANES] / a1[:, LANES:] - lam * (a2[:, :LANES] / a2[:, LANES:])
    o = o * lax.rsqrt(jnp.mean(o * o, axis=-1, keepdims=True) + NORM_EPS)
    o_ref[...] = (o * sw_ref[...] * (1.0 - lambda_init)).astype(o_ref.dtype)


def _diff_attention(qkv, lq1, lk1, lq2, lk2, subln_w, lambda_init, batch, seq, moe_weights, moe_layer):
    t = batch * seq
    tq = ATT_TILES_PER_STEP * min(ATT_TQ, seq // ATT_TILES_PER_STEP)
    nq = seq // tq
    h_n = ATTN_N_HEADS
    vec = lambda b, h, i: (0, 0)
    cast = _CastPlan(*moe_weights, moe_layer, batch * h_n, lambda b, h, i: b * h_n + h)
    o, *w_bf16 = pl.pallas_call(
        functools.partial(_attn_kernel, lambda_init=lambda_init, cast=cast),
        grid=(batch, h_n, nq),
        in_specs=[pl.BlockSpec((tq, LANES), lambda b, h, i: (b * nq + i, h)),
                  pl.BlockSpec((seq, LANES), lambda b, h, i: (b, h_n + h)),
                  pl.BlockSpec((seq, LANES), lambda b, h, i: (b, 2 * h_n + h)),
                  pl.BlockSpec((1, ATTN_HEAD_DIM), vec), pl.BlockSpec((1, ATTN_HEAD_DIM), vec),
                  pl.BlockSpec((1, ATTN_HEAD_DIM), vec), pl.BlockSpec((1, ATTN_HEAD_DIM), vec),
                  pl.BlockSpec((1, ATTN_V_DIM), vec)] + cast.in_specs,
        out_specs=[pl.BlockSpec((tq, LANES), lambda b, h, i: (b * nq + i, h))] + cast.out_specs,
        out_shape=[jax.ShapeDtypeStruct((t, h_n * ATTN_V_DIM), BF16)] + cast.out_shape,
        scratch_shapes=[pltpu.VMEM((seq, 2 * LANES), BF16),
                        pltpu.VMEM((tq, LANES), F32), pltpu.VMEM((tq, LANES), F32),
                        pltpu.VMEM((tq, 2 * LANES), F32), pltpu.VMEM((tq, 2 * LANES), F32)],
        compiler_params=_arb(3),
        name="diff_attn",
    )(qkv, qkv, qkv, lq1[None, :], lk1[None, :], lq2[None, :], lk2[None, :], subln_w[None, :], *cast.weights)
    return o, tuple(w_bf16)


def kernel(x, positions, ln_mix_g, ln_mix_b, ln_ffn_g, ln_ffn_b, ssm_w_in, ssm_conv_w, ssm_conv_b, ssm_dt_bias, ssm_a_log, ssm_d, ssm_norm_w, ssm_w_out, attn_w_qkv, attn_lam_q1, attn_lam_k1, attn_lam_q2, attn_lam_k2, attn_subln_w, attn_w_o, moe_w_group, moe_w_expert, moe_w_gate, moe_w_up, moe_w_down):
    batch, seq, d = x.shape
    t = batch * seq
    h = x.reshape(t, d)

    w_in = ssm_w_in[0].astype(BF16)

    moe_weights = (moe_w_gate, moe_w_up, moe_w_down)
    zx = _matmul(h, w_in, SSM_ZX_DIM, F32)
    yg, moe_w0 = _ssd(zx, h, w_in[:, SSM_ZX_DIM:], ssm_conv_w[0], ssm_conv_b[0], ssm_dt_bias[0],
                      ssm_a_log[0], ssm_d[0], ssm_norm_w[0], batch, seq, moe_weights, 0)
    ht, eid, wt, cnt = _mm_ln_router(yg, ssm_w_out[0], h, ln_mix_g[0], ln_mix_b[0],
                                     _router_weights(moe_w_group[0], moe_w_expert[0]), "mm_ssm_out_ln_router")
    h, qkv = _hier_moe_ln(ht, eid, wt, cnt, moe_w0, ln_ffn_g[0], ln_ffn_b[0], 0, (attn_w_qkv[0], positions))

    lambda_init = 0.8 - 0.6 * math.exp(-0.3 * 1)
    o, moe_w1 = _diff_attention(qkv, attn_lam_q1[0], attn_lam_k1[0], attn_lam_q2[0], attn_lam_k2[0],
                                attn_subln_w[0], lambda_init, batch, seq, moe_weights, 1)
    ht, eid, wt, cnt = _mm_ln_router(o, attn_w_o[0], h, ln_mix_g[1], ln_mix_b[1],
                                     _router_weights(moe_w_group[1], moe_w_expert[1]), "mm_attn_out_ln_router")
    h = _hier_moe_ln(ht, eid, wt, cnt, moe_w1, ln_ffn_g[1], ln_ffn_b[1], 1)
    return h.reshape(batch, seq, d)
```
